```python
import math
import jax, jax.numpy as jnp
from jax import lax
import numpy as np

D_MODEL = 4096
BATCH = 4
SEQ = 2048
DEPTH = 2

N_A_LAYERS = DEPTH // 2
N_B_LAYERS = DEPTH - N_A_LAYERS
ROPE_THETA = 10000.0
NORM_EPS = 1e-6
Q_BLOCK = 128
D_FF = -(-8 * D_MODEL // (3 * 256)) * 256

MLA_HEADS = D_MODEL // 128
MLA_Q_LORA = 1536
MLA_KV_LORA = 512
MLA_NOPE = 128
MLA_ROPE = 64
MLA_V = 128

NSA_HEADS = D_MODEL // 128
NSA_KV_HEADS = 4
NSA_HPG = NSA_HEADS // NSA_KV_HEADS
NSA_DK = 192
NSA_DV = 128
CMP_BLOCK = 32
CMP_STRIDE = 16
SEL_BLOCK = 64
SEL_TOPK = 16
WINDOW = 512
SEL_Q_CHUNK = 16
FORCE_BONUS = 1e4

kernel_name = "yoco_mla_nsa_hybrid"


def rms_norm(x, g):
    xf = x.astype(jnp.float32)
    y = xf * lax.rsqrt(jnp.mean(xf * xf, axis=-1, keepdims=True) + NORM_EPS)
    return (y * g.astype(jnp.float32)).astype(x.dtype)


def rope(x, pos):
    d = x.shape[-1]
    inv = ROPE_THETA ** (-jnp.arange(0, d, 2, dtype=jnp.float32) / d)
    ang = pos.astype(jnp.float32)[..., None] * inv
    ang = ang.reshape(ang.shape[:2] + (1,) * (x.ndim - 3) + (d // 2,))
    cos, sin = jnp.cos(ang), jnp.sin(ang)
    xf = x.astype(jnp.float32)
    x1, x2 = xf[..., : d // 2], xf[..., d // 2:]
    return jnp.concatenate([x1 * cos - x2 * sin, x2 * cos + x1 * sin], axis=-1).astype(x.dtype)


def masked_softmax(s, mask):
    s = jnp.where(mask, s.astype(jnp.float32), -jnp.inf)
    m = jnp.max(s, axis=-1, keepdims=True)
    m = jnp.where(jnp.isfinite(m), m, 0.0)
    e = jnp.exp(s - m)
    return e / jnp.maximum(jnp.sum(e, axis=-1, keepdims=True), 1e-30)


def swiglu(h, w_in, w_out):
    u = h @ w_in
    a, b = u[..., :D_FF], u[..., D_FF:]
    return (jax.nn.silu(a) * b) @ w_out


def causal_block_attention(q, k, v, scale):
    B, S, H, Dk = q.shape
    nb = S // Q_BLOCK
    qb = q.reshape(B, nb, Q_BLOCK, H, Dk).swapaxes(0, 1)
    kpos = jnp.arange(S)

    def step(args):
        qi, start = args
        s = jnp.einsum('bqhd,bkhd->bhqk', qi, k).astype(jnp.float32) * scale
        qpos = start + jnp.arange(Q_BLOCK)
        mask = kpos[None, :] <= qpos[:, None]
        p = masked_softmax(s, mask[None, None])
        return jnp.einsum('bhqk,bkhd->bqhd', p.astype(v.dtype), v)

    out = lax.map(step, (qb, jnp.arange(nb) * Q_BLOCK))
    return out.swapaxes(0, 1).reshape(B, S, H, v.shape[-1])


def mla_mixer(h, pos, w_in, g_q, w_uq, g_kv, w_ukv, w_o):
    B, S, _ = h.shape
    u = h @ w_in
    c_q = rms_norm(u[..., :MLA_Q_LORA], g_q)
    c_kv = rms_norm(u[..., MLA_Q_LORA:MLA_Q_LORA + MLA_KV_LORA], g_kv)
    k_rope = rope(u[..., MLA_Q_LORA + MLA_KV_LORA:][:, :, None, :], pos)
    q = (c_q @ w_uq).reshape(B, S, MLA_HEADS, MLA_NOPE + MLA_ROPE)
    q = jnp.concatenate([q[..., :MLA_NOPE], rope(q[..., MLA_NOPE:], pos)], axis=-1)
    kv = (c_kv @ w_ukv).reshape(B, S, MLA_HEADS, MLA_NOPE + MLA_V)
    k = jnp.concatenate([kv[..., :MLA_NOPE],
                         jnp.broadcast_to(k_rope, (B, S, MLA_HEADS, MLA_ROPE))], axis=-1)
    v = kv[..., MLA_NOPE:]
    o = causal_block_attention(q, k, v, (MLA_NOPE + MLA_ROPE) ** -0.5)
    return o.reshape(B, S, MLA_HEADS * MLA_V) @ w_o


def compress_blocks(t, pe, w1, w2):
    B, S, G, d = t.shape
    n_cmp = (S - CMP_BLOCK) // CMP_STRIDE + 1
    idx = (jnp.arange(n_cmp) * CMP_STRIDE)[:, None] + jnp.arange(CMP_BLOCK)[None, :]
    blocks = t[:, idx] + pe[None, None, :, None, :]
    flat = blocks.transpose(0, 1, 3, 2, 4).reshape(B, n_cmp, G, CMP_BLOCK * d)
    return jax.nn.gelu(flat @ w1) @ w2


def nsa_shared_kv(h, pos, g_s, w_kv, pe_k, w1_k, w2_k, pe_v, w1_v, w2_v):
    B, S, _ = h.shape
    u = (rms_norm(h, g_s) @ w_kv).reshape(B, S, 3, NSA_KV_HEADS, NSA_DK + NSA_DV)
    k_c, v_c = rope(u[:, :, 0, :, :NSA_DK], pos), u[:, :, 0, :, NSA_DK:]
    k_s, v_s = rope(u[:, :, 1, :, :NSA_DK], pos), u[:, :, 1, :, NSA_DK:]
    k_w, v_w = rope(u[:, :, 2, :, :NSA_DK], pos), u[:, :, 2, :, NSA_DK:]
    kc = compress_blocks(k_c, pe_k, w1_k, w2_k)
    vc = compress_blocks(v_c, pe_v, w1_v, w2_v)
    n_sel = S // SEL_BLOCK
    ks = k_s.reshape(B, n_sel, SEL_BLOCK, NSA_KV_HEADS, NSA_DK).transpose(0, 3, 1, 2, 4)
    vs = v_s.reshape(B, n_sel, SEL_BLOCK, NSA_KV_HEADS, NSA_DV).transpose(0, 3, 1, 2, 4)
    pad = ((0, 0), (WINDOW, 0), (0, 0), (0, 0))
    kw = jnp.pad(k_w, pad)
    vw = jnp.pad(v_w, pad)
    return kc, vc, ks, vs, kw, vw


def nsa_selected(q, blk_idx, blk_valid, ks, vs, scale):
    B, S, G, HPG, DK = q.shape
    nk = blk_idx.shape[-1]
    n_chunk = S // SEL_Q_CHUNK

    def chunked(a):
        return a.reshape((B, n_chunk, SEL_Q_CHUNK) + a.shape[2:]).swapaxes(0, 1)

    b_ix = jnp.arange(B)[:, None, None, None]
    g_ix = jnp.arange(G)[None, None, :, None]

    def step(args):
        qi, idx, val, start = args
        kb = ks[b_ix, g_ix, idx]
        vb = vs[b_ix, g_ix, idx]
        s = jnp.einsum('bqghd,bqgkld->bqghkl', qi, kb).astype(jnp.float32) * scale
        tok = idx[..., None] * SEL_BLOCK + jnp.arange(SEL_BLOCK)
        qpos = start + jnp.arange(SEL_Q_CHUNK)
        mask = val[..., None] & (tok <= qpos[None, :, None, None, None])
        mask = mask.reshape(B, SEL_Q_CHUNK, G, 1, nk * SEL_BLOCK)
        p = masked_softmax(s.reshape(B, SEL_Q_CHUNK, G, HPG, nk * SEL_BLOCK), mask)
        p = p.reshape(B, SEL_Q_CHUNK, G, HPG, nk, SEL_BLOCK)
        return jnp.einsum('bqghkl,bqgklv->bqghv', p.astype(vb.dtype), vb)

    out = lax.map(step, (chunked(q), chunked(blk_idx), chunked(blk_valid),
                         jnp.arange(n_chunk) * SEL_Q_CHUNK))
    return out.swapaxes(0, 1).reshape(B, S, G, HPG, vs.shape[-1])


def nsa_window(q, kw, vw, scale):
    B, S, G, HPG, DK = q.shape
    nb = S // Q_BLOCK
    span = WINDOW + Q_BLOCK
    qb = q.reshape(B, nb, Q_BLOCK, G, HPG, DK).swapaxes(0, 1)

    def step(args):
        qi, start = args
        kb = lax.dynamic_slice_in_dim(kw, start, span, axis=1)
        vb = lax.dynamic_slice_in_dim(vw, start, span, axis=1)
        s = jnp.einsum('bqghd,bkgd->bqghk', qi, kb).astype(jnp.float32) * scale
        kpos = start - WINDOW + jnp.arange(span)
        qpos = start + jnp.arange(Q_BLOCK)
        mask = ((kpos[None, :] <= qpos[:, None]) & (qpos[:, None] - kpos[None, :] < WINDOW)
                & (kpos[None, :] >= 0))
        p = masked_softmax(s, mask[None, :, None, None, :])
        return jnp.einsum('bqghk,bkgv->bqghv', p.astype(vb.dtype), vb)

    out = lax.map(step, (qb, jnp.arange(nb) * Q_BLOCK))
    return out.swapaxes(0, 1).reshape(B, S, G, HPG, vw.shape[-1])


def nsa_mixer(h, pos, shared, w_in, w_o):
    kc, vc, ks, vs, kw, vw = shared
    B, S, _ = h.shape
    G, HPG = NSA_KV_HEADS, NSA_HPG
    scale = NSA_DK ** -0.5
    u = h @ w_in
    q = rope(u[..., :NSA_HEADS * NSA_DK].reshape(B, S, G, HPG, NSA_DK), pos)
    gates = jax.nn.sigmoid(u[..., NSA_HEADS * NSA_DK:].astype(jnp.float32)).reshape(B, S, G, HPG, 3)
    t = jnp.arange(S)

    n_cmp = kc.shape[1]
    c_start = jnp.arange(n_cmp) * CMP_STRIDE
    mask_c = (c_start[None, :] + CMP_BLOCK - 1) <= t[:, None]
    s_c = jnp.einsum('bsghd,bcgd->bsghc', q, kc).astype(jnp.float32) * scale
    p_c = masked_softmax(s_c, mask_c[None, :, None, None, :])
    o_c = jnp.einsum('bsghc,bcgv->bsghv', p_c.astype(vc.dtype), vc)

    n_sel = S // SEL_BLOCK
    j_start = jnp.arange(n_sel) * SEL_BLOCK
    overlap = ((c_start[:, None] < j_start[None, :] + SEL_BLOCK)
               & (c_start[:, None] + CMP_BLOCK > j_start[None, :])).astype(jnp.float32)
    imp = jnp.einsum('bsghc,cj->bsgj', p_c, overlap)
    j = jnp.arange(n_sel)
    cur = (t // SEL_BLOCK)[:, None]
    valid = j_start[None, :] <= t[:, None]
    forced = valid & ((j[None, :] == 0) | (j[None, :] == cur) | (j[None, :] == cur - 1))
    score = jnp.where(valid[None, :, None, :],
                      imp + jnp.where(forced, FORCE_BONUS, 0.0)[None, :, None, :], -jnp.inf)
    top_vals, blk_idx = lax.top_k(score, min(SEL_TOPK, n_sel))
    blk_valid = jnp.isfinite(top_vals)
    o_s = nsa_selected(q, blk_idx, blk_valid, ks, vs, scale)

    o_w = nsa_window(q, kw, vw, scale)

    o = (gates[..., 0:1] * o_c.astype(jnp.float32) + gates[..., 1:2] * o_s.astype(jnp.float32)
         + gates[..., 2:3] * o_w.astype(jnp.float32)).astype(h.dtype)
    return o.reshape(B, S, NSA_HEADS * NSA_DV) @ w_o


def setup_inputs(seed: int = 0) -> dict:
    key = jax.random.key(seed)
    ks = jax.random.split(key, 32)

    def w(k, shape, fan_in):
        return jax.random.normal(k, shape, jnp.float32) * (fan_in ** -0.5)

    def gain(k, shape):
        return 1.0 + 0.02 * jax.random.normal(k, shape, jnp.float32)

    na, nb = N_A_LAYERS, N_B_LAYERS
    a_in_w = MLA_Q_LORA + MLA_KV_LORA + MLA_ROPE
    x = jax.random.normal(ks[0], (BATCH, SEQ, D_MODEL), jnp.float32)
    positions = (jnp.arange(SEQ, dtype=jnp.int32)[None, :]
                 + jax.random.randint(ks[1], (BATCH, 1), 0, SEQ, dtype=jnp.int32))
    return {
        "x": x,
        "positions": positions,
        "a_norm": gain(ks[2], (na, D_MODEL)),
        "a_w_in": w(ks[3], (na, D_MODEL, a_in_w), D_MODEL),
        "a_q_norm": gain(ks[4], (na, MLA_Q_LORA)),
        "a_w_uq": w(ks[5], (na, MLA_Q_LORA, MLA_HEADS * (MLA_NOPE + MLA_ROPE)), MLA_Q_LORA),
        "a_kv_norm": gain(ks[6], (na, MLA_KV_LORA)),
        "a_w_ukv": w(ks[7], (na, MLA_KV_LORA, MLA_HEADS * (MLA_NOPE + MLA_V)), MLA_KV_LORA),
        "a_w_o": w(ks[8], (na, MLA_HEADS * MLA_V, D_MODEL), MLA_HEADS * MLA_V),
        "s_norm": gain(ks[9], (D_MODEL,)),
        "s_w_kv": w(ks[10], (D_MODEL, 3 * NSA_KV_HEADS * (NSA_DK + NSA_DV)), D_MODEL),
        "s_cmp_pe_k": 0.02 * jax.random.normal(ks[11], (CMP_BLOCK, NSA_DK), jnp.float32),
        "s_cmp_w1_k": w(ks[12], (CMP_BLOCK * NSA_DK, NSA_DK), CMP_BLOCK * NSA_DK),
        "s_cmp_w2_k": w(ks[13], (NSA_DK, NSA_DK), NSA_DK),
        "s_cmp_pe_v": 0.02 * jax.random.normal(ks[14], (CMP_BLOCK, NSA_DV), jnp.float32),
        "s_cmp_w1_v": w(ks[15], (CMP_BLOCK * NSA_DV, NSA_DV), CMP_BLOCK * NSA_DV),
        "s_cmp_w2_v": w(ks[16], (NSA_DV, NSA_DV), NSA_DV),
        "b_norm": gain(ks[17], (nb, D_MODEL)),
        "b_w_in": w(ks[18], (nb, D_MODEL, NSA_HEADS * NSA_DK + 3 * NSA_HEADS), D_MODEL),
        "b_w_o": w(ks[19], (nb, NSA_HEADS * NSA_DV, D_MODEL), NSA_HEADS * NSA_DV),
        "f_norm": gain(ks[20], (DEPTH, D_MODEL)),
        "f_w_in": w(ks[21], (DEPTH, D_MODEL, 2 * D_FF), D_MODEL),
        "f_w_out": w(ks[22], (DEPTH, D_FF, D_MODEL), D_FF),
        "final_norm": gain(ks[23], (D_MODEL,)),
    }


def reference(x, positions, a_norm, a_w_in, a_q_norm, a_w_uq, a_kv_norm, a_w_ukv, a_w_o,
              s_norm, s_w_kv, s_cmp_pe_k, s_cmp_w1_k, s_cmp_w2_k, s_cmp_pe_v, s_cmp_w1_v,
              s_cmp_w2_v, b_norm, b_w_in, b_w_o, f_norm, f_w_in, f_w_out, final_norm):
    h = x
    shared = None
    for layer in range(DEPTH):
        if layer < N_A_LAYERS:
            i = layer
            h = h + mla_mixer(rms_norm(h, a_norm[i]), positions, a_w_in[i], a_q_norm[i],
                              a_w_uq[i], a_kv_norm[i], a_w_ukv[i], a_w_o[i])
        else:
            if layer == N_A_LAYERS:
                shared = nsa_shared_kv(h, positions, s_norm, s_w_kv, s_cmp_pe_k, s_cmp_w1_k,
                                       s_cmp_w2_k, s_cmp_pe_v, s_cmp_w1_v, s_cmp_w2_v)
            j = layer - N_A_LAYERS
            h = h + nsa_mixer(rms_norm(h, b_norm[j]), positions, shared, b_w_in[j], b_w_o[j])
        h = h + swiglu(rms_norm(h, f_norm[layer]), f_w_in[layer], f_w_out[layer])
    return rms_norm(h, final_norm)
```

```python
import functools

import numpy as np
import jax
import jax.numpy as jnp
from jax import lax
from jax.experimental import pallas as pl
from jax.experimental.pallas import tpu as pltpu

F32 = jnp.float32
BF16 = jnp.bfloat16

LANES = 128
VMEM_LIMIT_BYTES = 56 * 1024 * 1024

NORM_EPS = 1e-6
ROPE_THETA = 10000.0

MLA_Q_LORA = 1536
MLA_KV_LORA = 512
MLA_NOPE = 128
MLA_ROPE = 64
MLA_V = 128

NSA_G = 4
NSA_DK = 192
NSA_DV = 128
NSA_HALF = NSA_DK // 2
NSA_KPAD = 2 * LANES
NSA_KV_W = NSA_KPAD + NSA_DV
CMP_BLOCK = 32
CMP_STRIDE = 16
SEL_BLOCK = 64
SEL_TOPK = 16
WINDOW = 512
FORCE_BONUS = 1e4
NEG_BIG = -1e30


def _params(sem):
    return pltpu.CompilerParams(dimension_semantics=sem, vmem_limit_bytes=VMEM_LIMIT_BYTES)


def _rmsnorm_kernel(x_ref, g_ref, o_ref):
    x = x_ref[...]
    ms = jnp.mean(x * x, axis=-1, keepdims=True)
    o_ref[...] = (x * lax.rsqrt(ms + NORM_EPS) * g_ref[...]).astype(o_ref.dtype)


def _rmsnorm(x, g, out_dtype, tm=256):
    t, d = x.shape
    return pl.pallas_call(
        _rmsnorm_kernel,
        grid=(t // tm,),
        in_specs=[pl.BlockSpec((tm, d), lambda i: (i, 0)),
                  pl.BlockSpec((1, d), lambda i: (0, 0))],
        out_specs=pl.BlockSpec((tm, d), lambda i: (i, 0)),
        out_shape=jax.ShapeDtypeStruct((t, d), out_dtype),
        compiler_params=_params(("parallel",)),
        name="rmsnorm",
    )(x, g.reshape(1, d))


def _mm_kernel(n_w, n_e, n_o, nk, epilogue, *refs):
    a_ref = refs[0]
    w_refs = refs[1:1 + n_w]
    e_refs = refs[1 + n_w:1 + n_w + n_e]
    o_refs = refs[1 + n_w + n_e:1 + n_w + n_e + n_o]
    acc_refs = refs[1 + n_w + n_e + n_o:]
    j = pl.program_id(1)
    if nk == 1:
        accs = [jnp.dot(a_ref[...], w[...], preferred_element_type=F32) for w in w_refs]
        epilogue(accs, e_refs, o_refs, j)
        return
    k = pl.program_id(2)

    @pl.when(k == 0)
    def _():
        for w, acc in zip(w_refs, acc_refs):
            acc[...] = jnp.dot(a_ref[...], w[...], preferred_element_type=F32)

    @pl.when(k > 0)
    def _():
        for w, acc in zip(w_refs, acc_refs):
            acc[...] += jnp.dot(a_ref[...], w[...], preferred_element_type=F32)

    @pl.when(k == nk - 1)
    def _():
        epilogue([acc[...] for acc in acc_refs], e_refs, o_refs, j)


def _mm(a, w, epilogue, out_shape, out_specs, *, tm, tn, tk, nj, w_col_maps=None,
        extras=(), extra_specs=(), name="matmul"):
    m, kdim = a.shape
    ni, nk = m // tm, kdim // tk
    assert ni * tm == m and nk * tk == kdim
    if w_col_maps is None:
        w_col_maps = (lambda j: j,)
    n_w = len(w_col_maps)
    single = not isinstance(out_shape, (tuple, list))
    out_shapes = (out_shape,) if single else tuple(out_shape)
    out_specs_t = (out_specs,) if single else tuple(out_specs)
    in_specs = [pl.BlockSpec((tm, tk), lambda i, j, k: (i, k))]
    for cm in w_col_maps:
        in_specs.append(pl.BlockSpec((tk, tn), functools.partial(
            lambda i, j, k, cm: (k, cm(j)), cm=cm)))
    in_specs += list(extra_specs)
    scratch = [pltpu.VMEM((tm, tn), F32) for _ in range(n_w)] if nk > 1 else []
    kern = functools.partial(_mm_kernel, n_w, len(extras), len(out_shapes), nk, epilogue)
    res = pl.pallas_call(
        kern,
        grid=(ni, nj, nk),
        in_specs=in_specs,
        out_specs=out_specs_t,
        out_shape=out_shapes,
        scratch_shapes=scratch,
        compiler_params=_params(("parallel", "parallel", "arbitrary")),
        name=name,
    )(a, *([w] * n_w), *extras)
    return res[0] if single else res


def _rope_pairs_64(x, cos2, sin2):
    lane = lax.broadcasted_iota(jnp.int32, x.shape, 1)
    first_half = (lane % MLA_ROPE) < (MLA_ROPE // 2)
    partner = jnp.where(first_half,
                        pltpu.roll(x, LANES - MLA_ROPE // 2, 1),
                        pltpu.roll(x, MLA_ROPE // 2, 1))
    return x * cos2 + partner * sin2


def _epi_store(accs, e_refs, o_refs, j):
    o_refs[0][...] = accs[0].astype(o_refs[0].dtype)


def _epi_residual(accs, e_refs, o_refs, j):
    o_refs[0][...] = e_refs[0][...] + accs[0]


def _epi_swiglu(accs, e_refs, o_refs, j):
    a, b = accs
    o_refs[0][...] = (a * jax.nn.sigmoid(a) * b).astype(o_refs[0].dtype)


def _epi_sigmoid(accs, e_refs, o_refs, j):
    o_refs[0][...] = jax.nn.sigmoid(accs[0])


def _epi_mla_in(accs, e_refs, o_refs, j):
    u = accs[0]
    gq_ref, gkv_ref, cos_ref, sin_ref = e_refs
    cq = u[:, :MLA_Q_LORA]
    ms = jnp.mean(cq * cq, axis=-1, keepdims=True)
    o_refs[0][...] = (cq * lax.rsqrt(ms + NORM_EPS) * gq_ref[...]).astype(BF16)
    ckv = u[:, MLA_Q_LORA:MLA_Q_LORA + MLA_KV_LORA]
    ms = jnp.mean(ckv * ckv, axis=-1, keepdims=True)
    o_refs[1][...] = (ckv * lax.rsqrt(ms + NORM_EPS) * gkv_ref[...]).astype(BF16)
    kr = u[:, MLA_Q_LORA + MLA_KV_LORA:]
    o_refs[2][...] = _rope_pairs_64(kr, cos_ref[...], sin_ref[...]).astype(BF16)


def _epi_mla_q(n_nope_tiles, scale, accs, e_refs, o_refs, j):
    cos_ref, sin_ref = e_refs
    acc = accs[0]

    @pl.when(j < n_nope_tiles)
    def _():
        o_refs[0][...] = (acc * scale).astype(BF16)

    @pl.when(j >= n_nope_tiles)
    def _():
        for c in range(acc.shape[1] // LANES):
            x = acc[:, c * LANES:(c + 1) * LANES]
            r = _rope_pairs_64(x, cos_ref[...], sin_ref[...])
            o_refs[0][:, c * LANES:(c + 1) * LANES] = (r * scale).astype(BF16)


def _rope_halves(a, b, cos, sin):
    return a * cos - b * sin, b * cos + a * sin


def _epi_nsa_q(scale, accs, e_refs, o_refs, j):
    cos_ref, sin_ref = e_refs
    acc = accs[0]
    cos, sin = cos_ref[...], sin_ref[...]
    for r in range(acc.shape[1] // NSA_KPAD):
        a = acc[:, r * NSA_KPAD:r * NSA_KPAD + LANES]
        b = acc[:, r * NSA_KPAD + LANES:(r + 1) * NSA_KPAD]
        ra, rb = _rope_halves(a, b, cos, sin)
        o_refs[0][r, :, :LANES] = (ra * scale).astype(BF16)
        o_refs[0][r, :, LANES:] = (rb * scale).astype(BF16)


def _epi_nsa_kv(accs, e_refs, o_refs, j):
    cos_ref, sin_ref = e_refs
    acc = accs[0]
    ra, rb = _rope_halves(acc[:, :LANES], acc[:, LANES:NSA_KPAD], cos_ref[...], sin_ref[...])
    o_refs[0][0, :, :LANES] = ra.astype(BF16)
    o_refs[0][0, :, LANES:NSA_KPAD] = rb.astype(BF16)
    o_refs[0][0, :, NSA_KPAD:] = acc[:, NSA_KPAD:].astype(BF16)


def _flash_step(q, k, v, bias, m_ref, l_ref, acc_ref):
    s = lax.dot_general(q, k, (((1,), (1,)), ((), ())), preferred_element_type=F32)
    if bias is not None:
        s = s + bias
    m_prev = m_ref[...]
    m_new = jnp.maximum(m_prev, jnp.max(s, axis=-1, keepdims=True))
    alpha = jnp.exp(m_prev - m_new)
    p = jnp.exp(s - m_new)
    l_ref[...] = alpha * l_ref[...] + jnp.sum(p, axis=-1, keepdims=True)
    acc_ref[...] = alpha * acc_ref[...] + jnp.dot(p.astype(BF16), v, preferred_element_type=F32)
    m_ref[...] = m_new


def _flash_init(m_ref, l_ref, acc_ref):
    m_ref[...] = jnp.full(m_ref.shape, NEG_BIG, F32)
    l_ref[...] = jnp.zeros(l_ref.shape, F32)
    acc_ref[...] = jnp.zeros(acc_ref.shape, F32)


def _mla_attn_kernel(t_blk, qn_ref, qr_ref, kv_ref, kr_ref, o_ref, m_ref, l_ref, acc_ref):
    qi = pl.program_id(2)
    lane = lax.broadcasted_iota(jnp.int32, (t_blk, LANES), 1)
    row = lax.broadcasted_iota(jnp.int32, (t_blk, t_blk), 0)
    col = lax.broadcasted_iota(jnp.int32, (t_blk, t_blk), 1)
    diag_bias = jnp.where(col <= row, 0.0, NEG_BIG).astype(F32)
    qr = qr_ref[...]
    for hh in range(2):
        mine = (lane < MLA_ROPE) if hh == 0 else (lane >= MLA_ROPE)
        q = jnp.concatenate(
            [qn_ref[:, hh * MLA_NOPE:(hh + 1) * MLA_NOPE], jnp.where(mine, qr, jnp.zeros_like(qr))],
            axis=1)
        kv_w = MLA_NOPE + MLA_V
        _flash_init(m_ref, l_ref, acc_ref)

        def step(j, bias):
            k0 = pl.multiple_of(j * t_blk, t_blk)
            k = jnp.concatenate([kv_ref[pl.ds(k0, t_blk), hh * kv_w:hh * kv_w + MLA_NOPE],
                                 kr_ref[pl.ds(k0, t_blk), :]], axis=1)
            v = kv_ref[pl.ds(k0, t_blk), hh * kv_w + MLA_NOPE:(hh + 1) * kv_w]
            _flash_step(q, k, v, bias, m_ref, l_ref, acc_ref)

        def body(j, carry):
            step(j, None)
            return carry

        lax.fori_loop(0, qi, body, 0)
        step(qi, diag_bias)
        o_ref[:, hh * MLA_V:(hh + 1) * MLA_V] = (acc_ref[...] / l_ref[...]).astype(o_ref.dtype)


def _mla_attention(q, kv, kr, batch, seq, heads, t_blk=256):
    t = batch * seq
    nq = seq // t_blk
    n_nope_blk = heads * MLA_NOPE // LANES
    kern = functools.partial(_mla_attn_kernel, t_blk)
    return pl.pallas_call(
        kern,
        grid=(batch, heads // 2, nq),
        in_specs=[
            pl.BlockSpec((t_blk, 2 * MLA_NOPE), lambda b, hp, qi: (b * nq + qi, hp)),
            pl.BlockSpec((t_blk, LANES), lambda b, hp, qi: (b * nq + qi, n_nope_blk + hp)),
            pl.BlockSpec((seq, 2 * (MLA_NOPE + MLA_V)), lambda b, hp, qi: (b, hp)),
            pl.BlockSpec((seq, LANES), lambda b, hp, qi: (b, 0)),
        ],
        out_specs=pl.BlockSpec((t_blk, 2 * MLA_V), lambda b, hp, qi: (b * nq + qi, hp)),
        out_shape=jax.ShapeDtypeStruct((t, heads * MLA_V), BF16),
        scratch_shapes=[pltpu.VMEM((t_blk, 1), F32), pltpu.VMEM((t_blk, 1), F32),
                        pltpu.VMEM((t_blk, MLA_V), F32)],
        compiler_params=_params(("parallel", "parallel", "arbitrary")),
        name="mla_attention",
    )(q, q, kv, kr)


def _compress_kernel(a_ref, w1_ref, pe_ref, w2_ref, o_ref):
    a = a_ref[0]
    y_lo = jnp.dot(a, w1_ref[0], preferred_element_type=F32)
    y_hi = jnp.dot(a, w1_ref[1], preferred_element_type=F32)
    pe_c = (jnp.dot(pe_ref[0], w1_ref[0], preferred_element_type=F32)
            + jnp.dot(pe_ref[1], w1_ref[1], preferred_element_type=F32))
    n = y_hi.shape[0]
    hid = y_lo + pltpu.roll(y_hi, n - 1, 0) + pe_c[0:1, :]
    hid = jax.nn.gelu(hid, approximate=True)
    o_ref[0] = jnp.dot(hid.astype(BF16), w2_ref[...], preferred_element_type=F32).astype(BF16)


def _compress(u3c, w1, pe, w2, batch, seq):
    n_chunk = seq // CMP_STRIDE
    width = CMP_STRIDE * NSA_KV_W
    return pl.pallas_call(
        _compress_kernel,
        grid=(batch, NSA_G),
        in_specs=[
            pl.BlockSpec((1, n_chunk, width), lambda b, g: (g, b, 0)),
            pl.BlockSpec((2, width, NSA_KV_W), lambda b, g: (0, 0, 0)),
            pl.BlockSpec((2, 8, width), lambda b, g: (0, 0, 0)),
            pl.BlockSpec((NSA_KV_W, NSA_KV_W), lambda b, g: (0, 0)),
        ],
        out_specs=pl.BlockSpec((1, n_chunk, NSA_KV_W), lambda b, g: (b * NSA_G + g, 0, 0)),
        out_shape=jax.ShapeDtypeStruct((batch * NSA_G, n_chunk, NSA_KV_W), BF16),
        compiler_params=_params(("parallel", "parallel")),
        name="nsa_compress",
    )(u3c, w1, pe, w2)


def _nsa_attn_kernel(hpg, tq, tk, seq, q_ref, kc_ref, ks_ref, kw_ref, gate_ref, ovl_ref, exp_ref,
                     o_ref, bias_ref, oc_ref, os_ref, m_ref, l_ref, acc_ref):
    qi = pl.program_id(2)
    t0 = qi * tq
    m_rows = hpg * tq
    n_cmp = kc_ref.shape[1]
    n_sel = seq // SEL_BLOCK
    q = q_ref[...].reshape(m_rows, NSA_KPAD)

    kc = kc_ref[0, :, :NSA_KPAD]
    vc = kc_ref[0, :, NSA_KPAD:]
    s = lax.dot_general(q, kc, (((1,), (1,)), ((), ())), preferred_element_type=F32)
    tpos_c = t0 + lax.rem(lax.broadcasted_iota(jnp.int32, (m_rows, n_cmp), 0), tq)
    cidx = lax.broadcasted_iota(jnp.int32, (m_rows, n_cmp), 1)
    mask_c3 = (cidx * CMP_STRIDE + CMP_BLOCK - 1) <= tpos_c
    s = jnp.where(mask_c3, s, NEG_BIG)
    mx = jnp.max(s, axis=-1, keepdims=True)
    e = jnp.where(mask_c3, jnp.exp(s - mx), 0.0)
    p = e / jnp.maximum(jnp.sum(e, axis=-1, keepdims=True), 1e-30)
    oc_ref[...] = jnp.dot(p.astype(BF16), vc, preferred_element_type=F32)

    psum = p[0:tq]
    for h in range(1, hpg):
        psum = psum + p[h * tq:(h + 1) * tq]
    p1 = psum.astype(BF16)
    p2 = (psum - p1.astype(F32)).astype(BF16)
    p3 = (psum - p1.astype(F32) - p2.astype(F32)).astype(BF16)
    ovl = ovl_ref[...]
    imp = (jnp.dot(p1, ovl, preferred_element_type=F32)
           + jnp.dot(p2, ovl, preferred_element_type=F32)
           + jnp.dot(p3, ovl, preferred_element_type=F32))
    jl = lax.broadcasted_iota(jnp.int32, (tq, LANES), 1)
    tpos = t0 + lax.broadcasted_iota(jnp.int32, (tq, LANES), 0)
    valid = (jl * SEL_BLOCK <= tpos) & (jl < n_sel)
    cur = jnp.right_shift(tpos, SEL_BLOCK.bit_length() - 1)
    forced = valid & ((jl == 0) | (jl == cur) | (jl == cur - 1))
    score = jnp.where(valid, imp + jnp.where(forced, FORCE_BONUS, 0.0), -jnp.inf)
    rank = jnp.zeros((tq, LANES), jnp.int32)
    for i in range(n_sel):
        col = score[:, i:i + 1]
        beats = (col > score) | ((col == score) & (jl > i))
        rank = rank + beats.astype(jnp.int32)
    sel = valid & (rank < SEL_TOPK)
    sel_keys = jnp.dot(jnp.where(sel, 1.0, 0.0).astype(BF16), exp_ref[...],
                       preferred_element_type=F32)
    for jj in range(seq // tk):
        kpos = jj * tk + lax.broadcasted_iota(jnp.int32, (tq, tk), 1)
        tq_pos = t0 + lax.broadcasted_iota(jnp.int32, (tq, tk), 0)
        ok = (sel_keys[:, jj * tk:(jj + 1) * tk] > 0.5) & (kpos <= tq_pos)
        bias_ref[jj] = jnp.where(ok, 0.0, NEG_BIG).astype(F32)

    def run_branch(kv_ref, j_lo, j_hi, bias_fn):
        _flash_init(m_ref, l_ref, acc_ref)

        def body(j, carry):
            k0 = pl.multiple_of(j * tk, tk)
            k = kv_ref[0, pl.ds(k0, tk), :NSA_KPAD]
            v = kv_ref[0, pl.ds(k0, tk), NSA_KPAD:]
            bias = bias_fn(j)
            bias3 = jnp.concatenate([bias] * hpg, axis=0)
            _flash_step(q, k, v, bias3, m_ref, l_ref, acc_ref)
            return carry

        lax.fori_loop(j_lo, j_hi, body, 0)
        return acc_ref[...] / l_ref[...]

    j_hi = (t0 + tq - 1) // tk + 1
    os_ref[...] = run_branch(ks_ref, 0, j_hi, lambda j: bias_ref[j])

    def window_bias(j):
        kpos = j * tk + lax.broadcasted_iota(jnp.int32, (tq, tk), 1)
        tq_pos = t0 + lax.broadcasted_iota(jnp.int32, (tq, tk), 0)
        ok = (kpos <= tq_pos) & (tq_pos - kpos < WINDOW)
        return jnp.where(ok, 0.0, NEG_BIG).astype(F32)

    j_lo = jnp.maximum(t0 - (WINDOW - 1), 0) // tk
    o_w = run_branch(kw_ref, j_lo, j_hi, window_bias)

    gates = gate_ref[...]
    for h in range(hpg):
        rows = slice(h * tq, (h + 1) * tq)
        o = (gates[:, 3 * h:3 * h + 1] * oc_ref[rows, :]
             + gates[:, 3 * h + 1:3 * h + 2] * os_ref[rows, :]
             + gates[:, 3 * h + 2:3 * h + 3] * o_w[rows, :])
        o_ref[:, h * NSA_DV:(h + 1) * NSA_DV] = o.astype(o_ref.dtype)


def _nsa_attention(q, kc, u3, gates, ovl, expand, batch, seq, hpg, tq=128, tk=256):
    t = batch * seq
    nq = seq // tq
    n_chunk = seq // CMP_STRIDE
    m_rows = hpg * tq
    kern = functools.partial(_nsa_attn_kernel, hpg, tq, tk, seq)
    return pl.pallas_call(
        kern,
        grid=(batch, NSA_G, nq),
        in_specs=[
            pl.BlockSpec((hpg, tq, NSA_KPAD), lambda b, g, qi: (g, b * nq + qi, 0)),
            pl.BlockSpec((1, n_chunk, NSA_KV_W), lambda b, g, qi: (b * NSA_G + g, 0, 0)),
            pl.BlockSpec((1, seq, NSA_KV_W), lambda b, g, qi: (NSA_G + g, b, 0)),
            pl.BlockSpec((1, seq, NSA_KV_W), lambda b, g, qi: (2 * NSA_G + g, b, 0)),
            pl.BlockSpec((tq, LANES), lambda b, g, qi: (b * nq + qi, g)),
            pl.BlockSpec((n_chunk, LANES), lambda b, g, qi: (0, 0)),
            pl.BlockSpec((LANES, seq), lambda b, g, qi: (0, 0)),
        ],
        out_specs=pl.BlockSpec((tq, hpg * NSA_DV), lambda b, g, qi: (b * nq + qi, g)),
        out_shape=jax.ShapeDtypeStruct((t, NSA_G * hpg * NSA_DV), BF16),
        scratch_shapes=[
            pltpu.VMEM((seq // tk, tq, tk), F32),
            pltpu.VMEM((m_rows, NSA_DV), F32),
            pltpu.VMEM((m_rows, NSA_DV), F32),
            pltpu.VMEM((m_rows, 1), F32),
            pltpu.VMEM((m_rows, 1), F32),
            pltpu.VMEM((m_rows, NSA_DV), F32),
        ],
        compiler_params=_params(("parallel", "parallel", "arbitrary")),
        name="nsa_attention",
    )(q, kc, u3, u3, gates, ovl, expand)


def _gather_cols(w, idx):
    idx = np.asarray(idx)
    wz = jnp.concatenate([w, jnp.zeros((w.shape[0], 1), w.dtype)], axis=1)
    return jnp.take(wz, jnp.asarray(np.where(idx < 0, w.shape[1], idx), jnp.int32), axis=1)


def _nsa_k_layout():
    lay = np.full((NSA_KPAD,), -1, np.int64)
    lay[:NSA_HALF] = np.arange(NSA_HALF)
    lay[LANES:LANES + NSA_HALF] = NSA_HALF + np.arange(NSA_HALF)
    return lay


def _rope_tables(pos_flat):
    pos = pos_flat.astype(F32)[:, None]
    inv64 = ROPE_THETA ** (-jnp.arange(0, MLA_ROPE, 2, dtype=F32) / MLA_ROPE)
    ang = pos * inv64
    c, s = jnp.cos(ang), jnp.sin(ang)
    cos_mla = jnp.concatenate([c, c, c, c], axis=1)
    sin_mla = jnp.concatenate([-s, s, -s, s], axis=1)
    inv192 = ROPE_THETA ** (-jnp.arange(0, NSA_DK, 2, dtype=F32) / NSA_DK)
    ang = pos * inv192
    pad = jnp.zeros((pos.shape[0], LANES - NSA_HALF), F32)
    cos_nsa = jnp.concatenate([jnp.cos(ang), pad], axis=1)
    sin_nsa = jnp.concatenate([jnp.sin(ang), pad], axis=1)
    return cos_mla, sin_mla, cos_nsa, sin_nsa


def _round_up(x, m):
    return -(-x // m) * m


def kernel(x, positions, a_norm, a_w_in, a_q_norm, a_w_uq, a_kv_norm, a_w_ukv, a_w_o, s_norm, s_w_kv, s_cmp_pe_k, s_cmp_w1_k, s_cmp_w2_k, s_cmp_pe_v, s_cmp_w1_v, s_cmp_w2_v, b_norm, b_w_in, b_w_o, f_norm, f_w_in, f_w_out, final_norm):
    batch, seq, d = x.shape
    t = batch * seq
    heads = d // 128
    hpg = heads // NSA_G
    d_ff = f_w_out.shape[1]
    d_ffp = _round_up(d_ff, 1024)
    n_a, n_b = a_w_in.shape[0], b_w_in.shape[0]
    tm = 1024

    cos_mla, sin_mla, cos_nsa, sin_nsa = _rope_tables(positions.reshape(t))
    tab_spec = pl.BlockSpec((tm, LANES), lambda i, j, k: (i, 0))
    klay = _nsa_k_layout()

    def ffn(h, layer):
        w_in = f_w_in[layer].reshape(d, 2, d_ff)
        w_in = jnp.pad(w_in, ((0, 0), (0, 0), (0, d_ffp - d_ff))).reshape(d, 2 * d_ffp).astype(BF16)
        w_out = jnp.pad(f_w_out[layer], ((0, d_ffp - d_ff), (0, 0))).astype(BF16)
        hn = _rmsnorm(h, f_norm[layer], BF16)
        tn = 512
        nj = d_ffp // tn
        act = _mm(hn, w_in, _epi_swiglu, jax.ShapeDtypeStruct((t, d_ffp), BF16),
                  pl.BlockSpec((tm, tn), lambda i, j, k: (i, j)),
                  tm=tm, tn=tn, tk=1024, nj=nj,
                  w_col_maps=(lambda j: j, lambda j: j + nj), name="ffn_in")
        return _mm(act, w_out, _epi_residual, jax.ShapeDtypeStruct((t, d), F32),
                   pl.BlockSpec((tm, tn), lambda i, j, k: (i, j)),
                   tm=tm, tn=tn, tk=1024, nj=d // tn,
                   extras=(h,), extra_specs=(pl.BlockSpec((tm, tn), lambda i, j, k: (i, j)),),
                   name="ffn_out")

    def out_proj(o, w_o, h, name):
        tn = 512
        return _mm(o, w_o.astype(BF16), _epi_residual, jax.ShapeDtypeStruct((t, d), F32),
                   pl.BlockSpec((tm, tn), lambda i, j, k: (i, j)),
                   tm=tm, tn=tn, tk=1024, nj=d // tn,
                   extras=(h,), extra_specs=(pl.BlockSpec((tm, tn), lambda i, j, k: (i, j)),),
                   name=name)

    h = x.reshape(t, d)

    for i in range(n_a):
        hn = _rmsnorm(h, a_norm[i], BF16)
        rope_lo = MLA_Q_LORA + MLA_KV_LORA
        w_in = jnp.concatenate([a_w_in[i], a_w_in[i][:, rope_lo:]], axis=1).astype(BF16)
        n_in = w_in.shape[1]
        tm_in = 512
        row = lambda i_, j, k: (i_, 0)
        cq, ckv, kr = _mm(
            hn, w_in, _epi_mla_in,
            (jax.ShapeDtypeStruct((t, MLA_Q_LORA), BF16),
             jax.ShapeDtypeStruct((t, MLA_KV_LORA), BF16),
             jax.ShapeDtypeStruct((t, LANES), BF16)),
            (pl.BlockSpec((tm_in, MLA_Q_LORA), row), pl.BlockSpec((tm_in, MLA_KV_LORA), row),
             pl.BlockSpec((tm_in, LANES), row)),
            tm=tm_in, tn=n_in, tk=512, nj=1,
            extras=(a_q_norm[i].reshape(1, -1), a_kv_norm[i].reshape(1, -1), cos_mla, sin_mla),
            extra_specs=(pl.BlockSpec((1, MLA_Q_LORA), lambda i_, j, k: (0, 0)),
                         pl.BlockSpec((1, MLA_KV_LORA), lambda i_, j, k: (0, 0)),
                         pl.BlockSpec((tm_in, LANES), row), pl.BlockSpec((tm_in, LANES), row)),
            name="mla_in")

        qd = MLA_NOPE + MLA_ROPE
        perm = ([hh * qd + c for hh in range(heads) for c in range(MLA_NOPE)]
                + [hh * qd + MLA_NOPE + c for hh in range(heads) for c in range(MLA_ROPE)])
        w_uq = _gather_cols(a_w_uq[i], perm).astype(BF16)
        tn = 512
        q = _mm(cq, w_uq,
                functools.partial(_epi_mla_q, heads * MLA_NOPE // tn, qd ** -0.5),
                jax.ShapeDtypeStruct((t, heads * qd), BF16),
                pl.BlockSpec((tm, tn), lambda i_, j, k: (i_, j)),
                tm=tm, tn=tn, tk=MLA_Q_LORA, nj=heads * qd // tn,
                extras=(cos_mla, sin_mla), extra_specs=(tab_spec, tab_spec), name="mla_q")
        kv = _mm(ckv, a_w_ukv[i].astype(BF16), _epi_store,
                 jax.ShapeDtypeStruct((t, heads * (MLA_NOPE + MLA_V)), BF16),
                 pl.BlockSpec((tm, 1024), lambda i_, j, k: (i_, j)),
                 tm=tm, tn=1024, tk=MLA_KV_LORA, nj=heads * (MLA_NOPE + MLA_V) // 1024,
                 name="mla_kv")
        o = _mla_attention(q, kv, kr, batch, seq, heads)
        h = out_proj(o, a_w_o[i], h, "mla_out")
        h = ffn(h, i)

    hn = _rmsnorm(h, s_norm, BF16)
    per = NSA_DK + NSA_DV
    cols = []
    for br in range(3):
        for g in range(NSA_G):
            base = br * NSA_G * per + g * per
            cols += [(-1 if c < 0 else base + c) for c in klay]
            cols += [base + NSA_DK + c for c in range(NSA_DV)]
    w_kv = _gather_cols(s_w_kv, cols).astype(BF16)
    u3 = _mm(hn, w_kv, _epi_nsa_kv, jax.ShapeDtypeStruct((3 * NSA_G, t, NSA_KV_W), BF16),
             pl.BlockSpec((1, tm, NSA_KV_W), lambda i_, j, k: (j, i_, 0)),
             tm=tm, tn=NSA_KV_W, tk=1024, nj=3 * NSA_G,
             extras=(cos_nsa, sin_nsa), extra_specs=(tab_spec, tab_spec), name="nsa_kv")

    hid_w = NSA_KV_W
    w1k = jnp.take(jnp.concatenate([s_cmp_w1_k.reshape(CMP_BLOCK, NSA_DK, NSA_DK),
                                    jnp.zeros((CMP_BLOCK, 1, NSA_DK), F32)], axis=1),
                   jnp.asarray(np.where(klay < 0, NSA_DK, klay), jnp.int32), axis=1)
    w1k = jnp.pad(w1k, ((0, 0), (0, 0), (0, hid_w - NSA_DK)))
    w1v = jnp.pad(s_cmp_w1_v.reshape(CMP_BLOCK, NSA_DV, NSA_DV),
                  ((0, 0), (0, 0), (NSA_DK, hid_w - NSA_DK - NSA_DV)))
    w1 = jnp.concatenate([w1k, w1v], axis=1).reshape(2, CMP_STRIDE * NSA_KV_W, hid_w).astype(BF16)
    w2k = jnp.pad(_gather_cols(s_cmp_w2_k, klay), ((0, 0), (0, NSA_DV)))
    w2v = jnp.pad(s_cmp_w2_v, ((0, 0), (NSA_KPAD, 0)))
    w2 = jnp.concatenate([w2k, w2v, jnp.zeros((hid_w - NSA_DK - NSA_DV, NSA_KV_W), F32)],
                         axis=0).astype(BF16)
    pe = jnp.concatenate([_gather_cols(s_cmp_pe_k, klay), s_cmp_pe_v], axis=1)
    pe = jnp.pad(pe.reshape(2, 1, CMP_STRIDE * NSA_KV_W), ((0, 0), (0, 7), (0, 0))).astype(BF16)
    u3c = u3[:NSA_G].reshape(NSA_G, t // CMP_STRIDE, CMP_STRIDE * NSA_KV_W)
    kc = _compress(u3c, w1, pe, w2, batch, seq)

    n_chunk = seq // CMP_STRIDE
    n_sel = seq // SEL_BLOCK
    c_start = np.arange(n_chunk) * CMP_STRIDE
    j_start = np.arange(LANES) * SEL_BLOCK
    ovl = ((c_start[:, None] < j_start[None, :] + SEL_BLOCK)
           & (c_start[:, None] + CMP_BLOCK > j_start[None, :])
           & (np.arange(LANES)[None, :] < n_sel)
           & (np.arange(n_chunk)[:, None] < (seq - CMP_BLOCK) // CMP_STRIDE + 1))
    ovl = jnp.asarray(ovl, BF16)
    expand = jnp.asarray(np.arange(LANES)[:, None] == (np.arange(seq)[None, :] // SEL_BLOCK), BF16)

    for i in range(n_b):
        hn = _rmsnorm(h, b_norm[i], BF16)
        qcols = []
        for hd in range(heads):
            qcols += [(-1 if c < 0 else hd * NSA_DK + c) for c in klay]
        w_q = _gather_cols(b_w_in[i], qcols).astype(BF16)
        gcols = np.full((NSA_G * LANES,), -1, np.int64)
        for g in range(NSA_G):
            gcols[g * LANES:g * LANES + 3 * hpg] = heads * NSA_DK + g * 3 * hpg + np.arange(3 * hpg)
        w_g = _gather_cols(b_w_in[i], gcols).astype(BF16)
        heads_per_tile = 4
        tn = heads_per_tile * NSA_KPAD
        q = _mm(hn, w_q, functools.partial(_epi_nsa_q, NSA_DK ** -0.5),
                jax.ShapeDtypeStruct((heads, t, NSA_KPAD), BF16),
                pl.BlockSpec((heads_per_tile, tm, NSA_KPAD), lambda i_, j, k: (j, i_, 0)),
                tm=tm, tn=tn, tk=1024, nj=heads // heads_per_tile,
                extras=(cos_nsa, sin_nsa), extra_specs=(tab_spec, tab_spec), name="nsa_q")
        gates = _mm(hn, w_g, _epi_sigmoid, jax.ShapeDtypeStruct((t, NSA_G * LANES), F32),
                    pl.BlockSpec((tm, NSA_G * LANES), lambda i_, j, k: (i_, j)),
                    tm=tm, tn=NSA_G * LANES, tk=1024, nj=1, name="nsa_gates")
        o = _nsa_attention(q, kc, u3, gates, ovl, expand, batch, seq, hpg)
        h = out_proj(o, b_w_o[i], h, "nsa_out")
        h = ffn(h, n_a + i)

    out = _rmsnorm(h, final_norm, F32)
    return out.reshape(batch, seq, d)
```

```python
import functools

import numpy as np
import jax
import jax.numpy as jnp
from jax import lax
from jax.experimental import pallas as pl
from jax.experimental.pallas import tpu as pltpu

F32 = jnp.float32
BF16 = jnp.bfloat16

LANES = 128
VMEM_LIMIT_BYTES = 56 * 1024 * 1024

NORM_EPS = 1e-6
ROPE_THETA = 10000.0

MLA_Q_LORA = 1536
MLA_KV_LORA = 512
MLA_NOPE = 128
MLA_ROPE = 64
MLA_V = 128

NSA_G = 4
NSA_DK = 192
NSA_DV = 128
NSA_HALF = NSA_DK // 2
NSA_KPAD = 2 * LANES
NSA_KV_W = NSA_KPAD + NSA_DV
CMP_BLOCK = 32
CMP_STRIDE = 16
SEL_BLOCK = 64
SEL_TOPK = 16
WINDOW = 512
FORCE_BONUS = 1e4
NEG_BIG = -1e30

VT_TILE = LANES
NT_DIMS = (((1,), (1,)), ((), ()))


def _params(sem):
    return pltpu.CompilerParams(dimension_semantics=sem, vmem_limit_bytes=VMEM_LIMIT_BYTES)


def _rmsnorm_kernel(x_ref, g_ref, o_ref):
    x = x_ref[...]
    ms = jnp.mean(x * x, axis=-1, keepdims=True)
    o_ref[...] = (x * lax.rsqrt(ms + NORM_EPS) * g_ref[...]).astype(o_ref.dtype)


def _rmsnorm(x, g, out_dtype, tm=256):
    t, d = x.shape
    return pl.pallas_call(
        _rmsnorm_kernel,
        grid=(t // tm,),
        in_specs=[pl.BlockSpec((tm, d), lambda i: (i, 0)),
                  pl.BlockSpec((1, d), lambda i: (0, 0))],
        out_specs=pl.BlockSpec((tm, d), lambda i: (i, 0)),
        out_shape=jax.ShapeDtypeStruct((t, d), out_dtype),
        compiler_params=_params(("parallel",)),
        name="rmsnorm",
    )(x, g.reshape(1, d))


def _mm_kernel(n_w, n_e, n_o, nk, epilogue, *refs):
    a_ref = refs[0]
    w_refs = refs[1:1 + n_w]
    e_refs = refs[1 + n_w:1 + n_w + n_e]
    o_refs = refs[1 + n_w + n_e:1 + n_w + n_e + n_o]
    acc_refs = refs[1 + n_w + n_e + n_o:]
    j = pl.program_id(1)
    if nk == 1:
        accs = [jnp.dot(a_ref[...], w[...], preferred_element_type=F32) for w in w_refs]
        epilogue(accs, e_refs, o_refs, j)
        return
    k = pl.program_id(2)

    @pl.when(k == 0)
    def _():
        for w, acc in zip(w_refs, acc_refs):
            acc[...] = jnp.dot(a_ref[...], w[...], preferred_element_type=F32)

    @pl.when(k > 0)
    def _():
        for w, acc in zip(w_refs, acc_refs):
            acc[...] += jnp.dot(a_ref[...], w[...], preferred_element_type=F32)

    @pl.when(k == nk - 1)
    def _():
        epilogue([acc[...] for acc in acc_refs], e_refs, o_refs, j)


def _mm(a, w, epilogue, out_shape, out_specs, *, tm, tn, tk, nj, w_col_maps=None, w_layer=None,
        extras=(), extra_specs=(), name="matmul"):
    m, kdim = a.shape
    ni, nk = m // tm, kdim // tk
    assert ni * tm == m and nk * tk == kdim
    if w_col_maps is None:
        w_col_maps = (lambda j: j,)
    n_w = len(w_col_maps)
    single = not isinstance(out_shape, (tuple, list))
    out_shapes = (out_shape,) if single else tuple(out_shape)
    out_specs_t = (out_specs,) if single else tuple(out_specs)
    in_specs = [pl.BlockSpec((tm, tk), lambda i, j, k: (i, k))]
    for cm in w_col_maps:
        if w_layer is None:
            in_specs.append(pl.BlockSpec((tk, tn), functools.partial(
                lambda i, j, k, cm: (k, cm(j)), cm=cm)))
        else:
            in_specs.append(pl.BlockSpec((None, tk, tn), functools.partial(
                lambda i, j, k, cm: (w_layer, k, cm(j)), cm=cm)))
    in_specs += list(extra_specs)
    scratch = [pltpu.VMEM((tm, tn), F32) for _ in range(n_w)] if nk > 1 else []
    kern = functools.partial(_mm_kernel, n_w, len(extras), len(out_shapes), nk, epilogue)
    res = pl.pallas_call(
        kern,
        grid=(ni, nj, nk),
        in_specs=in_specs,
        out_specs=out_specs_t,
        out_shape=out_shapes,
        scratch_shapes=scratch,
        compiler_params=_params(("parallel", "parallel", "arbitrary")),
        name=name,
    )(a, *([w] * n_w), *extras)
    return res[0] if single else res


def _rope_pairs_64(x, cos2, sin2):
    lane = lax.broadcasted_iota(jnp.int32, x.shape, 1)
    first_half = (lane % MLA_ROPE) < (MLA_ROPE // 2)
    partner = jnp.where(first_half,
                        pltpu.roll(x, LANES - MLA_ROPE // 2, 1),
                        pltpu.roll(x, MLA_ROPE // 2, 1))
    return x * cos2 + partner * sin2


def _store_vt_tiles(vt_ref, lead, v):
    for c in range(v.shape[0] // VT_TILE):
        vt_ref[lead + (c,)] = v[c * VT_TILE:(c + 1) * VT_TILE, :].T.astype(vt_ref.dtype)


def _epi_residual(accs, e_refs, o_refs, j):
    o_refs[0][...] = e_refs[0][...] + accs[0]


def _epi_swiglu(accs, e_refs, o_refs, j):
    a, b = accs
    o_refs[0][...] = (a * jax.nn.sigmoid(a) * b).astype(o_refs[0].dtype)


def _epi_sigmoid(accs, e_refs, o_refs, j):
    o_refs[0][...] = jax.nn.sigmoid(accs[0])


def _epi_mla_in(accs, e_refs, o_refs, j):
    u = accs[0]
    gq_ref, gkv_ref, cos_ref, sin_ref = e_refs
    cq = u[:, :MLA_Q_LORA]
    ms = jnp.mean(cq * cq, axis=-1, keepdims=True)
    o_refs[0][...] = (cq * lax.rsqrt(ms + NORM_EPS) * gq_ref[...]).astype(BF16)
    ckv = u[:, MLA_Q_LORA:MLA_Q_LORA + MLA_KV_LORA]
    ms = jnp.mean(ckv * ckv, axis=-1, keepdims=True)
    o_refs[1][...] = (ckv * lax.rsqrt(ms + NORM_EPS) * gkv_ref[...]).astype(BF16)
    kr = u[:, MLA_Q_LORA + MLA_KV_LORA:]
    o_refs[2][...] = _rope_pairs_64(kr, cos_ref[...], sin_ref[...]).astype(BF16)


def _epi_mla_q(n_nope_tiles, scale, accs, e_refs, o_refs, j):
    cos_ref, sin_ref = e_refs
    acc = accs[0]

    @pl.when(j < n_nope_tiles)
    def _():
        o_refs[0][...] = (acc * scale).astype(BF16)

    @pl.when(j >= n_nope_tiles)
    def _():
        for c in range(acc.shape[1] // LANES):
            x = acc[:, c * LANES:(c + 1) * LANES]
            r = _rope_pairs_64(x, cos_ref[...], sin_ref[...])
            o_refs[0][:, c * LANES:(c + 1) * LANES] = (r * scale).astype(BF16)


def _epi_mla_kv(accs, e_refs, o_refs, j):
    acc = accs[0]
    kn_ref, vt_ref = o_refs
    per = MLA_NOPE + MLA_V
    for r in range(acc.shape[1] // per):
        kn_ref[:, r * MLA_NOPE:(r + 1) * MLA_NOPE] = acc[:, r * per:r * per + MLA_NOPE].astype(BF16)
        _store_vt_tiles(vt_ref, (r,), acc[:, r * per + MLA_NOPE:(r + 1) * per])


def _rope_halves(a, b, cos, sin):
    return a * cos - b * sin, b * cos + a * sin


def _epi_nsa_q(scale, accs, e_refs, o_refs, j):
    cos_ref, sin_ref = e_refs
    acc = accs[0]
    cos, sin = cos_ref[...], sin_ref[...]
    for r in range(acc.shape[1] // NSA_KPAD):
        a = acc[:, r * NSA_KPAD:r * NSA_KPAD + LANES]
        b = acc[:, r * NSA_KPAD + LANES:(r + 1) * NSA_KPAD]
        ra, rb = _rope_halves(a, b, cos, sin)
        o_refs[0][r, :, :LANES] = (ra * scale).astype(BF16)
        o_refs[0][r, :, LANES:] = (rb * scale).astype(BF16)


def _epi_nsa_kv(accs, e_refs, o_refs, j):
    cos_ref, sin_ref = e_refs
    k_ref, v_ref, vt_ref = o_refs
    acc = accs[0]
    ra, rb = _rope_halves(acc[:, :LANES], acc[:, LANES:NSA_KPAD], cos_ref[...], sin_ref[...])
    k_ref[0, :, :LANES] = ra.astype(BF16)
    k_ref[0, :, LANES:] = rb.astype(BF16)
    v = acc[:, NSA_KPAD:]
    v_ref[0] = v.astype(BF16)
    _store_vt_tiles(vt_ref, (0,), v)


def _flash_step_t(q, k, vt, bias, m, l, acc_ref):
    s = lax.dot_general(k, q, NT_DIMS, preferred_element_type=F32)
    if bias is not None:
        s = s + bias
    m_new = jnp.maximum(m, jnp.max(s, axis=0, keepdims=True))
    alpha = jnp.exp(m - m_new)
    p = jnp.exp(s - m_new)
    l_new = alpha * l + jnp.sum(p, axis=0, keepdims=True)
    acc_ref[...] = alpha * acc_ref[...] + jnp.dot(vt, p.astype(BF16), preferred_element_type=F32)
    return m_new, l_new


def _vt_tiles(vt_ref, lead, c0, n):
    return jnp.concatenate([vt_ref[lead + (c0 + c,)] for c in range(n)], axis=1)


def _mla_attn_kernel(t_blk, seq, qn_ref, qr_ref, kn_ref, kr_ref, vt_ref, o_ref, acc_ref):
    h = pl.program_id(1)
    n_vt = t_blk // VT_TILE
    lane_half = lax.broadcasted_iota(jnp.int32, (t_blk, LANES), 1) // MLA_ROPE
    kpos = lax.broadcasted_iota(jnp.int32, (t_blk, t_blk), 0)
    qpos = lax.broadcasted_iota(jnp.int32, (t_blk, t_blk), 1)
    diag_bias = jnp.where(kpos <= qpos, 0.0, NEG_BIG).astype(F32)

    def q_block(qi, carry):
        q0 = pl.multiple_of(qi * t_blk, t_blk)
        qr = qr_ref[pl.ds(q0, t_blk), :]
        qr = jnp.where(lane_half == (h % 2), qr, jnp.zeros_like(qr))
        q = jnp.concatenate([qn_ref[pl.ds(q0, t_blk), :], qr], axis=1)
        acc_ref[...] = jnp.zeros(acc_ref.shape, F32)

        def step(j, m, l, bias):
            k0 = pl.multiple_of(j * t_blk, t_blk)
            k = jnp.concatenate([kn_ref[pl.ds(k0, t_blk), :], kr_ref[pl.ds(k0, t_blk), :]], axis=1)
            vt = _vt_tiles(vt_ref, (0,), j * n_vt, n_vt)
            return _flash_step_t(q, k, vt, bias, m, l, acc_ref)

        m0 = jnp.full((1, t_blk), NEG_BIG, F32)
        l0 = jnp.zeros((1, t_blk), F32)
        m, l = lax.fori_loop(0, qi, lambda j, ml: step(j, ml[0], ml[1], None), (m0, l0))
        m, l = step(qi, m, l, diag_bias)
        o_ref[pl.ds(q0, t_blk), :] = (acc_ref[...] / l).T.astype(o_ref.dtype)
        return carry

    lax.fori_loop(0, seq // t_blk, q_block, 0)


def _mla_attention(q, kn, kr, vt, batch, seq, heads, t_blk=512):
    t = batch * seq
    n_nope_blk = heads * MLA_NOPE // LANES
    n_vt = seq // VT_TILE
    kern = functools.partial(_mla_attn_kernel, t_blk, seq)
    return pl.pallas_call(
        kern,
        grid=(batch, heads),
        in_specs=[
            pl.BlockSpec((seq, MLA_NOPE), lambda b, h: (b, h)),
            pl.BlockSpec((seq, LANES), lambda b, h: (b, n_nope_blk + h // 2)),
            pl.BlockSpec((seq, MLA_NOPE), lambda b, h: (b, h)),
            pl.BlockSpec((seq, LANES), lambda b, h: (b, 0)),
            pl.BlockSpec((1, n_vt, MLA_V, VT_TILE), lambda b, h: (h, b, 0, 0)),
        ],
        out_specs=pl.BlockSpec((seq, MLA_V), lambda b, h: (b, h)),
        out_shape=jax.ShapeDtypeStruct((t, heads * MLA_V), BF16),
        scratch_shapes=[pltpu.VMEM((MLA_V, t_blk), F32)],
        compiler_params=_params(("parallel", "parallel")),
        name="mla_attention",
    )(q, q, kn, kr, vt)


def _compress_mlp(a, w1_ref, pe_ref, w2_ref, transposed_out):
    y_lo = jnp.dot(a, w1_ref[0], preferred_element_type=F32)
    y_hi = jnp.dot(a, w1_ref[1], preferred_element_type=F32)
    pe_c = (jnp.dot(pe_ref[0], w1_ref[0], preferred_element_type=F32)
            + jnp.dot(pe_ref[1], w1_ref[1], preferred_element_type=F32))
    n = y_hi.shape[0]
    hid = y_lo + pltpu.roll(y_hi, n - 1, 0) + pe_c[0:1, :]
    hid = jax.nn.gelu(hid, approximate=True)
    if transposed_out:
        return lax.dot_general(w2_ref[...], hid.astype(BF16), NT_DIMS, preferred_element_type=F32)
    return jnp.dot(hid.astype(BF16), w2_ref[...], preferred_element_type=F32)


def _compress_kernel(ak_ref, av_ref, w1k_ref, pek_ref, w2k_ref, w1v_ref, pev_ref, w2vt_ref,
                     kc_ref, vct_ref):
    kc_ref[0] = _compress_mlp(ak_ref[0], w1k_ref, pek_ref, w2k_ref, False).astype(BF16)
    vct_ref[0] = _compress_mlp(av_ref[0], w1v_ref, pev_ref, w2vt_ref, True).astype(BF16)


def _compress(ak, av, wk, wv, batch, seq):
    n_chunk = seq // CMP_STRIDE
    const3 = lambda b, g: (0, 0, 0)
    const2 = lambda b, g: (0, 0)

    def wspecs(w):
        w1, pe, w2 = w
        return [pl.BlockSpec(w1.shape, const3), pl.BlockSpec(pe.shape, const3),
                pl.BlockSpec(w2.shape, const2)]

    return pl.pallas_call(
        _compress_kernel,
        grid=(batch, NSA_G),
        in_specs=[pl.BlockSpec((1, n_chunk, ak.shape[2]), lambda b, g: (g, b, 0)),
                  pl.BlockSpec((1, n_chunk, av.shape[2]), lambda b, g: (g, b, 0))]
        + wspecs(wk) + wspecs(wv),
        out_specs=(pl.BlockSpec((1, n_chunk, NSA_KPAD), lambda b, g: (b * NSA_G + g, 0, 0)),
                   pl.BlockSpec((1, NSA_DV, n_chunk), lambda b, g: (b * NSA_G + g, 0, 0))),
        out_shape=(jax.ShapeDtypeStruct((batch * NSA_G, n_chunk, NSA_KPAD), BF16),
                   jax.ShapeDtypeStruct((batch * NSA_G, NSA_DV, n_chunk), BF16)),
        compiler_params=_params(("parallel", "parallel")),
        name="nsa_compress",
    )(ak, av, *wk, *wv)


def _nsa_attn_kernel(hpg, tq, tk, seq, q_ref, kc_ref, vct_ref, ks_ref, vst_ref, kw_ref, vwt_ref,
                     gate_ref, ovl_ref, exp_ref, o_ref, bias_ref, oc_ref, os_ref, acc_ref):
    qi = pl.program_id(2)
    t0 = qi * tq
    n_rows = hpg * tq
    n_cmp = kc_ref.shape[1]
    n_sel = seq // SEL_BLOCK
    n_vt = tk // VT_TILE
    q = q_ref[...].reshape(n_rows, NSA_KPAD)

    def tile_heads(x):
        return jnp.concatenate([x] * hpg, axis=1)

    s = lax.dot_general(kc_ref[0], q, NT_DIMS, preferred_element_type=F32)
    cidx = lax.broadcasted_iota(jnp.int32, (n_cmp, tq), 0)
    tpos_c = t0 + lax.broadcasted_iota(jnp.int32, (n_cmp, tq), 1)
    mask_c = tile_heads((cidx * CMP_STRIDE + CMP_BLOCK - 1) <= tpos_c)
    s = jnp.where(mask_c, s, NEG_BIG)
    mx = jnp.max(s, axis=0, keepdims=True)
    e = jnp.where(mask_c, jnp.exp(s - mx), 0.0)
    p = e / jnp.maximum(jnp.sum(e, axis=0, keepdims=True), 1e-30)
    oc_ref[...] = jnp.dot(vct_ref[0], p.astype(BF16), preferred_element_type=F32)

    psum = p[:, 0:tq]
    for h in range(1, hpg):
        psum = psum + p[:, h * tq:(h + 1) * tq]
    p1 = psum.astype(BF16)
    p2 = (psum - p1.astype(F32)).astype(BF16)
    p3 = (psum - p1.astype(F32) - p2.astype(F32)).astype(BF16)
    ovl = ovl_ref[...]
    imp = (jnp.dot(ovl, p1, preferred_element_type=F32)
           + jnp.dot(ovl, p2, preferred_element_type=F32)
           + jnp.dot(ovl, p3, preferred_element_type=F32))
    jrow = lax.broadcasted_iota(jnp.int32, (n_sel, tq), 0)
    tpos = t0 + lax.broadcasted_iota(jnp.int32, (n_sel, tq), 1)
    valid = jrow * SEL_BLOCK <= tpos
    cur = jnp.right_shift(tpos, SEL_BLOCK.bit_length() - 1)
    forced = valid & ((jrow == 0) | (jrow == cur) | (jrow == cur - 1))
    score = jnp.where(valid, imp + jnp.where(forced, FORCE_BONUS, 0.0), -jnp.inf)
    rank = jnp.zeros((n_sel, tq), jnp.int32)
    for i in range(n_sel):
        row = score[i:i + 1, :]
        beats = (row > score) | ((row == score) & (jrow > i))
        rank = rank + beats.astype(jnp.int32)
    sel = valid & (rank < SEL_TOPK)
    sel_keys = jnp.dot(exp_ref[...], jnp.where(sel, 1.0, 0.0).astype(BF16),
                       preferred_element_type=F32)
    j_hi = (t0 + tq - 1) // tk + 1
    for jj in range(seq // tk):
        @pl.when(jj < j_hi)
        def _():
            kpos = jj * tk + lax.broadcasted_iota(jnp.int32, (tk, tq), 0)
            tq_pos = t0 + lax.broadcasted_iota(jnp.int32, (tk, tq), 1)
            ok = (sel_keys[jj * tk:(jj + 1) * tk, :] > 0.5) & (kpos <= tq_pos)
            bias_ref[jj] = jnp.where(ok, 0.0, NEG_BIG).astype(F32)

    m0 = jnp.full((1, n_rows), NEG_BIG, F32)
    l0 = jnp.zeros((1, n_rows), F32)

    acc_ref[...] = jnp.zeros(acc_ref.shape, F32)

    def sel_step(j, ml):
        k0 = pl.multiple_of(j * tk, tk)
        k = ks_ref[0, pl.ds(k0, tk), :]
        vt = _vt_tiles(vst_ref, (0,), j * n_vt, n_vt)
        return _flash_step_t(q, k, vt, tile_heads(bias_ref[j]), ml[0], ml[1], acc_ref)

    _, l = lax.fori_loop(0, j_hi, sel_step, (m0, l0))
    os_ref[...] = acc_ref[...] / l

    acc_ref[...] = jnp.zeros(acc_ref.shape, F32)
    span = WINDOW + tq
    span = -(-span // tk) * tk
    k_start = jnp.maximum(t0 + tq - span, 0)
    n_win = (t0 + tq - k_start + tk - 1) // tk

    def win_step(j, ml):
        k0 = pl.multiple_of(k_start + j * tk, VT_TILE)
        k = kw_ref[0, pl.ds(k0, tk), :]
        vt = _vt_tiles(vwt_ref, (0,), k0 // VT_TILE, n_vt)
        kpos = k0 + lax.broadcasted_iota(jnp.int32, (tk, tq), 0)
        tq_pos = t0 + lax.broadcasted_iota(jnp.int32, (tk, tq), 1)
        ok = (kpos <= tq_pos) & (tq_pos - kpos < WINDOW)
        bias = jnp.where(ok, 0.0, NEG_BIG).astype(F32)
        return _flash_step_t(q, k, vt, tile_heads(bias), ml[0], ml[1], acc_ref)

    _, l = lax.fori_loop(0, n_win, win_step, (m0, l0))
    o_w = acc_ref[...] / l

    gates_t = gate_ref[...].T
    for h in range(hpg):
        cols = slice(h * tq, (h + 1) * tq)
        o = (gates_t[3 * h:3 * h + 1, :] * oc_ref[:, cols]
             + gates_t[3 * h + 1:3 * h + 2, :] * os_ref[:, cols]
             + gates_t[3 * h + 2:3 * h + 3, :] * o_w[:, cols])
        o_ref[:, h * NSA_DV:(h + 1) * NSA_DV] = o.T.astype(o_ref.dtype)


def _nsa_attention(q, kc, vct, k3, vt3, gates, ovl, expand, batch, seq, hpg, tq=128, tk=256):
    t = batch * seq
    nq = seq // tq
    n_chunk = seq // CMP_STRIDE
    n_rows = hpg * tq
    n_vt = seq // VT_TILE
    assert (WINDOW + tq) % VT_TILE == 0 and seq >= -(-(WINDOW + tq) // tk) * tk
    kern = functools.partial(_nsa_attn_kernel, hpg, tq, tk, seq)

    def k_spec(branch):
        return pl.BlockSpec((1, seq, NSA_KPAD), lambda b, g, qi: (branch * NSA_G + g, b, 0))

    def vt_spec(branch):
        return pl.BlockSpec((1, n_vt, NSA_DV, VT_TILE), lambda b, g, qi: (branch * NSA_G + g, b, 0, 0))

    return pl.pallas_call(
        kern,
        grid=(batch, NSA_G, nq),
        in_specs=[
            pl.BlockSpec((hpg, tq, NSA_KPAD), lambda b, g, qi: (g, b * nq + qi, 0)),
            pl.BlockSpec((1, n_chunk, NSA_KPAD), lambda b, g, qi: (b * NSA_G + g, 0, 0)),
            pl.BlockSpec((1, NSA_DV, n_chunk), lambda b, g, qi: (b * NSA_G + g, 0, 0)),
            k_spec(1), vt_spec(1), k_spec(2), vt_spec(2),
            pl.BlockSpec((tq, LANES), lambda b, g, qi: (b * nq + qi, g)),
            pl.BlockSpec(ovl.shape, lambda b, g, qi: (0, 0)),
            pl.BlockSpec(expand.shape, lambda b, g, qi: (0, 0)),
        ],
        out_specs=pl.BlockSpec((tq, hpg * NSA_DV), lambda b, g, qi: (b * nq + qi, g)),
        out_shape=jax.ShapeDtypeStruct((t, NSA_G * hpg * NSA_DV), BF16),
        scratch_shapes=[
            pltpu.VMEM((seq // tk, tk, tq), F32),
            pltpu.VMEM((NSA_DV, n_rows), F32),
            pltpu.VMEM((NSA_DV, n_rows), F32),
            pltpu.VMEM((NSA_DV, n_rows), F32),
        ],
        compiler_params=_params(("parallel", "parallel", "arbitrary")),
        name="nsa_attention",
    )(q, kc, vct, k3, vt3, k3, vt3, gates, ovl, expand)


def _pad_rope_halves(w):
    w = w.reshape(w.shape[:-1] + (2, NSA_HALF))
    w = jnp.pad(w, [(0, 0)] * (w.ndim - 1) + [(0, LANES - NSA_HALF)])
    return w.reshape(w.shape[:-2] + (NSA_KPAD,))


def _rope_tables(pos_flat):
    pos = pos_flat.astype(F32)[:, None]
    inv64 = ROPE_THETA ** (-jnp.arange(0, MLA_ROPE, 2, dtype=F32) / MLA_ROPE)
    ang = pos * inv64
    c, s = jnp.cos(ang), jnp.sin(ang)
    cos_mla = jnp.concatenate([c, c, c, c], axis=1)
    sin_mla = jnp.concatenate([-s, s, -s, s], axis=1)
    inv192 = ROPE_THETA ** (-jnp.arange(0, NSA_DK, 2, dtype=F32) / NSA_DK)
    ang = pos * inv192
    pad = jnp.zeros((pos.shape[0], LANES - NSA_HALF), F32)
    cos_nsa = jnp.concatenate([jnp.cos(ang), pad], axis=1)
    sin_nsa = jnp.concatenate([jnp.sin(ang), pad], axis=1)
    return cos_mla, sin_mla, cos_nsa, sin_nsa


def _compress_weights(w1, pe, w2, *, pad_rope, transpose_w2):
    d_in, d_hid = w1.shape[0] // CMP_BLOCK, w1.shape[1]
    hid_p = -(-d_hid // LANES) * LANES
    w1 = w1.reshape(CMP_BLOCK, d_in, d_hid)
    if pad_rope:
        w1 = jnp.swapaxes(_pad_rope_halves(jnp.swapaxes(w1, 1, 2)), 1, 2)
        pe = _pad_rope_halves(pe)
        w2 = _pad_rope_halves(w2)
    width = w1.shape[1]
    w1 = jnp.pad(w1, ((0, 0), (0, 0), (0, hid_p - d_hid)))
    w1 = w1.reshape(2, CMP_STRIDE * width, hid_p).astype(BF16)
    pe = jnp.pad(pe.reshape(2, 1, CMP_STRIDE * width), ((0, 0), (0, 7), (0, 0))).astype(BF16)
    w2 = jnp.pad(w2, ((0, hid_p - d_hid), (0, 0))).astype(BF16)
    return w1, pe, (w2.T if transpose_w2 else w2)


def kernel(x, positions, a_norm, a_w_in, a_q_norm, a_w_uq, a_kv_norm, a_w_ukv, a_w_o, s_norm, s_w_kv, s_cmp_pe_k, s_cmp_w1_k, s_cmp_w2_k, s_cmp_pe_v, s_cmp_w1_v, s_cmp_w2_v, b_norm, b_w_in, b_w_o, f_norm, f_w_in, f_w_out, final_norm):
    batch, seq, d = x.shape
    t = batch * seq
    heads = d // 128
    hpg = heads // NSA_G
    d_ff = f_w_out.shape[1]
    n_a, n_b = a_w_in.shape[0], b_w_in.shape[0]
    tm = 1024

    cos_mla, sin_mla, cos_nsa, sin_nsa = _rope_tables(positions.reshape(t))
    tab_spec = pl.BlockSpec((tm, LANES), lambda i, j, k: (i, 0))
    f_w_in_b = f_w_in.astype(BF16)
    f_w_out_b = f_w_out.astype(BF16)

    def ffn(h, layer):
        hn = _rmsnorm(h, f_norm[layer], BF16)
        tn = 256
        nj = d_ff // tn
        act = _mm(hn, f_w_in_b, _epi_swiglu, jax.ShapeDtypeStruct((t, d_ff), BF16),
                  pl.BlockSpec((tm, tn), lambda i, j, k: (i, j)),
                  tm=tm, tn=tn, tk=d, nj=nj, w_layer=layer,
                  w_col_maps=(lambda j: j, lambda j: j + nj), name="ffn_in")
        tm_out = 512
        return _mm(act, f_w_out_b, _epi_residual, jax.ShapeDtypeStruct((t, d), F32),
                   pl.BlockSpec((tm_out, tn), lambda i, j, k: (i, j)),
                   tm=tm_out, tn=tn, tk=d_ff, nj=d // tn, w_layer=layer,
                   extras=(h,), extra_specs=(pl.BlockSpec((tm_out, tn), lambda i, j, k: (i, j)),),
                   name="ffn_out")

    def out_proj(o, w_o, h, name):
        tn = 512
        return _mm(o, w_o.astype(BF16), _epi_residual, jax.ShapeDtypeStruct((t, d), F32),
                   pl.BlockSpec((tm, tn), lambda i, j, k: (i, j)),
                   tm=tm, tn=tn, tk=o.shape[1], nj=d // tn,
                   extras=(h,), extra_specs=(pl.BlockSpec((tm, tn), lambda i, j, k: (i, j)),),
                   name=name)

    h = x.reshape(t, d)

    for i in range(n_a):
        hn = _rmsnorm(h, a_norm[i], BF16)
        rope_lo = MLA_Q_LORA + MLA_KV_LORA
        w_in = jnp.concatenate([a_w_in[i], a_w_in[i][:, rope_lo:]], axis=1).astype(BF16)
        n_in = w_in.shape[1]
        row = lambda i_, j, k: (i_, 0)
        cq, ckv, kr = _mm(
            hn, w_in, _epi_mla_in,
            (jax.ShapeDtypeStruct((t, MLA_Q_LORA), BF16),
             jax.ShapeDtypeStruct((t, MLA_KV_LORA), BF16),
             jax.ShapeDtypeStruct((t, LANES), BF16)),
            (pl.BlockSpec((tm, MLA_Q_LORA), row), pl.BlockSpec((tm, MLA_KV_LORA), row),
             pl.BlockSpec((tm, LANES), row)),
            tm=tm, tn=n_in, tk=1024, nj=1,
            extras=(a_q_norm[i].reshape(1, -1), a_kv_norm[i].reshape(1, -1), cos_mla, sin_mla),
            extra_specs=(pl.BlockSpec((1, MLA_Q_LORA), lambda i_, j, k: (0, 0)),
                         pl.BlockSpec((1, MLA_KV_LORA), lambda i_, j, k: (0, 0)),
                         pl.BlockSpec((tm, LANES), row), pl.BlockSpec((tm, LANES), row)),
            name="mla_in")

        qd = MLA_NOPE + MLA_ROPE
        w_uq = a_w_uq[i].reshape(MLA_Q_LORA, heads, qd)
        w_uq = jnp.concatenate([w_uq[:, :, :MLA_NOPE].reshape(MLA_Q_LORA, heads * MLA_NOPE),
                                w_uq[:, :, MLA_NOPE:].reshape(MLA_Q_LORA, heads * MLA_ROPE)],
                               axis=1).astype(BF16)
        tn = 512
        q = _mm(cq, w_uq,
                functools.partial(_epi_mla_q, heads * MLA_NOPE // tn, qd ** -0.5),
                jax.ShapeDtypeStruct((t, heads * qd), BF16),
                pl.BlockSpec((tm, tn), lambda i_, j, k: (i_, j)),
                tm=tm, tn=tn, tk=MLA_Q_LORA, nj=heads * qd // tn,
                extras=(cos_mla, sin_mla), extra_specs=(tab_spec, tab_spec), name="mla_q")
        heads_per_tile = 4
        kn, vt = _mm(ckv, a_w_ukv[i].astype(BF16), _epi_mla_kv,
                     (jax.ShapeDtypeStruct((t, heads * MLA_NOPE), BF16),
                      jax.ShapeDtypeStruct((heads, t // VT_TILE, MLA_V, VT_TILE), BF16)),
                     (pl.BlockSpec((tm, heads_per_tile * MLA_NOPE), lambda i_, j, k: (i_, j)),
                      pl.BlockSpec((heads_per_tile, tm // VT_TILE, MLA_V, VT_TILE),
                                   lambda i_, j, k: (j, i_, 0, 0))),
                     tm=tm, tn=heads_per_tile * (MLA_NOPE + MLA_V), tk=MLA_KV_LORA,
                     nj=heads // heads_per_tile, name="mla_kv")
        o = _mla_attention(q, kn, kr, vt, batch, seq, heads)
        h = out_proj(o, a_w_o[i], h, "mla_out")
        h = ffn(h, i)

    hn = _rmsnorm(h, s_norm, BF16)
    w_kv = s_w_kv.reshape(d, 3 * NSA_G, NSA_DK + NSA_DV)
    w_kv = jnp.concatenate([_pad_rope_halves(w_kv[:, :, :NSA_DK]), w_kv[:, :, NSA_DK:]], axis=2)
    w_kv = w_kv.reshape(d, 3 * NSA_G * NSA_KV_W).astype(BF16)
    k3, v3, vt3 = _mm(
        hn, w_kv, _epi_nsa_kv,
        (jax.ShapeDtypeStruct((3 * NSA_G, t, NSA_KPAD), BF16),
         jax.ShapeDtypeStruct((3 * NSA_G, t, NSA_DV), BF16),
         jax.ShapeDtypeStruct((3 * NSA_G, t // VT_TILE, NSA_DV, VT_TILE), BF16)),
        (pl.BlockSpec((1, tm, NSA_KPAD), lambda i_, j, k: (j, i_, 0)),
         pl.BlockSpec((1, tm, NSA_DV), lambda i_, j, k: (j, i_, 0)),
         pl.BlockSpec((1, tm // VT_TILE, NSA_DV, VT_TILE), lambda i_, j, k: (j, i_, 0, 0))),
        tm=tm, tn=NSA_KV_W, tk=d, nj=3 * NSA_G,
        extras=(cos_nsa, sin_nsa), extra_specs=(tab_spec, tab_spec), name="nsa_kv")

    ak = k3[:NSA_G].reshape(NSA_G, t // CMP_STRIDE, CMP_STRIDE * NSA_KPAD)
    av = v3[:NSA_G].reshape(NSA_G, t // CMP_STRIDE, CMP_STRIDE * NSA_DV)
    kc, vct = _compress(ak, av,
                        _compress_weights(s_cmp_w1_k, s_cmp_pe_k, s_cmp_w2_k,
                                          pad_rope=True, transpose_w2=False),
                        _compress_weights(s_cmp_w1_v, s_cmp_pe_v, s_cmp_w2_v,
                                          pad_rope=False, transpose_w2=True),
                        batch, seq)

    n_chunk = seq // CMP_STRIDE
    n_sel = seq // SEL_BLOCK
    c_start = np.arange(n_chunk) * CMP_STRIDE
    j_start = np.arange(n_sel) * SEL_BLOCK
    ovl = ((c_start[None, :] < j_start[:, None] + SEL_BLOCK)
           & (c_start[None, :] + CMP_BLOCK > j_start[:, None])
           & (np.arange(n_chunk)[None, :] < (seq - CMP_BLOCK) // CMP_STRIDE + 1))
    ovl = jnp.asarray(ovl, BF16)
    expand = jnp.asarray((np.arange(seq)[:, None] // SEL_BLOCK) == np.arange(n_sel)[None, :], BF16)

    for i in range(n_b):
        hn = _rmsnorm(h, b_norm[i], BF16)
        n_q = heads * NSA_DK
        w_q = _pad_rope_halves(b_w_in[i][:, :n_q].reshape(d, heads, NSA_DK))
        w_q = w_q.reshape(d, heads * NSA_KPAD).astype(BF16)
        w_g = b_w_in[i][:, n_q:].reshape(d, NSA_G, 3 * hpg)
        w_g = jnp.pad(w_g, ((0, 0), (0, 0), (0, LANES - 3 * hpg))).reshape(d, NSA_G * LANES)
        w_g = w_g.astype(BF16)
        heads_per_tile = 4
        tn = heads_per_tile * NSA_KPAD
        q = _mm(hn, w_q, functools.partial(_epi_nsa_q, NSA_DK ** -0.5),
                jax.ShapeDtypeStruct((heads, t, NSA_KPAD), BF16),
                pl.BlockSpec((heads_per_tile, tm, NSA_KPAD), lambda i_, j, k: (j, i_, 0)),
                tm=tm, tn=tn, tk=d, nj=heads // heads_per_tile,
                extras=(cos_nsa, sin_nsa), extra_specs=(tab_spec, tab_spec), name="nsa_q")
        gates = _mm(hn, w_g, _epi_sigmoid, jax.ShapeDtypeStruct((t, NSA_G * LANES), F32),
                    pl.BlockSpec((tm, NSA_G * LANES), lambda i_, j, k: (i_, j)),
                    tm=tm, tn=NSA_G * LANES, tk=d, nj=1, name="nsa_gates")
        o = _nsa_attention(q, kc, vct, k3, vt3, gates, ovl, expand, batch, seq, hpg)
        h = out_proj(o, b_w_o[i], h, "nsa_out")
        h = ffn(h, n_a + i)

    out = _rmsnorm(h, final_norm, F32)
    return out.reshape(batch, seq, d)
```

```python
import functools

import numpy as np
import jax
import jax.numpy as jnp
from jax import lax
from jax.experimental import pallas as pl
from jax.experimental.pallas import tpu as pltpu

F32 = jnp.float32
BF16 = jnp.bfloat16

LANES = 128
VMEM_LIMIT_BYTES = 56 * 1024 * 1024

NORM_EPS = 1e-6
ROPE_THETA = 10000.0

MLA_Q_LORA = 1536
MLA_KV_LORA = 512
MLA_NOPE = 128
MLA_ROPE = 64
MLA_V = 128

NSA_G = 4
NSA_DK = 192
NSA_DV = 128
NSA_HALF = NSA_DK // 2
NSA_KPAD = 2 * LANES
NSA_KV_W = NSA_KPAD + NSA_DV
CMP_BLOCK = 32
CMP_STRIDE = 16
SEL_BLOCK = 64
SEL_TOPK = 16
WINDOW = 512
FORCE_BONUS = 1e4
NEG_BIG = -1e30
LOG2_E = 1.4426950408889634

VT_TILE = LANES
NT_DIMS = (((1,), (1,)), ((), ()))


def _params(sem):
    return pltpu.CompilerParams(dimension_semantics=sem, vmem_limit_bytes=VMEM_LIMIT_BYTES)


def _rmsnorm_kernel(x_ref, g_ref, o_ref):
    x = x_ref[...]
    ms = jnp.mean(x * x, axis=-1, keepdims=True)
    o_ref[...] = (x * lax.rsqrt(ms + NORM_EPS) * g_ref[...]).astype(o_ref.dtype)


def _rmsnorm(x, g, out_dtype, tm=256):
    t, d = x.shape
    return pl.pallas_call(
        _rmsnorm_kernel,
        grid=(t // tm,),
        in_specs=[pl.BlockSpec((tm, d), lambda i: (i, 0)),
                  pl.BlockSpec((1, d), lambda i: (0, 0))],
        out_specs=pl.BlockSpec((tm, d), lambda i: (i, 0)),
        out_shape=jax.ShapeDtypeStruct((t, d), out_dtype),
        compiler_params=_params(("parallel",)),
        name="rmsnorm",
    )(x, g.reshape(1, d))


def _prenorm_kernel(x_ref, g_ref, hb_ref, ssq_ref):
    x = x_ref[...]
    hb_ref[...] = (x * g_ref[...]).astype(hb_ref.dtype)
    ssq_ref[...] = jnp.broadcast_to(jnp.sum(x * x, axis=-1, keepdims=True), ssq_ref.shape)


def _prenorm(x, g, tm=256):
    t, d = x.shape
    return pl.pallas_call(
        _prenorm_kernel,
        grid=(t // tm,),
        in_specs=[pl.BlockSpec((tm, d), lambda i: (i, 0)),
                  pl.BlockSpec((1, d), lambda i: (0, 0))],
        out_specs=(pl.BlockSpec((tm, d), lambda i: (i, 0)),
                   pl.BlockSpec((tm, LANES), lambda i: (i, 0))),
        out_shape=(jax.ShapeDtypeStruct((t, d), BF16), jax.ShapeDtypeStruct((t, LANES), F32)),
        compiler_params=_params(("parallel",)),
        name="prenorm",
    )(x, g.reshape(1, d))


def _mm_kernel(n_w, n_e, n_o, nk, epilogue, *refs):
    a_ref = refs[0]
    w_refs = refs[1:1 + n_w]
    e_refs = refs[1 + n_w:1 + n_w + n_e]
    o_refs = refs[1 + n_w + n_e:1 + n_w + n_e + n_o]
    acc_refs = refs[1 + n_w + n_e + n_o:]
    j = pl.program_id(1)

    def product(w):
        return jnp.dot(a_ref[...], w[...].astype(a_ref.dtype), preferred_element_type=F32)

    if nk == 1:
        epilogue([product(w) for w in w_refs], e_refs, o_refs, j)
        return
    k = pl.program_id(2)

    @pl.when(k == 0)
    def _():
        for w, acc in zip(w_refs, acc_refs):
            acc[...] = product(w)

    @pl.when(k > 0)
    def _():
        for w, acc in zip(w_refs, acc_refs):
            acc[...] += product(w)

    @pl.when(k == nk - 1)
    def _():
        epilogue([acc[...] for acc in acc_refs], e_refs, o_refs, j)


def _mm(a, w, epilogue, out_shape, out_specs, *, tm, tn, tk, nj, w_col_maps=None, w_layer=None,
        extras=(), extra_specs=(), a_buffers=None, name="matmul"):
    m, kdim = a.shape
    ni, nk = m // tm, kdim // tk
    assert ni * tm == m and nk * tk == kdim
    if w_col_maps is None:
        w_col_maps = (lambda j: j,)
    n_w = len(w_col_maps)
    single = not isinstance(out_shape, (tuple, list))
    out_shapes = (out_shape,) if single else tuple(out_shape)
    out_specs_t = (out_specs,) if single else tuple(out_specs)
    a_kwargs = {} if a_buffers is None else {"pipeline_mode": pl.Buffered(a_buffers)}
    in_specs = [pl.BlockSpec((tm, tk), lambda i, j, k: (i, k), **a_kwargs)]
    for cm in w_col_maps:
        if w_layer is None:
            in_specs.append(pl.BlockSpec((tk, tn), functools.partial(
                lambda i, j, k, cm: (k, cm(j)), cm=cm)))
        else:
            in_specs.append(pl.BlockSpec((None, tk, tn), functools.partial(
                lambda i, j, k, cm: (w_layer, k, cm(j)), cm=cm)))
    in_specs += list(extra_specs)
    scratch = [pltpu.VMEM((tm, tn), F32) for _ in range(n_w)] if nk > 1 else []
    kern = functools.partial(_mm_kernel, n_w, len(extras), len(out_shapes), nk, epilogue)
    res = pl.pallas_call(
        kern,
        grid=(ni, nj, nk),
        in_specs=in_specs,
        out_specs=out_specs_t,
        out_shape=out_shapes,
        scratch_shapes=scratch,
        compiler_params=_params(("parallel", "arbitrary", "arbitrary")),
        name=name,
    )(a, *([w] * n_w), *extras)
    return res[0] if single else res


def _rope_pairs_64(x, cos2, sin2):
    lane = lax.broadcasted_iota(jnp.int32, x.shape, 1)
    first_half = (lane % MLA_ROPE) < (MLA_ROPE // 2)
    partner = jnp.where(first_half,
                        pltpu.roll(x, LANES - MLA_ROPE // 2, 1),
                        pltpu.roll(x, MLA_ROPE // 2, 1))
    return x * cos2 + partner * sin2


def _store_vt_tiles(vt_ref, lead, v):
    for c in range(v.shape[0] // VT_TILE):
        vt_ref[lead + (c,)] = v[c * VT_TILE:(c + 1) * VT_TILE, :].T.astype(vt_ref.dtype)


def _row_scale(acc, ssq_ref, d_model):
    rstd = lax.rsqrt(ssq_ref[...] * (1.0 / d_model) + NORM_EPS)
    return acc * jnp.concatenate([rstd] * (acc.shape[1] // LANES), axis=1)


def _epi_residual(accs, e_refs, o_refs, j):
    o_refs[0][...] = e_refs[0][...] + accs[0]


def _epi_residual_prenorm(accs, e_refs, o_refs, j):
    h = e_refs[0][...] + accs[0]
    o_refs[0][...] = h
    ssq_ref = o_refs[1]
    part = jnp.broadcast_to(jnp.sum(h * h, axis=-1, keepdims=True), ssq_ref.shape)

    @pl.when(j == 0)
    def _():
        ssq_ref[...] = part

    @pl.when(j > 0)
    def _():
        ssq_ref[...] += part

    for g_ref, hb_ref in zip(e_refs[1:], o_refs[2:]):
        hb_ref[...] = (h * g_ref[...]).astype(hb_ref.dtype)


def _epi_swiglu(d_model, accs, e_refs, o_refs, j):
    a = _row_scale(accs[0], e_refs[0], d_model)
    b = _row_scale(accs[1], e_refs[0], d_model)
    o_refs[0][...] = (a * jax.nn.sigmoid(a) * b).astype(o_refs[0].dtype)


def _epi_sigmoid(d_model, accs, e_refs, o_refs, j):
    o_refs[0][...] = jax.nn.sigmoid(_row_scale(accs[0], e_refs[0], d_model))


def _epi_mla_in(d_model, accs, e_refs, o_refs, j):
    ssq_ref, gq_ref, gkv_ref, cos_ref, sin_ref = e_refs
    u = _row_scale(accs[0], ssq_ref, d_model)
    cq = u[:, :MLA_Q_LORA]
    ms = jnp.mean(cq * cq, axis=-1, keepdims=True)
    o_refs[0][...] = (cq * lax.rsqrt(ms + NORM_EPS) * gq_ref[...]).astype(BF16)
    ckv = u[:, MLA_Q_LORA:MLA_Q_LORA + MLA_KV_LORA]
    ms = jnp.mean(ckv * ckv, axis=-1, keepdims=True)
    o_refs[1][...] = (ckv * lax.rsqrt(ms + NORM_EPS) * gkv_ref[...]).astype(BF16)
    kr = u[:, MLA_Q_LORA + MLA_KV_LORA:]
    o_refs[2][...] = _rope_pairs_64(kr, cos_ref[...], sin_ref[...]).astype(BF16)


def _epi_mla_q(n_nope_tiles, scale, accs, e_refs, o_refs, j):
    cos_ref, sin_ref = e_refs
    acc = accs[0]

    @pl.when(j < n_nope_tiles)
    def _():
        o_refs[0][...] = (acc * scale).astype(BF16)

    @pl.when(j >= n_nope_tiles)
    def _():
        for c in range(acc.shape[1] // LANES):
            x = acc[:, c * LANES:(c + 1) * LANES]
            r = _rope_pairs_64(x, cos_ref[...], sin_ref[...])
            o_refs[0][:, c * LANES:(c + 1) * LANES] = (r * scale).astype(BF16)


def _epi_mla_kv(accs, e_refs, o_refs, j):
    acc = accs[0]
    kn_ref, vt_ref = o_refs
    per = MLA_NOPE + MLA_V
    for r in range(acc.shape[1] // per):
        kn_ref[:, r * MLA_NOPE:(r + 1) * MLA_NOPE] = acc[:, r * per:r * per + MLA_NOPE].astype(BF16)
        _store_vt_tiles(vt_ref, (r,), acc[:, r * per + MLA_NOPE:(r + 1) * per])


def _rope_halves(a, b, cos, sin):
    return a * cos - b * sin, b * cos + a * sin


def _epi_nsa_q(d_model, scale, accs, e_refs, o_refs, j):
    ssq_ref, cos_ref, sin_ref = e_refs
    acc = _row_scale(accs[0], ssq_ref, d_model)
    cos, sin = cos_ref[...], sin_ref[...]
    for r in range(acc.shape[1] // NSA_KPAD):
        a = acc[:, r * NSA_KPAD:r * NSA_KPAD + LANES]
        b = acc[:, r * NSA_KPAD + LANES:(r + 1) * NSA_KPAD]
        ra, rb = _rope_halves(a, b, cos, sin)
        o_refs[0][r, :, :LANES] = (ra * scale).astype(BF16)
        o_refs[0][r, :, LANES:] = (rb * scale).astype(BF16)


def _epi_nsa_kv(d_model, accs, e_refs, o_refs, j):
    ssq_ref, cos_ref, sin_ref = e_refs
    k_ref, v_ref, vt_ref = o_refs
    acc = _row_scale(accs[0], ssq_ref, d_model)
    cos, sin = cos_ref[...], sin_ref[...]
    for r in range(acc.shape[1] // NSA_KV_W):
        base = r * NSA_KV_W
        ra, rb = _rope_halves(acc[:, base:base + LANES], acc[:, base + LANES:base + NSA_KPAD],
                              cos, sin)
        k_ref[r, :, :LANES] = ra.astype(BF16)
        k_ref[r, :, LANES:] = rb.astype(BF16)
        v = acc[:, base + NSA_KPAD:base + NSA_KV_W]
        v_ref[r] = v.astype(BF16)
        _store_vt_tiles(vt_ref, (r,), v)


def _bias_chunk(bias, start, width):
    period = bias.shape[1]
    if period >= width:
        off = start % period
        return bias[:, off:off + width]
    return jnp.concatenate([bias] * (width // period), axis=1)


def _flash_step_t(q, k, vt, bias, m, l, acc_ref, n_split=1):
    width = q.shape[0] // n_split
    m_out, l_out = [], []
    for c in range(n_split):
        sl = slice(c * width, (c + 1) * width)
        s = lax.dot_general(k, q[sl], NT_DIMS, preferred_element_type=F32)
        if bias is not None:
            s = s + _bias_chunk(bias, c * width, width)
        m_new = jnp.maximum(m[:, sl], jnp.max(s, axis=0, keepdims=True))
        alpha = jnp.exp2(m[:, sl] - m_new)
        p = jnp.exp2(s - m_new)
        l_out.append(alpha * l[:, sl] + jnp.sum(p, axis=0, keepdims=True))
        acc_ref[:, sl] = alpha * acc_ref[:, sl] + jnp.dot(vt, p.astype(BF16),
                                                          preferred_element_type=F32)
        m_out.append(m_new)
    return jnp.concatenate(m_out, axis=1), jnp.concatenate(l_out, axis=1)


def _vt_tiles(vt_ref, lead, c0, n):
    return jnp.concatenate([vt_ref[lead + (c0 + c,)] for c in range(n)], axis=1)


def _mla_attn_kernel(t_blk, seq, qn_ref, qr_ref, kn_ref, kr_ref, vt_ref, o_ref, acc_ref):
    h = pl.program_id(1)
    n_vt = t_blk // VT_TILE
    lane_half = lax.broadcasted_iota(jnp.int32, (t_blk, LANES), 1) // MLA_ROPE
    kpos = lax.broadcasted_iota(jnp.int32, (t_blk, t_blk), 0)
    qpos = lax.broadcasted_iota(jnp.int32, (t_blk, t_blk), 1)
    diag_bias = jnp.where(kpos <= qpos, 0.0, NEG_BIG).astype(F32)

    def q_block(qi, carry):
        q0 = pl.multiple_of(qi * t_blk, t_blk)
        qr = qr_ref[pl.ds(q0, t_blk), :]
        qr = jnp.where(lane_half == (h % 2), qr, jnp.zeros_like(qr))
        q = jnp.concatenate([qn_ref[pl.ds(q0, t_blk), :], qr], axis=1)
        acc_ref[...] = jnp.zeros(acc_ref.shape, F32)

        def step(j, m, l, bias):
            k0 = pl.multiple_of(j * t_blk, t_blk)
            k = jnp.concatenate([kn_ref[pl.ds(k0, t_blk), :], kr_ref[pl.ds(k0, t_blk), :]], axis=1)
            vt = _vt_tiles(vt_ref, (0,), j * n_vt, n_vt)
            return _flash_step_t(q, k, vt, bias, m, l, acc_ref)

        m0 = jnp.full((1, t_blk), NEG_BIG, F32)
        l0 = jnp.zeros((1, t_blk), F32)
        m, l = lax.fori_loop(0, qi, lambda j, ml: step(j, ml[0], ml[1], None), (m0, l0))
        m, l = step(qi, m, l, diag_bias)
        o_ref[pl.ds(q0, t_blk), :] = (acc_ref[...] / l).T.astype(o_ref.dtype)
        return carry

    lax.fori_loop(0, seq // t_blk, q_block, 0)


def _mla_attention(q, kn, kr, vt, batch, seq, heads, t_blk=512):
    t = batch * seq
    n_nope_blk = heads * MLA_NOPE // LANES
    n_vt = seq // VT_TILE
    kern = functools.partial(_mla_attn_kernel, t_blk, seq)
    return pl.pallas_call(
        kern,
        grid=(batch, heads),
        in_specs=[
            pl.BlockSpec((seq, MLA_NOPE), lambda b, h: (b, h)),
            pl.BlockSpec((seq, LANES), lambda b, h: (b, n_nope_blk + h // 2)),
            pl.BlockSpec((seq, MLA_NOPE), lambda b, h: (b, h)),
            pl.BlockSpec((seq, LANES), lambda b, h: (b, 0)),
            pl.BlockSpec((1, n_vt, MLA_V, VT_TILE), lambda b, h: (h, b, 0, 0)),
        ],
        out_specs=pl.BlockSpec((seq, MLA_V), lambda b, h: (b, h)),
        out_shape=jax.ShapeDtypeStruct((t, heads * MLA_V), BF16),
        scratch_shapes=[pltpu.VMEM((MLA_V, t_blk), F32)],
        compiler_params=_params(("parallel", "parallel")),
        name="mla_attention",
    )(q, q, kn, kr, vt)


def _compress_mlp(a, w1_ref, pe_ref, w2_ref, transposed_out):
    y_lo = jnp.dot(a, w1_ref[0], preferred_element_type=F32)
    y_hi = jnp.dot(a, w1_ref[1], preferred_element_type=F32)
    pe_c = (jnp.dot(pe_ref[0], w1_ref[0], preferred_element_type=F32)
            + jnp.dot(pe_ref[1], w1_ref[1], preferred_element_type=F32))
    n = y_hi.shape[0]
    hid = y_lo + pltpu.roll(y_hi, n - 1, 0) + pe_c[0:1, :]
    hid = jax.nn.gelu(hid, approximate=True)
    if transposed_out:
        return lax.dot_general(w2_ref[...], hid.astype(BF16), NT_DIMS, preferred_element_type=F32)
    return jnp.dot(hid.astype(BF16), w2_ref[...], preferred_element_type=F32)


def _compress_kernel(ak_ref, av_ref, w1k_ref, pek_ref, w2k_ref, w1v_ref, pev_ref, w2vt_ref,
                     kc_ref, vct_ref):
    kc_ref[0] = _compress_mlp(ak_ref[0], w1k_ref, pek_ref, w2k_ref, False).astype(BF16)
    vct_ref[0] = _compress_mlp(av_ref[0], w1v_ref, pev_ref, w2vt_ref, True).astype(BF16)


def _compress(ak, av, wk, wv, batch, seq):
    n_chunk = seq // CMP_STRIDE
    const3 = lambda b, g: (0, 0, 0)
    const2 = lambda b, g: (0, 0)

    def wspecs(w):
        w1, pe, w2 = w
        return [pl.BlockSpec(w1.shape, const3), pl.BlockSpec(pe.shape, const3),
                pl.BlockSpec(w2.shape, const2)]

    return pl.pallas_call(
        _compress_kernel,
        grid=(batch, NSA_G),
        in_specs=[pl.BlockSpec((1, n_chunk, ak.shape[2]), lambda b, g: (g, b, 0)),
                  pl.BlockSpec((1, n_chunk, av.shape[2]), lambda b, g: (g, b, 0))]
        + wspecs(wk) + wspecs(wv),
        out_specs=(pl.BlockSpec((1, n_chunk, NSA_KPAD), lambda b, g: (b * NSA_G + g, 0, 0)),
                   pl.BlockSpec((1, NSA_DV, n_chunk), lambda b, g: (b * NSA_G + g, 0, 0))),
        out_shape=(jax.ShapeDtypeStruct((batch * NSA_G, n_chunk, NSA_KPAD), BF16),
                   jax.ShapeDtypeStruct((batch * NSA_G, NSA_DV, n_chunk), BF16)),
        compiler_params=_params(("parallel", "parallel")),
        name="nsa_compress",
    )(ak, av, *wk, *wv)


def _nsa_attn_kernel(hpg, tq, tk, seq, q_ref, kc_ref, vct_ref, ks_ref, vst_ref, kw_ref, vwt_ref,
                     gate_ref, ovl_ref, o_ref, bias_ref, oc_ref, os_ref, acc_ref):
    qi = pl.program_id(2)
    t0 = qi * tq
    n_rows = hpg * tq
    n_cmp = kc_ref.shape[1]
    n_sel = seq // SEL_BLOCK
    n_vt = tk // VT_TILE
    q = q_ref[...].reshape(n_rows, NSA_KPAD)

    def tile_heads(x):
        return jnp.concatenate([x] * hpg, axis=1)

    s = lax.dot_general(kc_ref[0], q, NT_DIMS, preferred_element_type=F32)
    cidx = lax.broadcasted_iota(jnp.int32, (n_cmp, tq), 0)
    tpos_c = t0 + lax.broadcasted_iota(jnp.int32, (n_cmp, tq), 1)
    mask_c = tile_heads((cidx * CMP_STRIDE + CMP_BLOCK - 1) <= tpos_c)
    s = jnp.where(mask_c, s, NEG_BIG)
    mx = jnp.max(s, axis=0, keepdims=True)
    e = jnp.where(mask_c, jnp.exp2(s - mx), 0.0)
    p = e / jnp.maximum(jnp.sum(e, axis=0, keepdims=True), 1e-30)
    oc_ref[...] = jnp.dot(vct_ref[0], p.astype(BF16), preferred_element_type=F32)

    psum = p[:, 0:tq]
    for h in range(1, hpg):
        psum = psum + p[:, h * tq:(h + 1) * tq]
    p1 = psum.astype(BF16)
    p2 = (psum - p1.astype(F32)).astype(BF16)
    p3 = (psum - p1.astype(F32) - p2.astype(F32)).astype(BF16)
    ovl = ovl_ref[...]
    imp = (jnp.dot(ovl, p1, preferred_element_type=F32)
           + jnp.dot(ovl, p2, preferred_element_type=F32)
           + jnp.dot(ovl, p3, preferred_element_type=F32))
    jrow = lax.broadcasted_iota(jnp.int32, (n_sel, tq), 0)
    tpos = t0 + lax.broadcasted_iota(jnp.int32, (n_sel, tq), 1)
    valid = jrow * SEL_BLOCK <= tpos
    cur = jnp.right_shift(tpos, SEL_BLOCK.bit_length() - 1)
    forced = valid & ((jrow == 0) | (jrow == cur) | (jrow == cur - 1))
    score = jnp.where(valid, imp + jnp.where(forced, FORCE_BONUS, 0.0), -jnp.inf)
    rank = jnp.zeros((n_sel, tq), jnp.int32)
    for i in range(n_sel):
        row = score[i:i + 1, :]
        beats = (row > score) | ((row == score) & (jrow > i))
        rank = rank + beats.astype(jnp.int32)
    sel_bias = jnp.where(valid & (rank < SEL_TOPK), 0.0, NEG_BIG).astype(F32)
    j_hi = (t0 + tq - 1) // tk + 1
    blocks_per_tile = tk // SEL_BLOCK
    for jj in range(seq // tk):
        @pl.when(jj < j_hi)
        def _():
            rows = [jnp.broadcast_to(sel_bias[b:b + 1, :], (SEL_BLOCK, tq))
                    for b in range(jj * blocks_per_tile, (jj + 1) * blocks_per_tile)]
            kpos = jj * tk + lax.broadcasted_iota(jnp.int32, (tk, tq), 0)
            tq_pos = t0 + lax.broadcasted_iota(jnp.int32, (tk, tq), 1)
            bias_ref[jj] = jnp.where(kpos <= tq_pos, jnp.concatenate(rows, axis=0), NEG_BIG)

    m0 = jnp.full((1, n_rows), NEG_BIG, F32)
    l0 = jnp.zeros((1, n_rows), F32)

    acc_ref[...] = jnp.zeros(acc_ref.shape, F32)

    def sel_step(j, ml):
        k0 = pl.multiple_of(j * tk, tk)
        k = ks_ref[0, pl.ds(k0, tk), :]
        vt = _vt_tiles(vst_ref, (0,), j * n_vt, n_vt)
        return _flash_step_t(q, k, vt, bias_ref[j], ml[0], ml[1], acc_ref)

    _, l = lax.fori_loop(0, j_hi, sel_step, (m0, l0))
    os_ref[...] = acc_ref[...] / l

    acc_ref[...] = jnp.zeros(acc_ref.shape, F32)
    span = WINDOW + tq
    span = -(-span // tk) * tk
    k_start = jnp.maximum(t0 + tq - span, 0)
    n_win = (t0 + tq - k_start + tk - 1) // tk

    def win_step(j, ml):
        k0 = pl.multiple_of(k_start + j * tk, VT_TILE)
        k = kw_ref[0, pl.ds(k0, tk), :]
        vt = _vt_tiles(vwt_ref, (0,), k0 // VT_TILE, n_vt)
        kpos = k0 + lax.broadcasted_iota(jnp.int32, (tk, tq), 0)
        tq_pos = t0 + lax.broadcasted_iota(jnp.int32, (tk, tq), 1)
        ok = (kpos <= tq_pos) & (tq_pos - kpos < WINDOW)
        bias = jnp.where(ok, 0.0, NEG_BIG).astype(F32)
        return _flash_step_t(q, k, vt, bias, ml[0], ml[1], acc_ref)

    _, l = lax.fori_loop(0, n_win, win_step, (m0, l0))
    o_w = acc_ref[...] / l

    gates_t = gate_ref[...].T
    for h in range(hpg):
        cols = slice(h * tq, (h + 1) * tq)
        o = (gates_t[3 * h:3 * h + 1, :] * oc_ref[:, cols]
             + gates_t[3 * h + 1:3 * h + 2, :] * os_ref[:, cols]
             + gates_t[3 * h + 2:3 * h + 3, :] * o_w[:, cols])
        o_ref[:, h * NSA_DV:(h + 1) * NSA_DV] = o.T.astype(o_ref.dtype)


def _nsa_attention(q, kc, vct, k3, vt3, gates, ovl, batch, seq, hpg, tq=128, tk=256):
    t = batch * seq
    nq = seq // tq
    n_chunk = seq // CMP_STRIDE
    n_rows = hpg * tq
    n_vt = seq // VT_TILE
    assert (WINDOW + tq) % VT_TILE == 0 and seq >= -(-(WINDOW + tq) // tk) * tk
    kern = functools.partial(_nsa_attn_kernel, hpg, tq, tk, seq)

    def k_spec(branch):
        return pl.BlockSpec((1, seq, NSA_KPAD), lambda b, g, qi: (branch * NSA_G + g, b, 0))

    def vt_spec(branch):
        return pl.BlockSpec((1, n_vt, NSA_DV, VT_TILE), lambda b, g, qi: (branch * NSA_G + g, b, 0, 0))

    return pl.pallas_call(
        kern,
        grid=(batch, NSA_G, nq),
        in_specs=[
            pl.BlockSpec((hpg, tq, NSA_KPAD), lambda b, g, qi: (g, b * nq + qi, 0)),
            pl.BlockSpec((1, n_chunk, NSA_KPAD), lambda b, g, qi: (b * NSA_G + g, 0, 0)),
            pl.BlockSpec((1, NSA_DV, n_chunk), lambda b, g, qi: (b * NSA_G + g, 0, 0)),
            k_spec(1), vt_spec(1), k_spec(2), vt_spec(2),
            pl.BlockSpec((tq, LANES), lambda b, g, qi: (b * nq + qi, g)),
            pl.BlockSpec(ovl.shape, lambda b, g, qi: (0, 0)),
        ],
        out_specs=pl.BlockSpec((tq, hpg * NSA_DV), lambda b, g, qi: (b * nq + qi, g)),
        out_shape=jax.ShapeDtypeStruct((t, NSA_G * hpg * NSA_DV), BF16),
        scratch_shapes=[
            pltpu.VMEM((seq // tk, tk, tq), F32),
            pltpu.VMEM((NSA_DV, n_rows), F32),
            pltpu.VMEM((NSA_DV, n_rows), F32),
            pltpu.VMEM((NSA_DV, n_rows), F32),
        ],
        compiler_params=_params(("parallel", "parallel", "arbitrary")),
        name="nsa_attention",
    )(q, kc, vct, k3, vt3, k3, vt3, gates, ovl)


def _pad_rope_halves(w):
    w = w.reshape(w.shape[:-1] + (2, NSA_HALF))
    w = jnp.pad(w, [(0, 0)] * (w.ndim - 1) + [(0, LANES - NSA_HALF)])
    return w.reshape(w.shape[:-2] + (NSA_KPAD,))


def _rope_tables(pos_flat):
    pos = pos_flat.astype(F32)[:, None]
    inv64 = ROPE_THETA ** (-jnp.arange(0, MLA_ROPE, 2, dtype=F32) / MLA_ROPE)
    ang = pos * inv64
    c, s = jnp.cos(ang), jnp.sin(ang)
    cos_mla = jnp.concatenate([c, c, c, c], axis=1)
    sin_mla = jnp.concatenate([-s, s, -s, s], axis=1)
    inv192 = ROPE_THETA ** (-jnp.arange(0, NSA_DK, 2, dtype=F32) / NSA_DK)
    ang = pos * inv192
    pad = jnp.zeros((pos.shape[0], LANES - NSA_HALF), F32)
    cos_nsa = jnp.concatenate([jnp.cos(ang), pad], axis=1)
    sin_nsa = jnp.concatenate([jnp.sin(ang), pad], axis=1)
    return cos_mla, sin_mla, cos_nsa, sin_nsa


def _compress_weights(w1, pe, w2, *, pad_rope, transpose_w2):
    d_in, d_hid = w1.shape[0] // CMP_BLOCK, w1.shape[1]
    hid_p = -(-d_hid // LANES) * LANES
    w1 = w1.reshape(CMP_BLOCK, d_in, d_hid)
    if pad_rope:
        w1 = jnp.swapaxes(_pad_rope_halves(jnp.swapaxes(w1, 1, 2)), 1, 2)
        pe = _pad_rope_halves(pe)
        w2 = _pad_rope_halves(w2)
    width = w1.shape[1]
    w1 = jnp.pad(w1, ((0, 0), (0, 0), (0, hid_p - d_hid)))
    w1 = w1.reshape(2, CMP_STRIDE * width, hid_p).astype(BF16)
    pe = jnp.pad(pe.reshape(2, 1, CMP_STRIDE * width), ((0, 0), (0, 7), (0, 0))).astype(BF16)
    w2 = jnp.pad(w2, ((0, hid_p - d_hid), (0, 0))).astype(BF16)
    return w1, pe, (w2.T if transpose_w2 else w2)


def kernel(x, positions, a_norm, a_w_in, a_q_norm, a_w_uq, a_kv_norm, a_w_ukv, a_w_o, s_norm, s_w_kv, s_cmp_pe_k, s_cmp_w1_k, s_cmp_w2_k, s_cmp_pe_v, s_cmp_w1_v, s_cmp_w2_v, b_norm, b_w_in, b_w_o, f_norm, f_w_in, f_w_out, final_norm):
    batch, seq, d = x.shape
    t = batch * seq
    heads = d // 128
    hpg = heads // NSA_G
    d_ff = f_w_out.shape[1]
    n_a, n_b = a_w_in.shape[0], b_w_in.shape[0]
    assert n_a >= 1 and n_b >= 1
    tm = 1024
    sds = jax.ShapeDtypeStruct

    cos_mla, sin_mla, cos_nsa, sin_nsa = _rope_tables(positions.reshape(t))
    row = lambda i, j, k: (i, 0)
    tile = lambda i, j, k: (i, j)
    tab_spec = pl.BlockSpec((tm, LANES), row)
    ssq_spec = pl.BlockSpec((tm, LANES), row)
    f_w_out_b = f_w_out.astype(BF16)

    def residual_mm(a, w, w_layer, h, gains, *, tm_, tn, name):
        blk = pl.BlockSpec((tm_, tn), tile)
        common = dict(tm=tm_, tn=tn, tk=a.shape[1], nj=d // tn, w_layer=w_layer, name=name)
        if not gains:
            h_new = _mm(a, w, _epi_residual, sds((t, d), F32), blk,
                        extras=(h,), extra_specs=(blk,), **common)
            return h_new, None, ()
        outs = _mm(a, w, _epi_residual_prenorm,
                   (sds((t, d), F32), sds((t, LANES), F32)) + tuple(sds((t, d), BF16) for _ in gains),
                   (blk, pl.BlockSpec((tm_, LANES), row)) + tuple(blk for _ in gains),
                   extras=(h,) + tuple(g.reshape(1, d) for g in gains),
                   extra_specs=(blk,) + tuple(pl.BlockSpec((1, tn), lambda i, j, k: (0, j))
                                              for _ in gains),
                   **common)
        return outs[0], outs[1], tuple(outs[2:])

    def ffn(hb, ssq, h, layer, gains):
        tn = 256
        nj = d_ff // tn
        tm_in = min(2 * tm, t)
        act = _mm(hb, f_w_in, functools.partial(_epi_swiglu, d), sds((t, d_ff), BF16),
                  pl.BlockSpec((tm_in, tn), tile),
                  tm=tm_in, tn=tn, tk=d, nj=nj, w_layer=layer, a_buffers=1,
                  w_col_maps=(lambda j: j, lambda j: j + nj),
                  extras=(ssq,), extra_specs=(pl.BlockSpec((tm_in, LANES), row),), name="ffn_in")
        return residual_mm(act, f_w_out_b, layer, h, gains, tm_=512, tn=tn, name="ffn_out")

    h = x.reshape(t, d)
    hb, ssq = _prenorm(h, a_norm[0])

    for i in range(n_a):
        rope_lo = MLA_Q_LORA + MLA_KV_LORA
        w_in = jnp.concatenate([a_w_in[i], a_w_in[i][:, rope_lo:]], axis=1).astype(BF16)
        n_in = w_in.shape[1]
        cq, ckv, kr = _mm(
            hb, w_in, functools.partial(_epi_mla_in, d),
            (sds((t, MLA_Q_LORA), BF16), sds((t, MLA_KV_LORA), BF16), sds((t, LANES), BF16)),
            (pl.BlockSpec((tm, MLA_Q_LORA), row), pl.BlockSpec((tm, MLA_KV_LORA), row),
             pl.BlockSpec((tm, LANES), row)),
            tm=tm, tn=n_in, tk=1024, nj=1,
            extras=(ssq, a_q_norm[i].reshape(1, -1), a_kv_norm[i].reshape(1, -1), cos_mla, sin_mla),
            extra_specs=(ssq_spec,
                         pl.BlockSpec((1, MLA_Q_LORA), lambda i_, j, k: (0, 0)),
                         pl.BlockSpec((1, MLA_KV_LORA), lambda i_, j, k: (0, 0)),
                         tab_spec, tab_spec),
            name="mla_in")

        qd = MLA_NOPE + MLA_ROPE
        w_uq = a_w_uq[i].reshape(MLA_Q_LORA, heads, qd)
        w_uq = jnp.concatenate([w_uq[:, :, :MLA_NOPE].reshape(MLA_Q_LORA, heads * MLA_NOPE),
                                w_uq[:, :, MLA_NOPE:].reshape(MLA_Q_LORA, heads * MLA_ROPE)],
                               axis=1).astype(BF16)
        tn = 512
        q = _mm(cq, w_uq,
                functools.partial(_epi_mla_q, heads * MLA_NOPE // tn, qd ** -0.5 * LOG2_E),
                jax.ShapeDtypeStruct((t, heads * qd), BF16),
                pl.BlockSpec((tm, tn), lambda i_, j, k: (i_, j)),
                tm=tm, tn=tn, tk=MLA_Q_LORA, nj=heads * qd // tn,
                extras=(cos_mla, sin_mla), extra_specs=(tab_spec, tab_spec), name="mla_q")
        heads_per_tile = 4
        kn, vt = _mm(ckv, a_w_ukv[i].astype(BF16), _epi_mla_kv,
                     (jax.ShapeDtypeStruct((t, heads * MLA_NOPE), BF16),
                      jax.ShapeDtypeStruct((heads, t // VT_TILE, MLA_V, VT_TILE), BF16)),
                     (pl.BlockSpec((tm, heads_per_tile * MLA_NOPE), lambda i_, j, k: (i_, j)),
                      pl.BlockSpec((heads_per_tile, tm // VT_TILE, MLA_V, VT_TILE),
                                   lambda i_, j, k: (j, i_, 0, 0))),
                     tm=tm, tn=heads_per_tile * (MLA_NOPE + MLA_V), tk=MLA_KV_LORA,
                     nj=heads // heads_per_tile, name="mla_kv")
        o = _mla_attention(q, kn, kr, vt, batch, seq, heads)
        h, ssq, (hb,) = residual_mm(o, a_w_o[i].astype(BF16), None, h, [f_norm[i]],
                                    tm_=tm, tn=512, name="mla_out")
        gains = [a_norm[i + 1]] if i + 1 < n_a else [s_norm, b_norm[0]]
        h, ssq, hbs = ffn(hb, ssq, h, i, gains)
        hb = hbs[0]

    hb_s, hb_b = hbs
    w_kv = s_w_kv.reshape(d, 3 * NSA_G, NSA_DK + NSA_DV)
    w_kv = jnp.concatenate([_pad_rope_halves(w_kv[:, :, :NSA_DK]), w_kv[:, :, NSA_DK:]], axis=2)
    w_kv = w_kv.reshape(d, 3 * NSA_G * NSA_KV_W).astype(BF16)
    per_tile = 2
    k3, v3, vt3 = _mm(
        hb_s, w_kv, functools.partial(_epi_nsa_kv, d),
        (sds((3 * NSA_G, t, NSA_KPAD), BF16), sds((3 * NSA_G, t, NSA_DV), BF16),
         sds((3 * NSA_G, t // VT_TILE, NSA_DV, VT_TILE), BF16)),
        (pl.BlockSpec((per_tile, tm, NSA_KPAD), lambda i_, j, k: (j, i_, 0)),
         pl.BlockSpec((per_tile, tm, NSA_DV), lambda i_, j, k: (j, i_, 0)),
         pl.BlockSpec((per_tile, tm // VT_TILE, NSA_DV, VT_TILE), lambda i_, j, k: (j, i_, 0, 0))),
        tm=tm, tn=per_tile * NSA_KV_W, tk=d, nj=3 * NSA_G // per_tile,
        extras=(ssq, cos_nsa, sin_nsa), extra_specs=(ssq_spec, tab_spec, tab_spec), name="nsa_kv")

    ak = k3[:NSA_G].reshape(NSA_G, t // CMP_STRIDE, CMP_STRIDE * NSA_KPAD)
    av = v3[:NSA_G].reshape(NSA_G, t // CMP_STRIDE, CMP_STRIDE * NSA_DV)
    kc, vct = _compress(ak, av,
                        _compress_weights(s_cmp_w1_k, s_cmp_pe_k, s_cmp_w2_k,
                                          pad_rope=True, transpose_w2=False),
                        _compress_weights(s_cmp_w1_v, s_cmp_pe_v, s_cmp_w2_v,
                                          pad_rope=False, transpose_w2=True),
                        batch, seq)

    n_chunk = seq // CMP_STRIDE
    n_sel = seq // SEL_BLOCK
    c_start = np.arange(n_chunk) * CMP_STRIDE
    j_start = np.arange(n_sel) * SEL_BLOCK
    ovl = ((c_start[None, :] < j_start[:, None] + SEL_BLOCK)
           & (c_start[None, :] + CMP_BLOCK > j_start[:, None])
           & (np.arange(n_chunk)[None, :] < (seq - CMP_BLOCK) // CMP_STRIDE + 1))
    ovl = jnp.asarray(ovl, BF16)

    for i in range(n_b):
        n_q = heads * NSA_DK
        w_q = _pad_rope_halves(b_w_in[i][:, :n_q].reshape(d, heads, NSA_DK))
        w_q = w_q.reshape(d, heads * NSA_KPAD).astype(BF16)
        w_g = b_w_in[i][:, n_q:].reshape(d, NSA_G, 3 * hpg)
        w_g = jnp.pad(w_g, ((0, 0), (0, 0), (0, LANES - 3 * hpg))).reshape(d, NSA_G * LANES)
        w_g = w_g.astype(BF16)
        heads_per_tile = 4
        tn = heads_per_tile * NSA_KPAD
        q = _mm(hb_b, w_q, functools.partial(_epi_nsa_q, d, NSA_DK ** -0.5 * LOG2_E),
                sds((heads, t, NSA_KPAD), BF16),
                pl.BlockSpec((heads_per_tile, tm, NSA_KPAD), lambda i_, j, k: (j, i_, 0)),
                tm=tm, tn=tn, tk=d, nj=heads // heads_per_tile,
                extras=(ssq, cos_nsa, sin_nsa), extra_specs=(ssq_spec, tab_spec, tab_spec),
                name="nsa_q")
        gates = _mm(hb_b, w_g, functools.partial(_epi_sigmoid, d), sds((t, NSA_G * LANES), F32),
                    pl.BlockSpec((tm, NSA_G * LANES), tile),
                    tm=tm, tn=NSA_G * LANES, tk=d, nj=1,
                    extras=(ssq,), extra_specs=(ssq_spec,), name="nsa_gates")
        o = _nsa_attention(q, kc, vct, k3, vt3, gates, ovl, batch, seq, hpg)
        h, ssq, (hb,) = residual_mm(o, b_w_o[i].astype(BF16), None, h, [f_norm[n_a + i]],
                                    tm_=tm, tn=512, name="nsa_out")
        gains = [b_norm[i + 1]] if i + 1 < n_b else []
        h, ssq, hbs = ffn(hb, ssq, h, n_a + i, gains)
        if hbs:
            hb_b = hbs[0]

    out = _rmsnorm(h, final_norm, F32)
    return out.reshape(batch, seq, d)
```

```python
import functools

import numpy as np
import jax
import jax.numpy as jnp
from jax import lax
from jax.experimental import pallas as pl
from jax.experimental.pallas import tpu as pltpu

F32 = jnp.float32
BF16 = jnp.bfloat16

LANES = 128
VMEM_LIMIT_BYTES = 56 * 1024 * 1024

NORM_EPS = 1e-6
ROPE_THETA = 10000.0

MLA_Q_LORA = 1536
MLA_KV_LORA = 512
MLA_NOPE = 128
MLA_ROPE = 64
MLA_V = 128

NSA_G = 4
NSA_DK = 192
NSA_DV = 128
NSA_HALF = NSA_DK // 2
NSA_KPAD = 2 * LANES
NSA_KV_W = NSA_KPAD + NSA_DV
CMP_BLOCK = 32
CMP_STRIDE = 16
SEL_BLOCK = 64
SEL_TOPK = 16
WINDOW = 512
FORCE_BONUS = 1e4
NEG_BIG = -1e30
LOG2_E = 1.4426950408889634

VT_TILE = LANES
NT_DIMS = (((1,), (1,)), ((), ()))


def _params(sem):
    return pltpu.CompilerParams(dimension_semantics=sem, vmem_limit_bytes=VMEM_LIMIT_BYTES)


def _rmsnorm_kernel(x_ref, g_ref, o_ref):
    x = x_ref[...]
    ms = jnp.mean(x * x, axis=-1, keepdims=True)
    o_ref[...] = (x * lax.rsqrt(ms + NORM_EPS) * g_ref[...]).astype(o_ref.dtype)


def _rmsnorm(x, g, out_dtype, tm=256):
    t, d = x.shape
    return pl.pallas_call(
        _rmsnorm_kernel,
        grid=(t // tm,),
        in_specs=[pl.BlockSpec((tm, d), lambda i: (i, 0)),
                  pl.BlockSpec((1, d), lambda i: (0, 0))],
        out_specs=pl.BlockSpec((tm, d), lambda i: (i, 0)),
        out_shape=jax.ShapeDtypeStruct((t, d), out_dtype),
        compiler_params=_params(("parallel",)),
        name="rmsnorm",
    )(x, g.reshape(1, d))


def _prenorm_kernel(x_ref, g_ref, hb_ref, ssq_ref):
    x = x_ref[...]
    hb_ref[...] = (x * g_ref[...]).astype(hb_ref.dtype)
    ssq_ref[...] = jnp.broadcast_to(jnp.sum(x * x, axis=-1, keepdims=True), ssq_ref.shape)


def _prenorm(x, g, tm=256):
    t, d = x.shape
    return pl.pallas_call(
        _prenorm_kernel,
        grid=(t // tm,),
        in_specs=[pl.BlockSpec((tm, d), lambda i: (i, 0)),
                  pl.BlockSpec((1, d), lambda i: (0, 0))],
        out_specs=(pl.BlockSpec((tm, d), lambda i: (i, 0)),
                   pl.BlockSpec((tm, LANES), lambda i: (i, 0))),
        out_shape=(jax.ShapeDtypeStruct((t, d), BF16), jax.ShapeDtypeStruct((t, LANES), F32)),
        compiler_params=_params(("parallel",)),
        name="prenorm",
    )(x, g.reshape(1, d))


def _mm_kernel(n_w, n_e, n_o, nk, epilogue, *refs):
    a_ref = refs[0]
    w_refs = refs[1:1 + n_w]
    e_refs = refs[1 + n_w:1 + n_w + n_e]
    o_refs = refs[1 + n_w + n_e:1 + n_w + n_e + n_o]
    acc_refs = refs[1 + n_w + n_e + n_o:]
    j = pl.program_id(1)

    def product(w):
        return jnp.dot(a_ref[...], w[...].astype(a_ref.dtype), preferred_element_type=F32)

    if nk == 1:
        epilogue([product(w) for w in w_refs], e_refs, o_refs, j)
        return
    k = pl.program_id(2)

    @pl.when(k == 0)
    def _():
        for w, acc in zip(w_refs, acc_refs):
            acc[...] = product(w)

    @pl.when(k > 0)
    def _():
        for w, acc in zip(w_refs, acc_refs):
            acc[...] += product(w)

    @pl.when(k == nk - 1)
    def _():
        epilogue([acc[...] for acc in acc_refs], e_refs, o_refs, j)


def _mm(a, w, epilogue, out_shape, out_specs, *, tm, tn, tk, nj, w_col_maps=None, w_layer=None,
        extras=(), extra_specs=(), a_buffers=None, name="matmul"):
    m, kdim = a.shape
    ni, nk = m // tm, kdim // tk
    assert ni * tm == m and nk * tk == kdim
    if w_col_maps is None:
        w_col_maps = (lambda j: j,)
    n_w = len(w_col_maps)
    single = not isinstance(out_shape, (tuple, list))
    out_shapes = (out_shape,) if single else tuple(out_shape)
    out_specs_t = (out_specs,) if single else tuple(out_specs)
    a_kwargs = {} if a_buffers is None else {"pipeline_mode": pl.Buffered(a_buffers)}
    in_specs = [pl.BlockSpec((tm, tk), lambda i, j, k: (i, k), **a_kwargs)]
    for cm in w_col_maps:
        if w_layer is None:
            in_specs.append(pl.BlockSpec((tk, tn), functools.partial(
                lambda i, j, k, cm: (k, cm(j)), cm=cm)))
        else:
            in_specs.append(pl.BlockSpec((None, tk, tn), functools.partial(
                lambda i, j, k, cm: (w_layer, k, cm(j)), cm=cm)))
    in_specs += list(extra_specs)
    scratch = [pltpu.VMEM((tm, tn), F32) for _ in range(n_w)] if nk > 1 else []
    kern = functools.partial(_mm_kernel, n_w, len(extras), len(out_shapes), nk, epilogue)
    res = pl.pallas_call(
        kern,
        grid=(ni, nj, nk),
        in_specs=in_specs,
        out_specs=out_specs_t,
        out_shape=out_shapes,
        scratch_shapes=scratch,
        compiler_params=_params(("parallel", "arbitrary", "arbitrary")),
        name=name,
    )(a, *([w] * n_w), *extras)
    return res[0] if single else res


def _rope_pairs_64(x, cos2, sin2):
    lane = lax.broadcasted_iota(jnp.int32, x.shape, 1)
    first_half = (lane % MLA_ROPE) < (MLA_ROPE // 2)
    partner = jnp.where(first_half,
                        pltpu.roll(x, LANES - MLA_ROPE // 2, 1),
                        pltpu.roll(x, MLA_ROPE // 2, 1))
    return x * cos2 + partner * sin2


def _store_vt_tiles(vt_ref, lead, v):
    for c in range(v.shape[0] // VT_TILE):
        vt_ref[lead + (c,)] = v[c * VT_TILE:(c + 1) * VT_TILE, :].T.astype(vt_ref.dtype)


def _row_scale(acc, ssq_ref, d_model):
    rstd = lax.rsqrt(ssq_ref[...] * (1.0 / d_model) + NORM_EPS)
    return acc * jnp.concatenate([rstd] * (acc.shape[1] // LANES), axis=1)


def _epi_residual(accs, e_refs, o_refs, j):
    o_refs[0][...] = e_refs[0][...] + accs[0]


def _epi_residual_prenorm(accs, e_refs, o_refs, j):
    h = e_refs[0][...] + accs[0]
    o_refs[0][...] = h
    ssq_ref = o_refs[1]
    part = jnp.broadcast_to(jnp.sum(h * h, axis=-1, keepdims=True), ssq_ref.shape)

    @pl.when(j == 0)
    def _():
        ssq_ref[...] = part

    @pl.when(j > 0)
    def _():
        ssq_ref[...] += part

    for g_ref, hb_ref in zip(e_refs[1:], o_refs[2:]):
        hb_ref[...] = (h * g_ref[...]).astype(hb_ref.dtype)


def _epi_swiglu(d_model, accs, e_refs, o_refs, j):
    a = _row_scale(accs[0], e_refs[0], d_model)
    b = _row_scale(accs[1], e_refs[0], d_model)
    o_refs[0][...] = (a * jax.nn.sigmoid(a) * b).astype(o_refs[0].dtype)


def _epi_sigmoid(d_model, accs, e_refs, o_refs, j):
    o_refs[0][...] = jax.nn.sigmoid(_row_scale(accs[0], e_refs[0], d_model))


def _epi_mla_in(d_model, accs, e_refs, o_refs, j):
    ssq_ref, gq_ref, gkv_ref, cos_ref, sin_ref = e_refs
    u = _row_scale(accs[0], ssq_ref, d_model)
    cq = u[:, :MLA_Q_LORA]
    ms = jnp.mean(cq * cq, axis=-1, keepdims=True)
    o_refs[0][...] = (cq * lax.rsqrt(ms + NORM_EPS) * gq_ref[...]).astype(BF16)
    ckv = u[:, MLA_Q_LORA:MLA_Q_LORA + MLA_KV_LORA]
    ms = jnp.mean(ckv * ckv, axis=-1, keepdims=True)
    o_refs[1][...] = (ckv * lax.rsqrt(ms + NORM_EPS) * gkv_ref[...]).astype(BF16)
    kr = u[:, MLA_Q_LORA + MLA_KV_LORA:]
    o_refs[2][...] = _rope_pairs_64(kr, cos_ref[...], sin_ref[...]).astype(BF16)


def _epi_mla_q(n_nope_tiles, scale, accs, e_refs, o_refs, j):
    cos_ref, sin_ref = e_refs
    acc = accs[0]

    @pl.when(j < n_nope_tiles)
    def _():
        o_refs[0][...] = (acc * scale).astype(BF16)

    @pl.when(j >= n_nope_tiles)
    def _():
        for c in range(acc.shape[1] // LANES):
            x = acc[:, c * LANES:(c + 1) * LANES]
            r = _rope_pairs_64(x, cos_ref[...], sin_ref[...])
            o_refs[0][:, c * LANES:(c + 1) * LANES] = (r * scale).astype(BF16)


def _epi_mla_kv(accs, e_refs, o_refs, j):
    acc = accs[0]
    kn_ref, vt_ref = o_refs
    per = MLA_NOPE + MLA_V
    for r in range(acc.shape[1] // per):
        kn_ref[:, r * MLA_NOPE:(r + 1) * MLA_NOPE] = acc[:, r * per:r * per + MLA_NOPE].astype(BF16)
        _store_vt_tiles(vt_ref, (r,), acc[:, r * per + MLA_NOPE:(r + 1) * per])


def _rope_halves(a, b, cos, sin):
    return a * cos - b * sin, b * cos + a * sin


def _epi_nsa_q(d_model, scale, accs, e_refs, o_refs, j):
    ssq_ref, cos_ref, sin_ref = e_refs
    acc = _row_scale(accs[0], ssq_ref, d_model)
    cos, sin = cos_ref[...], sin_ref[...]
    for r in range(acc.shape[1] // NSA_KPAD):
        a = acc[:, r * NSA_KPAD:r * NSA_KPAD + LANES]
        b = acc[:, r * NSA_KPAD + LANES:(r + 1) * NSA_KPAD]
        ra, rb = _rope_halves(a, b, cos, sin)
        o_refs[0][r, :, :LANES] = (ra * scale).astype(BF16)
        o_refs[0][r, :, LANES:] = (rb * scale).astype(BF16)


def _epi_nsa_kv(d_model, accs, e_refs, o_refs, j):
    ssq_ref, cos_ref, sin_ref = e_refs
    k_ref, v_ref, vt_ref = o_refs
    acc = _row_scale(accs[0], ssq_ref, d_model)
    cos, sin = cos_ref[...], sin_ref[...]
    for r in range(acc.shape[1] // NSA_KV_W):
        base = r * NSA_KV_W
        ra, rb = _rope_halves(acc[:, base:base + LANES], acc[:, base + LANES:base + NSA_KPAD],
                              cos, sin)
        k_ref[r, :, :LANES] = ra.astype(BF16)
        k_ref[r, :, LANES:] = rb.astype(BF16)
        v = acc[:, base + NSA_KPAD:base + NSA_KV_W]
        v_ref[r] = v.astype(BF16)
        _store_vt_tiles(vt_ref, (r,), v)


def _bias_chunk(bias, start, width):
    period = bias.shape[1]
    if period >= width:
        off = start % period
        return bias[:, off:off + width]
    return jnp.concatenate([bias] * (width // period), axis=1)


def _flash_step_t(q, k, vt, bias, m, l, acc_ref, n_split=1):
    width = q.shape[0] // n_split
    m_out, l_out = [], []
    for c in range(n_split):
        sl = slice(c * width, (c + 1) * width)
        s = lax.dot_general(k, q[sl], NT_DIMS, preferred_element_type=F32)
        if bias is not None:
            s = s + _bias_chunk(bias, c * width, width)
        m_new = jnp.maximum(m[:, sl], jnp.max(s, axis=0, keepdims=True))
        alpha = jnp.exp2(m[:, sl] - m_new)
        p = jnp.exp2(s - m_new)
        l_out.append(alpha * l[:, sl] + jnp.sum(p, axis=0, keepdims=True))
        acc_ref[:, sl] = alpha * acc_ref[:, sl] + jnp.dot(vt, p.astype(BF16),
                                                          preferred_element_type=F32)
        m_out.append(m_new)
    return jnp.concatenate(m_out, axis=1), jnp.concatenate(l_out, axis=1)


def _vt_tiles(vt_ref, lead, c0, n):
    return jnp.concatenate([vt_ref[lead + (c0 + c,)] for c in range(n)], axis=1)


def _mla_attn_kernel(t_blk, seq, qn_ref, qr_ref, kn_ref, kr_ref, vt_ref, o_ref, acc_ref):
    h = pl.program_id(1)
    n_vt = t_blk // VT_TILE
    lane_half = lax.broadcasted_iota(jnp.int32, (t_blk, LANES), 1) // MLA_ROPE
    kpos = lax.broadcasted_iota(jnp.int32, (t_blk, t_blk), 0)
    qpos = lax.broadcasted_iota(jnp.int32, (t_blk, t_blk), 1)
    diag_bias = jnp.where(kpos <= qpos, 0.0, NEG_BIG).astype(F32)

    for qi in range(seq // t_blk):
        q0 = qi * t_blk
        qr = qr_ref[q0:q0 + t_blk, :]
        qr = jnp.where(lane_half == (h % 2), qr, jnp.zeros_like(qr))
        q = jnp.concatenate([qn_ref[q0:q0 + t_blk, :], qr], axis=1)
        acc_ref[...] = jnp.zeros(acc_ref.shape, F32)
        m = jnp.full((1, t_blk), NEG_BIG, F32)
        l = jnp.zeros((1, t_blk), F32)
        for j in range(qi + 1):
            k0 = j * t_blk
            k = jnp.concatenate([kn_ref[k0:k0 + t_blk, :], kr_ref[k0:k0 + t_blk, :]], axis=1)
            vt = _vt_tiles(vt_ref, (0,), j * n_vt, n_vt)
            m, l = _flash_step_t(q, k, vt, diag_bias if j == qi else None, m, l, acc_ref)
        o_ref[q0:q0 + t_blk, :] = (acc_ref[...] / l).T.astype(o_ref.dtype)


def _mla_attention(q, kn, kr, vt, batch, seq, heads, t_blk=512):
    t = batch * seq
    n_nope_blk = heads * MLA_NOPE // LANES
    n_vt = seq // VT_TILE
    kern = functools.partial(_mla_attn_kernel, t_blk, seq)
    return pl.pallas_call(
        kern,
        grid=(batch, heads),
        in_specs=[
            pl.BlockSpec((seq, MLA_NOPE), lambda b, h: (b, h)),
            pl.BlockSpec((seq, LANES), lambda b, h: (b, n_nope_blk + h // 2)),
            pl.BlockSpec((seq, MLA_NOPE), lambda b, h: (b, h)),
            pl.BlockSpec((seq, LANES), lambda b, h: (b, 0)),
            pl.BlockSpec((1, n_vt, MLA_V, VT_TILE), lambda b, h: (h, b, 0, 0)),
        ],
        out_specs=pl.BlockSpec((seq, MLA_V), lambda b, h: (b, h)),
        out_shape=jax.ShapeDtypeStruct((t, heads * MLA_V), BF16),
        scratch_shapes=[pltpu.VMEM((MLA_V, t_blk), F32)],
        compiler_params=_params(("parallel", "parallel")),
        name="mla_attention",
    )(q, q, kn, kr, vt)


def _compress_mlp(a, w1_ref, pe_ref, w2_ref, transposed_out):
    y_lo = jnp.dot(a, w1_ref[0], preferred_element_type=F32)
    y_hi = jnp.dot(a, w1_ref[1], preferred_element_type=F32)
    pe_c = (jnp.dot(pe_ref[0], w1_ref[0], preferred_element_type=F32)
            + jnp.dot(pe_ref[1], w1_ref[1], preferred_element_type=F32))
    n = y_hi.shape[0]
    hid = y_lo + pltpu.roll(y_hi, n - 1, 0) + pe_c[0:1, :]
    hid = jax.nn.gelu(hid, approximate=True)
    if transposed_out:
        return lax.dot_general(w2_ref[...], hid.astype(BF16), NT_DIMS, preferred_element_type=F32)
    return jnp.dot(hid.astype(BF16), w2_ref[...], preferred_element_type=F32)


def _compress_kernel(ak_ref, av_ref, w1k_ref, pek_ref, w2k_ref, w1v_ref, pev_ref, w2vt_ref,
                     kc_ref, vct_ref):
    kc_ref[0] = _compress_mlp(ak_ref[0], w1k_ref, pek_ref, w2k_ref, False).astype(BF16)
    vct_ref[0] = _compress_mlp(av_ref[0], w1v_ref, pev_ref, w2vt_ref, True).astype(BF16)


def _compress(ak, av, wk, wv, batch, seq):
    n_chunk = seq // CMP_STRIDE
    const3 = lambda b, g: (0, 0, 0)
    const2 = lambda b, g: (0, 0)

    def wspecs(w):
        w1, pe, w2 = w
        return [pl.BlockSpec(w1.shape, const3), pl.BlockSpec(pe.shape, const3),
                pl.BlockSpec(w2.shape, const2)]

    return pl.pallas_call(
        _compress_kernel,
        grid=(batch, NSA_G),
        in_specs=[pl.BlockSpec((1, n_chunk, ak.shape[2]), lambda b, g: (g, b, 0)),
                  pl.BlockSpec((1, n_chunk, av.shape[2]), lambda b, g: (g, b, 0))]
        + wspecs(wk) + wspecs(wv),
        out_specs=(pl.BlockSpec((1, n_chunk, NSA_KPAD), lambda b, g: (b * NSA_G + g, 0, 0)),
                   pl.BlockSpec((1, NSA_DV, n_chunk), lambda b, g: (b * NSA_G + g, 0, 0))),
        out_shape=(jax.ShapeDtypeStruct((batch * NSA_G, n_chunk, NSA_KPAD), BF16),
                   jax.ShapeDtypeStruct((batch * NSA_G, NSA_DV, n_chunk), BF16)),
        compiler_params=_params(("parallel", "parallel")),
        name="nsa_compress",
    )(ak, av, *wk, *wv)


def _nsa_attn_kernel(hpg, tq, tk, seq, q_ref, kc_ref, vct_ref, ks_ref, vst_ref, kw_ref, vwt_ref,
                     gate_ref, ovl_ref, o_ref, bias_ref, oc_ref, os_ref, acc_ref):
    qi = pl.program_id(2)
    t0 = qi * tq
    n_rows = hpg * tq
    n_cmp = kc_ref.shape[1]
    n_sel = seq // SEL_BLOCK
    n_vt = tk // VT_TILE
    q = q_ref[...].reshape(n_rows, NSA_KPAD)

    def tile_heads(x):
        return jnp.concatenate([x] * hpg, axis=1)

    s = lax.dot_general(kc_ref[0], q, NT_DIMS, preferred_element_type=F32)
    cidx = lax.broadcasted_iota(jnp.int32, (n_cmp, tq), 0)
    tpos_c = t0 + lax.broadcasted_iota(jnp.int32, (n_cmp, tq), 1)
    mask_c = tile_heads((cidx * CMP_STRIDE + CMP_BLOCK - 1) <= tpos_c)
    s = jnp.where(mask_c, s, NEG_BIG)
    mx = jnp.max(s, axis=0, keepdims=True)
    e = jnp.where(mask_c, jnp.exp2(s - mx), 0.0)
    p = e / jnp.maximum(jnp.sum(e, axis=0, keepdims=True), 1e-30)
    oc_ref[...] = jnp.dot(vct_ref[0], p.astype(BF16), preferred_element_type=F32)

    psum = p[:, 0:tq]
    for h in range(1, hpg):
        psum = psum + p[:, h * tq:(h + 1) * tq]
    p1 = psum.astype(BF16)
    p2 = (psum - p1.astype(F32)).astype(BF16)
    p3 = (psum - p1.astype(F32) - p2.astype(F32)).astype(BF16)
    ovl = ovl_ref[...]
    imp = (jnp.dot(ovl, p1, preferred_element_type=F32)
           + jnp.dot(ovl, p2, preferred_element_type=F32)
           + jnp.dot(ovl, p3, preferred_element_type=F32))
    jrow = lax.broadcasted_iota(jnp.int32, (n_sel, tq), 0)
    tpos = t0 + lax.broadcasted_iota(jnp.int32, (n_sel, tq), 1)
    valid = jrow * SEL_BLOCK <= tpos
    cur = jnp.right_shift(tpos, SEL_BLOCK.bit_length() - 1)
    forced = valid & ((jrow == 0) | (jrow == cur) | (jrow == cur - 1))
    score = jnp.where(valid, imp + jnp.where(forced, FORCE_BONUS, 0.0), -jnp.inf)
    rank = jnp.zeros((n_sel, tq), jnp.int32)
    for i in range(n_sel):
        row = score[i:i + 1, :]
        beats = (row > score) | ((row == score) & (jrow > i))
        rank = rank + beats.astype(jnp.int32)
    sel_bias = jnp.where(valid & (rank < SEL_TOPK), 0.0, NEG_BIG).astype(F32)
    n_pairs = ((t0 + tq - 1) // tk + 2) // 2
    blocks_per_tile = tk // SEL_BLOCK
    for jj in range(seq // tk):
        @pl.when(jj < 2 * n_pairs)
        def _():
            rows = [jnp.broadcast_to(sel_bias[b:b + 1, :], (SEL_BLOCK, tq))
                    for b in range(jj * blocks_per_tile, (jj + 1) * blocks_per_tile)]
            kpos = jj * tk + lax.broadcasted_iota(jnp.int32, (tk, tq), 0)
            tq_pos = t0 + lax.broadcasted_iota(jnp.int32, (tk, tq), 1)
            bias_ref[jj] = jnp.where(kpos <= tq_pos, jnp.concatenate(rows, axis=0), NEG_BIG)

    m0 = jnp.full((1, n_rows), NEG_BIG, F32)
    l0 = jnp.zeros((1, n_rows), F32)

    acc_ref[...] = jnp.zeros(acc_ref.shape, F32)

    def sel_step(j, ml):
        k0 = pl.multiple_of(j * tk, tk)
        k = ks_ref[0, pl.ds(k0, tk), :]
        vt = _vt_tiles(vst_ref, (0,), j * n_vt, n_vt)
        return _flash_step_t(q, k, vt, bias_ref[j], ml[0], ml[1], acc_ref)

    def sel_pair(pair, ml):
        return sel_step(2 * pair + 1, sel_step(2 * pair, ml))

    _, l = lax.fori_loop(0, n_pairs, sel_pair, (m0, l0))
    os_ref[...] = acc_ref[...] / l

    acc_ref[...] = jnp.zeros(acc_ref.shape, F32)
    span = -(-(WINDOW + tq) // tk) * tk
    k_start = jnp.maximum(t0 + tq - span, 0)

    def win_step(c, ml):
        k0 = pl.multiple_of(k_start + c * tk, VT_TILE)
        k = kw_ref[0, pl.ds(k0, tk), :]
        vt = _vt_tiles(vwt_ref, (0,), k0 // VT_TILE, n_vt)
        kpos = k0 + lax.broadcasted_iota(jnp.int32, (tk, tq), 0)
        tq_pos = t0 + lax.broadcasted_iota(jnp.int32, (tk, tq), 1)
        ok = (kpos <= tq_pos) & (tq_pos - kpos < WINDOW)
        bias = jnp.where(ok, 0.0, NEG_BIG).astype(F32)
        return _flash_step_t(q, k, vt, bias, ml[0], ml[1], acc_ref)

    ml = (m0, l0)
    for c in range(span // tk):
        ml = win_step(c, ml)
    o_w = acc_ref[...] / ml[1]

    gates_t = gate_ref[...].T
    for h in range(hpg):
        cols = slice(h * tq, (h + 1) * tq)
        o = (gates_t[3 * h:3 * h + 1, :] * oc_ref[:, cols]
             + gates_t[3 * h + 1:3 * h + 2, :] * os_ref[:, cols]
             + gates_t[3 * h + 2:3 * h + 3, :] * o_w[:, cols])
        o_ref[:, h * NSA_DV:(h + 1) * NSA_DV] = o.T.astype(o_ref.dtype)


def _nsa_attention(q, kc, vct, k3, vt3, gates, ovl, batch, seq, hpg, tq=128, tk=256):
    t = batch * seq
    nq = seq // tq
    n_chunk = seq // CMP_STRIDE
    n_rows = hpg * tq
    n_vt = seq // VT_TILE
    assert (WINDOW + tq) % VT_TILE == 0 and seq >= -(-(WINDOW + tq) // tk) * tk
    kern = functools.partial(_nsa_attn_kernel, hpg, tq, tk, seq)

    def k_spec(branch):
        return pl.BlockSpec((1, seq, NSA_KPAD), lambda b, g, qi: (branch * NSA_G + g, b, 0))

    def vt_spec(branch):
        return pl.BlockSpec((1, n_vt, NSA_DV, VT_TILE), lambda b, g, qi: (branch * NSA_G + g, b, 0, 0))

    return pl.pallas_call(
        kern,
        grid=(batch, NSA_G, nq),
        in_specs=[
            pl.BlockSpec((hpg, tq, NSA_KPAD), lambda b, g, qi: (g, b * nq + qi, 0)),
            pl.BlockSpec((1, n_chunk, NSA_KPAD), lambda b, g, qi: (b * NSA_G + g, 0, 0)),
            pl.BlockSpec((1, NSA_DV, n_chunk), lambda b, g, qi: (b * NSA_G + g, 0, 0)),
            k_spec(1), vt_spec(1), k_spec(2), vt_spec(2),
            pl.BlockSpec((tq, LANES), lambda b, g, qi: (b * nq + qi, g)),
            pl.BlockSpec(ovl.shape, lambda b, g, qi: (0, 0)),
        ],
        out_specs=pl.BlockSpec((tq, hpg * NSA_DV), lambda b, g, qi: (b * nq + qi, g)),
        out_shape=jax.ShapeDtypeStruct((t, NSA_G * hpg * NSA_DV), BF16),
        scratch_shapes=[
            pltpu.VMEM((seq // tk, tk, tq), F32),
            pltpu.VMEM((NSA_DV, n_rows), F32),
            pltpu.VMEM((NSA_DV, n_rows), F32),
            pltpu.VMEM((NSA_DV, n_rows), F32),
        ],
        compiler_params=_params(("parallel", "parallel", "arbitrary")),
        name="nsa_attention",
    )(q, kc, vct, k3, vt3, k3, vt3, gates, ovl)


def _pad_rope_halves(w):
    w = w.reshape(w.shape[:-1] + (2, NSA_HALF))
    w = jnp.pad(w, [(0, 0)] * (w.ndim - 1) + [(0, LANES - NSA_HALF)])
    return w.reshape(w.shape[:-2] + (NSA_KPAD,))


def _rope_tables(pos_flat):
    pos = pos_flat.astype(F32)[:, None]
    inv64 = ROPE_THETA ** (-jnp.arange(0, MLA_ROPE, 2, dtype=F32) / MLA_ROPE)
    ang = pos * inv64
    c, s = jnp.cos(ang), jnp.sin(ang)
    cos_mla = jnp.concatenate([c, c, c, c], axis=1)
    sin_mla = jnp.concatenate([-s, s, -s, s], axis=1)
    inv192 = ROPE_THETA ** (-jnp.arange(0, NSA_DK, 2, dtype=F32) / NSA_DK)
    ang = pos * inv192
    pad = jnp.zeros((pos.shape[0], LANES - NSA_HALF), F32)
    cos_nsa = jnp.concatenate([jnp.cos(ang), pad], axis=1)
    sin_nsa = jnp.concatenate([jnp.sin(ang), pad], axis=1)
    return cos_mla, sin_mla, cos_nsa, sin_nsa


def _compress_weights(w1, pe, w2, *, pad_rope, transpose_w2):
    d_in, d_hid = w1.shape[0] // CMP_BLOCK, w1.shape[1]
    hid_p = -(-d_hid // LANES) * LANES
    w1 = w1.reshape(CMP_BLOCK, d_in, d_hid)
    if pad_rope:
        w1 = jnp.swapaxes(_pad_rope_halves(jnp.swapaxes(w1, 1, 2)), 1, 2)
        pe = _pad_rope_halves(pe)
        w2 = _pad_rope_halves(w2)
    width = w1.shape[1]
    w1 = jnp.pad(w1, ((0, 0), (0, 0), (0, hid_p - d_hid)))
    w1 = w1.reshape(2, CMP_STRIDE * width, hid_p).astype(BF16)
    pe = jnp.pad(pe.reshape(2, 1, CMP_STRIDE * width), ((0, 0), (0, 7), (0, 0))).astype(BF16)
    w2 = jnp.pad(w2, ((0, hid_p - d_hid), (0, 0))).astype(BF16)
    return w1, pe, (w2.T if transpose_w2 else w2)


def kernel(x, positions, a_norm, a_w_in, a_q_norm, a_w_uq, a_kv_norm, a_w_ukv, a_w_o, s_norm, s_w_kv, s_cmp_pe_k, s_cmp_w1_k, s_cmp_w2_k, s_cmp_pe_v, s_cmp_w1_v, s_cmp_w2_v, b_norm, b_w_in, b_w_o, f_norm, f_w_in, f_w_out, final_norm):
    batch, seq, d = x.shape
    t = batch * seq
    heads = d // 128
    hpg = heads // NSA_G
    d_ff = f_w_out.shape[1]
    n_a, n_b = a_w_in.shape[0], b_w_in.shape[0]
    assert n_a >= 1 and n_b >= 1
    tm = 1024
    sds = jax.ShapeDtypeStruct

    cos_mla, sin_mla, cos_nsa, sin_nsa = _rope_tables(positions.reshape(t))
    row = lambda i, j, k: (i, 0)
    tile = lambda i, j, k: (i, j)
    tab_spec = pl.BlockSpec((tm, LANES), row)
    ssq_spec = pl.BlockSpec((tm, LANES), row)
    f_w_out_b = f_w_out.astype(BF16)

    def residual_mm(a, w, w_layer, h, gains, *, tm_, tn, name):
        blk = pl.BlockSpec((tm_, tn), tile)
        common = dict(tm=tm_, tn=tn, tk=a.shape[1], nj=d // tn, w_layer=w_layer, name=name)
        if not gains:
            h_new = _mm(a, w, _epi_residual, sds((t, d), F32), blk,
                        extras=(h,), extra_specs=(blk,), **common)
            return h_new, None, ()
        outs = _mm(a, w, _epi_residual_prenorm,
                   (sds((t, d), F32), sds((t, LANES), F32)) + tuple(sds((t, d), BF16) for _ in gains),
                   (blk, pl.BlockSpec((tm_, LANES), row)) + tuple(blk for _ in gains),
                   extras=(h,) + tuple(g.reshape(1, d) for g in gains),
                   extra_specs=(blk,) + tuple(pl.BlockSpec((1, tn), lambda i, j, k: (0, j))
                                              for _ in gains),
                   **common)
        return outs[0], outs[1], tuple(outs[2:])

    def ffn(hb, ssq, h, layer, gains):
        tn = 256
        nj = d_ff // tn
        tm_in = min(2 * tm, t)
        act = _mm(hb, f_w_in, functools.partial(_epi_swiglu, d), sds((t, d_ff), BF16),
                  pl.BlockSpec((tm_in, tn), tile),
                  tm=tm_in, tn=tn, tk=d, nj=nj, w_layer=layer, a_buffers=1,
                  w_col_maps=(lambda j: j, lambda j: j + nj),
                  extras=(ssq,), extra_specs=(pl.BlockSpec((tm_in, LANES), row),), name="ffn_in")
        return residual_mm(act, f_w_out_b, layer, h, gains, tm_=512, tn=tn, name="ffn_out")

    h = x.reshape(t, d)
    hb, ssq = _prenorm(h, a_norm[0])

    for i in range(n_a):
        rope_lo = MLA_Q_LORA + MLA_KV_LORA
        w_in = jnp.concatenate([a_w_in[i], a_w_in[i][:, rope_lo:]], axis=1).astype(BF16)
        n_in = w_in.shape[1]
        cq, ckv, kr = _mm(
            hb, w_in, functools.partial(_epi_mla_in, d),
            (sds((t, MLA_Q_LORA), BF16), sds((t, MLA_KV_LORA), BF16), sds((t, LANES), BF16)),
            (pl.BlockSpec((tm, MLA_Q_LORA), row), pl.BlockSpec((tm, MLA_KV_LORA), row),
             pl.BlockSpec((tm, LANES), row)),
            tm=tm, tn=n_in, tk=1024, nj=1,
            extras=(ssq, a_q_norm[i].reshape(1, -1), a_kv_norm[i].reshape(1, -1), cos_mla, sin_mla),
            extra_specs=(ssq_spec,
                         pl.BlockSpec((1, MLA_Q_LORA), lambda i_, j, k: (0, 0)),
                         pl.BlockSpec((1, MLA_KV_LORA), lambda i_, j, k: (0, 0)),
                         tab_spec, tab_spec),
            name="mla_in")

        qd = MLA_NOPE + MLA_ROPE
        w_uq = a_w_uq[i].reshape(MLA_Q_LORA, heads, qd)
        w_uq = jnp.concatenate([w_uq[:, :, :MLA_NOPE].reshape(MLA_Q_LORA, heads * MLA_NOPE),
                                w_uq[:, :, MLA_NOPE:].reshape(MLA_Q_LORA, heads * MLA_ROPE)],
                               axis=1).astype(BF16)
        tn = 512
        q = _mm(cq, w_uq,
                functools.partial(_epi_mla_q, heads * MLA_NOPE // tn, qd ** -0.5 * LOG2_E),
                jax.ShapeDtypeStruct((t, heads * qd), BF16),
                pl.BlockSpec((tm, tn), lambda i_, j, k: (i_, j)),
                tm=tm, tn=tn, tk=MLA_Q_LORA, nj=heads * qd // tn,
                extras=(cos_mla, sin_mla), extra_specs=(tab_spec, tab_spec), name="mla_q")
        heads_per_tile = 4
        kn, vt = _mm(ckv, a_w_ukv[i].astype(BF16), _epi_mla_kv,
                     (jax.ShapeDtypeStruct((t, heads * MLA_NOPE), BF16),
                      jax.ShapeDtypeStruct((heads, t // VT_TILE, MLA_V, VT_TILE), BF16)),
                     (pl.BlockSpec((tm, heads_per_tile * MLA_NOPE), lambda i_, j, k: (i_, j)),
                      pl.BlockSpec((heads_per_tile, tm // VT_TILE, MLA_V, VT_TILE),
                                   lambda i_, j, k: (j, i_, 0, 0))),
                     tm=tm, tn=heads_per_tile * (MLA_NOPE + MLA_V), tk=MLA_KV_LORA,
                     nj=heads // heads_per_tile, name="mla_kv")
        o = _mla_attention(q, kn, kr, vt, batch, seq, heads)
        h, ssq, (hb,) = residual_mm(o, a_w_o[i].astype(BF16), None, h, [f_norm[i]],
                                    tm_=tm, tn=512, name="mla_out")
        gains = [a_norm[i + 1]] if i + 1 < n_a else [s_norm, b_norm[0]]
        h, ssq, hbs = ffn(hb, ssq, h, i, gains)
        hb = hbs[0]

    hb_s, hb_b = hbs
    w_kv = s_w_kv.reshape(d, 3 * NSA_G, NSA_DK + NSA_DV)
    w_kv = jnp.concatenate([_pad_rope_halves(w_kv[:, :, :NSA_DK]), w_kv[:, :, NSA_DK:]], axis=2)
    w_kv = w_kv.reshape(d, 3 * NSA_G * NSA_KV_W).astype(BF16)
    per_tile = 2
    k3, v3, vt3 = _mm(
        hb_s, w_kv, functools.partial(_epi_nsa_kv, d),
        (sds((3 * NSA_G, t, NSA_KPAD), BF16), sds((3 * NSA_G, t, NSA_DV), BF16),
         sds((3 * NSA_G, t // VT_TILE, NSA_DV, VT_TILE), BF16)),
        (pl.BlockSpec((per_tile, tm, NSA_KPAD), lambda i_, j, k: (j, i_, 0)),
         pl.BlockSpec((per_tile, tm, NSA_DV), lambda i_, j, k: (j, i_, 0)),
         pl.BlockSpec((per_tile, tm // VT_TILE, NSA_DV, VT_TILE), lambda i_, j, k: (j, i_, 0, 0))),
        tm=tm, tn=per_tile * NSA_KV_W, tk=d, nj=3 * NSA_G // per_tile,
        extras=(ssq, cos_nsa, sin_nsa), extra_specs=(ssq_spec, tab_spec, tab_spec), name="nsa_kv")

    ak = k3[:NSA_G].reshape(NSA_G, t // CMP_STRIDE, CMP_STRIDE * NSA_KPAD)
    av = v3[:NSA_G].reshape(NSA_G, t // CMP_STRIDE, CMP_STRIDE * NSA_DV)
    kc, vct = _compress(ak, av,
                        _compress_weights(s_cmp_w1_k, s_cmp_pe_k, s_cmp_w2_k,
                                          pad_rope=True, transpose_w2=False),
                        _compress_weights(s_cmp_w1_v, s_cmp_pe_v, s_cmp_w2_v,
                                          pad_rope=False, transpose_w2=True),
                        batch, seq)

    n_chunk = seq // CMP_STRIDE
    n_sel = seq // SEL_BLOCK
    c_start = np.arange(n_chunk) * CMP_STRIDE
    j_start = np.arange(n_sel) * SEL_BLOCK
    ovl = ((c_start[None, :] < j_start[:, None] + SEL_BLOCK)
           & (c_start[None, :] + CMP_BLOCK > j_start[:, None])
           & (np.arange(n_chunk)[None, :] < (seq - CMP_BLOCK) // CMP_STRIDE + 1))
    ovl = jnp.asarray(ovl, BF16)

    for i in range(n_b):
        n_q = heads * NSA_DK
        w_q = _pad_rope_halves(b_w_in[i][:, :n_q].reshape(d, heads, NSA_DK))
        w_q = w_q.reshape(d, heads * NSA_KPAD).astype(BF16)
        w_g = b_w_in[i][:, n_q:].reshape(d, NSA_G, 3 * hpg)
        w_g = jnp.pad(w_g, ((0, 0), (0, 0), (0, LANES - 3 * hpg))).reshape(d, NSA_G * LANES)
        w_g = w_g.astype(BF16)
        heads_per_tile = 4
        tn = heads_per_tile * NSA_KPAD
        q = _mm(hb_b, w_q, functools.partial(_epi_nsa_q, d, NSA_DK ** -0.5 * LOG2_E),
                sds((heads, t, NSA_KPAD), BF16),
                pl.BlockSpec((heads_per_tile, tm, NSA_KPAD), lambda i_, j, k: (j, i_, 0)),
                tm=tm, tn=tn, tk=d, nj=heads // heads_per_tile,
                extras=(ssq, cos_nsa, sin_nsa), extra_specs=(ssq_spec, tab_spec, tab_spec),
                name="nsa_q")
        gates = _mm(hb_b, w_g, functools.partial(_epi_sigmoid, d), sds((t, NSA_G * LANES), F32),
                    pl.BlockSpec((tm, NSA_G * LANES), tile),
                    tm=tm, tn=NSA_G * LANES, tk=d, nj=1,
                    extras=(ssq,), extra_specs=(ssq_spec,), name="nsa_gates")
        o = _nsa_attention(q, kc, vct, k3, vt3, gates, ovl, batch, seq, hpg)
        h, ssq, (hb,) = residual_mm(o, b_w_o[i].astype(BF16), None, h, [f_norm[n_a + i]],
                                    tm_=tm, tn=512, name="nsa_out")
        gains = [b_norm[i + 1]] if i + 1 < n_b else []
        h, ssq, hbs = ffn(hb, ssq, h, n_a + i, gains)
        if hbs:
            hb_b = hbs[0]

    out = _rmsnorm(h, final_norm, F32)
    return out.reshape(batch, seq, d)
```

```python
import functools

import numpy as np
import jax
import jax.numpy as jnp
from jax import lax
from jax.experimental import pallas as pl
from jax.experimental.pallas import tpu as pltpu

F32 = jnp.float32
BF16 = jnp.bfloat16

LANES = 128
VMEM_LIMIT_BYTES = 56 * 1024 * 1024

NORM_EPS = 1e-6
ROPE_THETA = 10000.0

MLA_Q_LORA = 1536
MLA_KV_LORA = 512
MLA_NOPE = 128
MLA_ROPE = 64
MLA_V = 128

NSA_G = 4
NSA_DK = 192
NSA_DV = 128
NSA_HALF = NSA_DK // 2
NSA_ROPE_TILE = 4 * NSA_DK
CMP_BLOCK = 32
CMP_STRIDE = 16
SEL_BLOCK = 64
SEL_TOPK = 16
WINDOW = 512
FORCE_BONUS = 1e4
NEG_BIG = -1e30
LOG2_E = 1.4426950408889634

VT_TILE = LANES
NT_DIMS = (((1,), (1,)), ((), ()))


def _params(sem):
    return pltpu.CompilerParams(dimension_semantics=sem, vmem_limit_bytes=VMEM_LIMIT_BYTES)


def _rmsnorm_kernel(x_ref, g_ref, o_ref):
    x = x_ref[...]
    ms = jnp.mean(x * x, axis=-1, keepdims=True)
    o_ref[...] = (x * lax.rsqrt(ms + NORM_EPS) * g_ref[...]).astype(o_ref.dtype)


def _rmsnorm(x, g, out_dtype, tm=256):
    t, d = x.shape
    return pl.pallas_call(
        _rmsnorm_kernel,
        grid=(t // tm,),
        in_specs=[pl.BlockSpec((tm, d), lambda i: (i, 0)),
                  pl.BlockSpec((1, d), lambda i: (0, 0))],
        out_specs=pl.BlockSpec((tm, d), lambda i: (i, 0)),
        out_shape=jax.ShapeDtypeStruct((t, d), out_dtype),
        compiler_params=_params(("parallel",)),
        name="rmsnorm",
    )(x, g.reshape(1, d))


def _prenorm_kernel(x_ref, g_ref, hb_ref, ssq_ref):
    x = x_ref[...]
    hb_ref[...] = (x * g_ref[...]).astype(hb_ref.dtype)
    ssq_ref[...] = jnp.broadcast_to(jnp.sum(x * x, axis=-1, keepdims=True), ssq_ref.shape)


def _prenorm(x, g, tm=256):
    t, d = x.shape
    return pl.pallas_call(
        _prenorm_kernel,
        grid=(t // tm,),
        in_specs=[pl.BlockSpec((tm, d), lambda i: (i, 0)),
                  pl.BlockSpec((1, d), lambda i: (0, 0))],
        out_specs=(pl.BlockSpec((tm, d), lambda i: (i, 0)),
                   pl.BlockSpec((tm, LANES), lambda i: (i, 0))),
        out_shape=(jax.ShapeDtypeStruct((t, d), BF16), jax.ShapeDtypeStruct((t, LANES), F32)),
        compiler_params=_params(("parallel",)),
        name="prenorm",
    )(x, g.reshape(1, d))


def _mm_kernel(n_w, n_e, n_o, nk, epilogue, *refs):
    a_ref = refs[0]
    w_refs = refs[1:1 + n_w]
    e_refs = refs[1 + n_w:1 + n_w + n_e]
    o_refs = refs[1 + n_w + n_e:1 + n_w + n_e + n_o]
    acc_refs = refs[1 + n_w + n_e + n_o:]
    j = pl.program_id(1)

    def product(w):
        return jnp.dot(a_ref[...], w[...].astype(a_ref.dtype), preferred_element_type=F32)

    if nk == 1:
        epilogue([product(w) for w in w_refs], e_refs, o_refs, j)
        return
    k = pl.program_id(2)

    @pl.when(k == 0)
    def _():
        for w, acc in zip(w_refs, acc_refs):
            acc[...] = product(w)

    @pl.when(k > 0)
    def _():
        for w, acc in zip(w_refs, acc_refs):
            acc[...] += product(w)

    @pl.when(k == nk - 1)
    def _():
        epilogue([acc[...] for acc in acc_refs], e_refs, o_refs, j)


def _mm(a, w, epilogue, out_shape, out_specs, *, tm, tn, tk, nj, w_col_maps=None, w_layer=None,
        extras=(), extra_specs=(), a_buffers=None, name="matmul"):
    m, kdim = a.shape
    ni, nk = m // tm, kdim // tk
    assert ni * tm == m and nk * tk == kdim
    if w_col_maps is None:
        w_col_maps = (lambda j: j,)
    n_w = len(w_col_maps)
    single = not isinstance(out_shape, (tuple, list))
    out_shapes = (out_shape,) if single else tuple(out_shape)
    out_specs_t = (out_specs,) if single else tuple(out_specs)
    a_kwargs = {} if a_buffers is None else {"pipeline_mode": pl.Buffered(a_buffers)}
    in_specs = [pl.BlockSpec((tm, tk), lambda i, j, k: (i, k), **a_kwargs)]
    for cm in w_col_maps:
        if w_layer is None:
            in_specs.append(pl.BlockSpec((tk, tn), functools.partial(
                lambda i, j, k, cm: (k, cm(j)), cm=cm)))
        else:
            in_specs.append(pl.BlockSpec((None, tk, tn), functools.partial(
                lambda i, j, k, cm: (w_layer, k, cm(j)), cm=cm)))
    in_specs += list(extra_specs)
    scratch = [pltpu.VMEM((tm, tn), F32) for _ in range(n_w)] if nk > 1 else []
    kern = functools.partial(_mm_kernel, n_w, len(extras), len(out_shapes), nk, epilogue)
    res = pl.pallas_call(
        kern,
        grid=(ni, nj, nk),
        in_specs=in_specs,
        out_specs=out_specs_t,
        out_shape=out_shapes,
        scratch_shapes=scratch,
        compiler_params=_params(("parallel", "arbitrary", "arbitrary")),
        name=name,
    )(a, *([w] * n_w), *extras)
    return res[0] if single else res


def _rope_pairs_64(x, cos2, sin2):
    lane = lax.broadcasted_iota(jnp.int32, x.shape, 1)
    first_half = (lane % MLA_ROPE) < (MLA_ROPE // 2)
    partner = jnp.where(first_half,
                        pltpu.roll(x, LANES - MLA_ROPE // 2, 1),
                        pltpu.roll(x, MLA_ROPE // 2, 1))
    return x * cos2 + partner * sin2


def _store_vt_tiles(vt_ref, lead, v):
    for c in range(v.shape[0] // VT_TILE):
        vt_ref[lead + (c,)] = v[c * VT_TILE:(c + 1) * VT_TILE, :].T.astype(vt_ref.dtype)


def _row_scale(acc, ssq_ref, d_model):
    rstd = lax.rsqrt(ssq_ref[...] * (1.0 / d_model) + NORM_EPS)
    return acc * jnp.concatenate([rstd] * (acc.shape[1] // LANES), axis=1)


def _epi_residual(accs, e_refs, o_refs, j):
    o_refs[0][...] = e_refs[0][...] + accs[0]


def _epi_residual_prenorm(accs, e_refs, o_refs, j):
    h = e_refs[0][...] + accs[0]
    o_refs[0][...] = h
    ssq_ref = o_refs[1]
    part = jnp.broadcast_to(jnp.sum(h * h, axis=-1, keepdims=True), ssq_ref.shape)

    @pl.when(j == 0)
    def _():
        ssq_ref[...] = part

    @pl.when(j > 0)
    def _():
        ssq_ref[...] += part

    for g_ref, hb_ref in zip(e_refs[1:], o_refs[2:]):
        hb_ref[...] = (h * g_ref[...]).astype(hb_ref.dtype)


def _epi_swiglu(d_model, accs, e_refs, o_refs, j):
    a = _row_scale(accs[0], e_refs[0], d_model)
    b = _row_scale(accs[1], e_refs[0], d_model)
    o_refs[0][...] = (a * jax.nn.sigmoid(a) * b).astype(o_refs[0].dtype)


def _epi_sigmoid(d_model, accs, e_refs, o_refs, j):
    o_refs[0][...] = jax.nn.sigmoid(_row_scale(accs[0], e_refs[0], d_model))


def _epi_mla_in(d_model, accs, e_refs, o_refs, j):
    ssq_ref, gq_ref, gkv_ref, cos_ref, sin_ref = e_refs
    u = _row_scale(accs[0], ssq_ref, d_model)
    cq = u[:, :MLA_Q_LORA]
    ms = jnp.mean(cq * cq, axis=-1, keepdims=True)
    o_refs[0][...] = (cq * lax.rsqrt(ms + NORM_EPS) * gq_ref[...]).astype(BF16)
    ckv = u[:, MLA_Q_LORA:MLA_Q_LORA + MLA_KV_LORA]
    ms = jnp.mean(ckv * ckv, axis=-1, keepdims=True)
    o_refs[1][...] = (ckv * lax.rsqrt(ms + NORM_EPS) * gkv_ref[...]).astype(BF16)
    kr = u[:, MLA_Q_LORA + MLA_KV_LORA:]
    o_refs[2][...] = _rope_pairs_64(kr, cos_ref[...], sin_ref[...]).astype(BF16)


def _epi_mla_q(n_nope_tiles, scale, accs, e_refs, o_refs, j):
    cos_ref, sin_ref = e_refs
    acc = accs[0]

    @pl.when(j < n_nope_tiles)
    def _():
        o_refs[0][...] = (acc * scale).astype(BF16)

    @pl.when(j >= n_nope_tiles)
    def _():
        for c in range(acc.shape[1] // LANES):
            x = acc[:, c * LANES:(c + 1) * LANES]
            r = _rope_pairs_64(x, cos_ref[...], sin_ref[...])
            o_refs[0][:, c * LANES:(c + 1) * LANES] = (r * scale).astype(BF16)


def _epi_mla_kv(accs, e_refs, o_refs, j):
    acc = accs[0]
    kn_ref, vt_ref = o_refs
    per = MLA_NOPE + MLA_V
    for r in range(acc.shape[1] // per):
        kn_ref[:, r * MLA_NOPE:(r + 1) * MLA_NOPE] = acc[:, r * per:r * per + MLA_NOPE].astype(BF16)
        _store_vt_tiles(vt_ref, (r,), acc[:, r * per + MLA_NOPE:(r + 1) * per])


def _rope_192(x, cos, sin):
    width = x.shape[1]
    lane = lax.broadcasted_iota(jnp.int32, x.shape, 1)
    partner = jnp.where((lane % NSA_DK) < NSA_HALF,
                        pltpu.roll(x, width - NSA_HALF, 1),
                        pltpu.roll(x, NSA_HALF, 1))
    return x * cos + partner * sin


def _epi_nsa_rope(d_model, scale, accs, e_refs, o_refs, j):
    ssq_ref, cos_ref, sin_ref = e_refs
    acc = _row_scale(accs[0], ssq_ref, d_model)
    roped = _rope_192(acc, cos_ref[...], sin_ref[...])
    if scale is not None:
        roped = roped * scale
    for r in range(acc.shape[1] // NSA_DK):
        o_refs[0][r] = roped[:, r * NSA_DK:(r + 1) * NSA_DK].astype(BF16)


def _epi_nsa_v(d_model, accs, e_refs, o_refs, j):
    v_ref, vt_ref = o_refs
    acc = _row_scale(accs[0], e_refs[0], d_model)
    for r in range(acc.shape[1] // NSA_DV):
        v = acc[:, r * NSA_DV:(r + 1) * NSA_DV]
        v_ref[r] = v.astype(BF16)
        _store_vt_tiles(vt_ref, (r,), v)


def _bias_chunk(bias, start, width):
    period = bias.shape[1]
    if period >= width:
        off = start % period
        return bias[:, off:off + width]
    return jnp.concatenate([bias] * (width // period), axis=1)


def _flash_step_t(q, k, vt, bias, m, l, acc_ref, n_split=1):
    width = q.shape[0] // n_split
    m_out, l_out = [], []
    for c in range(n_split):
        sl = slice(c * width, (c + 1) * width)
        s = lax.dot_general(k, q[sl], NT_DIMS, preferred_element_type=F32)
        if bias is not None:
            s = s + _bias_chunk(bias, c * width, width)
        m_new = jnp.maximum(m[:, sl], jnp.max(s, axis=0, keepdims=True))
        alpha = jnp.exp2(m[:, sl] - m_new)
        p = jnp.exp2(s - m_new)
        l_out.append(alpha * l[:, sl] + jnp.sum(p, axis=0, keepdims=True))
        acc_ref[:, sl] = alpha * acc_ref[:, sl] + jnp.dot(vt, p.astype(BF16),
                                                          preferred_element_type=F32)
        m_out.append(m_new)
    return jnp.concatenate(m_out, axis=1), jnp.concatenate(l_out, axis=1)


def _vt_tiles(vt_ref, lead, c0, n):
    return jnp.concatenate([vt_ref[lead + (c0 + c,)] for c in range(n)], axis=1)


def _mla_attn_kernel(t_blk, seq, qn_ref, qr_ref, kn_ref, kr_ref, vt_ref, o_ref, acc_ref):
    h = pl.program_id(1)
    n_vt = t_blk // VT_TILE
    lane_half = lax.broadcasted_iota(jnp.int32, (t_blk, LANES), 1) // MLA_ROPE
    kpos = lax.broadcasted_iota(jnp.int32, (t_blk, t_blk), 0)
    qpos = lax.broadcasted_iota(jnp.int32, (t_blk, t_blk), 1)
    diag_bias = jnp.where(kpos <= qpos, 0.0, NEG_BIG).astype(F32)

    for qi in range(seq // t_blk):
        q0 = qi * t_blk
        qr = qr_ref[q0:q0 + t_blk, :]
        qr = jnp.where(lane_half == (h % 2), qr, jnp.zeros_like(qr))
        q = jnp.concatenate([qn_ref[q0:q0 + t_blk, :], qr], axis=1)
        acc_ref[...] = jnp.zeros(acc_ref.shape, F32)
        m = jnp.full((1, t_blk), NEG_BIG, F32)
        l = jnp.zeros((1, t_blk), F32)
        for j in range(qi + 1):
            k0 = j * t_blk
            k = jnp.concatenate([kn_ref[k0:k0 + t_blk, :], kr_ref[k0:k0 + t_blk, :]], axis=1)
            vt = _vt_tiles(vt_ref, (0,), j * n_vt, n_vt)
            m, l = _flash_step_t(q, k, vt, diag_bias if j == qi else None, m, l, acc_ref)
        o_ref[q0:q0 + t_blk, :] = (acc_ref[...] / l).T.astype(o_ref.dtype)


def _mla_attention(q, kn, kr, vt, batch, seq, heads, t_blk=512):
    t = batch * seq
    n_nope_blk = heads * MLA_NOPE // LANES
    n_vt = seq // VT_TILE
    kern = functools.partial(_mla_attn_kernel, t_blk, seq)
    return pl.pallas_call(
        kern,
        grid=(batch, heads),
        in_specs=[
            pl.BlockSpec((seq, MLA_NOPE), lambda b, h: (b, h)),
            pl.BlockSpec((seq, LANES), lambda b, h: (b, n_nope_blk + h // 2)),
            pl.BlockSpec((seq, MLA_NOPE), lambda b, h: (b, h)),
            pl.BlockSpec((seq, LANES), lambda b, h: (b, 0)),
            pl.BlockSpec((1, n_vt, MLA_V, VT_TILE), lambda b, h: (h, b, 0, 0)),
        ],
        out_specs=pl.BlockSpec((seq, MLA_V), lambda b, h: (b, h)),
        out_shape=jax.ShapeDtypeStruct((t, heads * MLA_V), BF16),
        scratch_shapes=[pltpu.VMEM((MLA_V, t_blk), F32)],
        compiler_params=_params(("parallel", "parallel")),
        name="mla_attention",
    )(q, q, kn, kr, vt)


def _compress_mlp(a, w1_ref, pe_ref, w2_ref, transposed_out):
    y_lo = jnp.dot(a, w1_ref[0], preferred_element_type=F32)
    y_hi = jnp.dot(a, w1_ref[1], preferred_element_type=F32)
    pe_c = (jnp.dot(pe_ref[0], w1_ref[0], preferred_element_type=F32)
            + jnp.dot(pe_ref[1], w1_ref[1], preferred_element_type=F32))
    n = y_hi.shape[0]
    hid = y_lo + pltpu.roll(y_hi, n - 1, 0) + pe_c[0:1, :]
    hid = jax.nn.gelu(hid, approximate=True)
    if transposed_out:
        return lax.dot_general(w2_ref[...], hid.astype(BF16), NT_DIMS, preferred_element_type=F32)
    return jnp.dot(hid.astype(BF16), w2_ref[...], preferred_element_type=F32)


def _compress_kernel(ak_ref, av_ref, w1k_ref, pek_ref, w2k_ref, w1v_ref, pev_ref, w2vt_ref,
                     kc_ref, vct_ref):
    kc_ref[0] = _compress_mlp(ak_ref[0], w1k_ref, pek_ref, w2k_ref, False).astype(BF16)
    vct_ref[0] = _compress_mlp(av_ref[0], w1v_ref, pev_ref, w2vt_ref, True).astype(BF16)


def _compress(ak, av, wk, wv, batch, seq):
    n_chunk = seq // CMP_STRIDE
    const3 = lambda b, g: (0, 0, 0)
    const2 = lambda b, g: (0, 0)

    def wspecs(w):
        w1, pe, w2 = w
        return [pl.BlockSpec(w1.shape, const3), pl.BlockSpec(pe.shape, const3),
                pl.BlockSpec(w2.shape, const2)]

    return pl.pallas_call(
        _compress_kernel,
        grid=(batch, NSA_G),
        in_specs=[pl.BlockSpec((1, n_chunk, ak.shape[2]), lambda b, g: (g, b, 0)),
                  pl.BlockSpec((1, n_chunk, av.shape[2]), lambda b, g: (g, b, 0))]
        + wspecs(wk) + wspecs(wv),
        out_specs=(pl.BlockSpec((1, n_chunk, NSA_DK), lambda b, g: (b * NSA_G + g, 0, 0)),
                   pl.BlockSpec((1, NSA_DV, n_chunk), lambda b, g: (b * NSA_G + g, 0, 0))),
        out_shape=(jax.ShapeDtypeStruct((batch * NSA_G, n_chunk, NSA_DK), BF16),
                   jax.ShapeDtypeStruct((batch * NSA_G, NSA_DV, n_chunk), BF16)),
        compiler_params=_params(("parallel", "parallel")),
        name="nsa_compress",
    )(ak, av, *wk, *wv)


def _nsa_attn_kernel(hpg, tq, tk, seq, q_ref, kc_ref, vct_ref, ks_ref, vst_ref, kw_ref, vwt_ref,
                     gate_ref, ovl_ref, o_ref, bias_ref, oc_ref, os_ref, acc_ref):
    qi = pl.program_id(2)
    t0 = qi * tq
    n_rows = hpg * tq
    n_cmp = kc_ref.shape[1]
    n_sel = seq // SEL_BLOCK
    n_vt = tk // VT_TILE
    q = q_ref[...].reshape(n_rows, NSA_DK)

    def tile_heads(x):
        return jnp.concatenate([x] * hpg, axis=1)

    s = lax.dot_general(kc_ref[0], q, NT_DIMS, preferred_element_type=F32)
    cidx = lax.broadcasted_iota(jnp.int32, (n_cmp, tq), 0)
    tpos_c = t0 + lax.broadcasted_iota(jnp.int32, (n_cmp, tq), 1)
    mask_c = tile_heads((cidx * CMP_STRIDE + CMP_BLOCK - 1) <= tpos_c)
    s = jnp.where(mask_c, s, NEG_BIG)
    mx = jnp.max(s, axis=0, keepdims=True)
    e = jnp.where(mask_c, jnp.exp2(s - mx), 0.0)
    p = e / jnp.maximum(jnp.sum(e, axis=0, keepdims=True), 1e-30)
    oc_ref[...] = jnp.dot(vct_ref[0], p.astype(BF16), preferred_element_type=F32)

    psum = p[:, 0:tq]
    for h in range(1, hpg):
        psum = psum + p[:, h * tq:(h + 1) * tq]
    p1 = psum.astype(BF16)
    p2 = (psum - p1.astype(F32)).astype(BF16)
    p3 = (psum - p1.astype(F32) - p2.astype(F32)).astype(BF16)
    ovl = ovl_ref[...]
    imp = (jnp.dot(ovl, p1, preferred_element_type=F32)
           + jnp.dot(ovl, p2, preferred_element_type=F32)
           + jnp.dot(ovl, p3, preferred_element_type=F32))
    jrow = lax.broadcasted_iota(jnp.int32, (n_sel, tq), 0)
    tpos = t0 + lax.broadcasted_iota(jnp.int32, (n_sel, tq), 1)
    valid = jrow * SEL_BLOCK <= tpos
    cur = jnp.right_shift(tpos, SEL_BLOCK.bit_length() - 1)
    forced = valid & ((jrow == 0) | (jrow == cur) | (jrow == cur - 1))
    score = jnp.where(valid, imp + jnp.where(forced, FORCE_BONUS, 0.0), -jnp.inf)
    rank = jnp.zeros((n_sel, tq), jnp.int32)
    for i in range(n_sel):
        row = score[i:i + 1, :]
        beats = (row > score) | ((row == score) & (jrow > i))
        rank = rank + beats.astype(jnp.int32)
    sel_bias = jnp.where(valid & (rank < SEL_TOPK), 0.0, NEG_BIG).astype(F32)
    n_pairs = ((t0 + tq - 1) // tk + 2) // 2
    blocks_per_tile = tk // SEL_BLOCK
    for jj in range(seq // tk):
        @pl.when(jj < 2 * n_pairs)
        def _():
            rows = [jnp.broadcast_to(sel_bias[b:b + 1, :], (SEL_BLOCK, tq))
                    for b in range(jj * blocks_per_tile, (jj + 1) * blocks_per_tile)]
            kpos = jj * tk + lax.broadcasted_iota(jnp.int32, (tk, tq), 0)
            tq_pos = t0 + lax.broadcasted_iota(jnp.int32, (tk, tq), 1)
            bias_ref[jj] = jnp.where(kpos <= tq_pos, jnp.concatenate(rows, axis=0), NEG_BIG)

    m0 = jnp.full((1, n_rows), NEG_BIG, F32)
    l0 = jnp.zeros((1, n_rows), F32)

    acc_ref[...] = jnp.zeros(acc_ref.shape, F32)

    def sel_step(j, ml):
        k0 = pl.multiple_of(j * tk, tk)
        k = ks_ref[0, pl.ds(k0, tk), :]
        vt = _vt_tiles(vst_ref, (0,), j * n_vt, n_vt)
        return _flash_step_t(q, k, vt, bias_ref[j], ml[0], ml[1], acc_ref)

    def sel_pair(pair, ml):
        return sel_step(2 * pair + 1, sel_step(2 * pair, ml))

    _, l = lax.fori_loop(0, n_pairs, sel_pair, (m0, l0))
    os_ref[...] = acc_ref[...] / l

    acc_ref[...] = jnp.zeros(acc_ref.shape, F32)
    span = -(-(WINDOW + tq) // tk) * tk
    k_start = jnp.maximum(t0 + tq - span, 0)

    def win_step(c, ml):
        k0 = pl.multiple_of(k_start + c * tk, VT_TILE)
        k = kw_ref[0, pl.ds(k0, tk), :]
        vt = _vt_tiles(vwt_ref, (0,), k0 // VT_TILE, n_vt)
        kpos = k0 + lax.broadcasted_iota(jnp.int32, (tk, tq), 0)
        tq_pos = t0 + lax.broadcasted_iota(jnp.int32, (tk, tq), 1)
        ok = (kpos <= tq_pos) & (tq_pos - kpos < WINDOW)
        bias = jnp.where(ok, 0.0, NEG_BIG).astype(F32)
        return _flash_step_t(q, k, vt, bias, ml[0], ml[1], acc_ref)

    ml = (m0, l0)
    for c in range(span // tk):
        ml = win_step(c, ml)
    o_w = acc_ref[...] / ml[1]

    gates_t = gate_ref[...].T
    for h in range(hpg):
        cols = slice(h * tq, (h + 1) * tq)
        o = (gates_t[3 * h:3 * h + 1, :] * oc_ref[:, cols]
             + gates_t[3 * h + 1:3 * h + 2, :] * os_ref[:, cols]
             + gates_t[3 * h + 2:3 * h + 3, :] * o_w[:, cols])
        o_ref[:, h * NSA_DV:(h + 1) * NSA_DV] = o.T.astype(o_ref.dtype)


def _nsa_attention(q, kc, vct, k3, vt3, gates, ovl, batch, seq, hpg, tq=128, tk=256):
    t = batch * seq
    nq = seq // tq
    n_chunk = seq // CMP_STRIDE
    n_rows = hpg * tq
    n_vt = seq // VT_TILE
    assert (WINDOW + tq) % VT_TILE == 0 and seq >= -(-(WINDOW + tq) // tk) * tk
    kern = functools.partial(_nsa_attn_kernel, hpg, tq, tk, seq)

    def k_spec(branch):
        return pl.BlockSpec((1, seq, NSA_DK), lambda b, g, qi: (branch * NSA_G + g, b, 0))

    def vt_spec(branch):
        return pl.BlockSpec((1, n_vt, NSA_DV, VT_TILE), lambda b, g, qi: (branch * NSA_G + g, b, 0, 0))

    return pl.pallas_call(
        kern,
        grid=(batch, NSA_G, nq),
        in_specs=[
            pl.BlockSpec((hpg, tq, NSA_DK), lambda b, g, qi: (g, b * nq + qi, 0)),
            pl.BlockSpec((1, n_chunk, NSA_DK), lambda b, g, qi: (b * NSA_G + g, 0, 0)),
            pl.BlockSpec((1, NSA_DV, n_chunk), lambda b, g, qi: (b * NSA_G + g, 0, 0)),
            k_spec(1), vt_spec(1), k_spec(2), vt_spec(2),
            pl.BlockSpec((tq, LANES), lambda b, g, qi: (b * nq + qi, g)),
            pl.BlockSpec(ovl.shape, lambda b, g, qi: (0, 0)),
        ],
        out_specs=pl.BlockSpec((tq, hpg * NSA_DV), lambda b, g, qi: (b * nq + qi, g)),
        out_shape=jax.ShapeDtypeStruct((t, NSA_G * hpg * NSA_DV), BF16),
        scratch_shapes=[
            pltpu.VMEM((seq // tk, tk, tq), F32),
            pltpu.VMEM((NSA_DV, n_rows), F32),
            pltpu.VMEM((NSA_DV, n_rows), F32),
            pltpu.VMEM((NSA_DV, n_rows), F32),
        ],
        compiler_params=_params(("parallel", "parallel", "arbitrary")),
        name="nsa_attention",
    )(q, kc, vct, k3, vt3, k3, vt3, gates, ovl)


def _rope_tables(pos_flat):
    pos = pos_flat.astype(F32)[:, None]
    inv64 = ROPE_THETA ** (-jnp.arange(0, MLA_ROPE, 2, dtype=F32) / MLA_ROPE)
    ang = pos * inv64
    c, s = jnp.cos(ang), jnp.sin(ang)
    cos_mla = jnp.concatenate([c, c, c, c], axis=1)
    sin_mla = jnp.concatenate([-s, s, -s, s], axis=1)
    inv192 = ROPE_THETA ** (-jnp.arange(0, NSA_DK, 2, dtype=F32) / NSA_DK)
    ang = pos * inv192
    c, s = jnp.cos(ang), jnp.sin(ang)
    reps = NSA_ROPE_TILE // NSA_DK
    cos_nsa = jnp.concatenate([c, c] * reps, axis=1)
    sin_nsa = jnp.concatenate([-s, s] * reps, axis=1)
    return cos_mla, sin_mla, cos_nsa, sin_nsa


def _compress_weights(w1, pe, w2, *, transpose_w2):
    width, d_hid = w1.shape[0] // CMP_BLOCK, w1.shape[1]
    hid_p = -(-d_hid // LANES) * LANES
    w1 = jnp.pad(w1, ((0, 0), (0, hid_p - d_hid)))
    w1 = w1.reshape(2, CMP_STRIDE * width, hid_p).astype(BF16)
    pe = jnp.pad(pe.reshape(2, 1, CMP_STRIDE * width), ((0, 0), (0, 7), (0, 0))).astype(BF16)
    w2 = jnp.pad(w2, ((0, hid_p - d_hid), (0, 0))).astype(BF16)
    return w1, pe, (w2.T if transpose_w2 else w2)


def kernel(x, positions, a_norm, a_w_in, a_q_norm, a_w_uq, a_kv_norm, a_w_ukv, a_w_o, s_norm, s_w_kv, s_cmp_pe_k, s_cmp_w1_k, s_cmp_w2_k, s_cmp_pe_v, s_cmp_w1_v, s_cmp_w2_v, b_norm, b_w_in, b_w_o, f_norm, f_w_in, f_w_out, final_norm):
    batch, seq, d = x.shape
    t = batch * seq
    heads = d // 128
    hpg = heads // NSA_G
    d_ff = f_w_out.shape[1]
    n_a, n_b = a_w_in.shape[0], b_w_in.shape[0]
    assert n_a >= 1 and n_b >= 1
    tm = 1024
    sds = jax.ShapeDtypeStruct

    cos_mla, sin_mla, cos_nsa, sin_nsa = _rope_tables(positions.reshape(t))
    row = lambda i, j, k: (i, 0)
    tile = lambda i, j, k: (i, j)
    tab_spec = pl.BlockSpec((tm, LANES), row)
    ssq_spec = pl.BlockSpec((tm, LANES), row)
    f_w_out_b = f_w_out.astype(BF16)

    def residual_mm(a, w, w_layer, h, gains, *, tm_, tn, name):
        blk = pl.BlockSpec((tm_, tn), tile)
        common = dict(tm=tm_, tn=tn, tk=a.shape[1], nj=d // tn, w_layer=w_layer, name=name)
        if not gains:
            h_new = _mm(a, w, _epi_residual, sds((t, d), F32), blk,
                        extras=(h,), extra_specs=(blk,), **common)
            return h_new, None, ()
        outs = _mm(a, w, _epi_residual_prenorm,
                   (sds((t, d), F32), sds((t, LANES), F32)) + tuple(sds((t, d), BF16) for _ in gains),
                   (blk, pl.BlockSpec((tm_, LANES), row)) + tuple(blk for _ in gains),
                   extras=(h,) + tuple(g.reshape(1, d) for g in gains),
                   extra_specs=(blk,) + tuple(pl.BlockSpec((1, tn), lambda i, j, k: (0, j))
                                              for _ in gains),
                   **common)
        return outs[0], outs[1], tuple(outs[2:])

    def ffn(hb, ssq, h, layer, gains):
        tn = 256
        nj = d_ff // tn
        tm_in = min(2 * tm, t)
        act = _mm(hb, f_w_in, functools.partial(_epi_swiglu, d), sds((t, d_ff), BF16),
                  pl.BlockSpec((tm_in, tn), tile),
                  tm=tm_in, tn=tn, tk=d, nj=nj, w_layer=layer, a_buffers=1,
                  w_col_maps=(lambda j: j, lambda j: j + nj),
                  extras=(ssq,), extra_specs=(pl.BlockSpec((tm_in, LANES), row),), name="ffn_in")
        return residual_mm(act, f_w_out_b, layer, h, gains, tm_=512, tn=tn, name="ffn_out")

    h = x.reshape(t, d)
    hb, ssq = _prenorm(h, a_norm[0])

    for i in range(n_a):
        rope_lo = MLA_Q_LORA + MLA_KV_LORA
        w_in = jnp.concatenate([a_w_in[i], a_w_in[i][:, rope_lo:]], axis=1).astype(BF16)
        n_in = w_in.shape[1]
        cq, ckv, kr = _mm(
            hb, w_in, functools.partial(_epi_mla_in, d),
            (sds((t, MLA_Q_LORA), BF16), sds((t, MLA_KV_LORA), BF16), sds((t, LANES), BF16)),
            (pl.BlockSpec((tm, MLA_Q_LORA), row), pl.BlockSpec((tm, MLA_KV_LORA), row),
             pl.BlockSpec((tm, LANES), row)),
            tm=tm, tn=n_in, tk=1024, nj=1,
            extras=(ssq, a_q_norm[i].reshape(1, -1), a_kv_norm[i].reshape(1, -1), cos_mla, sin_mla),
            extra_specs=(ssq_spec,
                         pl.BlockSpec((1, MLA_Q_LORA), lambda i_, j, k: (0, 0)),
                         pl.BlockSpec((1, MLA_KV_LORA), lambda i_, j, k: (0, 0)),
                         tab_spec, tab_spec),
            name="mla_in")

        qd = MLA_NOPE + MLA_ROPE
        w_uq = a_w_uq[i].reshape(MLA_Q_LORA, heads, qd)
        w_uq = jnp.concatenate([w_uq[:, :, :MLA_NOPE].reshape(MLA_Q_LORA, heads * MLA_NOPE),
                                w_uq[:, :, MLA_NOPE:].reshape(MLA_Q_LORA, heads * MLA_ROPE)],
                               axis=1).astype(BF16)
        tn = 512
        q = _mm(cq, w_uq,
                functools.partial(_epi_mla_q, heads * MLA_NOPE // tn, qd ** -0.5 * LOG2_E),
                jax.ShapeDtypeStruct((t, heads * qd), BF16),
                pl.BlockSpec((tm, tn), lambda i_, j, k: (i_, j)),
                tm=tm, tn=tn, tk=MLA_Q_LORA, nj=heads * qd // tn,
                extras=(cos_mla, sin_mla), extra_specs=(tab_spec, tab_spec), name="mla_q")
        heads_per_tile = 4
        kn, vt = _mm(ckv, a_w_ukv[i].astype(BF16), _epi_mla_kv,
                     (jax.ShapeDtypeStruct((t, heads * MLA_NOPE), BF16),
                      jax.ShapeDtypeStruct((heads, t // VT_TILE, MLA_V, VT_TILE), BF16)),
                     (pl.BlockSpec((tm, heads_per_tile * MLA_NOPE), lambda i_, j, k: (i_, j)),
                      pl.BlockSpec((heads_per_tile, tm // VT_TILE, MLA_V, VT_TILE),
                                   lambda i_, j, k: (j, i_, 0, 0))),
                     tm=tm, tn=heads_per_tile * (MLA_NOPE + MLA_V), tk=MLA_KV_LORA,
                     nj=heads // heads_per_tile, name="mla_kv")
        o = _mla_attention(q, kn, kr, vt, batch, seq, heads)
        h, ssq, (hb,) = residual_mm(o, a_w_o[i].astype(BF16), None, h, [f_norm[i]],
                                    tm_=tm, tn=512, name="mla_out")
        gains = [a_norm[i + 1]] if i + 1 < n_a else [s_norm, b_norm[0]]
        h, ssq, hbs = ffn(hb, ssq, h, i, gains)
        hb = hbs[0]

    hb_s, hb_b = hbs
    n_sets = 3 * NSA_G
    per_tile = NSA_ROPE_TILE // NSA_DK
    tm_rope = 512
    rope_tab = pl.BlockSpec((tm_rope, NSA_ROPE_TILE), row)
    w_kv = s_w_kv.reshape(d, n_sets, NSA_DK + NSA_DV)
    w_k = w_kv[:, :, :NSA_DK].reshape(d, n_sets * NSA_DK).astype(BF16)
    w_v = w_kv[:, :, NSA_DK:].reshape(d, n_sets * NSA_DV).astype(BF16)
    k3 = _mm(hb_s, w_k, functools.partial(_epi_nsa_rope, d, None),
             sds((n_sets, t, NSA_DK), BF16),
             pl.BlockSpec((per_tile, tm_rope, NSA_DK), lambda i_, j, k: (j, i_, 0)),
             tm=tm_rope, tn=NSA_ROPE_TILE, tk=d, nj=n_sets // per_tile,
             extras=(ssq, cos_nsa, sin_nsa),
             extra_specs=(pl.BlockSpec((tm_rope, LANES), row), rope_tab, rope_tab), name="nsa_k")
    v3, vt3 = _mm(hb_s, w_v, functools.partial(_epi_nsa_v, d),
                  (sds((n_sets, t, NSA_DV), BF16),
                   sds((n_sets, t // VT_TILE, NSA_DV, VT_TILE), BF16)),
                  (pl.BlockSpec((per_tile, tm, NSA_DV), lambda i_, j, k: (j, i_, 0)),
                   pl.BlockSpec((per_tile, tm // VT_TILE, NSA_DV, VT_TILE),
                                lambda i_, j, k: (j, i_, 0, 0))),
                  tm=tm, tn=per_tile * NSA_DV, tk=d, nj=n_sets // per_tile,
                  extras=(ssq,), extra_specs=(ssq_spec,), name="nsa_v")

    ak = k3[:NSA_G].reshape(NSA_G, t // CMP_STRIDE, CMP_STRIDE * NSA_DK)
    av = v3[:NSA_G].reshape(NSA_G, t // CMP_STRIDE, CMP_STRIDE * NSA_DV)
    kc, vct = _compress(ak, av,
                        _compress_weights(s_cmp_w1_k, s_cmp_pe_k, s_cmp_w2_k, transpose_w2=False),
                        _compress_weights(s_cmp_w1_v, s_cmp_pe_v, s_cmp_w2_v, transpose_w2=True),
                        batch, seq)

    n_chunk = seq // CMP_STRIDE
    n_sel = seq // SEL_BLOCK
    c_start = np.arange(n_chunk) * CMP_STRIDE
    j_start = np.arange(n_sel) * SEL_BLOCK
    ovl = ((c_start[None, :] < j_start[:, None] + SEL_BLOCK)
           & (c_start[None, :] + CMP_BLOCK > j_start[:, None])
           & (np.arange(n_chunk)[None, :] < (seq - CMP_BLOCK) // CMP_STRIDE + 1))
    ovl = jnp.asarray(ovl, BF16)

    for i in range(n_b):
        n_q = heads * NSA_DK
        w_q = b_w_in[i][:, :n_q].astype(BF16)
        w_g = b_w_in[i][:, n_q:].reshape(d, NSA_G, 3 * hpg)
        w_g = jnp.pad(w_g, ((0, 0), (0, 0), (0, LANES - 3 * hpg))).reshape(d, NSA_G * LANES)
        w_g = w_g.astype(BF16)
        q = _mm(hb_b, w_q, functools.partial(_epi_nsa_rope, d, NSA_DK ** -0.5 * LOG2_E),
                sds((heads, t, NSA_DK), BF16),
                pl.BlockSpec((per_tile, tm_rope, NSA_DK), lambda i_, j, k: (j, i_, 0)),
                tm=tm_rope, tn=NSA_ROPE_TILE, tk=d, nj=heads // per_tile,
                extras=(ssq, cos_nsa, sin_nsa),
                extra_specs=(pl.BlockSpec((tm_rope, LANES), row), rope_tab, rope_tab),
                name="nsa_q")
        gates = _mm(hb_b, w_g, functools.partial(_epi_sigmoid, d), sds((t, NSA_G * LANES), F32),
                    pl.BlockSpec((tm, NSA_G * LANES), tile),
                    tm=tm, tn=NSA_G * LANES, tk=d, nj=1,
                    extras=(ssq,), extra_specs=(ssq_spec,), name="nsa_gates")
        o = _nsa_attention(q, kc, vct, k3, vt3, gates, ovl, batch, seq, hpg)
        h, ssq, (hb,) = residual_mm(o, b_w_o[i].astype(BF16), None, h, [f_norm[n_a + i]],
                                    tm_=tm, tn=512, name="nsa_out")
        gains = [b_norm[i + 1]] if i + 1 < n_b else []
        h, ssq, hbs = ffn(hb, ssq, h, n_a + i, gains)
        if hbs:
            hb_b = hbs[0]

    out = _rmsnorm(h, final_norm, F32)
    return out.reshape(batch, seq, d)
```

```python
import functools

import numpy as np
import jax
import jax.numpy as jnp
from jax import lax
from jax.experimental import pallas as pl
from jax.experimental.pallas import tpu as pltpu

F32 = jnp.float32
BF16 = jnp.bfloat16

LANES = 128
VMEM_LIMIT_BYTES = 56 * 1024 * 1024

NORM_EPS = 1e-6
ROPE_THETA = 10000.0

MLA_Q_LORA = 1536
MLA_KV_LORA = 512
MLA_NOPE = 128
MLA_ROPE = 64
MLA_V = 128

NSA_G = 4
NSA_DK = 192
NSA_DV = 128
NSA_HALF = NSA_DK // 2
NSA_ROPE_TILE = 4 * NSA_DK
CMP_BLOCK = 32
CMP_STRIDE = 16
SEL_BLOCK = 64
SEL_TOPK = 16
WINDOW = 512
FORCE_BONUS = 1e4
NEG_BIG = -1e30
LOG2_E = 1.4426950408889634

VT_TILE = LANES
NT_DIMS = (((1,), (1,)), ((), ()))


def _params(sem):
    return pltpu.CompilerParams(dimension_semantics=sem, vmem_limit_bytes=VMEM_LIMIT_BYTES)


def _rmsnorm_kernel(x_ref, g_ref, o_ref):
    x = x_ref[...]
    ms = jnp.mean(x * x, axis=-1, keepdims=True)
    o_ref[...] = (x * lax.rsqrt(ms + NORM_EPS) * g_ref[...]).astype(o_ref.dtype)


def _rmsnorm(x, g, out_dtype, tm=256):
    t, d = x.shape
    return pl.pallas_call(
        _rmsnorm_kernel,
        grid=(t // tm,),
        in_specs=[pl.BlockSpec((tm, d), lambda i: (i, 0)),
                  pl.BlockSpec((1, d), lambda i: (0, 0))],
        out_specs=pl.BlockSpec((tm, d), lambda i: (i, 0)),
        out_shape=jax.ShapeDtypeStruct((t, d), out_dtype),
        compiler_params=_params(("parallel",)),
        name="rmsnorm",
    )(x, g.reshape(1, d))


def _prenorm_kernel(x_ref, g_ref, hb_ref, ssq_ref):
    x = x_ref[...]
    hb_ref[...] = (x * g_ref[...]).astype(hb_ref.dtype)
    ssq_ref[...] = jnp.broadcast_to(jnp.sum(x * x, axis=-1, keepdims=True), ssq_ref.shape)


def _prenorm(x, g, tm=256):
    t, d = x.shape
    return pl.pallas_call(
        _prenorm_kernel,
        grid=(t // tm,),
        in_specs=[pl.BlockSpec((tm, d), lambda i: (i, 0)),
                  pl.BlockSpec((1, d), lambda i: (0, 0))],
        out_specs=(pl.BlockSpec((tm, d), lambda i: (i, 0)),
                   pl.BlockSpec((tm, LANES), lambda i: (i, 0))),
        out_shape=(jax.ShapeDtypeStruct((t, d), BF16), jax.ShapeDtypeStruct((t, LANES), F32)),
        compiler_params=_params(("parallel",)),
        name="prenorm",
    )(x, g.reshape(1, d))


def _mm_kernel(n_w, n_e, n_o, nk, epilogue, first_tile_init, *refs):
    a_ref = refs[0]
    w_refs = refs[1:1 + n_w]
    e_refs = refs[1 + n_w:1 + n_w + n_e]
    o_refs = refs[1 + n_w + n_e:1 + n_w + n_e + n_o]
    acc_refs = refs[1 + n_w + n_e + n_o:]
    j = pl.program_id(1)
    if first_tile_init is not None:
        pl.when((j == 0) & (pl.program_id(2) == 0))(lambda: first_tile_init(o_refs))

    def product(w):
        return jnp.dot(a_ref[...], w[...].astype(a_ref.dtype), preferred_element_type=F32)

    if nk == 1:
        epilogue([product(w) for w in w_refs], e_refs, o_refs, j)
        return
    k = pl.program_id(2)

    @pl.when(k == 0)
    def _():
        for w, acc in zip(w_refs, acc_refs):
            acc[...] = product(w)

    @pl.when(k > 0)
    def _():
        for w, acc in zip(w_refs, acc_refs):
            acc[...] += product(w)

    @pl.when(k == nk - 1)
    def _():
        epilogue([acc[...] for acc in acc_refs], e_refs, o_refs, j)


def _mm(a, w, epilogue, out_shape, out_specs, *, tm, tn, tk, nj, w_col_maps=None, w_layer=None,
        extras=(), extra_specs=(), a_buffers=None, first_tile_init=None, name="matmul"):
    m, kdim = a.shape
    ni, nk = m // tm, kdim // tk
    assert ni * tm == m and nk * tk == kdim
    if w_col_maps is None:
        w_col_maps = (lambda j: j,)
    n_w = len(w_col_maps)
    single = not isinstance(out_shape, (tuple, list))
    out_shapes = (out_shape,) if single else tuple(out_shape)
    out_specs_t = (out_specs,) if single else tuple(out_specs)
    a_kwargs = {} if a_buffers is None else {"pipeline_mode": pl.Buffered(a_buffers)}
    in_specs = [pl.BlockSpec((tm, tk), lambda i, j, k: (i, k), **a_kwargs)]
    for cm in w_col_maps:
        if w_layer is None:
            in_specs.append(pl.BlockSpec((tk, tn), functools.partial(
                lambda i, j, k, cm: (k, cm(j)), cm=cm)))
        else:
            in_specs.append(pl.BlockSpec((None, tk, tn), functools.partial(
                lambda i, j, k, cm: (w_layer, k, cm(j)), cm=cm)))
    in_specs += list(extra_specs)
    scratch = [pltpu.VMEM((tm, tn), F32) for _ in range(n_w)] if nk > 1 else []
    kern = functools.partial(_mm_kernel, n_w, len(extras), len(out_shapes), nk, epilogue,
                             first_tile_init)
    res = pl.pallas_call(
        kern,
        grid=(ni, nj, nk),
        in_specs=in_specs,
        out_specs=out_specs_t,
        out_shape=out_shapes,
        scratch_shapes=scratch,
        compiler_params=_params(("parallel", "arbitrary", "arbitrary")),
        name=name,
    )(a, *([w] * n_w), *extras)
    return res[0] if single else res


def _rope_pairs_64(x, cos2, sin2):
    lane = lax.broadcasted_iota(jnp.int32, x.shape, 1)
    first_half = (lane % MLA_ROPE) < (MLA_ROPE // 2)
    partner = jnp.where(first_half,
                        pltpu.roll(x, LANES - MLA_ROPE // 2, 1),
                        pltpu.roll(x, MLA_ROPE // 2, 1))
    return x * cos2 + partner * sin2


def _store_vt_tiles(vt_ref, lead, v):
    for c in range(v.shape[0] // VT_TILE):
        vt_ref[lead + (c,)] = v[c * VT_TILE:(c + 1) * VT_TILE, :].T.astype(vt_ref.dtype)


def _row_scale(acc, ssq_ref, d_model):
    rstd = lax.rsqrt(ssq_ref[...] * (1.0 / d_model) + NORM_EPS)
    return acc * jnp.concatenate([rstd] * (acc.shape[1] // LANES), axis=1)


def _epi_residual(accs, e_refs, o_refs, j):
    o_refs[0][...] = e_refs[0][...] + accs[0]


def _zero_ssq(o_refs):
    o_refs[1][...] = jnp.zeros(o_refs[1].shape, F32)


def _epi_residual_prenorm(accs, e_refs, o_refs, j):
    h = e_refs[0][...] + accs[0]
    o_refs[0][...] = h
    ssq_ref = o_refs[1]
    ssq_ref[...] += jnp.broadcast_to(jnp.sum(h * h, axis=-1, keepdims=True), ssq_ref.shape)
    for g_ref, hb_ref in zip(e_refs[1:], o_refs[2:]):
        hb_ref[...] = (h * g_ref[...]).astype(hb_ref.dtype)


def _epi_swiglu(d_model, accs, e_refs, o_refs, j):
    a = _row_scale(accs[0], e_refs[0], d_model)
    b = _row_scale(accs[1], e_refs[0], d_model)
    o_refs[0][...] = (a * jax.nn.sigmoid(a) * b).astype(o_refs[0].dtype)


def _epi_sigmoid(d_model, accs, e_refs, o_refs, j):
    o_refs[0][...] = jax.nn.sigmoid(_row_scale(accs[0], e_refs[0], d_model))


def _epi_mla_in(d_model, accs, e_refs, o_refs, j):
    ssq_ref, gq_ref, gkv_ref, cos_ref, sin_ref = e_refs
    u = _row_scale(accs[0], ssq_ref, d_model)
    cq = u[:, :MLA_Q_LORA]
    ms = jnp.mean(cq * cq, axis=-1, keepdims=True)
    o_refs[0][...] = (cq * lax.rsqrt(ms + NORM_EPS) * gq_ref[...]).astype(BF16)
    ckv = u[:, MLA_Q_LORA:MLA_Q_LORA + MLA_KV_LORA]
    ms = jnp.mean(ckv * ckv, axis=-1, keepdims=True)
    o_refs[1][...] = (ckv * lax.rsqrt(ms + NORM_EPS) * gkv_ref[...]).astype(BF16)
    kr = u[:, MLA_Q_LORA + MLA_KV_LORA:]
    o_refs[2][...] = _rope_pairs_64(kr, cos_ref[...], sin_ref[...]).astype(BF16)


def _epi_mla_q(scale, accs, e_refs, o_refs, j):
    cos_ref, sin_ref = e_refs
    acc = accs[0]
    for c in range(acc.shape[1] // LANES):
        x = acc[:, c * LANES:(c + 1) * LANES]
        r = _rope_pairs_64(x, cos_ref[...], sin_ref[...])
        o_refs[0][:, c * LANES:(c + 1) * LANES] = (r * scale).astype(BF16)


def _epi_mla_kv(accs, e_refs, o_refs, j):
    acc = accs[0]
    kn_ref, vt_ref = o_refs
    per = MLA_NOPE + MLA_V
    for r in range(acc.shape[1] // per):
        kn_ref[:, r * MLA_NOPE:(r + 1) * MLA_NOPE] = acc[:, r * per:r * per + MLA_NOPE].astype(BF16)
        _store_vt_tiles(vt_ref, (r,), acc[:, r * per + MLA_NOPE:(r + 1) * per])


def _rope_192(x, cos, sin):
    width = x.shape[1]
    lane = lax.broadcasted_iota(jnp.int32, x.shape, 1)
    partner = jnp.where((lane % NSA_DK) < NSA_HALF,
                        pltpu.roll(x, width - NSA_HALF, 1),
                        pltpu.roll(x, NSA_HALF, 1))
    return x * cos + partner * sin


def _epi_nsa_rope(d_model, scale, accs, e_refs, o_refs, j):
    ssq_ref, cos_ref, sin_ref = e_refs
    acc = _row_scale(accs[0], ssq_ref, d_model)
    roped = _rope_192(acc, cos_ref[...], sin_ref[...])
    if scale is not None:
        roped = roped * scale
    for r in range(acc.shape[1] // NSA_DK):
        o_refs[0][r] = roped[:, r * NSA_DK:(r + 1) * NSA_DK].astype(BF16)


def _epi_nsa_v(d_model, accs, e_refs, o_refs, j):
    v_ref, vt_ref = o_refs
    acc = _row_scale(accs[0], e_refs[0], d_model)
    for r in range(acc.shape[1] // NSA_DV):
        v = acc[:, r * NSA_DV:(r + 1) * NSA_DV]
        v_ref[r] = v.astype(BF16)
        _store_vt_tiles(vt_ref, (r,), v)


def _bias_chunk(bias, start, width):
    period = bias.shape[1]
    if period >= width:
        off = start % period
        return bias[:, off:off + width]
    return jnp.concatenate([bias] * (width // period), axis=1)


def _flash_step_t(q, k, vt, bias, m, l, acc_ref, n_split=1):
    width = q.shape[0] // n_split
    m_out, l_out = [], []
    for c in range(n_split):
        sl = slice(c * width, (c + 1) * width)
        s = lax.dot_general(k, q[sl], NT_DIMS, preferred_element_type=F32)
        if bias is not None:
            s = s + _bias_chunk(bias, c * width, width)
        m_new = jnp.maximum(m[:, sl], jnp.max(s, axis=0, keepdims=True))
        alpha = jnp.exp2(m[:, sl] - m_new)
        p = jnp.exp2(s - m_new)
        l_out.append(alpha * l[:, sl] + jnp.sum(p, axis=0, keepdims=True))
        acc_ref[:, sl] = alpha * acc_ref[:, sl] + jnp.dot(vt, p.astype(BF16),
                                                          preferred_element_type=F32)
        m_out.append(m_new)
    return jnp.concatenate(m_out, axis=1), jnp.concatenate(l_out, axis=1)


def _vt_tiles(vt_ref, lead, c0, n):
    return jnp.concatenate([vt_ref[lead + (c0 + c,)] for c in range(n)], axis=1)


def _mla_attn_kernel(t_blk, seq, qn_ref, qr_ref, kn_ref, kr_ref, vt_ref, o_ref, acc_ref):
    h = pl.program_id(1)
    n_vt = t_blk // VT_TILE
    lane_half = lax.broadcasted_iota(jnp.int32, (t_blk, LANES), 1) // MLA_ROPE
    kpos = lax.broadcasted_iota(jnp.int32, (t_blk, t_blk), 0)
    qpos = lax.broadcasted_iota(jnp.int32, (t_blk, t_blk), 1)
    diag_bias = jnp.where(kpos <= qpos, 0.0, NEG_BIG).astype(F32)

    for qi in range(seq // t_blk):
        q0 = qi * t_blk
        qr = qr_ref[q0:q0 + t_blk, :]
        qr = jnp.where(lane_half == (h % 2), qr, jnp.zeros_like(qr))
        q = jnp.concatenate([qn_ref[q0:q0 + t_blk, :], qr], axis=1)
        acc_ref[...] = jnp.zeros(acc_ref.shape, F32)
        m = jnp.full((1, t_blk), NEG_BIG, F32)
        l = jnp.zeros((1, t_blk), F32)
        for j in range(qi + 1):
            k0 = j * t_blk
            k = jnp.concatenate([kn_ref[k0:k0 + t_blk, :], kr_ref[k0:k0 + t_blk, :]], axis=1)
            vt = _vt_tiles(vt_ref, (0,), j * n_vt, n_vt)
            m, l = _flash_step_t(q, k, vt, diag_bias if j == qi else None, m, l, acc_ref)
        o_ref[q0:q0 + t_blk, :] = (acc_ref[...] / l).T.astype(o_ref.dtype)


def _mla_attention(q, kn, kr, vt, batch, seq, heads, t_blk=512):
    t = batch * seq
    n_nope_blk = heads * MLA_NOPE // LANES
    n_vt = seq // VT_TILE
    kern = functools.partial(_mla_attn_kernel, t_blk, seq)
    return pl.pallas_call(
        kern,
        grid=(batch, heads),
        in_specs=[
            pl.BlockSpec((seq, MLA_NOPE), lambda b, h: (b, h)),
            pl.BlockSpec((seq, LANES), lambda b, h: (b, n_nope_blk + h // 2)),
            pl.BlockSpec((seq, MLA_NOPE), lambda b, h: (b, h)),
            pl.BlockSpec((seq, LANES), lambda b, h: (b, 0)),
            pl.BlockSpec((1, n_vt, MLA_V, VT_TILE), lambda b, h: (h, b, 0, 0)),
        ],
        out_specs=pl.BlockSpec((seq, MLA_V), lambda b, h: (b, h)),
        out_shape=jax.ShapeDtypeStruct((t, heads * MLA_V), BF16),
        scratch_shapes=[pltpu.VMEM((MLA_V, t_blk), F32)],
        compiler_params=_params(("parallel", "parallel")),
        name="mla_attention",
    )(q, q, kn, kr, vt)


def _compress_mlp(a, w1_ref, pe_ref, w2_ref, transposed_out):
    y_lo = jnp.dot(a, w1_ref[0], preferred_element_type=F32)
    y_hi = jnp.dot(a, w1_ref[1], preferred_element_type=F32)
    pe_c = (jnp.dot(pe_ref[0], w1_ref[0], preferred_element_type=F32)
            + jnp.dot(pe_ref[1], w1_ref[1], preferred_element_type=F32))
    n = y_hi.shape[0]
    hid = y_lo + pltpu.roll(y_hi, n - 1, 0) + pe_c[0:1, :]
    hid = jax.nn.gelu(hid, approximate=True)
    if transposed_out:
        return lax.dot_general(w2_ref[...], hid.astype(BF16), NT_DIMS, preferred_element_type=F32)
    return jnp.dot(hid.astype(BF16), w2_ref[...], preferred_element_type=F32)


def _compress_kernel(ak_ref, av_ref, w1k_ref, pek_ref, w2k_ref, w1v_ref, pev_ref, w2vt_ref,
                     kc_ref, vct_ref):
    kc_ref[0] = _compress_mlp(ak_ref[0], w1k_ref, pek_ref, w2k_ref, False).astype(BF16)
    vct_ref[0] = _compress_mlp(av_ref[0], w1v_ref, pev_ref, w2vt_ref, True).astype(BF16)


def _compress(ak, av, wk, wv, batch, seq):
    n_chunk = seq // CMP_STRIDE
    const3 = lambda b, g: (0, 0, 0)
    const2 = lambda b, g: (0, 0)

    def wspecs(w):
        w1, pe, w2 = w
        return [pl.BlockSpec(w1.shape, const3), pl.BlockSpec(pe.shape, const3),
                pl.BlockSpec(w2.shape, const2)]

    return pl.pallas_call(
        _compress_kernel,
        grid=(batch, NSA_G),
        in_specs=[pl.BlockSpec((1, n_chunk, ak.shape[2]), lambda b, g: (g, b, 0)),
                  pl.BlockSpec((1, n_chunk, av.shape[2]), lambda b, g: (g, b, 0))]
        + wspecs(wk) + wspecs(wv),
        out_specs=(pl.BlockSpec((1, n_chunk, NSA_DK), lambda b, g: (b * NSA_G + g, 0, 0)),
                   pl.BlockSpec((1, NSA_DV, n_chunk), lambda b, g: (b * NSA_G + g, 0, 0))),
        out_shape=(jax.ShapeDtypeStruct((batch * NSA_G, n_chunk, NSA_DK), BF16),
                   jax.ShapeDtypeStruct((batch * NSA_G, NSA_DV, n_chunk), BF16)),
        compiler_params=_params(("parallel", "parallel")),
        name="nsa_compress",
    )(ak, av, *wk, *wv)


def _nsa_attn_kernel(hpg, tq, tk, seq, q_ref, kc_ref, vct_ref, ks_ref, vst_ref, kw_ref, vwt_ref,
                     gate_ref, ovl_ref, o_ref, bias_ref, oc_ref, os_ref, acc_ref):
    qi = pl.program_id(2)
    t0 = qi * tq
    n_rows = hpg * tq
    n_cmp = kc_ref.shape[1]
    n_sel = seq // SEL_BLOCK
    n_vt = tk // VT_TILE
    q = q_ref[...].reshape(n_rows, NSA_DK)

    def tile_heads(x):
        return jnp.concatenate([x] * hpg, axis=1)

    s = lax.dot_general(kc_ref[0], q, NT_DIMS, preferred_element_type=F32)
    cidx = lax.broadcasted_iota(jnp.int32, (n_cmp, tq), 0)
    tpos_c = t0 + lax.broadcasted_iota(jnp.int32, (n_cmp, tq), 1)
    mask_c = tile_heads((cidx * CMP_STRIDE + CMP_BLOCK - 1) <= tpos_c)
    s = jnp.where(mask_c, s, NEG_BIG)
    mx = jnp.max(s, axis=0, keepdims=True)
    e = jnp.where(mask_c, jnp.exp2(s - mx), 0.0)
    p = e / jnp.maximum(jnp.sum(e, axis=0, keepdims=True), 1e-30)
    oc_ref[...] = jnp.dot(vct_ref[0], p.astype(BF16), preferred_element_type=F32)

    psum = p[:, 0:tq]
    for h in range(1, hpg):
        psum = psum + p[:, h * tq:(h + 1) * tq]
    p1 = psum.astype(BF16)
    p2 = (psum - p1.astype(F32)).astype(BF16)
    p3 = (psum - p1.astype(F32) - p2.astype(F32)).astype(BF16)
    ovl = ovl_ref[...]
    imp = (jnp.dot(ovl, p1, preferred_element_type=F32)
           + jnp.dot(ovl, p2, preferred_element_type=F32)
           + jnp.dot(ovl, p3, preferred_element_type=F32))
    jrow = lax.broadcasted_iota(jnp.int32, (n_sel, tq), 0)
    tpos = t0 + lax.broadcasted_iota(jnp.int32, (n_sel, tq), 1)
    valid = jrow * SEL_BLOCK <= tpos
    cur = jnp.right_shift(tpos, SEL_BLOCK.bit_length() - 1)
    forced = valid & ((jrow == 0) | (jrow == cur) | (jrow == cur - 1))
    score = jnp.where(valid, imp + jnp.where(forced, FORCE_BONUS, 0.0), -jnp.inf)
    rank = jnp.zeros((n_sel, tq), jnp.int32)
    for i in range(n_sel):
        row = score[i:i + 1, :]
        beats = (row > score) | ((row == score) & (jrow > i))
        rank = rank + beats.astype(jnp.int32)
    sel_bias = jnp.where(valid & (rank < SEL_TOPK), 0.0, NEG_BIG).astype(F32)
    n_pairs = ((t0 + tq - 1) // tk + 2) // 2
    blocks_per_tile = tk // SEL_BLOCK
    for jj in range(seq // tk):
        @pl.when(jj < 2 * n_pairs)
        def _():
            rows = [jnp.broadcast_to(sel_bias[b:b + 1, :], (SEL_BLOCK, tq))
                    for b in range(jj * blocks_per_tile, (jj + 1) * blocks_per_tile)]
            kpos = jj * tk + lax.broadcasted_iota(jnp.int32, (tk, tq), 0)
            tq_pos = t0 + lax.broadcasted_iota(jnp.int32, (tk, tq), 1)
            bias_ref[jj] = jnp.where(kpos <= tq_pos, jnp.concatenate(rows, axis=0), NEG_BIG)

    m0 = jnp.full((1, n_rows), NEG_BIG, F32)
    l0 = jnp.zeros((1, n_rows), F32)

    acc_ref[...] = jnp.zeros(acc_ref.shape, F32)

    def sel_step(j, ml):
        k0 = pl.multiple_of(j * tk, tk)
        k = ks_ref[0, pl.ds(k0, tk), :]
        vt = _vt_tiles(vst_ref, (0,), j * n_vt, n_vt)
        return _flash_step_t(q, k, vt, bias_ref[j], ml[0], ml[1], acc_ref)

    def sel_pair(pair, ml):
        return sel_step(2 * pair + 1, sel_step(2 * pair, ml))

    _, l = lax.fori_loop(0, n_pairs, sel_pair, (m0, l0))
    os_ref[...] = acc_ref[...] / l

    acc_ref[...] = jnp.zeros(acc_ref.shape, F32)
    span = -(-(WINDOW + tq) // tk) * tk
    k_start = jnp.maximum(t0 + tq - span, 0)

    def win_step(c, ml):
        k0 = pl.multiple_of(k_start + c * tk, VT_TILE)
        k = kw_ref[0, pl.ds(k0, tk), :]
        vt = _vt_tiles(vwt_ref, (0,), k0 // VT_TILE, n_vt)
        kpos = k0 + lax.broadcasted_iota(jnp.int32, (tk, tq), 0)
        tq_pos = t0 + lax.broadcasted_iota(jnp.int32, (tk, tq), 1)
        ok = (kpos <= tq_pos) & (tq_pos - kpos < WINDOW)
        bias = jnp.where(ok, 0.0, NEG_BIG).astype(F32)
        return _flash_step_t(q, k, vt, bias, ml[0], ml[1], acc_ref)

    ml = (m0, l0)
    for c in range(span // tk):
        ml = win_step(c, ml)
    o_w = acc_ref[...] / ml[1]

    gates_t = gate_ref[...].T
    for h in range(hpg):
        cols = slice(h * tq, (h + 1) * tq)
        o = (gates_t[3 * h:3 * h + 1, :] * oc_ref[:, cols]
             + gates_t[3 * h + 1:3 * h + 2, :] * os_ref[:, cols]
             + gates_t[3 * h + 2:3 * h + 3, :] * o_w[:, cols])
        o_ref[:, h * NSA_DV:(h + 1) * NSA_DV] = o.T.astype(o_ref.dtype)


def _nsa_attention(q, kc, vct, k3, vt3, gates, ovl, batch, seq, hpg, tq=128, tk=256):
    t = batch * seq
    nq = seq // tq
    n_chunk = seq // CMP_STRIDE
    n_rows = hpg * tq
    n_vt = seq // VT_TILE
    assert (WINDOW + tq) % VT_TILE == 0 and seq >= -(-(WINDOW + tq) // tk) * tk
    kern = functools.partial(_nsa_attn_kernel, hpg, tq, tk, seq)

    def k_spec(branch):
        return pl.BlockSpec((1, seq, NSA_DK), lambda b, g, qi: (branch * NSA_G + g, b, 0))

    def vt_spec(branch):
        return pl.BlockSpec((1, n_vt, NSA_DV, VT_TILE), lambda b, g, qi: (branch * NSA_G + g, b, 0, 0))

    return pl.pallas_call(
        kern,
        grid=(batch, NSA_G, nq),
        in_specs=[
            pl.BlockSpec((hpg, tq, NSA_DK), lambda b, g, qi: (g, b * nq + qi, 0)),
            pl.BlockSpec((1, n_chunk, NSA_DK), lambda b, g, qi: (b * NSA_G + g, 0, 0)),
            pl.BlockSpec((1, NSA_DV, n_chunk), lambda b, g, qi: (b * NSA_G + g, 0, 0)),
            k_spec(1), vt_spec(1), k_spec(2), vt_spec(2),
            pl.BlockSpec((tq, LANES), lambda b, g, qi: (b * nq + qi, g)),
            pl.BlockSpec(ovl.shape, lambda b, g, qi: (0, 0)),
        ],
        out_specs=pl.BlockSpec((tq, hpg * NSA_DV), lambda b, g, qi: (b * nq + qi, g)),
        out_shape=jax.ShapeDtypeStruct((t, NSA_G * hpg * NSA_DV), BF16),
        scratch_shapes=[
            pltpu.VMEM((seq // tk, tk, tq), F32),
            pltpu.VMEM((NSA_DV, n_rows), F32),
            pltpu.VMEM((NSA_DV, n_rows), F32),
            pltpu.VMEM((NSA_DV, n_rows), F32),
        ],
        compiler_params=_params(("parallel", "parallel", "arbitrary")),
        name="nsa_attention",
    )(q, kc, vct, k3, vt3, k3, vt3, gates, ovl)


def _rope_tables(pos_flat):
    pos = pos_flat.astype(F32)[:, None]
    inv64 = ROPE_THETA ** (-jnp.arange(0, MLA_ROPE, 2, dtype=F32) / MLA_ROPE)
    ang = pos * inv64
    c, s = jnp.cos(ang), jnp.sin(ang)
    cos_mla = jnp.concatenate([c, c, c, c], axis=1)
    sin_mla = jnp.concatenate([-s, s, -s, s], axis=1)
    inv192 = ROPE_THETA ** (-jnp.arange(0, NSA_DK, 2, dtype=F32) / NSA_DK)
    ang = pos * inv192
    c, s = jnp.cos(ang), jnp.sin(ang)
    reps = NSA_ROPE_TILE // NSA_DK
    cos_nsa = jnp.concatenate([c, c] * reps, axis=1)
    sin_nsa = jnp.concatenate([-s, s] * reps, axis=1)
    return cos_mla, sin_mla, cos_nsa, sin_nsa


def _compress_weights(w1, pe, w2, *, transpose_w2):
    width, d_hid = w1.shape[0] // CMP_BLOCK, w1.shape[1]
    hid_p = -(-d_hid // LANES) * LANES
    w1 = jnp.pad(w1, ((0, 0), (0, hid_p - d_hid)))
    w1 = w1.reshape(2, CMP_STRIDE * width, hid_p).astype(BF16)
    pe = jnp.pad(pe.reshape(2, 1, CMP_STRIDE * width), ((0, 0), (0, 7), (0, 0))).astype(BF16)
    w2 = jnp.pad(w2, ((0, hid_p - d_hid), (0, 0))).astype(BF16)
    return w1, pe, (w2.T if transpose_w2 else w2)


def kernel(x, positions, a_norm, a_w_in, a_q_norm, a_w_uq, a_kv_norm, a_w_ukv, a_w_o, s_norm, s_w_kv, s_cmp_pe_k, s_cmp_w1_k, s_cmp_w2_k, s_cmp_pe_v, s_cmp_w1_v, s_cmp_w2_v, b_norm, b_w_in, b_w_o, f_norm, f_w_in, f_w_out, final_norm):
    batch, seq, d = x.shape
    t = batch * seq
    heads = d // 128
    hpg = heads // NSA_G
    d_ff = f_w_out.shape[1]
    n_a, n_b = a_w_in.shape[0], b_w_in.shape[0]
    assert n_a >= 1 and n_b >= 1
    tm = 1024
    sds = jax.ShapeDtypeStruct

    cos_mla, sin_mla, cos_nsa, sin_nsa = _rope_tables(positions.reshape(t))
    row = lambda i, j, k: (i, 0)
    tile = lambda i, j, k: (i, j)
    tab_spec = pl.BlockSpec((tm, LANES), row)
    ssq_spec = pl.BlockSpec((tm, LANES), row)
    f_w_out_b = f_w_out.astype(BF16)

    def residual_mm(a, w, w_layer, h, gains, *, tm_, tn, name, a_buffers=None):
        blk = pl.BlockSpec((tm_, tn), tile)
        common = dict(tm=tm_, tn=tn, tk=a.shape[1], nj=d // tn, w_layer=w_layer, name=name,
                      a_buffers=a_buffers)
        if not gains:
            h_new = _mm(a, w, _epi_residual, sds((t, d), F32), blk,
                        extras=(h,), extra_specs=(blk,), **common)
            return h_new, None, ()
        outs = _mm(a, w, _epi_residual_prenorm,
                   (sds((t, d), F32), sds((t, LANES), F32)) + tuple(sds((t, d), BF16) for _ in gains),
                   (blk, pl.BlockSpec((tm_, LANES), row)) + tuple(blk for _ in gains),
                   extras=(h,) + tuple(g.reshape(1, d) for g in gains),
                   extra_specs=(blk,) + tuple(pl.BlockSpec((1, tn), lambda i, j, k: (0, j))
                                              for _ in gains),
                   first_tile_init=_zero_ssq, **common)
        return outs[0], outs[1], tuple(outs[2:])

    def ffn(hb, ssq, h, layer, gains):
        tn = 256
        nj = d_ff // tn
        tm_in = min(2 * tm, t)
        act = _mm(hb, f_w_in, functools.partial(_epi_swiglu, d), sds((t, d_ff), BF16),
                  pl.BlockSpec((tm_in, tn), tile),
                  tm=tm_in, tn=tn, tk=d, nj=nj, w_layer=layer, a_buffers=1,
                  w_col_maps=(lambda j: j, lambda j: j + nj),
                  extras=(ssq,), extra_specs=(pl.BlockSpec((tm_in, LANES), row),), name="ffn_in")
        return residual_mm(act, f_w_out_b, layer, h, gains, tm_=tm, tn=tn, name="ffn_out",
                           a_buffers=1)

    h = x.reshape(t, d)
    hb, ssq = _prenorm(h, a_norm[0])

    for i in range(n_a):
        rope_lo = MLA_Q_LORA + MLA_KV_LORA
        w_in = jnp.concatenate([a_w_in[i], a_w_in[i][:, rope_lo:]], axis=1).astype(BF16)
        n_in = w_in.shape[1]
        cq, ckv, kr = _mm(
            hb, w_in, functools.partial(_epi_mla_in, d),
            (sds((t, MLA_Q_LORA), BF16), sds((t, MLA_KV_LORA), BF16), sds((t, LANES), BF16)),
            (pl.BlockSpec((tm, MLA_Q_LORA), row), pl.BlockSpec((tm, MLA_KV_LORA), row),
             pl.BlockSpec((tm, LANES), row)),
            tm=tm, tn=n_in, tk=1024, nj=1,
            extras=(ssq, a_q_norm[i].reshape(1, -1), a_kv_norm[i].reshape(1, -1), cos_mla, sin_mla),
            extra_specs=(ssq_spec,
                         pl.BlockSpec((1, MLA_Q_LORA), lambda i_, j, k: (0, 0)),
                         pl.BlockSpec((1, MLA_KV_LORA), lambda i_, j, k: (0, 0)),
                         tab_spec, tab_spec),
            name="mla_in")

        qd = MLA_NOPE + MLA_ROPE
        w_uq = a_w_uq[i].reshape(MLA_Q_LORA, heads, qd)
        w_uq = jnp.concatenate([w_uq[:, :, :MLA_NOPE].reshape(MLA_Q_LORA, heads * MLA_NOPE),
                                w_uq[:, :, MLA_NOPE:].reshape(MLA_Q_LORA, heads * MLA_ROPE)],
                               axis=1).astype(BF16)
        tn = 512
        n_nope_tiles = heads * MLA_NOPE // tn
        q_tab = pl.BlockSpec((None, tm, LANES),
                             lambda i_, j, k: (jnp.where(j >= n_nope_tiles, 1, 0), i_, 0))
        q = _mm(cq, w_uq, functools.partial(_epi_mla_q, qd ** -0.5 * LOG2_E),
                jax.ShapeDtypeStruct((t, heads * qd), BF16),
                pl.BlockSpec((tm, tn), lambda i_, j, k: (i_, j)),
                tm=tm, tn=tn, tk=MLA_Q_LORA, nj=heads * qd // tn,
                extras=(jnp.stack([jnp.ones_like(cos_mla), cos_mla]),
                        jnp.stack([jnp.zeros_like(sin_mla), sin_mla])),
                extra_specs=(q_tab, q_tab), name="mla_q")
        heads_per_tile = 4
        kn, vt = _mm(ckv, a_w_ukv[i].astype(BF16), _epi_mla_kv,
                     (jax.ShapeDtypeStruct((t, heads * MLA_NOPE), BF16),
                      jax.ShapeDtypeStruct((heads, t // VT_TILE, MLA_V, VT_TILE), BF16)),
                     (pl.BlockSpec((tm, heads_per_tile * MLA_NOPE), lambda i_, j, k: (i_, j)),
                      pl.BlockSpec((heads_per_tile, tm // VT_TILE, MLA_V, VT_TILE),
                                   lambda i_, j, k: (j, i_, 0, 0))),
                     tm=tm, tn=heads_per_tile * (MLA_NOPE + MLA_V), tk=MLA_KV_LORA,
                     nj=heads // heads_per_tile, name="mla_kv")
        o = _mla_attention(q, kn, kr, vt, batch, seq, heads)
        h, ssq, (hb,) = residual_mm(o, a_w_o[i].astype(BF16), None, h, [f_norm[i]],
                                    tm_=tm, tn=512, name="mla_out")
        gains = [a_norm[i + 1]] if i + 1 < n_a else [s_norm, b_norm[0]]
        h, ssq, hbs = ffn(hb, ssq, h, i, gains)
        hb = hbs[0]

    hb_s, hb_b = hbs
    n_sets = 3 * NSA_G
    per_tile = NSA_ROPE_TILE // NSA_DK
    tm_rope = 512
    rope_tab = pl.BlockSpec((tm_rope, NSA_ROPE_TILE), row)
    w_kv = s_w_kv.reshape(d, n_sets, NSA_DK + NSA_DV)
    w_k = w_kv[:, :, :NSA_DK].reshape(d, n_sets * NSA_DK).astype(BF16)
    w_v = w_kv[:, :, NSA_DK:].reshape(d, n_sets * NSA_DV).astype(BF16)
    k3 = _mm(hb_s, w_k, functools.partial(_epi_nsa_rope, d, None),
             sds((n_sets, t, NSA_DK), BF16),
             pl.BlockSpec((per_tile, tm_rope, NSA_DK), lambda i_, j, k: (j, i_, 0)),
             tm=tm_rope, tn=NSA_ROPE_TILE, tk=d, nj=n_sets // per_tile,
             extras=(ssq, cos_nsa, sin_nsa),
             extra_specs=(pl.BlockSpec((tm_rope, LANES), row), rope_tab, rope_tab), name="nsa_k")
    v3, vt3 = _mm(hb_s, w_v, functools.partial(_epi_nsa_v, d),
                  (sds((n_sets, t, NSA_DV), BF16),
                   sds((n_sets, t // VT_TILE, NSA_DV, VT_TILE), BF16)),
                  (pl.BlockSpec((per_tile, tm, NSA_DV), lambda i_, j, k: (j, i_, 0)),
                   pl.BlockSpec((per_tile, tm // VT_TILE, NSA_DV, VT_TILE),
                                lambda i_, j, k: (j, i_, 0, 0))),
                  tm=tm, tn=per_tile * NSA_DV, tk=d, nj=n_sets // per_tile,
                  extras=(ssq,), extra_specs=(ssq_spec,), name="nsa_v")

    ak = k3[:NSA_G].reshape(NSA_G, t // CMP_STRIDE, CMP_STRIDE * NSA_DK)
    av = v3[:NSA_G].reshape(NSA_G, t // CMP_STRIDE, CMP_STRIDE * NSA_DV)
    kc, vct = _compress(ak, av,
                        _compress_weights(s_cmp_w1_k, s_cmp_pe_k, s_cmp_w2_k, transpose_w2=False),
                        _compress_weights(s_cmp_w1_v, s_cmp_pe_v, s_cmp_w2_v, transpose_w2=True),
                        batch, seq)

    n_chunk = seq // CMP_STRIDE
    n_sel = seq // SEL_BLOCK
    c_start = np.arange(n_chunk) * CMP_STRIDE
    j_start = np.arange(n_sel) * SEL_BLOCK
    ovl = ((c_start[None, :] < j_start[:, None] + SEL_BLOCK)
           & (c_start[None, :] + CMP_BLOCK > j_start[:, None])
           & (np.arange(n_chunk)[None, :] < (seq - CMP_BLOCK) // CMP_STRIDE + 1))
    ovl = jnp.asarray(ovl, BF16)

    for i in range(n_b):
        n_q = heads * NSA_DK
        w_q = b_w_in[i][:, :n_q].astype(BF16)
        w_g = b_w_in[i][:, n_q:].reshape(d, NSA_G, 3 * hpg)
        w_g = jnp.pad(w_g, ((0, 0), (0, 0), (0, LANES - 3 * hpg))).reshape(d, NSA_G * LANES)
        w_g = w_g.astype(BF16)
        q = _mm(hb_b, w_q, functools.partial(_epi_nsa_rope, d, NSA_DK ** -0.5 * LOG2_E),
                sds((heads, t, NSA_DK), BF16),
                pl.BlockSpec((per_tile, tm_rope, NSA_DK), lambda i_, j, k: (j, i_, 0)),
                tm=tm_rope, tn=NSA_ROPE_TILE, tk=d, nj=heads // per_tile,
                extras=(ssq, cos_nsa, sin_nsa),
                extra_specs=(pl.BlockSpec((tm_rope, LANES), row), rope_tab, rope_tab),
                name="nsa_q")
        gates = _mm(hb_b, w_g, functools.partial(_epi_sigmoid, d), sds((t, NSA_G * LANES), F32),
                    pl.BlockSpec((tm, NSA_G * LANES), tile),
                    tm=tm, tn=NSA_G * LANES, tk=d, nj=1,
                    extras=(ssq,), extra_specs=(ssq_spec,), name="nsa_gates")
        o = _nsa_attention(q, kc, vct, k3, vt3, gates, ovl, batch, seq, hpg)
        h, ssq, (hb,) = residual_mm(o, b_w_o[i].astype(BF16), None, h, [f_norm[n_a + i]],
                                    tm_=tm, tn=512, name="nsa_out")
        gains = [b_norm[i + 1]] if i + 1 < n_b else []
        h, ssq, hbs = ffn(hb, ssq, h, n_a + i, gains)
        if hbs:
            hb_b = hbs[0]

    out = _rmsnorm(h, final_norm, F32)
    return out.reshape(batch, seq, d)
```

```python
import functools

import numpy as np
import jax
import jax.numpy as jnp
from jax import lax
from jax.experimental import pallas as pl
from jax.experimental.pallas import tpu as pltpu

F32 = jnp.float32
BF16 = jnp.bfloat16

LANES = 128
VMEM_LIMIT_BYTES = 56 * 1024 * 1024

NORM_EPS = 1e-6
ROPE_THETA = 10000.0

MLA_Q_LORA = 1536
MLA_KV_LORA = 512
MLA_NOPE = 128
MLA_ROPE = 64
MLA_V = 128

NSA_G = 4
NSA_DK = 192
NSA_DV = 128
NSA_HALF = NSA_DK // 2
NSA_ROPE_TILE = 4 * NSA_DK
CMP_BLOCK = 32
CMP_STRIDE = 16
SEL_BLOCK = 64
SEL_TOPK = 16
WINDOW = 512
FORCE_BONUS = 1e4
NEG_BIG = -1e30
LOG2_E = 1.4426950408889634

VT_TILE = LANES
NT_DIMS = (((1,), (1,)), ((), ()))


def _params(sem):
    return pltpu.CompilerParams(dimension_semantics=sem, vmem_limit_bytes=VMEM_LIMIT_BYTES)


def _rmsnorm_kernel(x_ref, g_ref, o_ref):
    x = x_ref[...]
    ms = jnp.mean(x * x, axis=-1, keepdims=True)
    o_ref[...] = (x * lax.rsqrt(ms + NORM_EPS) * g_ref[...]).astype(o_ref.dtype)


def _rmsnorm(x, g, out_dtype, tm=256):
    t, d = x.shape
    return pl.pallas_call(
        _rmsnorm_kernel,
        grid=(t // tm,),
        in_specs=[pl.BlockSpec((tm, d), lambda i: (i, 0)),
                  pl.BlockSpec((1, d), lambda i: (0, 0))],
        out_specs=pl.BlockSpec((tm, d), lambda i: (i, 0)),
        out_shape=jax.ShapeDtypeStruct((t, d), out_dtype),
        compiler_params=_params(("parallel",)),
        name="rmsnorm",
    )(x, g.reshape(1, d))


def _prenorm_kernel(x_ref, g_ref, hb_ref, ssq_ref):
    x = x_ref[...]
    hb_ref[...] = (x * g_ref[...]).astype(hb_ref.dtype)
    ssq_ref[...] = jnp.broadcast_to(jnp.sum(x * x, axis=-1, keepdims=True), ssq_ref.shape)


def _prenorm(x, g, tm=256):
    t, d = x.shape
    return pl.pallas_call(
        _prenorm_kernel,
        grid=(t // tm,),
        in_specs=[pl.BlockSpec((tm, d), lambda i: (i, 0)),
                  pl.BlockSpec((1, d), lambda i: (0, 0))],
        out_specs=(pl.BlockSpec((tm, d), lambda i: (i, 0)),
                   pl.BlockSpec((tm, LANES), lambda i: (i, 0))),
        out_shape=(jax.ShapeDtypeStruct((t, d), BF16), jax.ShapeDtypeStruct((t, LANES), F32)),
        compiler_params=_params(("parallel",)),
        name="prenorm",
    )(x, g.reshape(1, d))


def _mm_kernel(n_w, n_e, n_o, nk, epilogue, first_tile_init, *refs):
    a_ref = refs[0]
    w_refs = refs[1:1 + n_w]
    e_refs = refs[1 + n_w:1 + n_w + n_e]
    o_refs = refs[1 + n_w + n_e:1 + n_w + n_e + n_o]
    acc_refs = refs[1 + n_w + n_e + n_o:]
    j = pl.program_id(1)
    if first_tile_init is not None:
        pl.when((j == 0) & (pl.program_id(2) == 0))(lambda: first_tile_init(o_refs))

    def product(w):
        return jnp.dot(a_ref[...], w[...].astype(a_ref.dtype), preferred_element_type=F32)

    if nk == 1:
        epilogue([product(w) for w in w_refs], e_refs, o_refs, j)
        return
    k = pl.program_id(2)

    @pl.when(k == 0)
    def _():
        for w, acc in zip(w_refs, acc_refs):
            acc[...] = product(w)

    @pl.when(k > 0)
    def _():
        for w, acc in zip(w_refs, acc_refs):
            acc[...] += product(w)

    @pl.when(k == nk - 1)
    def _():
        epilogue([acc[...] for acc in acc_refs], e_refs, o_refs, j)


def _mm(a, w, epilogue, out_shape, out_specs, *, tm, tn, tk, nj, w_col_maps=None, w_layer=None,
        extras=(), extra_specs=(), a_buffers=None, first_tile_init=None, name="matmul"):
    m, kdim = a.shape
    ni, nk = m // tm, kdim // tk
    assert ni * tm == m and nk * tk == kdim
    if w_col_maps is None:
        w_col_maps = (lambda j: j,)
    n_w = len(w_col_maps)
    single = not isinstance(out_shape, (tuple, list))
    out_shapes = (out_shape,) if single else tuple(out_shape)
    out_specs_t = (out_specs,) if single else tuple(out_specs)
    a_kwargs = {} if a_buffers is None else {"pipeline_mode": pl.Buffered(a_buffers)}
    in_specs = [pl.BlockSpec((tm, tk), lambda i, j, k: (i, k), **a_kwargs)]
    for cm in w_col_maps:
        if w_layer is None:
            in_specs.append(pl.BlockSpec((tk, tn), functools.partial(
                lambda i, j, k, cm: (k, cm(j)), cm=cm)))
        else:
            in_specs.append(pl.BlockSpec((None, tk, tn), functools.partial(
                lambda i, j, k, cm: (w_layer, k, cm(j)), cm=cm)))
    in_specs += list(extra_specs)
    scratch = [pltpu.VMEM((tm, tn), F32) for _ in range(n_w)] if nk > 1 else []
    kern = functools.partial(_mm_kernel, n_w, len(extras), len(out_shapes), nk, epilogue,
                             first_tile_init)
    res = pl.pallas_call(
        kern,
        grid=(ni, nj, nk),
        in_specs=in_specs,
        out_specs=out_specs_t,
        out_shape=out_shapes,
        scratch_shapes=scratch,
        compiler_params=_params(("parallel", "arbitrary", "arbitrary")),
        name=name,
    )(a, *([w] * n_w), *extras)
    return res[0] if single else res


def _rope_pairs_64(x, cos2, sin2):
    lane = lax.broadcasted_iota(jnp.int32, x.shape, 1)
    first_half = (lane % MLA_ROPE) < (MLA_ROPE // 2)
    partner = jnp.where(first_half,
                        pltpu.roll(x, LANES - MLA_ROPE // 2, 1),
                        pltpu.roll(x, MLA_ROPE // 2, 1))
    return x * cos2 + partner * sin2


def _store_vt_tiles(vt_ref, lead, v):
    for c in range(v.shape[0] // VT_TILE):
        vt_ref[lead + (c,)] = v[c * VT_TILE:(c + 1) * VT_TILE, :].T.astype(vt_ref.dtype)


def _row_scale(acc, ssq_ref, d_model):
    rstd = lax.rsqrt(ssq_ref[...] * (1.0 / d_model) + NORM_EPS)
    return acc * jnp.concatenate([rstd] * (acc.shape[1] // LANES), axis=1)


def _epi_residual(accs, e_refs, o_refs, j):
    o_refs[0][...] = e_refs[0][...] + accs[0]


def _zero_ssq(o_refs):
    o_refs[1][...] = jnp.zeros(o_refs[1].shape, F32)


def _epi_residual_prenorm(accs, e_refs, o_refs, j):
    h = e_refs[0][...] + accs[0]
    o_refs[0][...] = h
    ssq_ref = o_refs[1]
    ssq_ref[...] += jnp.broadcast_to(jnp.sum(h * h, axis=-1, keepdims=True), ssq_ref.shape)
    for g_ref, hb_ref in zip(e_refs[1:], o_refs[2:]):
        hb_ref[...] = (h * g_ref[...]).astype(hb_ref.dtype)


def _epi_swiglu(d_model, accs, e_refs, o_refs, j):
    a = _row_scale(accs[0], e_refs[0], d_model)
    b = _row_scale(accs[1], e_refs[0], d_model)
    o_refs[0][...] = (a * jax.nn.sigmoid(a) * b).astype(o_refs[0].dtype)
    if len(o_refs) > 1:
        o_refs[1][...] = e_refs[1][...].astype(o_refs[1].dtype)


def _epi_sigmoid(d_model, accs, e_refs, o_refs, j):
    o_refs[0][...] = jax.nn.sigmoid(_row_scale(accs[0], e_refs[0], d_model))


def _epi_mla_in(d_model, accs, e_refs, o_refs, j):
    ssq_ref, gq_ref, gkv_ref, cos_ref, sin_ref = e_refs
    u = _row_scale(accs[0], ssq_ref, d_model)
    cq = u[:, :MLA_Q_LORA]
    ms = jnp.mean(cq * cq, axis=-1, keepdims=True)
    o_refs[0][...] = (cq * lax.rsqrt(ms + NORM_EPS) * gq_ref[...]).astype(BF16)
    ckv = u[:, MLA_Q_LORA:MLA_Q_LORA + MLA_KV_LORA]
    ms = jnp.mean(ckv * ckv, axis=-1, keepdims=True)
    o_refs[1][...] = (ckv * lax.rsqrt(ms + NORM_EPS) * gkv_ref[...]).astype(BF16)
    kr = u[:, MLA_Q_LORA + MLA_KV_LORA:]
    o_refs[2][...] = _rope_pairs_64(kr, cos_ref[...], sin_ref[...]).astype(BF16)


def _epi_mla_q(n_nope_tiles, scale, accs, e_refs, o_refs, j):
    cos_ref, sin_ref = e_refs
    acc = accs[0]

    @pl.when(j < n_nope_tiles)
    def _():
        o_refs[0][...] = (acc * scale).astype(BF16)

    @pl.when(j >= n_nope_tiles)
    def _():
        for c in range(acc.shape[1] // LANES):
            x = acc[:, c * LANES:(c + 1) * LANES]
            r = _rope_pairs_64(x, cos_ref[...], sin_ref[...])
            o_refs[0][:, c * LANES:(c + 1) * LANES] = (r * scale).astype(BF16)


def _epi_mla_kv(accs, e_refs, o_refs, j):
    acc = accs[0]
    kn_ref, vt_ref = o_refs
    per = MLA_NOPE + MLA_V
    for r in range(acc.shape[1] // per):
        kn_ref[:, r * MLA_NOPE:(r + 1) * MLA_NOPE] = acc[:, r * per:r * per + MLA_NOPE].astype(BF16)
        _store_vt_tiles(vt_ref, (r,), acc[:, r * per + MLA_NOPE:(r + 1) * per])


def _rope_192(x, cos, sin):
    width = x.shape[1]
    lane = lax.broadcasted_iota(jnp.int32, x.shape, 1)
    partner = jnp.where((lane % NSA_DK) < NSA_HALF,
                        pltpu.roll(x, width - NSA_HALF, 1),
                        pltpu.roll(x, NSA_HALF, 1))
    return x * cos + partner * sin


def _epi_nsa_rope(d_model, scale, accs, e_refs, o_refs, j):
    ssq_ref, cos_ref, sin_ref = e_refs
    acc = _row_scale(accs[0], ssq_ref, d_model)
    roped = _rope_192(acc, cos_ref[...], sin_ref[...])
    if scale is not None:
        roped = roped * scale
    for r in range(acc.shape[1] // NSA_DK):
        o_refs[0][r] = roped[:, r * NSA_DK:(r + 1) * NSA_DK].astype(BF16)


def _epi_nsa_v(d_model, accs, e_refs, o_refs, j):
    v_ref, vt_ref = o_refs
    acc = _row_scale(accs[0], e_refs[0], d_model)
    for r in range(acc.shape[1] // NSA_DV):
        v = acc[:, r * NSA_DV:(r + 1) * NSA_DV]
        v_ref[r] = v.astype(BF16)
        _store_vt_tiles(vt_ref, (r,), v)


def _bias_chunk(bias, start, width):
    period = bias.shape[1]
    if period >= width:
        off = start % period
        return bias[:, off:off + width]
    return jnp.concatenate([bias] * (width // period), axis=1)


def _flash_step_t(q, k, vt, bias, m, l, acc_ref, n_split=1):
    width = q.shape[0] // n_split
    m_out, l_out = [], []
    for c in range(n_split):
        sl = slice(c * width, (c + 1) * width)
        s = lax.dot_general(k, q[sl], NT_DIMS, preferred_element_type=F32)
        if bias is not None:
            s = s + _bias_chunk(bias, c * width, width)
        m_new = jnp.maximum(m[:, sl], jnp.max(s, axis=0, keepdims=True))
        alpha = jnp.exp2(m[:, sl] - m_new)
        p = jnp.exp2(s - m_new)
        l_out.append(alpha * l[:, sl] + jnp.sum(p, axis=0, keepdims=True))
        acc_ref[:, sl] = alpha * acc_ref[:, sl] + jnp.dot(vt, p.astype(BF16),
                                                          preferred_element_type=F32)
        m_out.append(m_new)
    return jnp.concatenate(m_out, axis=1), jnp.concatenate(l_out, axis=1)


def _vt_tiles(vt_ref, lead, c0, n):
    return jnp.concatenate([vt_ref[lead + (c0 + c,)] for c in range(n)], axis=1)


def _mla_attn_kernel(t_blk, seq, qn_ref, qr_ref, kn_ref, kr_ref, vt_ref, o_ref, acc_ref):
    h = pl.program_id(1)
    n_vt = t_blk // VT_TILE
    lane_half = lax.broadcasted_iota(jnp.int32, (t_blk, LANES), 1) // MLA_ROPE
    kpos = lax.broadcasted_iota(jnp.int32, (t_blk, t_blk), 0)
    qpos = lax.broadcasted_iota(jnp.int32, (t_blk, t_blk), 1)
    diag_bias = jnp.where(kpos <= qpos, 0.0, NEG_BIG).astype(F32)

    for qi in range(seq // t_blk):
        q0 = qi * t_blk
        qr = qr_ref[q0:q0 + t_blk, :]
        qr = jnp.where(lane_half == (h % 2), qr, jnp.zeros_like(qr))
        q = jnp.concatenate([qn_ref[q0:q0 + t_blk, :], qr], axis=1)
        acc_ref[...] = jnp.zeros(acc_ref.shape, F32)
        m = jnp.full((1, t_blk), NEG_BIG, F32)
        l = jnp.zeros((1, t_blk), F32)
        for j in range(qi + 1):
            k0 = j * t_blk
            k = jnp.concatenate([kn_ref[k0:k0 + t_blk, :], kr_ref[k0:k0 + t_blk, :]], axis=1)
            vt = _vt_tiles(vt_ref, (0,), j * n_vt, n_vt)
            m, l = _flash_step_t(q, k, vt, diag_bias if j == qi else None, m, l, acc_ref)
        o_ref[q0:q0 + t_blk, :] = (acc_ref[...] / l).T.astype(o_ref.dtype)


def _mla_attention(q, kn, kr, vt, batch, seq, heads, t_blk=512):
    t = batch * seq
    n_nope_blk = heads * MLA_NOPE // LANES
    n_vt = seq // VT_TILE
    kern = functools.partial(_mla_attn_kernel, t_blk, seq)
    return pl.pallas_call(
        kern,
        grid=(batch, heads),
        in_specs=[
            pl.BlockSpec((seq, MLA_NOPE), lambda b, h: (b, h)),
            pl.BlockSpec((seq, LANES), lambda b, h: (b, n_nope_blk + h // 2)),
            pl.BlockSpec((seq, MLA_NOPE), lambda b, h: (b, h)),
            pl.BlockSpec((seq, LANES), lambda b, h: (b, 0)),
            pl.BlockSpec((1, n_vt, MLA_V, VT_TILE), lambda b, h: (h, b, 0, 0)),
        ],
        out_specs=pl.BlockSpec((seq, MLA_V), lambda b, h: (b, h)),
        out_shape=jax.ShapeDtypeStruct((t, heads * MLA_V), BF16),
        scratch_shapes=[pltpu.VMEM((MLA_V, t_blk), F32)],
        compiler_params=_params(("parallel", "parallel")),
        name="mla_attention",
    )(q, q, kn, kr, vt)


def _compress_mlp(a, w1_ref, pe_ref, w2_ref, transposed_out):
    y_lo = jnp.dot(a, w1_ref[0], preferred_element_type=F32)
    y_hi = jnp.dot(a, w1_ref[1], preferred_element_type=F32)
    pe_c = (jnp.dot(pe_ref[0], w1_ref[0], preferred_element_type=F32)
            + jnp.dot(pe_ref[1], w1_ref[1], preferred_element_type=F32))
    n = y_hi.shape[0]
    hid = y_lo + pltpu.roll(y_hi, n - 1, 0) + pe_c[0:1, :]
    hid = jax.nn.gelu(hid, approximate=True)
    if transposed_out:
        return lax.dot_general(w2_ref[...], hid.astype(BF16), NT_DIMS, preferred_element_type=F32)
    return jnp.dot(hid.astype(BF16), w2_ref[...], preferred_element_type=F32)


def _compress_kernel(ak_ref, av_ref, w1k_ref, pek_ref, w2k_ref, w1v_ref, pev_ref, w2vt_ref,
                     kc_ref, vct_ref):
    kc_ref[0] = _compress_mlp(ak_ref[0], w1k_ref, pek_ref, w2k_ref, False).astype(BF16)
    vct_ref[0] = _compress_mlp(av_ref[0], w1v_ref, pev_ref, w2vt_ref, True).astype(BF16)


def _compress(ak, av, wk, wv, batch, seq):
    n_chunk = seq // CMP_STRIDE
    const3 = lambda b, g: (0, 0, 0)
    const2 = lambda b, g: (0, 0)

    def wspecs(w):
        w1, pe, w2 = w
        return [pl.BlockSpec(w1.shape, const3), pl.BlockSpec(pe.shape, const3),
                pl.BlockSpec(w2.shape, const2)]

    return pl.pallas_call(
        _compress_kernel,
        grid=(batch, NSA_G),
        in_specs=[pl.BlockSpec((1, n_chunk, ak.shape[2]), lambda b, g: (g, b, 0)),
                  pl.BlockSpec((1, n_chunk, av.shape[2]), lambda b, g: (g, b, 0))]
        + wspecs(wk) + wspecs(wv),
        out_specs=(pl.BlockSpec((1, n_chunk, NSA_DK), lambda b, g: (b * NSA_G + g, 0, 0)),
                   pl.BlockSpec((1, NSA_DV, n_chunk), lambda b, g: (b * NSA_G + g, 0, 0))),
        out_shape=(jax.ShapeDtypeStruct((batch * NSA_G, n_chunk, NSA_DK), BF16),
                   jax.ShapeDtypeStruct((batch * NSA_G, NSA_DV, n_chunk), BF16)),
        compiler_params=_params(("parallel", "parallel")),
        name="nsa_compress",
    )(ak, av, *wk, *wv)


def _nsa_attn_kernel(hpg, tq, tk, seq, q_ref, kc_ref, vct_ref, ks_ref, vst_ref, kw_ref, vwt_ref,
                     gate_ref, ovl_ref, o_ref, bias_ref, oc_ref, os_ref, acc_ref):
    qi = pl.program_id(2)
    t0 = qi * tq
    n_rows = hpg * tq
    n_cmp = kc_ref.shape[1]
    n_sel = seq // SEL_BLOCK
    n_vt = tk // VT_TILE
    q = q_ref[...].reshape(n_rows, NSA_DK)

    def tile_heads(x):
        return jnp.concatenate([x] * hpg, axis=1)

    s = lax.dot_general(kc_ref[0], q, NT_DIMS, preferred_element_type=F32)
    cidx = lax.broadcasted_iota(jnp.int32, (n_cmp, tq), 0)
    tpos_c = t0 + lax.broadcasted_iota(jnp.int32, (n_cmp, tq), 1)
    mask_c = tile_heads((cidx * CMP_STRIDE + CMP_BLOCK - 1) <= tpos_c)
    s = jnp.where(mask_c, s, NEG_BIG)
    mx = jnp.max(s, axis=0, keepdims=True)
    e = jnp.where(mask_c, jnp.exp2(s - mx), 0.0)
    p = e / jnp.maximum(jnp.sum(e, axis=0, keepdims=True), 1e-30)
    oc_ref[...] = jnp.dot(vct_ref[0], p.astype(BF16), preferred_element_type=F32)

    psum = p[:, 0:tq]
    for h in range(1, hpg):
        psum = psum + p[:, h * tq:(h + 1) * tq]
    p1 = psum.astype(BF16)
    p2 = (psum - p1.astype(F32)).astype(BF16)
    p3 = (psum - p1.astype(F32) - p2.astype(F32)).astype(BF16)
    ovl = ovl_ref[...]
    imp = (jnp.dot(ovl, p1, preferred_element_type=F32)
           + jnp.dot(ovl, p2, preferred_element_type=F32)
           + jnp.dot(ovl, p3, preferred_element_type=F32))
    jrow = lax.broadcasted_iota(jnp.int32, (n_sel, tq), 0)
    tpos = t0 + lax.broadcasted_iota(jnp.int32, (n_sel, tq), 1)
    valid = jrow * SEL_BLOCK <= tpos
    cur = jnp.right_shift(tpos, SEL_BLOCK.bit_length() - 1)
    forced = valid & ((jrow == 0) | (jrow == cur) | (jrow == cur - 1))
    score = jnp.where(valid, imp + jnp.where(forced, FORCE_BONUS, 0.0), -jnp.inf)
    rank = jnp.zeros((n_sel, tq), jnp.int32)
    for i in range(n_sel):
        row = score[i:i + 1, :]
        beats = (row > score) | ((row == score) & (jrow > i))
        rank = rank + beats.astype(jnp.int32)
    sel_bias = jnp.where(valid & (rank < SEL_TOPK), 0.0, NEG_BIG).astype(F32)
    n_pairs = ((t0 + tq - 1) // tk + 2) // 2
    blocks_per_tile = tk // SEL_BLOCK
    for jj in range(seq // tk):
        @pl.when(jj < 2 * n_pairs)
        def _():
            rows = [jnp.broadcast_to(sel_bias[b:b + 1, :], (SEL_BLOCK, tq))
                    for b in range(jj * blocks_per_tile, (jj + 1) * blocks_per_tile)]
            kpos = jj * tk + lax.broadcasted_iota(jnp.int32, (tk, tq), 0)
            tq_pos = t0 + lax.broadcasted_iota(jnp.int32, (tk, tq), 1)
            bias_ref[jj] = jnp.where(kpos <= tq_pos, jnp.concatenate(rows, axis=0), NEG_BIG)

    m0 = jnp.full((1, n_rows), NEG_BIG, F32)
    l0 = jnp.zeros((1, n_rows), F32)

    acc_ref[...] = jnp.zeros(acc_ref.shape, F32)

    def sel_step(j, ml):
        k0 = pl.multiple_of(j * tk, tk)
        k = ks_ref[0, pl.ds(k0, tk), :]
        vt = _vt_tiles(vst_ref, (0,), j * n_vt, n_vt)
        return _flash_step_t(q, k, vt, bias_ref[j], ml[0], ml[1], acc_ref)

    def sel_pair(pair, ml):
        return sel_step(2 * pair + 1, sel_step(2 * pair, ml))

    _, l = lax.fori_loop(0, n_pairs, sel_pair, (m0, l0))
    os_ref[...] = acc_ref[...] / l

    acc_ref[...] = jnp.zeros(acc_ref.shape, F32)
    span = -(-(WINDOW + tq) // tk) * tk
    k_start = jnp.maximum(t0 + tq - span, 0)

    def win_step(c, ml):
        k0 = pl.multiple_of(k_start + c * tk, VT_TILE)
        k = kw_ref[0, pl.ds(k0, tk), :]
        vt = _vt_tiles(vwt_ref, (0,), k0 // VT_TILE, n_vt)
        kpos = k0 + lax.broadcasted_iota(jnp.int32, (tk, tq), 0)
        tq_pos = t0 + lax.broadcasted_iota(jnp.int32, (tk, tq), 1)
        ok = (kpos <= tq_pos) & (tq_pos - kpos < WINDOW)
        bias = jnp.where(ok, 0.0, NEG_BIG).astype(F32)
        return _flash_step_t(q, k, vt, bias, ml[0], ml[1], acc_ref)

    ml = (m0, l0)
    for c in range(span // tk):
        ml = win_step(c, ml)
    o_w = acc_ref[...] / ml[1]

    gates_t = gate_ref[...].T
    for h in range(hpg):
        cols = slice(h * tq, (h + 1) * tq)
        o = (gates_t[3 * h:3 * h + 1, :] * oc_ref[:, cols]
             + gates_t[3 * h + 1:3 * h + 2, :] * os_ref[:, cols]
             + gates_t[3 * h + 2:3 * h + 3, :] * o_w[:, cols])
        o_ref[:, h * NSA_DV:(h + 1) * NSA_DV] = o.T.astype(o_ref.dtype)


def _nsa_attention(q, kc, vct, k3, vt3, gates, ovl, batch, seq, hpg, tq=128, tk=256):
    t = batch * seq
    nq = seq // tq
    n_chunk = seq // CMP_STRIDE
    n_rows = hpg * tq
    n_vt = seq // VT_TILE
    assert (WINDOW + tq) % VT_TILE == 0 and seq >= -(-(WINDOW + tq) // tk) * tk
    kern = functools.partial(_nsa_attn_kernel, hpg, tq, tk, seq)

    def k_spec(branch):
        return pl.BlockSpec((1, seq, NSA_DK), lambda b, g, qi: (branch * NSA_G + g, b, 0))

    def vt_spec(branch):
        return pl.BlockSpec((1, n_vt, NSA_DV, VT_TILE), lambda b, g, qi: (branch * NSA_G + g, b, 0, 0))

    return pl.pallas_call(
        kern,
        grid=(batch, NSA_G, nq),
        in_specs=[
            pl.BlockSpec((hpg, tq, NSA_DK), lambda b, g, qi: (g, b * nq + qi, 0)),
            pl.BlockSpec((1, n_chunk, NSA_DK), lambda b, g, qi: (b * NSA_G + g, 0, 0)),
            pl.BlockSpec((1, NSA_DV, n_chunk), lambda b, g, qi: (b * NSA_G + g, 0, 0)),
            k_spec(1), vt_spec(1), k_spec(2), vt_spec(2),
            pl.BlockSpec((tq, LANES), lambda b, g, qi: (b * nq + qi, g)),
            pl.BlockSpec(ovl.shape, lambda b, g, qi: (0, 0)),
        ],
        out_specs=pl.BlockSpec((tq, hpg * NSA_DV), lambda b, g, qi: (b * nq + qi, g)),
        out_shape=jax.ShapeDtypeStruct((t, NSA_G * hpg * NSA_DV), BF16),
        scratch_shapes=[
            pltpu.VMEM((seq // tk, tk, tq), F32),
            pltpu.VMEM((NSA_DV, n_rows), F32),
            pltpu.VMEM((NSA_DV, n_rows), F32),
            pltpu.VMEM((NSA_DV, n_rows), F32),
        ],
        compiler_params=_params(("parallel", "parallel", "arbitrary")),
        name="nsa_attention",
    )(q, kc, vct, k3, vt3, k3, vt3, gates, ovl)


def _rope_tables(pos_flat):
    pos = pos_flat.astype(F32)[:, None]
    inv64 = ROPE_THETA ** (-jnp.arange(0, MLA_ROPE, 2, dtype=F32) / MLA_ROPE)
    ang = pos * inv64
    c, s = jnp.cos(ang), jnp.sin(ang)
    cos_mla = jnp.concatenate([c, c, c, c], axis=1)
    sin_mla = jnp.concatenate([-s, s, -s, s], axis=1)
    inv192 = ROPE_THETA ** (-jnp.arange(0, NSA_DK, 2, dtype=F32) / NSA_DK)
    ang = pos * inv192
    c, s = jnp.cos(ang), jnp.sin(ang)
    reps = NSA_ROPE_TILE // NSA_DK
    cos_nsa = jnp.concatenate([c, c] * reps, axis=1)
    sin_nsa = jnp.concatenate([-s, s] * reps, axis=1)
    return cos_mla, sin_mla, cos_nsa, sin_nsa


def _compress_weights(w1, pe, w2, *, transpose_w2):
    width, d_hid = w1.shape[0] // CMP_BLOCK, w1.shape[1]
    hid_p = -(-d_hid // LANES) * LANES
    w1 = jnp.pad(w1, ((0, 0), (0, hid_p - d_hid)))
    w1 = w1.reshape(2, CMP_STRIDE * width, hid_p).astype(BF16)
    pe = jnp.pad(pe.reshape(2, 1, CMP_STRIDE * width), ((0, 0), (0, 7), (0, 0))).astype(BF16)
    w2 = jnp.pad(w2, ((0, hid_p - d_hid), (0, 0))).astype(BF16)
    return w1, pe, (w2.T if transpose_w2 else w2)


def kernel(x, positions, a_norm, a_w_in, a_q_norm, a_w_uq, a_kv_norm, a_w_ukv, a_w_o, s_norm, s_w_kv, s_cmp_pe_k, s_cmp_w1_k, s_cmp_w2_k, s_cmp_pe_v, s_cmp_w1_v, s_cmp_w2_v, b_norm, b_w_in, b_w_o, f_norm, f_w_in, f_w_out, final_norm):
    batch, seq, d = x.shape
    t = batch * seq
    heads = d // 128
    hpg = heads // NSA_G
    d_ff = f_w_out.shape[1]
    n_a, n_b = a_w_in.shape[0], b_w_in.shape[0]
    assert n_a >= 1 and n_b >= 1
    tm = 1024
    sds = jax.ShapeDtypeStruct

    cos_mla, sin_mla, cos_nsa, sin_nsa = _rope_tables(positions.reshape(t))
    row = lambda i, j, k: (i, 0)
    tile = lambda i, j, k: (i, j)
    tab_spec = pl.BlockSpec((tm, LANES), row)
    ssq_spec = pl.BlockSpec((tm, LANES), row)

    def residual_mm(a, w, w_layer, h, gains, *, tm_, tn, name, a_buffers=None):
        blk = pl.BlockSpec((tm_, tn), tile)
        common = dict(tm=tm_, tn=tn, tk=a.shape[1], nj=d // tn, w_layer=w_layer, name=name,
                      a_buffers=a_buffers)
        if not gains:
            h_new = _mm(a, w, _epi_residual, sds((t, d), F32), blk,
                        extras=(h,), extra_specs=(blk,), **common)
            return h_new, None, ()
        outs = _mm(a, w, _epi_residual_prenorm,
                   (sds((t, d), F32), sds((t, LANES), F32)) + tuple(sds((t, d), BF16) for _ in gains),
                   (blk, pl.BlockSpec((tm_, LANES), row)) + tuple(blk for _ in gains),
                   extras=(h,) + tuple(g.reshape(1, d) for g in gains),
                   extra_specs=(blk,) + tuple(pl.BlockSpec((1, tn), lambda i, j, k: (0, j))
                                              for _ in gains),
                   first_tile_init=_zero_ssq, **common)
        return outs[0], outs[1], tuple(outs[2:])

    def ffn(hb, ssq, h, layer, gains):
        tn = 256
        nj = d_ff // tn
        tm_in = min(2 * tm, t)
        steps = (t // tm_in) * nj
        slab = d_ff // steps
        assert slab * steps == d_ff and slab % 16 == 0
        act, w_out = _mm(
            hb, f_w_in, functools.partial(_epi_swiglu, d),
            (sds((t, d_ff), BF16), sds((d_ff, d), BF16)),
            (pl.BlockSpec((tm_in, tn), tile),
             pl.BlockSpec((slab, d), lambda i, j, k: (i * nj + j, 0))),
            tm=tm_in, tn=tn, tk=d, nj=nj, w_layer=layer, a_buffers=1,
            w_col_maps=(lambda j: j, lambda j: j + nj),
            extras=(ssq, f_w_out),
            extra_specs=(pl.BlockSpec((tm_in, LANES), row),
                         pl.BlockSpec((None, slab, d), lambda i, j, k: (layer, i * nj + j, 0))),
            name="ffn_in")
        return residual_mm(act, w_out, None, h, gains, tm_=tm, tn=tn, name="ffn_out",
                           a_buffers=1)

    h = x.reshape(t, d)
    hb, ssq = _prenorm(h, a_norm[0])

    for i in range(n_a):
        rope_lo = MLA_Q_LORA + MLA_KV_LORA
        w_in = jnp.concatenate([a_w_in[i], a_w_in[i][:, rope_lo:]], axis=1).astype(BF16)
        n_in = w_in.shape[1]
        cq, ckv, kr = _mm(
            hb, w_in, functools.partial(_epi_mla_in, d),
            (sds((t, MLA_Q_LORA), BF16), sds((t, MLA_KV_LORA), BF16), sds((t, LANES), BF16)),
            (pl.BlockSpec((tm, MLA_Q_LORA), row), pl.BlockSpec((tm, MLA_KV_LORA), row),
             pl.BlockSpec((tm, LANES), row)),
            tm=tm, tn=n_in, tk=1024, nj=1,
            extras=(ssq, a_q_norm[i].reshape(1, -1), a_kv_norm[i].reshape(1, -1), cos_mla, sin_mla),
            extra_specs=(ssq_spec,
                         pl.BlockSpec((1, MLA_Q_LORA), lambda i_, j, k: (0, 0)),
                         pl.BlockSpec((1, MLA_KV_LORA), lambda i_, j, k: (0, 0)),
                         tab_spec, tab_spec),
            name="mla_in")

        qd = MLA_NOPE + MLA_ROPE
        w_uq = a_w_uq[i].reshape(MLA_Q_LORA, heads, qd)
        w_uq = jnp.concatenate([w_uq[:, :, :MLA_NOPE].reshape(MLA_Q_LORA, heads * MLA_NOPE),
                                w_uq[:, :, MLA_NOPE:].reshape(MLA_Q_LORA, heads * MLA_ROPE)],
                               axis=1).astype(BF16)
        tn = 512
        q = _mm(cq, w_uq,
                functools.partial(_epi_mla_q, heads * MLA_NOPE // tn, qd ** -0.5 * LOG2_E),
                jax.ShapeDtypeStruct((t, heads * qd), BF16),
                pl.BlockSpec((tm, tn), lambda i_, j, k: (i_, j)),
                tm=tm, tn=tn, tk=MLA_Q_LORA, nj=heads * qd // tn,
                extras=(cos_mla, sin_mla), extra_specs=(tab_spec, tab_spec), name="mla_q")
        heads_per_tile = 4
        kn, vt = _mm(ckv, a_w_ukv[i].astype(BF16), _epi_mla_kv,
                     (jax.ShapeDtypeStruct((t, heads * MLA_NOPE), BF16),
                      jax.ShapeDtypeStruct((heads, t // VT_TILE, MLA_V, VT_TILE), BF16)),
                     (pl.BlockSpec((tm, heads_per_tile * MLA_NOPE), lambda i_, j, k: (i_, j)),
                      pl.BlockSpec((heads_per_tile, tm // VT_TILE, MLA_V, VT_TILE),
                                   lambda i_, j, k: (j, i_, 0, 0))),
                     tm=tm, tn=heads_per_tile * (MLA_NOPE + MLA_V), tk=MLA_KV_LORA,
                     nj=heads // heads_per_tile, name="mla_kv")
        o = _mla_attention(q, kn, kr, vt, batch, seq, heads)
        h, ssq, (hb,) = residual_mm(o, a_w_o[i].astype(BF16), None, h, [f_norm[i]],
                                    tm_=tm, tn=512, name="mla_out")
        gains = [a_norm[i + 1]] if i + 1 < n_a else [s_norm, b_norm[0]]
        h, ssq, hbs = ffn(hb, ssq, h, i, gains)
        hb = hbs[0]

    hb_s, hb_b = hbs
    n_sets = 3 * NSA_G
    per_tile = NSA_ROPE_TILE // NSA_DK
    tm_rope = 512
    rope_tab = pl.BlockSpec((tm_rope, NSA_ROPE_TILE), row)
    w_kv = s_w_kv.reshape(d, n_sets, NSA_DK + NSA_DV)
    w_k = w_kv[:, :, :NSA_DK].reshape(d, n_sets * NSA_DK).astype(BF16)
    w_v = w_kv[:, :, NSA_DK:].reshape(d, n_sets * NSA_DV).astype(BF16)
    k3 = _mm(hb_s, w_k, functools.partial(_epi_nsa_rope, d, None),
             sds((n_sets, t, NSA_DK), BF16),
             pl.BlockSpec((per_tile, tm_rope, NSA_DK), lambda i_, j, k: (j, i_, 0)),
             tm=tm_rope, tn=NSA_ROPE_TILE, tk=d, nj=n_sets // per_tile,
             extras=(ssq, cos_nsa, sin_nsa),
             extra_specs=(pl.BlockSpec((tm_rope, LANES), row), rope_tab, rope_tab), name="nsa_k")
    v3, vt3 = _mm(hb_s, w_v, functools.partial(_epi_nsa_v, d),
                  (sds((n_sets, t, NSA_DV), BF16),
                   sds((n_sets, t // VT_TILE, NSA_DV, VT_TILE), BF16)),
                  (pl.BlockSpec((per_tile, tm, NSA_DV), lambda i_, j, k: (j, i_, 0)),
                   pl.BlockSpec((per_tile, tm // VT_TILE, NSA_DV, VT_TILE),
                                lambda i_, j, k: (j, i_, 0, 0))),
                  tm=tm, tn=per_tile * NSA_DV, tk=d, nj=n_sets // per_tile,
                  extras=(ssq,), extra_specs=(ssq_spec,), name="nsa_v")

    ak = k3[:NSA_G].reshape(NSA_G, t // CMP_STRIDE, CMP_STRIDE * NSA_DK)
    av = v3[:NSA_G].reshape(NSA_G, t // CMP_STRIDE, CMP_STRIDE * NSA_DV)
    kc, vct = _compress(ak, av,
                        _compress_weights(s_cmp_w1_k, s_cmp_pe_k, s_cmp_w2_k, transpose_w2=False),
                        _compress_weights(s_cmp_w1_v, s_cmp_pe_v, s_cmp_w2_v, transpose_w2=True),
                        batch, seq)

    n_chunk = seq // CMP_STRIDE
    n_sel = seq // SEL_BLOCK
    c_start = np.arange(n_chunk) * CMP_STRIDE
    j_start = np.arange(n_sel) * SEL_BLOCK
    ovl = ((c_start[None, :] < j_start[:, None] + SEL_BLOCK)
           & (c_start[None, :] + CMP_BLOCK > j_start[:, None])
           & (np.arange(n_chunk)[None, :] < (seq - CMP_BLOCK) // CMP_STRIDE + 1))
    ovl = jnp.asarray(ovl, BF16)

    for i in range(n_b):
        n_q = heads * NSA_DK
        w_q = b_w_in[i][:, :n_q].astype(BF16)
        w_g = b_w_in[i][:, n_q:].reshape(d, NSA_G, 3 * hpg)
        w_g = jnp.pad(w_g, ((0, 0), (0, 0), (0, LANES - 3 * hpg))).reshape(d, NSA_G * LANES)
        w_g = w_g.astype(BF16)
        q = _mm(hb_b, w_q, functools.partial(_epi_nsa_rope, d, NSA_DK ** -0.5 * LOG2_E),
                sds((heads, t, NSA_DK), BF16),
                pl.BlockSpec((per_tile, tm_rope, NSA_DK), lambda i_, j, k: (j, i_, 0)),
                tm=tm_rope, tn=NSA_ROPE_TILE, tk=d, nj=heads // per_tile,
                extras=(ssq, cos_nsa, sin_nsa),
                extra_specs=(pl.BlockSpec((tm_rope, LANES), row), rope_tab, rope_tab),
                name="nsa_q")
        gates = _mm(hb_b, w_g, functools.partial(_epi_sigmoid, d), sds((t, NSA_G * LANES), F32),
                    pl.BlockSpec((tm, NSA_G * LANES), tile),
                    tm=tm, tn=NSA_G * LANES, tk=d, nj=1,
                    extras=(ssq,), extra_specs=(ssq_spec,), name="nsa_gates")
        o = _nsa_attention(q, kc, vct, k3, vt3, gates, ovl, batch, seq, hpg)
        h, ssq, (hb,) = residual_mm(o, b_w_o[i].astype(BF16), None, h, [f_norm[n_a + i]],
                                    tm_=tm, tn=512, name="nsa_out")
        gains = [b_norm[i + 1]] if i + 1 < n_b else []
        h, ssq, hbs = ffn(hb, ssq, h, n_a + i, gains)
        if hbs:
            hb_b = hbs[0]

    out = _rmsnorm(h, final_norm, F32)
    return out.reshape(batch, seq, d)
```

```python
import functools

import numpy as np
import jax
import jax.numpy as jnp
from jax import lax
from jax.experimental import pallas as pl
from jax.experimental.pallas import tpu as pltpu

F32 = jnp.float32
BF16 = jnp.bfloat16

LANES = 128
VMEM_LIMIT_BYTES = 56 * 1024 * 1024

NORM_EPS = 1e-6
ROPE_THETA = 10000.0

MLA_Q_LORA = 1536
MLA_KV_LORA = 512
MLA_NOPE = 128
MLA_ROPE = 64
MLA_V = 128

NSA_G = 4
NSA_DK = 192
NSA_DV = 128
NSA_HALF = NSA_DK // 2
NSA_ROPE_TILE = 4 * NSA_DK
CMP_BLOCK = 32
CMP_STRIDE = 16
SEL_BLOCK = 64
SEL_TOPK = 16
WINDOW = 512
FORCE_BONUS = 1e4
NEG_BIG = -1e30
LOG2_E = 1.4426950408889634

VT_TILE = LANES
NT_DIMS = (((1,), (1,)), ((), ()))


def _params(sem):
    return pltpu.CompilerParams(dimension_semantics=sem, vmem_limit_bytes=VMEM_LIMIT_BYTES)


def _rmsnorm_kernel(x_ref, g_ref, o_ref):
    x = x_ref[...]
    ms = jnp.mean(x * x, axis=-1, keepdims=True)
    o_ref[...] = (x * lax.rsqrt(ms + NORM_EPS) * g_ref[...]).astype(o_ref.dtype)


def _rmsnorm(x, g, out_dtype, tm=256):
    t, d = x.shape
    return pl.pallas_call(
        _rmsnorm_kernel,
        grid=(t // tm,),
        in_specs=[pl.BlockSpec((tm, d), lambda i: (i, 0)),
                  pl.BlockSpec((1, d), lambda i: (0, 0))],
        out_specs=pl.BlockSpec((tm, d), lambda i: (i, 0)),
        out_shape=jax.ShapeDtypeStruct((t, d), out_dtype),
        compiler_params=_params(("parallel",)),
        name="rmsnorm",
    )(x, g.reshape(1, d))


def _prenorm_kernel(x_ref, g_ref, hb_ref, ssq_ref):
    x = x_ref[...]
    hb_ref[...] = (x * g_ref[...]).astype(hb_ref.dtype)
    ssq_ref[...] = jnp.broadcast_to(jnp.sum(x * x, axis=-1, keepdims=True), ssq_ref.shape)


def _prenorm(x, g, tm=256):
    t, d = x.shape
    return pl.pallas_call(
        _prenorm_kernel,
        grid=(t // tm,),
        in_specs=[pl.BlockSpec((tm, d), lambda i: (i, 0)),
                  pl.BlockSpec((1, d), lambda i: (0, 0))],
        out_specs=(pl.BlockSpec((tm, d), lambda i: (i, 0)),
                   pl.BlockSpec((tm, LANES), lambda i: (i, 0))),
        out_shape=(jax.ShapeDtypeStruct((t, d), BF16), jax.ShapeDtypeStruct((t, LANES), F32)),
        compiler_params=_params(("parallel",)),
        name="prenorm",
    )(x, g.reshape(1, d))


def _mm_kernel(n_w, n_e, n_o, nk, epilogue, first_tile_init, *refs):
    a_ref = refs[0]
    w_refs = refs[1:1 + n_w]
    e_refs = refs[1 + n_w:1 + n_w + n_e]
    o_refs = refs[1 + n_w + n_e:1 + n_w + n_e + n_o]
    acc_refs = refs[1 + n_w + n_e + n_o:]
    j = pl.program_id(1)
    if first_tile_init is not None:
        pl.when((j == 0) & (pl.program_id(2) == 0))(lambda: first_tile_init(o_refs))

    def product(w):
        return jnp.dot(a_ref[...], w[...].astype(a_ref.dtype), preferred_element_type=F32)

    if nk == 1:
        epilogue([product(w) for w in w_refs], e_refs, o_refs, j)
        return
    k = pl.program_id(2)

    @pl.when(k == 0)
    def _():
        for w, acc in zip(w_refs, acc_refs):
            acc[...] = product(w)

    @pl.when(k > 0)
    def _():
        for w, acc in zip(w_refs, acc_refs):
            acc[...] += product(w)

    @pl.when(k == nk - 1)
    def _():
        epilogue([acc[...] for acc in acc_refs], e_refs, o_refs, j)


def _mm(a, w, epilogue, out_shape, out_specs, *, tm, tn, tk, nj, w_col_maps=None, w_layer=None,
        extras=(), extra_specs=(), a_buffers=None, first_tile_init=None, name="matmul"):
    m, kdim = a.shape
    ni, nk = m // tm, kdim // tk
    assert ni * tm == m and nk * tk == kdim
    if w_col_maps is None:
        w_col_maps = (lambda j: j,)
    n_w = len(w_col_maps)
    single = not isinstance(out_shape, (tuple, list))
    out_shapes = (out_shape,) if single else tuple(out_shape)
    out_specs_t = (out_specs,) if single else tuple(out_specs)
    a_kwargs = {} if a_buffers is None else {"pipeline_mode": pl.Buffered(a_buffers)}
    in_specs = [pl.BlockSpec((tm, tk), lambda i, j, k: (i, k), **a_kwargs)]
    for cm in w_col_maps:
        if w_layer is None:
            in_specs.append(pl.BlockSpec((tk, tn), functools.partial(
                lambda i, j, k, cm: (k, cm(j)), cm=cm)))
        else:
            in_specs.append(pl.BlockSpec((None, tk, tn), functools.partial(
                lambda i, j, k, cm: (w_layer, k, cm(j)), cm=cm)))
    in_specs += list(extra_specs)
    scratch = [pltpu.VMEM((tm, tn), F32) for _ in range(n_w)] if nk > 1 else []
    kern = functools.partial(_mm_kernel, n_w, len(extras), len(out_shapes), nk, epilogue,
                             first_tile_init)
    res = pl.pallas_call(
        kern,
        grid=(ni, nj, nk),
        in_specs=in_specs,
        out_specs=out_specs_t,
        out_shape=out_shapes,
        scratch_shapes=scratch,
        compiler_params=_params(("parallel", "arbitrary", "arbitrary")),
        name=name,
    )(a, *([w] * n_w), *extras)
    return res[0] if single else res


def _rope_pairs_64(x, cos2, sin2):
    lane = lax.broadcasted_iota(jnp.int32, x.shape, 1)
    first_half = (lane % MLA_ROPE) < (MLA_ROPE // 2)
    partner = jnp.where(first_half,
                        pltpu.roll(x, LANES - MLA_ROPE // 2, 1),
                        pltpu.roll(x, MLA_ROPE // 2, 1))
    return x * cos2 + partner * sin2


def _store_vt_tiles(vt_ref, lead, v):
    for c in range(v.shape[0] // VT_TILE):
        vt_ref[lead + (c,)] = v[c * VT_TILE:(c + 1) * VT_TILE, :].T.astype(vt_ref.dtype)


def _row_scale(acc, ssq_ref, d_model):
    rstd = lax.rsqrt(ssq_ref[...] * (1.0 / d_model) + NORM_EPS)
    return acc * jnp.concatenate([rstd] * (acc.shape[1] // LANES), axis=1)


def _epi_residual(accs, e_refs, o_refs, j):
    o_refs[0][...] = e_refs[0][...] + accs[0]


def _zero_ssq(o_refs):
    o_refs[1][...] = jnp.zeros(o_refs[1].shape, F32)


def _epi_residual_prenorm(accs, e_refs, o_refs, j):
    h = e_refs[0][...] + accs[0]
    o_refs[0][...] = h
    ssq_ref = o_refs[1]
    ssq_ref[...] += jnp.broadcast_to(jnp.sum(h * h, axis=-1, keepdims=True), ssq_ref.shape)
    for g_ref, hb_ref in zip(e_refs[1:], o_refs[2:]):
        hb_ref[...] = (h * g_ref[...]).astype(hb_ref.dtype)


def _epi_swiglu(d_model, accs, e_refs, o_refs, j):
    a = _row_scale(accs[0], e_refs[0], d_model)
    b = _row_scale(accs[1], e_refs[0], d_model)
    o_refs[0][...] = (a * jax.nn.sigmoid(a) * b).astype(o_refs[0].dtype)
    if len(o_refs) > 1:
        o_refs[1][...] = e_refs[1][...].astype(o_refs[1].dtype)


def _epi_sigmoid(d_model, accs, e_refs, o_refs, j):
    o_refs[0][...] = jax.nn.sigmoid(_row_scale(accs[0], e_refs[0], d_model))


def _epi_mla_in(d_model, accs, e_refs, o_refs, j):
    ssq_ref, gq_ref, gkv_ref, cos_ref, sin_ref = e_refs
    u = _row_scale(accs[0], ssq_ref, d_model)
    cq = u[:, :MLA_Q_LORA]
    ms = jnp.mean(cq * cq, axis=-1, keepdims=True)
    o_refs[0][...] = (cq * lax.rsqrt(ms + NORM_EPS) * gq_ref[...]).astype(BF16)
    ckv = u[:, MLA_Q_LORA:MLA_Q_LORA + MLA_KV_LORA]
    ms = jnp.mean(ckv * ckv, axis=-1, keepdims=True)
    o_refs[1][...] = (ckv * lax.rsqrt(ms + NORM_EPS) * gkv_ref[...]).astype(BF16)
    kr = u[:, MLA_Q_LORA + MLA_KV_LORA:]
    o_refs[2][...] = _rope_pairs_64(kr, cos_ref[...], sin_ref[...]).astype(BF16)


def _epi_mla_q(n_nope_tiles, scale, accs, e_refs, o_refs, j):
    cos_ref, sin_ref = e_refs
    acc = accs[0]

    @pl.when(j < n_nope_tiles)
    def _():
        o_refs[0][...] = (acc * scale).astype(BF16)

    @pl.when(j >= n_nope_tiles)
    def _():
        for c in range(acc.shape[1] // LANES):
            x = acc[:, c * LANES:(c + 1) * LANES]
            r = _rope_pairs_64(x, cos_ref[...], sin_ref[...])
            o_refs[0][:, c * LANES:(c + 1) * LANES] = (r * scale).astype(BF16)


def _epi_mla_kv(accs, e_refs, o_refs, j):
    acc = accs[0]
    kn_ref, vt_ref = o_refs
    per = MLA_NOPE + MLA_V
    for r in range(acc.shape[1] // per):
        kn_ref[:, r * MLA_NOPE:(r + 1) * MLA_NOPE] = acc[:, r * per:r * per + MLA_NOPE].astype(BF16)
        _store_vt_tiles(vt_ref, (r,), acc[:, r * per + MLA_NOPE:(r + 1) * per])


def _rope_192(x, cos, sin):
    width = x.shape[1]
    lane = lax.broadcasted_iota(jnp.int32, x.shape, 1)
    partner = jnp.where((lane % NSA_DK) < NSA_HALF,
                        pltpu.roll(x, width - NSA_HALF, 1),
                        pltpu.roll(x, NSA_HALF, 1))
    return x * cos + partner * sin


def _epi_nsa_rope(d_model, scale, accs, e_refs, o_refs, j):
    ssq_ref, cos_ref, sin_ref = e_refs
    acc = _row_scale(accs[0], ssq_ref, d_model)
    roped = _rope_192(acc, cos_ref[...], sin_ref[...])
    if scale is not None:
        roped = roped * scale
    for r in range(acc.shape[1] // NSA_DK):
        o_refs[0][r] = roped[:, r * NSA_DK:(r + 1) * NSA_DK].astype(BF16)


def _epi_nsa_v(d_model, accs, e_refs, o_refs, j):
    v_ref, vt_ref = o_refs
    acc = _row_scale(accs[0], e_refs[0], d_model)
    for r in range(acc.shape[1] // NSA_DV):
        v = acc[:, r * NSA_DV:(r + 1) * NSA_DV]
        v_ref[r] = v.astype(BF16)
        _store_vt_tiles(vt_ref, (r,), v)


SUM_ROWS = 16


def _flash_step_t(q, k, vt, bias, m, acc_ref):
    s = lax.dot_general(k, q, NT_DIMS, preferred_element_type=F32)
    if bias is not None:
        s = s + jnp.concatenate([bias] * (s.shape[1] // bias.shape[1]), axis=1)
    m_new = jnp.maximum(m, jnp.max(s, axis=0, keepdims=True))
    p = jnp.exp2(s - m_new).astype(BF16)
    vt1 = jnp.concatenate([vt, jnp.ones((SUM_ROWS, vt.shape[1]), BF16)], axis=0)
    acc_ref[...] = jnp.exp2(m - m_new) * acc_ref[...] + jnp.dot(vt1, p, preferred_element_type=F32)
    return m_new


def _flash_result(acc_ref, dv):
    acc = acc_ref[...]
    return acc[:dv] / acc[dv:dv + 1]


def _vt_tiles(vt_ref, lead, c0, n):
    return jnp.concatenate([vt_ref[lead + (c0 + c,)] for c in range(n)], axis=1)


def _mla_attn_kernel(t_blk, seq, qn_ref, qr_ref, kn_ref, kr_ref, vt_ref, o_ref, acc_ref):
    h = pl.program_id(1)
    n_vt = t_blk // VT_TILE
    lane_half = lax.broadcasted_iota(jnp.int32, (t_blk, LANES), 1) // MLA_ROPE
    kpos = lax.broadcasted_iota(jnp.int32, (t_blk, t_blk), 0)
    qpos = lax.broadcasted_iota(jnp.int32, (t_blk, t_blk), 1)
    diag_bias = jnp.where(kpos <= qpos, 0.0, NEG_BIG).astype(F32)

    for qi in range(seq // t_blk):
        q0 = qi * t_blk
        qr = qr_ref[q0:q0 + t_blk, :]
        qr = jnp.where(lane_half == (h % 2), qr, jnp.zeros_like(qr))
        q = jnp.concatenate([qn_ref[q0:q0 + t_blk, :], qr], axis=1)
        acc_ref[...] = jnp.zeros(acc_ref.shape, F32)
        m = jnp.full((1, t_blk), NEG_BIG, F32)
        for j in range(qi + 1):
            k0 = j * t_blk
            k = jnp.concatenate([kn_ref[k0:k0 + t_blk, :], kr_ref[k0:k0 + t_blk, :]], axis=1)
            vt = _vt_tiles(vt_ref, (0,), j * n_vt, n_vt)
            m = _flash_step_t(q, k, vt, diag_bias if j == qi else None, m, acc_ref)
        o_ref[q0:q0 + t_blk, :] = _flash_result(acc_ref, MLA_V).T.astype(o_ref.dtype)


def _mla_attention(q, kn, kr, vt, batch, seq, heads, t_blk=512):
    t = batch * seq
    n_nope_blk = heads * MLA_NOPE // LANES
    n_vt = seq // VT_TILE
    kern = functools.partial(_mla_attn_kernel, t_blk, seq)
    return pl.pallas_call(
        kern,
        grid=(batch, heads),
        in_specs=[
            pl.BlockSpec((seq, MLA_NOPE), lambda b, h: (b, h)),
            pl.BlockSpec((seq, LANES), lambda b, h: (b, n_nope_blk + h // 2)),
            pl.BlockSpec((seq, MLA_NOPE), lambda b, h: (b, h)),
            pl.BlockSpec((seq, LANES), lambda b, h: (b, 0)),
            pl.BlockSpec((1, n_vt, MLA_V, VT_TILE), lambda b, h: (h, b, 0, 0)),
        ],
        out_specs=pl.BlockSpec((seq, MLA_V), lambda b, h: (b, h)),
        out_shape=jax.ShapeDtypeStruct((t, heads * MLA_V), BF16),
        scratch_shapes=[pltpu.VMEM((MLA_V + SUM_ROWS, t_blk), F32)],
        compiler_params=_params(("parallel", "parallel")),
        name="mla_attention",
    )(q, q, kn, kr, vt)


def _compress_mlp(a, w1_ref, pe_ref, w2_ref, transposed_out):
    y_lo = jnp.dot(a, w1_ref[0], preferred_element_type=F32)
    y_hi = jnp.dot(a, w1_ref[1], preferred_element_type=F32)
    pe_c = (jnp.dot(pe_ref[0], w1_ref[0], preferred_element_type=F32)
            + jnp.dot(pe_ref[1], w1_ref[1], preferred_element_type=F32))
    n = y_hi.shape[0]
    hid = y_lo + pltpu.roll(y_hi, n - 1, 0) + pe_c[0:1, :]
    hid = jax.nn.gelu(hid, approximate=True)
    if transposed_out:
        return lax.dot_general(w2_ref[...], hid.astype(BF16), NT_DIMS, preferred_element_type=F32)
    return jnp.dot(hid.astype(BF16), w2_ref[...], preferred_element_type=F32)


def _compress_kernel(ak_ref, av_ref, w1k_ref, pek_ref, w2k_ref, w1v_ref, pev_ref, w2vt_ref,
                     kc_ref, vct_ref):
    kc_ref[0] = _compress_mlp(ak_ref[0], w1k_ref, pek_ref, w2k_ref, False).astype(BF16)
    vct_ref[0] = _compress_mlp(av_ref[0], w1v_ref, pev_ref, w2vt_ref, True).astype(BF16)


def _compress(ak, av, wk, wv, batch, seq):
    n_chunk = seq // CMP_STRIDE
    const3 = lambda b, g: (0, 0, 0)
    const2 = lambda b, g: (0, 0)

    def wspecs(w):
        w1, pe, w2 = w
        return [pl.BlockSpec(w1.shape, const3), pl.BlockSpec(pe.shape, const3),
                pl.BlockSpec(w2.shape, const2)]

    return pl.pallas_call(
        _compress_kernel,
        grid=(batch, NSA_G),
        in_specs=[pl.BlockSpec((1, n_chunk, ak.shape[2]), lambda b, g: (g, b, 0)),
                  pl.BlockSpec((1, n_chunk, av.shape[2]), lambda b, g: (g, b, 0))]
        + wspecs(wk) + wspecs(wv),
        out_specs=(pl.BlockSpec((1, n_chunk, NSA_DK), lambda b, g: (b * NSA_G + g, 0, 0)),
                   pl.BlockSpec((1, NSA_DV, n_chunk), lambda b, g: (b * NSA_G + g, 0, 0))),
        out_shape=(jax.ShapeDtypeStruct((batch * NSA_G, n_chunk, NSA_DK), BF16),
                   jax.ShapeDtypeStruct((batch * NSA_G, NSA_DV, n_chunk), BF16)),
        compiler_params=_params(("parallel", "parallel")),
        name="nsa_compress",
    )(ak, av, *wk, *wv)


def _nsa_attn_kernel(hpg, tq, tk, seq, q_ref, kc_ref, vct_ref, ks_ref, vst_ref, kw_ref, vwt_ref,
                     gate_ref, ovl_ref, o_ref, bias_ref, oc_ref, os_ref, acc_ref):
    qi = pl.program_id(2)
    t0 = qi * tq
    n_rows = hpg * tq
    n_cmp = kc_ref.shape[1]
    n_sel = seq // SEL_BLOCK
    n_vt = tk // VT_TILE
    q = q_ref[...].reshape(n_rows, NSA_DK)

    def tile_heads(x):
        return jnp.concatenate([x] * hpg, axis=1)

    s = lax.dot_general(kc_ref[0], q, NT_DIMS, preferred_element_type=F32)
    cidx = lax.broadcasted_iota(jnp.int32, (n_cmp, tq), 0)
    tpos_c = t0 + lax.broadcasted_iota(jnp.int32, (n_cmp, tq), 1)
    mask_c = tile_heads((cidx * CMP_STRIDE + CMP_BLOCK - 1) <= tpos_c)
    s = jnp.where(mask_c, s, NEG_BIG)
    mx = jnp.max(s, axis=0, keepdims=True)
    e = jnp.where(mask_c, jnp.exp2(s - mx), 0.0)
    p = e / jnp.maximum(jnp.sum(e, axis=0, keepdims=True), 1e-30)
    oc_ref[...] = jnp.dot(vct_ref[0], p.astype(BF16), preferred_element_type=F32)

    psum = p[:, 0:tq]
    for h in range(1, hpg):
        psum = psum + p[:, h * tq:(h + 1) * tq]
    p1 = psum.astype(BF16)
    p2 = (psum - p1.astype(F32)).astype(BF16)
    p3 = (psum - p1.astype(F32) - p2.astype(F32)).astype(BF16)
    ovl = ovl_ref[...]
    imp = (jnp.dot(ovl, p1, preferred_element_type=F32)
           + jnp.dot(ovl, p2, preferred_element_type=F32)
           + jnp.dot(ovl, p3, preferred_element_type=F32))
    jrow = lax.broadcasted_iota(jnp.int32, (n_sel, tq), 0)
    tpos = t0 + lax.broadcasted_iota(jnp.int32, (n_sel, tq), 1)
    valid = jrow * SEL_BLOCK <= tpos
    cur = jnp.right_shift(tpos, SEL_BLOCK.bit_length() - 1)
    forced = valid & ((jrow == 0) | (jrow == cur) | (jrow == cur - 1))
    score = jnp.where(valid, imp + jnp.where(forced, FORCE_BONUS, 0.0), -jnp.inf)
    rank = jnp.zeros((n_sel, tq), jnp.int32)
    for i in range(n_sel):
        row = score[i:i + 1, :]
        beats = (row > score) | ((row == score) & (jrow > i))
        rank = rank + beats.astype(jnp.int32)
    sel_bias = jnp.where(valid & (rank < SEL_TOPK), 0.0, NEG_BIG).astype(F32)
    tk_s = bias_ref.shape[1]
    n_sel_tiles = (t0 + tq - 1) // tk_s + 1
    blocks_per_tile = tk_s // SEL_BLOCK
    for jj in range(seq // tk_s):
        @pl.when(jj < n_sel_tiles)
        def _():
            rows = [jnp.broadcast_to(sel_bias[b:b + 1, :], (SEL_BLOCK, tq))
                    for b in range(jj * blocks_per_tile, (jj + 1) * blocks_per_tile)]
            kpos = jj * tk_s + lax.broadcasted_iota(jnp.int32, (tk_s, tq), 0)
            tq_pos = t0 + lax.broadcasted_iota(jnp.int32, (tk_s, tq), 1)
            bias_ref[jj] = jnp.where(kpos <= tq_pos, jnp.concatenate(rows, axis=0), NEG_BIG)

    m0 = jnp.full((1, n_rows), NEG_BIG, F32)

    acc_ref[...] = jnp.zeros(acc_ref.shape, F32)

    def sel_step(j, m):
        k0 = pl.multiple_of(j * tk_s, tk_s)
        k = ks_ref[0, pl.ds(k0, tk_s), :]
        vt = _vt_tiles(vst_ref, (0,), j * (tk_s // VT_TILE), tk_s // VT_TILE)
        return _flash_step_t(q, k, vt, bias_ref[j], m, acc_ref)

    lax.fori_loop(0, n_sel_tiles, sel_step, m0)
    os_ref[...] = _flash_result(acc_ref, NSA_DV)

    acc_ref[...] = jnp.zeros(acc_ref.shape, F32)
    span = -(-(WINDOW + tq) // tk) * tk
    k_start = jnp.maximum(t0 + tq - span, 0)

    def win_step(c, m):
        k0 = pl.multiple_of(k_start + c * tk, VT_TILE)
        k = kw_ref[0, pl.ds(k0, tk), :]
        vt = _vt_tiles(vwt_ref, (0,), k0 // VT_TILE, n_vt)
        kpos = k0 + lax.broadcasted_iota(jnp.int32, (tk, tq), 0)
        tq_pos = t0 + lax.broadcasted_iota(jnp.int32, (tk, tq), 1)
        ok = (kpos <= tq_pos) & (tq_pos - kpos < WINDOW)
        bias = jnp.where(ok, 0.0, NEG_BIG).astype(F32)
        return _flash_step_t(q, k, vt, bias, m, acc_ref)

    m = m0
    for c in range(span // tk):
        m = win_step(c, m)
    o_w = _flash_result(acc_ref, NSA_DV)

    gates_t = gate_ref[...].T
    for h in range(hpg):
        cols = slice(h * tq, (h + 1) * tq)
        o = (gates_t[3 * h:3 * h + 1, :] * oc_ref[:, cols]
             + gates_t[3 * h + 1:3 * h + 2, :] * os_ref[:, cols]
             + gates_t[3 * h + 2:3 * h + 3, :] * o_w[:, cols])
        o_ref[:, h * NSA_DV:(h + 1) * NSA_DV] = o.T.astype(o_ref.dtype)


def _nsa_attention(q, kc, vct, k3, vt3, gates, ovl, batch, seq, hpg, tq=128, tk_sel=512):
    tk = WINDOW + tq
    t = batch * seq
    nq = seq // tq
    n_chunk = seq // CMP_STRIDE
    n_rows = hpg * tq
    n_vt = seq // VT_TILE
    assert (WINDOW + tq) % VT_TILE == 0 and seq >= -(-(WINDOW + tq) // tk) * tk
    kern = functools.partial(_nsa_attn_kernel, hpg, tq, tk, seq)

    def k_spec(branch):
        return pl.BlockSpec((1, seq, NSA_DK), lambda b, g, qi: (branch * NSA_G + g, b, 0))

    def vt_spec(branch):
        return pl.BlockSpec((1, n_vt, NSA_DV, VT_TILE), lambda b, g, qi: (branch * NSA_G + g, b, 0, 0))

    return pl.pallas_call(
        kern,
        grid=(batch, NSA_G, nq),
        in_specs=[
            pl.BlockSpec((hpg, tq, NSA_DK), lambda b, g, qi: (g, b * nq + qi, 0)),
            pl.BlockSpec((1, n_chunk, NSA_DK), lambda b, g, qi: (b * NSA_G + g, 0, 0)),
            pl.BlockSpec((1, NSA_DV, n_chunk), lambda b, g, qi: (b * NSA_G + g, 0, 0)),
            k_spec(1), vt_spec(1), k_spec(2), vt_spec(2),
            pl.BlockSpec((tq, LANES), lambda b, g, qi: (b * nq + qi, g)),
            pl.BlockSpec(ovl.shape, lambda b, g, qi: (0, 0)),
        ],
        out_specs=pl.BlockSpec((tq, hpg * NSA_DV), lambda b, g, qi: (b * nq + qi, g)),
        out_shape=jax.ShapeDtypeStruct((t, NSA_G * hpg * NSA_DV), BF16),
        scratch_shapes=[
            pltpu.VMEM((seq // tk_sel, tk_sel, tq), F32),
            pltpu.VMEM((NSA_DV, n_rows), F32),
            pltpu.VMEM((NSA_DV, n_rows), F32),
            pltpu.VMEM((NSA_DV + SUM_ROWS, n_rows), F32),
        ],
        compiler_params=_params(("parallel", "parallel", "arbitrary")),
        name="nsa_attention",
    )(q, kc, vct, k3, vt3, k3, vt3, gates, ovl)


def _rope_tables(pos_flat):
    pos = pos_flat.astype(F32)[:, None]
    inv64 = ROPE_THETA ** (-jnp.arange(0, MLA_ROPE, 2, dtype=F32) / MLA_ROPE)
    ang = pos * inv64
    c, s = jnp.cos(ang), jnp.sin(ang)
    cos_mla = jnp.concatenate([c, c, c, c], axis=1)
    sin_mla = jnp.concatenate([-s, s, -s, s], axis=1)
    inv192 = ROPE_THETA ** (-jnp.arange(0, NSA_DK, 2, dtype=F32) / NSA_DK)
    ang = pos * inv192
    c, s = jnp.cos(ang), jnp.sin(ang)
    reps = NSA_ROPE_TILE // NSA_DK
    cos_nsa = jnp.concatenate([c, c] * reps, axis=1)
    sin_nsa = jnp.concatenate([-s, s] * reps, axis=1)
    return cos_mla, sin_mla, cos_nsa, sin_nsa


def _compress_weights(w1, pe, w2, *, transpose_w2):
    width, d_hid = w1.shape[0] // CMP_BLOCK, w1.shape[1]
    hid_p = -(-d_hid // LANES) * LANES
    w1 = jnp.pad(w1, ((0, 0), (0, hid_p - d_hid)))
    w1 = w1.reshape(2, CMP_STRIDE * width, hid_p).astype(BF16)
    pe = jnp.pad(pe.reshape(2, 1, CMP_STRIDE * width), ((0, 0), (0, 7), (0, 0))).astype(BF16)
    w2 = jnp.pad(w2, ((0, hid_p - d_hid), (0, 0))).astype(BF16)
    return w1, pe, (w2.T if transpose_w2 else w2)


def kernel(x, positions, a_norm, a_w_in, a_q_norm, a_w_uq, a_kv_norm, a_w_ukv, a_w_o, s_norm, s_w_kv, s_cmp_pe_k, s_cmp_w1_k, s_cmp_w2_k, s_cmp_pe_v, s_cmp_w1_v, s_cmp_w2_v, b_norm, b_w_in, b_w_o, f_norm, f_w_in, f_w_out, final_norm):
    batch, seq, d = x.shape
    t = batch * seq
    heads = d // 128
    hpg = heads // NSA_G
    d_ff = f_w_out.shape[1]
    n_a, n_b = a_w_in.shape[0], b_w_in.shape[0]
    assert n_a >= 1 and n_b >= 1
    tm = 1024
    sds = jax.ShapeDtypeStruct

    cos_mla, sin_mla, cos_nsa, sin_nsa = _rope_tables(positions.reshape(t))
    row = lambda i, j, k: (i, 0)
    tile = lambda i, j, k: (i, j)
    tab_spec = pl.BlockSpec((tm, LANES), row)
    ssq_spec = pl.BlockSpec((tm, LANES), row)

    def residual_mm(a, w, w_layer, h, gains, *, tm_, tn, name, a_buffers=None):
        blk = pl.BlockSpec((tm_, tn), tile)
        common = dict(tm=tm_, tn=tn, tk=a.shape[1], nj=d // tn, w_layer=w_layer, name=name,
                      a_buffers=a_buffers)
        if not gains:
            h_new = _mm(a, w, _epi_residual, sds((t, d), F32), blk,
                        extras=(h,), extra_specs=(blk,), **common)
            return h_new, None, ()
        outs = _mm(a, w, _epi_residual_prenorm,
                   (sds((t, d), F32), sds((t, LANES), F32)) + tuple(sds((t, d), BF16) for _ in gains),
                   (blk, pl.BlockSpec((tm_, LANES), row)) + tuple(blk for _ in gains),
                   extras=(h,) + tuple(g.reshape(1, d) for g in gains),
                   extra_specs=(blk,) + tuple(pl.BlockSpec((1, tn), lambda i, j, k: (0, j))
                                              for _ in gains),
                   first_tile_init=_zero_ssq, **common)
        return outs[0], outs[1], tuple(outs[2:])

    def ffn(hb, ssq, h, layer, gains):
        tn = 256
        nj = d_ff // tn
        tm_in = min(2 * tm, t)
        steps = (t // tm_in) * nj
        slab = d_ff // steps
        assert slab * steps == d_ff and slab % 16 == 0
        act, w_out = _mm(
            hb, f_w_in, functools.partial(_epi_swiglu, d),
            (sds((t, d_ff), BF16), sds((d_ff, d), BF16)),
            (pl.BlockSpec((tm_in, tn), tile),
             pl.BlockSpec((slab, d), lambda i, j, k: (i * nj + j, 0))),
            tm=tm_in, tn=tn, tk=d, nj=nj, w_layer=layer, a_buffers=1,
            w_col_maps=(lambda j: j, lambda j: j + nj),
            extras=(ssq, f_w_out),
            extra_specs=(pl.BlockSpec((tm_in, LANES), row),
                         pl.BlockSpec((None, slab, d), lambda i, j, k: (layer, i * nj + j, 0))),
            name="ffn_in")
        return residual_mm(act, w_out, None, h, gains, tm_=tm, tn=tn, name="ffn_out",
                           a_buffers=1)

    h = x.reshape(t, d)
    hb, ssq = _prenorm(h, a_norm[0])

    for i in range(n_a):
        rope_lo = MLA_Q_LORA + MLA_KV_LORA
        w_in = jnp.concatenate([a_w_in[i], a_w_in[i][:, rope_lo:]], axis=1).astype(BF16)
        n_in = w_in.shape[1]
        cq, ckv, kr = _mm(
            hb, w_in, functools.partial(_epi_mla_in, d),
            (sds((t, MLA_Q_LORA), BF16), sds((t, MLA_KV_LORA), BF16), sds((t, LANES), BF16)),
            (pl.BlockSpec((tm, MLA_Q_LORA), row), pl.BlockSpec((tm, MLA_KV_LORA), row),
             pl.BlockSpec((tm, LANES), row)),
            tm=tm, tn=n_in, tk=1024, nj=1,
            extras=(ssq, a_q_norm[i].reshape(1, -1), a_kv_norm[i].reshape(1, -1), cos_mla, sin_mla),
            extra_specs=(ssq_spec,
                         pl.BlockSpec((1, MLA_Q_LORA), lambda i_, j, k: (0, 0)),
                         pl.BlockSpec((1, MLA_KV_LORA), lambda i_, j, k: (0, 0)),
                         tab_spec, tab_spec),
            name="mla_in")

        qd = MLA_NOPE + MLA_ROPE
        w_uq = a_w_uq[i].reshape(MLA_Q_LORA, heads, qd)
        w_uq = jnp.concatenate([w_uq[:, :, :MLA_NOPE].reshape(MLA_Q_LORA, heads * MLA_NOPE),
                                w_uq[:, :, MLA_NOPE:].reshape(MLA_Q_LORA, heads * MLA_ROPE)],
                               axis=1).astype(BF16)
        tn = 512
        q = _mm(cq, w_uq,
                functools.partial(_epi_mla_q, heads * MLA_NOPE // tn, qd ** -0.5 * LOG2_E),
                jax.ShapeDtypeStruct((t, heads * qd), BF16),
                pl.BlockSpec((tm, tn), lambda i_, j, k: (i_, j)),
                tm=tm, tn=tn, tk=MLA_Q_LORA, nj=heads * qd // tn,
                extras=(cos_mla, sin_mla), extra_specs=(tab_spec, tab_spec), name="mla_q")
        heads_per_tile = 4
        kn, vt = _mm(ckv, a_w_ukv[i].astype(BF16), _epi_mla_kv,
                     (jax.ShapeDtypeStruct((t, heads * MLA_NOPE), BF16),
                      jax.ShapeDtypeStruct((heads, t // VT_TILE, MLA_V, VT_TILE), BF16)),
                     (pl.BlockSpec((tm, heads_per_tile * MLA_NOPE), lambda i_, j, k: (i_, j)),
                      pl.BlockSpec((heads_per_tile, tm // VT_TILE, MLA_V, VT_TILE),
                                   lambda i_, j, k: (j, i_, 0, 0))),
                     tm=tm, tn=heads_per_tile * (MLA_NOPE + MLA_V), tk=MLA_KV_LORA,
                     nj=heads // heads_per_tile, name="mla_kv")
        o = _mla_attention(q, kn, kr, vt, batch, seq, heads)
        h, ssq, (hb,) = residual_mm(o, a_w_o[i].astype(BF16), None, h, [f_norm[i]],
                                    tm_=tm, tn=512, name="mla_out")
        gains = [a_norm[i + 1]] if i + 1 < n_a else [s_norm, b_norm[0]]
        h, ssq, hbs = ffn(hb, ssq, h, i, gains)
        hb = hbs[0]

    hb_s, hb_b = hbs
    n_sets = 3 * NSA_G
    per_tile = NSA_ROPE_TILE // NSA_DK
    tm_rope = 512
    rope_tab = pl.BlockSpec((tm_rope, NSA_ROPE_TILE), row)
    w_kv = s_w_kv.reshape(d, n_sets, NSA_DK + NSA_DV)
    w_k = w_kv[:, :, :NSA_DK].reshape(d, n_sets * NSA_DK).astype(BF16)
    w_v = w_kv[:, :, NSA_DK:].reshape(d, n_sets * NSA_DV).astype(BF16)
    k3 = _mm(hb_s, w_k, functools.partial(_epi_nsa_rope, d, None),
             sds((n_sets, t, NSA_DK), BF16),
             pl.BlockSpec((per_tile, tm_rope, NSA_DK), lambda i_, j, k: (j, i_, 0)),
             tm=tm_rope, tn=NSA_ROPE_TILE, tk=d, nj=n_sets // per_tile,
             extras=(ssq, cos_nsa, sin_nsa),
             extra_specs=(pl.BlockSpec((tm_rope, LANES), row), rope_tab, rope_tab), name="nsa_k")
    v3, vt3 = _mm(hb_s, w_v, functools.partial(_epi_nsa_v, d),
                  (sds((n_sets, t, NSA_DV), BF16),
                   sds((n_sets, t // VT_TILE, NSA_DV, VT_TILE), BF16)),
                  (pl.BlockSpec((per_tile, tm, NSA_DV), lambda i_, j, k: (j, i_, 0)),
                   pl.BlockSpec((per_tile, tm // VT_TILE, NSA_DV, VT_TILE),
                                lambda i_, j, k: (j, i_, 0, 0))),
                  tm=tm, tn=per_tile * NSA_DV, tk=d, nj=n_sets // per_tile,
                  extras=(ssq,), extra_specs=(ssq_spec,), name="nsa_v")

    ak = k3[:NSA_G].reshape(NSA_G, t // CMP_STRIDE, CMP_STRIDE * NSA_DK)
    av = v3[:NSA_G].reshape(NSA_G, t // CMP_STRIDE, CMP_STRIDE * NSA_DV)
    kc, vct = _compress(ak, av,
                        _compress_weights(s_cmp_w1_k, s_cmp_pe_k, s_cmp_w2_k, transpose_w2=False),
                        _compress_weights(s_cmp_w1_v, s_cmp_pe_v, s_cmp_w2_v, transpose_w2=True),
                        batch, seq)

    n_chunk = seq // CMP_STRIDE
    n_sel = seq // SEL_BLOCK
    c_start = np.arange(n_chunk) * CMP_STRIDE
    j_start = np.arange(n_sel) * SEL_BLOCK
    ovl = ((c_start[None, :] < j_start[:, None] + SEL_BLOCK)
           & (c_start[None, :] + CMP_BLOCK > j_start[:, None])
           & (np.arange(n_chunk)[None, :] < (seq - CMP_BLOCK) // CMP_STRIDE + 1))
    ovl = jnp.asarray(ovl, BF16)

    for i in range(n_b):
        n_q = heads * NSA_DK
        w_q = b_w_in[i][:, :n_q].astype(BF16)
        w_g = b_w_in[i][:, n_q:].reshape(d, NSA_G, 3 * hpg)
        w_g = jnp.pad(w_g, ((0, 0), (0, 0), (0, LANES - 3 * hpg))).reshape(d, NSA_G * LANES)
        w_g = w_g.astype(BF16)
        q = _mm(hb_b, w_q, functools.partial(_epi_nsa_rope, d, NSA_DK ** -0.5 * LOG2_E),
                sds((heads, t, NSA_DK), BF16),
                pl.BlockSpec((per_tile, tm_rope, NSA_DK), lambda i_, j, k: (j, i_, 0)),
                tm=tm_rope, tn=NSA_ROPE_TILE, tk=d, nj=heads // per_tile,
                extras=(ssq, cos_nsa, sin_nsa),
                extra_specs=(pl.BlockSpec((tm_rope, LANES), row), rope_tab, rope_tab),
                name="nsa_q")
        gates = _mm(hb_b, w_g, functools.partial(_epi_sigmoid, d), sds((t, NSA_G * LANES), F32),
                    pl.BlockSpec((tm, NSA_G * LANES), tile),
                    tm=tm, tn=NSA_G * LANES, tk=d, nj=1,
                    extras=(ssq,), extra_specs=(ssq_spec,), name="nsa_gates")
        o = _nsa_attention(q, kc, vct, k3, vt3, gates, ovl, batch, seq, hpg)
        h, ssq, (hb,) = residual_mm(o, b_w_o[i].astype(BF16), None, h, [f_norm[n_a + i]],
                                    tm_=tm, tn=512, name="nsa_out")
        gains = [b_norm[i + 1]] if i + 1 < n_b else []
        h, ssq, hbs = ffn(hb, ssq, h, n_a + i, gains)
        if hbs:
            hb_b = hbs[0]

    out = _rmsnorm(h, final_norm, F32)
    return out.reshape(batch, seq, d)
```

```python
import functools

import numpy as np
import jax
import jax.numpy as jnp
from jax import lax
from jax.experimental import pallas as pl
from jax.experimental.pallas import tpu as pltpu

F32 = jnp.float32
BF16 = jnp.bfloat16

LANES = 128
VMEM_LIMIT_BYTES = 56 * 1024 * 1024

NORM_EPS = 1e-6
ROPE_THETA = 10000.0

MLA_Q_LORA = 1536
MLA_KV_LORA = 512
MLA_NOPE = 128
MLA_ROPE = 64
MLA_V = 128

NSA_G = 4
NSA_DK = 192
NSA_DV = 128
NSA_HALF = NSA_DK // 2
NSA_ROPE_TILE = 4 * NSA_DK
CMP_BLOCK = 32
CMP_STRIDE = 16
SEL_BLOCK = 64
SEL_TOPK = 16
WINDOW = 512
FORCE_BONUS = 1e4
NEG_BIG = -1e30
LOG2_E = 1.4426950408889634

VT_TILE = LANES
NT_DIMS = (((1,), (1,)), ((), ()))


def _params(sem):
    return pltpu.CompilerParams(dimension_semantics=sem, vmem_limit_bytes=VMEM_LIMIT_BYTES)


def _rmsnorm_kernel(x_ref, g_ref, o_ref):
    x = x_ref[...]
    ms = jnp.mean(x * x, axis=-1, keepdims=True)
    o_ref[...] = (x * lax.rsqrt(ms + NORM_EPS) * g_ref[...]).astype(o_ref.dtype)


def _rmsnorm(x, g, out_dtype, tm=256):
    t, d = x.shape
    return pl.pallas_call(
        _rmsnorm_kernel,
        grid=(t // tm,),
        in_specs=[pl.BlockSpec((tm, d), lambda i: (i, 0)),
                  pl.BlockSpec((1, d), lambda i: (0, 0))],
        out_specs=pl.BlockSpec((tm, d), lambda i: (i, 0)),
        out_shape=jax.ShapeDtypeStruct((t, d), out_dtype),
        compiler_params=_params(("parallel",)),
        name="rmsnorm",
    )(x, g.reshape(1, d))


def _prenorm_kernel(x_ref, g_ref, hb_ref, ssq_ref):
    x = x_ref[...]
    hb_ref[...] = (x * g_ref[...]).astype(hb_ref.dtype)
    ssq_ref[...] = jnp.broadcast_to(jnp.sum(x * x, axis=-1, keepdims=True), ssq_ref.shape)


def _prenorm(x, g, tm=256):
    t, d = x.shape
    return pl.pallas_call(
        _prenorm_kernel,
        grid=(t // tm,),
        in_specs=[pl.BlockSpec((tm, d), lambda i: (i, 0)),
                  pl.BlockSpec((1, d), lambda i: (0, 0))],
        out_specs=(pl.BlockSpec((tm, d), lambda i: (i, 0)),
                   pl.BlockSpec((tm, LANES), lambda i: (i, 0))),
        out_shape=(jax.ShapeDtypeStruct((t, d), BF16), jax.ShapeDtypeStruct((t, LANES), F32)),
        compiler_params=_params(("parallel",)),
        name="prenorm",
    )(x, g.reshape(1, d))


def _mm_kernel(n_w, n_e, n_o, nk, epilogue, first_tile_init, *refs):
    a_ref = refs[0]
    w_refs = refs[1:1 + n_w]
    e_refs = refs[1 + n_w:1 + n_w + n_e]
    o_refs = refs[1 + n_w + n_e:1 + n_w + n_e + n_o]
    acc_refs = refs[1 + n_w + n_e + n_o:]
    j = pl.program_id(1)
    if first_tile_init is not None:
        pl.when((j == 0) & (pl.program_id(2) == 0))(lambda: first_tile_init(o_refs))

    def product(w):
        return jnp.dot(a_ref[...], w[...].astype(a_ref.dtype), preferred_element_type=F32)

    if nk == 1:
        epilogue([product(w) for w in w_refs], e_refs, o_refs, j)
        return
    k = pl.program_id(2)

    @pl.when(k == 0)
    def _():
        for w, acc in zip(w_refs, acc_refs):
            acc[...] = product(w)

    @pl.when(k > 0)
    def _():
        for w, acc in zip(w_refs, acc_refs):
            acc[...] += product(w)

    @pl.when(k == nk - 1)
    def _():
        epilogue([acc[...] for acc in acc_refs], e_refs, o_refs, j)


def _mm(a, w, epilogue, out_shape, out_specs, *, tm, tn, tk, nj, w_col_maps=None, w_layer=None,
        extras=(), extra_specs=(), a_buffers=None, first_tile_init=None, name="matmul"):
    m, kdim = a.shape
    ni, nk = m // tm, kdim // tk
    assert ni * tm == m and nk * tk == kdim
    if w_col_maps is None:
        w_col_maps = (lambda j: j,)
    n_w = len(w_col_maps)
    single = not isinstance(out_shape, (tuple, list))
    out_shapes = (out_shape,) if single else tuple(out_shape)
    out_specs_t = (out_specs,) if single else tuple(out_specs)
    a_kwargs = {} if a_buffers is None else {"pipeline_mode": pl.Buffered(a_buffers)}
    in_specs = [pl.BlockSpec((tm, tk), lambda i, j, k: (i, k), **a_kwargs)]
    for cm in w_col_maps:
        if w_layer is None:
            in_specs.append(pl.BlockSpec((tk, tn), functools.partial(
                lambda i, j, k, cm: (k, cm(j)), cm=cm)))
        else:
            in_specs.append(pl.BlockSpec((None, tk, tn), functools.partial(
                lambda i, j, k, cm: (w_layer, k, cm(j)), cm=cm)))
    in_specs += list(extra_specs)
    scratch = [pltpu.VMEM((tm, tn), F32) for _ in range(n_w)] if nk > 1 else []
    kern = functools.partial(_mm_kernel, n_w, len(extras), len(out_shapes), nk, epilogue,
                             first_tile_init)
    res = pl.pallas_call(
        kern,
        grid=(ni, nj, nk),
        in_specs=in_specs,
        out_specs=out_specs_t,
        out_shape=out_shapes,
        scratch_shapes=scratch,
        compiler_params=_params(("parallel", "arbitrary", "arbitrary")),
        name=name,
    )(a, *([w] * n_w), *extras)
    return res[0] if single else res


def _rope_pairs_64(x, cos2, sin2):
    lane = lax.broadcasted_iota(jnp.int32, x.shape, 1)
    first_half = (lane % MLA_ROPE) < (MLA_ROPE // 2)
    partner = jnp.where(first_half,
                        pltpu.roll(x, LANES - MLA_ROPE // 2, 1),
                        pltpu.roll(x, MLA_ROPE // 2, 1))
    return x * cos2 + partner * sin2


def _store_vt_tiles(vt_ref, lead, v):
    for c in range(v.shape[0] // VT_TILE):
        vt_ref[lead + (c,)] = v[c * VT_TILE:(c + 1) * VT_TILE, :].T.astype(vt_ref.dtype)


def _row_scale(acc, ssq_ref, d_model):
    rstd = lax.rsqrt(ssq_ref[...] * (1.0 / d_model) + NORM_EPS)
    return acc * jnp.concatenate([rstd] * (acc.shape[1] // LANES), axis=1)


def _epi_residual(accs, e_refs, o_refs, j):
    o_refs[0][...] = e_refs[0][...] + accs[0]


def _zero_ssq(o_refs):
    o_refs[1][...] = jnp.zeros(o_refs[1].shape, F32)


def _epi_residual_prenorm(accs, e_refs, o_refs, j):
    h = e_refs[0][...] + accs[0]
    o_refs[0][...] = h
    ssq_ref = o_refs[1]
    ssq_ref[...] += jnp.broadcast_to(jnp.sum(h * h, axis=-1, keepdims=True), ssq_ref.shape)
    for g_ref, hb_ref in zip(e_refs[1:], o_refs[2:]):
        hb_ref[...] = (h * g_ref[...]).astype(hb_ref.dtype)


def _epi_swiglu(d_model, accs, e_refs, o_refs, j):
    a = _row_scale(accs[0], e_refs[0], d_model)
    b = _row_scale(accs[1], e_refs[0], d_model)
    o_refs[0][...] = (a * jax.nn.sigmoid(a) * b).astype(o_refs[0].dtype)
    if len(o_refs) > 1:
        o_refs[1][...] = e_refs[1][...].astype(o_refs[1].dtype)


def _epi_sigmoid(d_model, accs, e_refs, o_refs, j):
    o_refs[0][...] = jax.nn.sigmoid(_row_scale(accs[0], e_refs[0], d_model))


def _epi_mla_in(d_model, accs, e_refs, o_refs, j):
    ssq_ref, gq_ref, gkv_ref, cos_ref, sin_ref = e_refs
    u = _row_scale(accs[0], ssq_ref, d_model)
    cq = u[:, :MLA_Q_LORA]
    ms = jnp.mean(cq * cq, axis=-1, keepdims=True)
    o_refs[0][...] = (cq * lax.rsqrt(ms + NORM_EPS) * gq_ref[...]).astype(BF16)
    ckv = u[:, MLA_Q_LORA:MLA_Q_LORA + MLA_KV_LORA]
    ms = jnp.mean(ckv * ckv, axis=-1, keepdims=True)
    o_refs[1][...] = (ckv * lax.rsqrt(ms + NORM_EPS) * gkv_ref[...]).astype(BF16)
    kr = u[:, MLA_Q_LORA + MLA_KV_LORA:]
    o_refs[2][...] = _rope_pairs_64(kr, cos_ref[...], sin_ref[...]).astype(BF16)


def _epi_mla_q(n_nope_tiles, scale, accs, e_refs, o_refs, j):
    cos_ref, sin_ref = e_refs
    acc = accs[0]

    @pl.when(j < n_nope_tiles)
    def _():
        o_refs[0][...] = (acc * scale).astype(BF16)

    @pl.when(j >= n_nope_tiles)
    def _():
        for c in range(acc.shape[1] // LANES):
            x = acc[:, c * LANES:(c + 1) * LANES]
            r = _rope_pairs_64(x, cos_ref[...], sin_ref[...])
            o_refs[0][:, c * LANES:(c + 1) * LANES] = (r * scale).astype(BF16)


def _epi_mla_kv(accs, e_refs, o_refs, j):
    acc = accs[0]
    kn_ref, vt_ref = o_refs
    per = MLA_NOPE + MLA_V
    for r in range(acc.shape[1] // per):
        kn_ref[:, r * MLA_NOPE:(r + 1) * MLA_NOPE] = acc[:, r * per:r * per + MLA_NOPE].astype(BF16)
        _store_vt_tiles(vt_ref, (r,), acc[:, r * per + MLA_NOPE:(r + 1) * per])


def _rope_192(x, cos, sin):
    width = x.shape[1]
    lane = lax.broadcasted_iota(jnp.int32, x.shape, 1)
    partner = jnp.where((lane % NSA_DK) < NSA_HALF,
                        pltpu.roll(x, width - NSA_HALF, 1),
                        pltpu.roll(x, NSA_HALF, 1))
    return x * cos + partner * sin


def _epi_nsa_rope(d_model, scale, accs, e_refs, o_refs, j):
    ssq_ref, cos_ref, sin_ref = e_refs
    acc = _row_scale(accs[0], ssq_ref, d_model)
    roped = _rope_192(acc, cos_ref[...], sin_ref[...])
    if scale is not None:
        roped = roped * scale
    for r in range(acc.shape[1] // NSA_DK):
        o_refs[0][r] = roped[:, r * NSA_DK:(r + 1) * NSA_DK].astype(BF16)


def _epi_nsa_v(d_model, accs, e_refs, o_refs, j):
    v_ref, vt_ref = o_refs
    acc = _row_scale(accs[0], e_refs[0], d_model)
    for r in range(acc.shape[1] // NSA_DV):
        v = acc[:, r * NSA_DV:(r + 1) * NSA_DV]
        v_ref[r] = v.astype(BF16)
        _store_vt_tiles(vt_ref, (r,), v)


SUM_ROWS = 16


def _flash_step_t(q, k, vt, bias, m, acc_ref):
    s = lax.dot_general(k, q, NT_DIMS, preferred_element_type=F32)
    if bias is not None:
        bias = jnp.concatenate([bias] * (s.shape[1] // bias.shape[1]), axis=1)
        free = s.shape[0] - bias.shape[0]
        s = s + bias if free == 0 else jnp.concatenate([s[:free], s[free:] + bias], axis=0)
    m_new = jnp.maximum(m, jnp.max(s, axis=0, keepdims=True))
    p = jnp.exp2(s - m_new).astype(BF16)
    vt1 = jnp.concatenate([vt, jnp.ones((SUM_ROWS, vt.shape[1]), BF16)], axis=0)
    acc_ref[...] = jnp.exp2(m - m_new) * acc_ref[...] + jnp.dot(vt1, p, preferred_element_type=F32)
    return m_new


def _flash_result(acc_ref, dv):
    acc = acc_ref[...]
    return acc[:dv] / acc[dv:dv + 1]


def _vt_tiles(vt_ref, lead, c0, n):
    return jnp.concatenate([vt_ref[lead + (c0 + c,)] for c in range(n)], axis=1)


def _mla_attn_kernel(t_blk, max_tile, seq, qn_ref, qr_ref, kn_ref, kr_ref, vt_ref, o_ref, acc_ref):
    h = pl.program_id(1)
    lane_half = lax.broadcasted_iota(jnp.int32, (t_blk, LANES), 1) // MLA_ROPE
    kpos = lax.broadcasted_iota(jnp.int32, (t_blk, t_blk), 0)
    qpos = lax.broadcasted_iota(jnp.int32, (t_blk, t_blk), 1)
    diag_bias = jnp.where(kpos <= qpos, 0.0, NEG_BIG).astype(F32)

    for qi in range(seq // t_blk):
        q0 = qi * t_blk
        qr = qr_ref[q0:q0 + t_blk, :]
        qr = jnp.where(lane_half == (h % 2), qr, jnp.zeros_like(qr))
        q = jnp.concatenate([qn_ref[q0:q0 + t_blk, :], qr], axis=1)
        acc_ref[...] = jnp.zeros(acc_ref.shape, F32)
        m = jnp.full((1, t_blk), NEG_BIG, F32)
        n_keys = q0 + t_blk
        for k0 in range(0, n_keys, max_tile):
            k1 = min(k0 + max_tile, n_keys)
            k = jnp.concatenate([kn_ref[k0:k1, :], kr_ref[k0:k1, :]], axis=1)
            vt = _vt_tiles(vt_ref, (0,), k0 // VT_TILE, (k1 - k0) // VT_TILE)
            m = _flash_step_t(q, k, vt, diag_bias if k1 == n_keys else None, m, acc_ref)
        o_ref[q0:q0 + t_blk, :] = _flash_result(acc_ref, MLA_V).T.astype(o_ref.dtype)


def _mla_attention(q, kn, kr, vt, batch, seq, heads, t_blk=512, max_tile=2048):
    t = batch * seq
    n_nope_blk = heads * MLA_NOPE // LANES
    n_vt = seq // VT_TILE
    kern = functools.partial(_mla_attn_kernel, t_blk, max_tile, seq)
    return pl.pallas_call(
        kern,
        grid=(batch, heads),
        in_specs=[
            pl.BlockSpec((seq, MLA_NOPE), lambda b, h: (b, h)),
            pl.BlockSpec((seq, LANES), lambda b, h: (b, n_nope_blk + h // 2)),
            pl.BlockSpec((seq, MLA_NOPE), lambda b, h: (b, h)),
            pl.BlockSpec((seq, LANES), lambda b, h: (b, 0)),
            pl.BlockSpec((1, n_vt, MLA_V, VT_TILE), lambda b, h: (h, b, 0, 0)),
        ],
        out_specs=pl.BlockSpec((seq, MLA_V), lambda b, h: (b, h)),
        out_shape=jax.ShapeDtypeStruct((t, heads * MLA_V), BF16),
        scratch_shapes=[pltpu.VMEM((MLA_V + SUM_ROWS, t_blk), F32)],
        compiler_params=_params(("parallel", "parallel")),
        name="mla_attention",
    )(q, q, kn, kr, vt)


def _compress_mlp(a, w1_ref, pe_ref, w2_ref, transposed_out):
    y_lo = jnp.dot(a, w1_ref[0], preferred_element_type=F32)
    y_hi = jnp.dot(a, w1_ref[1], preferred_element_type=F32)
    pe_c = (jnp.dot(pe_ref[0], w1_ref[0], preferred_element_type=F32)
            + jnp.dot(pe_ref[1], w1_ref[1], preferred_element_type=F32))
    n = y_hi.shape[0]
    hid = y_lo + pltpu.roll(y_hi, n - 1, 0) + pe_c[0:1, :]
    hid = jax.nn.gelu(hid, approximate=True)
    if transposed_out:
        return lax.dot_general(w2_ref[...], hid.astype(BF16), NT_DIMS, preferred_element_type=F32)
    return jnp.dot(hid.astype(BF16), w2_ref[...], preferred_element_type=F32)


def _compress_kernel(ak_ref, av_ref, w1k_ref, pek_ref, w2k_ref, w1v_ref, pev_ref, w2vt_ref,
                     kc_ref, vct_ref):
    kc_ref[0] = _compress_mlp(ak_ref[0], w1k_ref, pek_ref, w2k_ref, False).astype(BF16)
    vct_ref[0] = _compress_mlp(av_ref[0], w1v_ref, pev_ref, w2vt_ref, True).astype(BF16)


def _compress(ak, av, wk, wv, batch, seq):
    n_chunk = seq // CMP_STRIDE
    const3 = lambda b, g: (0, 0, 0)
    const2 = lambda b, g: (0, 0)

    def wspecs(w):
        w1, pe, w2 = w
        return [pl.BlockSpec(w1.shape, const3), pl.BlockSpec(pe.shape, const3),
                pl.BlockSpec(w2.shape, const2)]

    return pl.pallas_call(
        _compress_kernel,
        grid=(batch, NSA_G),
        in_specs=[pl.BlockSpec((1, n_chunk, ak.shape[2]), lambda b, g: (g, b, 0)),
                  pl.BlockSpec((1, n_chunk, av.shape[2]), lambda b, g: (g, b, 0))]
        + wspecs(wk) + wspecs(wv),
        out_specs=(pl.BlockSpec((1, n_chunk, NSA_DK), lambda b, g: (b * NSA_G + g, 0, 0)),
                   pl.BlockSpec((1, NSA_DV, n_chunk), lambda b, g: (b * NSA_G + g, 0, 0))),
        out_shape=(jax.ShapeDtypeStruct((batch * NSA_G, n_chunk, NSA_DK), BF16),
                   jax.ShapeDtypeStruct((batch * NSA_G, NSA_DV, n_chunk), BF16)),
        compiler_params=_params(("parallel", "parallel")),
        name="nsa_compress",
    )(ak, av, *wk, *wv)


def _nsa_attn_kernel(hpg, tq, tk, seq, q_ref, kc_ref, vct_ref, ks_ref, vst_ref, kw_ref, vwt_ref,
                     gate_ref, ovl_ref, o_ref, bias_ref, oc_ref, os_ref, acc_ref):
    qi = pl.program_id(2)
    t0 = qi * tq
    n_rows = hpg * tq
    n_cmp = kc_ref.shape[1]
    n_sel = seq // SEL_BLOCK
    n_vt = tk // VT_TILE
    q = q_ref[...].reshape(n_rows, NSA_DK)

    def tile_heads(x):
        return jnp.concatenate([x] * hpg, axis=1)

    s = lax.dot_general(kc_ref[0], q, NT_DIMS, preferred_element_type=F32)
    cidx = lax.broadcasted_iota(jnp.int32, (n_cmp, tq), 0)
    tpos_c = t0 + lax.broadcasted_iota(jnp.int32, (n_cmp, tq), 1)
    mask_c = tile_heads((cidx * CMP_STRIDE + CMP_BLOCK - 1) <= tpos_c)
    s = jnp.where(mask_c, s, NEG_BIG)
    mx = jnp.max(s, axis=0, keepdims=True)
    e = jnp.where(mask_c, jnp.exp2(s - mx), 0.0)
    p = e / jnp.maximum(jnp.sum(e, axis=0, keepdims=True), 1e-30)
    oc_ref[...] = jnp.dot(vct_ref[0], p.astype(BF16), preferred_element_type=F32)

    psum = p[:, 0:tq]
    for h in range(1, hpg):
        psum = psum + p[:, h * tq:(h + 1) * tq]
    p1 = psum.astype(BF16)
    p2 = (psum - p1.astype(F32)).astype(BF16)
    p3 = (psum - p1.astype(F32) - p2.astype(F32)).astype(BF16)
    ovl = ovl_ref[...]
    imp = (jnp.dot(ovl, p1, preferred_element_type=F32)
           + jnp.dot(ovl, p2, preferred_element_type=F32)
           + jnp.dot(ovl, p3, preferred_element_type=F32))
    jrow = lax.broadcasted_iota(jnp.int32, (n_sel, tq), 0)
    tpos = t0 + lax.broadcasted_iota(jnp.int32, (n_sel, tq), 1)
    valid = jrow * SEL_BLOCK <= tpos
    cur = jnp.right_shift(tpos, SEL_BLOCK.bit_length() - 1)
    forced = valid & ((jrow == 0) | (jrow == cur) | (jrow == cur - 1))
    score = jnp.where(valid, imp + jnp.where(forced, FORCE_BONUS, 0.0), -jnp.inf)
    rank = jnp.zeros((n_sel, tq), jnp.int32)
    for i in range(n_sel):
        row = score[i:i + 1, :]
        beats = (row > score) | ((row == score) & (jrow > i))
        rank = rank + beats.astype(jnp.int32)
    sel_bias = jnp.where(valid & (rank < SEL_TOPK), 0.0, NEG_BIG).astype(F32)
    tk_s = bias_ref.shape[1]
    n_sel_tiles = (t0 + tq - 1) // tk_s + 1
    blocks_per_tile = tk_s // SEL_BLOCK
    for jj in range(seq // tk_s):
        @pl.when(jj < n_sel_tiles)
        def _():
            rows = [jnp.broadcast_to(sel_bias[b:b + 1, :], (SEL_BLOCK, tq))
                    for b in range(jj * blocks_per_tile, (jj + 1) * blocks_per_tile)]
            kpos = jj * tk_s + lax.broadcasted_iota(jnp.int32, (tk_s, tq), 0)
            tq_pos = t0 + lax.broadcasted_iota(jnp.int32, (tk_s, tq), 1)
            bias_ref[jj] = jnp.where(kpos <= tq_pos, jnp.concatenate(rows, axis=0), NEG_BIG)

    m0 = jnp.full((1, n_rows), NEG_BIG, F32)

    acc_ref[...] = jnp.zeros(acc_ref.shape, F32)

    for n in range(1, seq // tk_s + 1):
        @pl.when(n_sel_tiles == n)
        def _():
            keys = n * tk_s
            vt = _vt_tiles(vst_ref, (0,), 0, keys // VT_TILE)
            _flash_step_t(q, ks_ref[0, 0:keys, :], vt, bias_ref[0:n].reshape(keys, tq), m0, acc_ref)

    os_ref[...] = _flash_result(acc_ref, NSA_DV)

    acc_ref[...] = jnp.zeros(acc_ref.shape, F32)
    span = -(-(WINDOW + tq) // tk) * tk
    k_start = jnp.maximum(t0 + tq - span, 0)

    def win_step(c, m):
        k0 = pl.multiple_of(k_start + c * tk, VT_TILE)
        k = kw_ref[0, pl.ds(k0, tk), :]
        vt = _vt_tiles(vwt_ref, (0,), k0 // VT_TILE, n_vt)
        kpos = k0 + lax.broadcasted_iota(jnp.int32, (tk, tq), 0)
        tq_pos = t0 + lax.broadcasted_iota(jnp.int32, (tk, tq), 1)
        ok = (kpos <= tq_pos) & (tq_pos - kpos < WINDOW)
        bias = jnp.where(ok, 0.0, NEG_BIG).astype(F32)
        return _flash_step_t(q, k, vt, bias, m, acc_ref)

    m = m0
    for c in range(span // tk):
        m = win_step(c, m)
    o_w = _flash_result(acc_ref, NSA_DV)

    gates_t = gate_ref[...].T
    for h in range(hpg):
        cols = slice(h * tq, (h + 1) * tq)
        o = (gates_t[3 * h:3 * h + 1, :] * oc_ref[:, cols]
             + gates_t[3 * h + 1:3 * h + 2, :] * os_ref[:, cols]
             + gates_t[3 * h + 2:3 * h + 3, :] * o_w[:, cols])
        o_ref[:, h * NSA_DV:(h + 1) * NSA_DV] = o.T.astype(o_ref.dtype)


def _nsa_attention(q, kc, vct, k3, vt3, gates, ovl, batch, seq, hpg, tq=128, tk_sel=512):
    tk = WINDOW + tq
    t = batch * seq
    nq = seq // tq
    n_chunk = seq // CMP_STRIDE
    n_rows = hpg * tq
    n_vt = seq // VT_TILE
    assert (WINDOW + tq) % VT_TILE == 0 and seq >= -(-(WINDOW + tq) // tk) * tk
    kern = functools.partial(_nsa_attn_kernel, hpg, tq, tk, seq)

    def k_spec(branch):
        return pl.BlockSpec((1, seq, NSA_DK), lambda b, g, qi: (branch * NSA_G + g, b, 0))

    def vt_spec(branch):
        return pl.BlockSpec((1, n_vt, NSA_DV, VT_TILE), lambda b, g, qi: (branch * NSA_G + g, b, 0, 0))

    return pl.pallas_call(
        kern,
        grid=(batch, NSA_G, nq),
        in_specs=[
            pl.BlockSpec((hpg, tq, NSA_DK), lambda b, g, qi: (g, b * nq + qi, 0)),
            pl.BlockSpec((1, n_chunk, NSA_DK), lambda b, g, qi: (b * NSA_G + g, 0, 0)),
            pl.BlockSpec((1, NSA_DV, n_chunk), lambda b, g, qi: (b * NSA_G + g, 0, 0)),
            k_spec(1), vt_spec(1), k_spec(2), vt_spec(2),
            pl.BlockSpec((tq, LANES), lambda b, g, qi: (b * nq + qi, g)),
            pl.BlockSpec(ovl.shape, lambda b, g, qi: (0, 0)),
        ],
        out_specs=pl.BlockSpec((tq, hpg * NSA_DV), lambda b, g, qi: (b * nq + qi, g)),
        out_shape=jax.ShapeDtypeStruct((t, NSA_G * hpg * NSA_DV), BF16),
        scratch_shapes=[
            pltpu.VMEM((seq // tk_sel, tk_sel, tq), F32),
            pltpu.VMEM((NSA_DV, n_rows), F32),
            pltpu.VMEM((NSA_DV, n_rows), F32),
            pltpu.VMEM((NSA_DV + SUM_ROWS, n_rows), F32),
        ],
        compiler_params=_params(("parallel", "parallel", "arbitrary")),
        name="nsa_attention",
    )(q, kc, vct, k3, vt3, k3, vt3, gates, ovl)


def _rope_tables(pos_flat):
    pos = pos_flat.astype(F32)[:, None]
    inv64 = ROPE_THETA ** (-jnp.arange(0, MLA_ROPE, 2, dtype=F32) / MLA_ROPE)
    ang = pos * inv64
    c, s = jnp.cos(ang), jnp.sin(ang)
    cos_mla = jnp.concatenate([c, c, c, c], axis=1)
    sin_mla = jnp.concatenate([-s, s, -s, s], axis=1)
    inv192 = ROPE_THETA ** (-jnp.arange(0, NSA_DK, 2, dtype=F32) / NSA_DK)
    ang = pos * inv192
    c, s = jnp.cos(ang), jnp.sin(ang)
    reps = NSA_ROPE_TILE // NSA_DK
    cos_nsa = jnp.concatenate([c, c] * reps, axis=1)
    sin_nsa = jnp.concatenate([-s, s] * reps, axis=1)
    return cos_mla, sin_mla, cos_nsa, sin_nsa


def _compress_weights(w1, pe, w2, *, transpose_w2):
    width, d_hid = w1.shape[0] // CMP_BLOCK, w1.shape[1]
    hid_p = -(-d_hid // LANES) * LANES
    w1 = jnp.pad(w1, ((0, 0), (0, hid_p - d_hid)))
    w1 = w1.reshape(2, CMP_STRIDE * width, hid_p).astype(BF16)
    pe = jnp.pad(pe.reshape(2, 1, CMP_STRIDE * width), ((0, 0), (0, 7), (0, 0))).astype(BF16)
    w2 = jnp.pad(w2, ((0, hid_p - d_hid), (0, 0))).astype(BF16)
    return w1, pe, (w2.T if transpose_w2 else w2)


def kernel(x, positions, a_norm, a_w_in, a_q_norm, a_w_uq, a_kv_norm, a_w_ukv, a_w_o, s_norm, s_w_kv, s_cmp_pe_k, s_cmp_w1_k, s_cmp_w2_k, s_cmp_pe_v, s_cmp_w1_v, s_cmp_w2_v, b_norm, b_w_in, b_w_o, f_norm, f_w_in, f_w_out, final_norm):
    batch, seq, d = x.shape
    t = batch * seq
    heads = d // 128
    hpg = heads // NSA_G
    d_ff = f_w_out.shape[1]
    n_a, n_b = a_w_in.shape[0], b_w_in.shape[0]
    assert n_a >= 1 and n_b >= 1
    tm = 1024
    sds = jax.ShapeDtypeStruct

    cos_mla, sin_mla, cos_nsa, sin_nsa = _rope_tables(positions.reshape(t))
    row = lambda i, j, k: (i, 0)
    tile = lambda i, j, k: (i, j)
    tab_spec = pl.BlockSpec((tm, LANES), row)
    ssq_spec = pl.BlockSpec((tm, LANES), row)

    def residual_mm(a, w, w_layer, h, gains, *, tm_, tn, name, a_buffers=None):
        blk = pl.BlockSpec((tm_, tn), tile)
        common = dict(tm=tm_, tn=tn, tk=a.shape[1], nj=d // tn, w_layer=w_layer, name=name,
                      a_buffers=a_buffers)
        if not gains:
            h_new = _mm(a, w, _epi_residual, sds((t, d), F32), blk,
                        extras=(h,), extra_specs=(blk,), **common)
            return h_new, None, ()
        outs = _mm(a, w, _epi_residual_prenorm,
                   (sds((t, d), F32), sds((t, LANES), F32)) + tuple(sds((t, d), BF16) for _ in gains),
                   (blk, pl.BlockSpec((tm_, LANES), row)) + tuple(blk for _ in gains),
                   extras=(h,) + tuple(g.reshape(1, d) for g in gains),
                   extra_specs=(blk,) + tuple(pl.BlockSpec((1, tn), lambda i, j, k: (0, j))
                                              for _ in gains),
                   first_tile_init=_zero_ssq, **common)
        return outs[0], outs[1], tuple(outs[2:])

    def ffn(hb, ssq, h, layer, gains):
        tn = 256
        nj = d_ff // tn
        tm_in = min(2 * tm, t)
        steps = (t // tm_in) * nj
        slab = d_ff // steps
        assert slab * steps == d_ff and slab % 16 == 0
        act, w_out = _mm(
            hb, f_w_in, functools.partial(_epi_swiglu, d),
            (sds((t, d_ff), BF16), sds((d_ff, d), BF16)),
            (pl.BlockSpec((tm_in, tn), tile),
             pl.BlockSpec((slab, d), lambda i, j, k: (i * nj + j, 0))),
            tm=tm_in, tn=tn, tk=d, nj=nj, w_layer=layer, a_buffers=1,
            w_col_maps=(lambda j: j, lambda j: j + nj),
            extras=(ssq, f_w_out),
            extra_specs=(pl.BlockSpec((tm_in, LANES), row),
                         pl.BlockSpec((None, slab, d), lambda i, j, k: (layer, i * nj + j, 0))),
            name="ffn_in")
        return residual_mm(act, w_out, None, h, gains, tm_=tm, tn=tn, name="ffn_out",
                           a_buffers=1)

    h = x.reshape(t, d)
    hb, ssq = _prenorm(h, a_norm[0])

    for i in range(n_a):
        rope_lo = MLA_Q_LORA + MLA_KV_LORA
        w_in = jnp.concatenate([a_w_in[i], a_w_in[i][:, rope_lo:]], axis=1).astype(BF16)
        n_in = w_in.shape[1]
        cq, ckv, kr = _mm(
            hb, w_in, functools.partial(_epi_mla_in, d),
            (sds((t, MLA_Q_LORA), BF16), sds((t, MLA_KV_LORA), BF16), sds((t, LANES), BF16)),
            (pl.BlockSpec((tm, MLA_Q_LORA), row), pl.BlockSpec((tm, MLA_KV_LORA), row),
             pl.BlockSpec((tm, LANES), row)),
            tm=tm, tn=n_in, tk=1024, nj=1,
            extras=(ssq, a_q_norm[i].reshape(1, -1), a_kv_norm[i].reshape(1, -1), cos_mla, sin_mla),
            extra_specs=(ssq_spec,
                         pl.BlockSpec((1, MLA_Q_LORA), lambda i_, j, k: (0, 0)),
                         pl.BlockSpec((1, MLA_KV_LORA), lambda i_, j, k: (0, 0)),
                         tab_spec, tab_spec),
            name="mla_in")

        qd = MLA_NOPE + MLA_ROPE
        w_uq = a_w_uq[i].reshape(MLA_Q_LORA, heads, qd)
        w_uq = jnp.concatenate([w_uq[:, :, :MLA_NOPE].reshape(MLA_Q_LORA, heads * MLA_NOPE),
                                w_uq[:, :, MLA_NOPE:].reshape(MLA_Q_LORA, heads * MLA_ROPE)],
                               axis=1).astype(BF16)
        tn = 512
        q = _mm(cq, w_uq,
                functools.partial(_epi_mla_q, heads * MLA_NOPE // tn, qd ** -0.5 * LOG2_E),
                jax.ShapeDtypeStruct((t, heads * qd), BF16),
                pl.BlockSpec((tm, tn), lambda i_, j, k: (i_, j)),
                tm=tm, tn=tn, tk=MLA_Q_LORA, nj=heads * qd // tn,
                extras=(cos_mla, sin_mla), extra_specs=(tab_spec, tab_spec), name="mla_q")
        heads_per_tile = 4
        kn, vt = _mm(ckv, a_w_ukv[i].astype(BF16), _epi_mla_kv,
                     (jax.ShapeDtypeStruct((t, heads * MLA_NOPE), BF16),
                      jax.ShapeDtypeStruct((heads, t // VT_TILE, MLA_V, VT_TILE), BF16)),
                     (pl.BlockSpec((tm, heads_per_tile * MLA_NOPE), lambda i_, j, k: (i_, j)),
                      pl.BlockSpec((heads_per_tile, tm // VT_TILE, MLA_V, VT_TILE),
                                   lambda i_, j, k: (j, i_, 0, 0))),
                     tm=tm, tn=heads_per_tile * (MLA_NOPE + MLA_V), tk=MLA_KV_LORA,
                     nj=heads // heads_per_tile, name="mla_kv")
        o = _mla_attention(q, kn, kr, vt, batch, seq, heads)
        h, ssq, (hb,) = residual_mm(o, a_w_o[i].astype(BF16), None, h, [f_norm[i]],
                                    tm_=tm, tn=512, name="mla_out")
        gains = [a_norm[i + 1]] if i + 1 < n_a else [s_norm, b_norm[0]]
        h, ssq, hbs = ffn(hb, ssq, h, i, gains)
        hb = hbs[0]

    hb_s, hb_b = hbs
    n_sets = 3 * NSA_G
    per_tile = NSA_ROPE_TILE // NSA_DK
    tm_rope = 512
    rope_tab = pl.BlockSpec((tm_rope, NSA_ROPE_TILE), row)
    w_kv = s_w_kv.reshape(d, n_sets, NSA_DK + NSA_DV)
    w_k = w_kv[:, :, :NSA_DK].reshape(d, n_sets * NSA_DK).astype(BF16)
    w_v = w_kv[:, :, NSA_DK:].reshape(d, n_sets * NSA_DV).astype(BF16)
    k3 = _mm(hb_s, w_k, functools.partial(_epi_nsa_rope, d, None),
             sds((n_sets, t, NSA_DK), BF16),
             pl.BlockSpec((per_tile, tm_rope, NSA_DK), lambda i_, j, k: (j, i_, 0)),
             tm=tm_rope, tn=NSA_ROPE_TILE, tk=d, nj=n_sets // per_tile,
             extras=(ssq, cos_nsa, sin_nsa),
             extra_specs=(pl.BlockSpec((tm_rope, LANES), row), rope_tab, rope_tab), name="nsa_k")
    v3, vt3 = _mm(hb_s, w_v, functools.partial(_epi_nsa_v, d),
                  (sds((n_sets, t, NSA_DV), BF16),
                   sds((n_sets, t // VT_TILE, NSA_DV, VT_TILE), BF16)),
                  (pl.BlockSpec((per_tile, tm, NSA_DV), lambda i_, j, k: (j, i_, 0)),
                   pl.BlockSpec((per_tile, tm // VT_TILE, NSA_DV, VT_TILE),
                                lambda i_, j, k: (j, i_, 0, 0))),
                  tm=tm, tn=per_tile * NSA_DV, tk=d, nj=n_sets // per_tile,
                  extras=(ssq,), extra_specs=(ssq_spec,), name="nsa_v")

    ak = k3[:NSA_G].reshape(NSA_G, t // CMP_STRIDE, CMP_STRIDE * NSA_DK)
    av = v3[:NSA_G].reshape(NSA_G, t // CMP_STRIDE, CMP_STRIDE * NSA_DV)
    kc, vct = _compress(ak, av,
                        _compress_weights(s_cmp_w1_k, s_cmp_pe_k, s_cmp_w2_k, transpose_w2=False),
                        _compress_weights(s_cmp_w1_v, s_cmp_pe_v, s_cmp_w2_v, transpose_w2=True),
                        batch, seq)

    n_chunk = seq // CMP_STRIDE
    n_sel = seq // SEL_BLOCK
    c_start = np.arange(n_chunk) * CMP_STRIDE
    j_start = np.arange(n_sel) * SEL_BLOCK
    ovl = ((c_start[None, :] < j_start[:, None] + SEL_BLOCK)
           & (c_start[None, :] + CMP_BLOCK > j_start[:, None])
           & (np.arange(n_chunk)[None, :] < (seq - CMP_BLOCK) // CMP_STRIDE + 1))
    ovl = jnp.asarray(ovl, BF16)

    for i in range(n_b):
        n_q = heads * NSA_DK
        w_q = b_w_in[i][:, :n_q].astype(BF16)
        w_g = b_w_in[i][:, n_q:].reshape(d, NSA_G, 3 * hpg)
        w_g = jnp.pad(w_g, ((0, 0), (0, 0), (0, LANES - 3 * hpg))).reshape(d, NSA_G * LANES)
        w_g = w_g.astype(BF16)
        q = _mm(hb_b, w_q, functools.partial(_epi_nsa_rope, d, NSA_DK ** -0.5 * LOG2_E),
                sds((heads, t, NSA_DK), BF16),
                pl.BlockSpec((per_tile, tm_rope, NSA_DK), lambda i_, j, k: (j, i_, 0)),
                tm=tm_rope, tn=NSA_ROPE_TILE, tk=d, nj=heads // per_tile,
                extras=(ssq, cos_nsa, sin_nsa),
                extra_specs=(pl.BlockSpec((tm_rope, LANES), row), rope_tab, rope_tab),
                name="nsa_q")
        gates = _mm(hb_b, w_g, functools.partial(_epi_sigmoid, d), sds((t, NSA_G * LANES), F32),
                    pl.BlockSpec((tm, NSA_G * LANES), tile),
                    tm=tm, tn=NSA_G * LANES, tk=d, nj=1,
                    extras=(ssq,), extra_specs=(ssq_spec,), name="nsa_gates")
        o = _nsa_attention(q, kc, vct, k3, vt3, gates, ovl, batch, seq, hpg)
        h, ssq, (hb,) = residual_mm(o, b_w_o[i].astype(BF16), None, h, [f_norm[n_a + i]],
                                    tm_=tm, tn=512, name="nsa_out")
        gains = [b_norm[i + 1]] if i + 1 < n_b else []
        h, ssq, hbs = ffn(hb, ssq, h, n_a + i, gains)
        if hbs:
            hb_b = hbs[0]

    out = _rmsnorm(h, final_norm, F32)
    return out.reshape(batch, seq, d)
```

```python
import functools

import numpy as np
import jax
import jax.numpy as jnp
from jax import lax
from jax.experimental import pallas as pl
from jax.experimental.pallas import tpu as pltpu

F32 = jnp.float32
BF16 = jnp.bfloat16

LANES = 128
VMEM_LIMIT_BYTES = 56 * 1024 * 1024

NORM_EPS = 1e-6
ROPE_THETA = 10000.0

MLA_Q_LORA = 1536
MLA_KV_LORA = 512
MLA_NOPE = 128
MLA_ROPE = 64
MLA_V = 128

NSA_G = 4
NSA_DK = 192
NSA_DV = 128
NSA_HALF = NSA_DK // 2
NSA_ROPE_TILE = 4 * NSA_DK
CMP_BLOCK = 32
CMP_STRIDE = 16
SEL_BLOCK = 64
SEL_TOPK = 16
WINDOW = 512
FORCE_BONUS = 1e4
NEG_BIG = -1e30
LOG2_E = 1.4426950408889634

VT_TILE = LANES
NT_DIMS = (((1,), (1,)), ((), ()))


def _params(sem):
    return pltpu.CompilerParams(dimension_semantics=sem, vmem_limit_bytes=VMEM_LIMIT_BYTES)


def _rmsnorm_kernel(x_ref, g_ref, o_ref):
    x = x_ref[...]
    ms = jnp.mean(x * x, axis=-1, keepdims=True)
    o_ref[...] = (x * lax.rsqrt(ms + NORM_EPS) * g_ref[...]).astype(o_ref.dtype)


def _rmsnorm(x, g, out_dtype, tm=256):
    t, d = x.shape
    return pl.pallas_call(
        _rmsnorm_kernel,
        grid=(t // tm,),
        in_specs=[pl.BlockSpec((tm, d), lambda i: (i, 0)),
                  pl.BlockSpec((1, d), lambda i: (0, 0))],
        out_specs=pl.BlockSpec((tm, d), lambda i: (i, 0)),
        out_shape=jax.ShapeDtypeStruct((t, d), out_dtype),
        compiler_params=_params(("parallel",)),
        name="rmsnorm",
    )(x, g.reshape(1, d))


def _prenorm_kernel(x_ref, g_ref, hb_ref, ssq_ref):
    x = x_ref[...]
    hb_ref[...] = (x * g_ref[...]).astype(hb_ref.dtype)
    ssq_ref[...] = jnp.broadcast_to(jnp.sum(x * x, axis=-1, keepdims=True), ssq_ref.shape)


def _prenorm(x, g, tm=256):
    t, d = x.shape
    return pl.pallas_call(
        _prenorm_kernel,
        grid=(t // tm,),
        in_specs=[pl.BlockSpec((tm, d), lambda i: (i, 0)),
                  pl.BlockSpec((1, d), lambda i: (0, 0))],
        out_specs=(pl.BlockSpec((tm, d), lambda i: (i, 0)),
                   pl.BlockSpec((tm, LANES), lambda i: (i, 0))),
        out_shape=(jax.ShapeDtypeStruct((t, d), BF16), jax.ShapeDtypeStruct((t, LANES), F32)),
        compiler_params=_params(("parallel",)),
        name="prenorm",
    )(x, g.reshape(1, d))


def _mm_kernel(n_w, n_e, n_o, nk, epilogue, first_tile_init, *refs):
    a_ref = refs[0]
    w_refs = refs[1:1 + n_w]
    e_refs = refs[1 + n_w:1 + n_w + n_e]
    o_refs = refs[1 + n_w + n_e:1 + n_w + n_e + n_o]
    acc_refs = refs[1 + n_w + n_e + n_o:]
    j = pl.program_id(1)
    if first_tile_init is not None:
        pl.when((j == 0) & (pl.program_id(2) == 0))(lambda: first_tile_init(o_refs))

    def product(w):
        return jnp.dot(a_ref[...], w[...].astype(a_ref.dtype), preferred_element_type=F32)

    if nk == 1:
        epilogue([product(w) for w in w_refs], e_refs, o_refs, j)
        return
    k = pl.program_id(2)

    @pl.when(k == 0)
    def _():
        for w, acc in zip(w_refs, acc_refs):
            acc[...] = product(w)

    @pl.when(k > 0)
    def _():
        for w, acc in zip(w_refs, acc_refs):
            acc[...] += product(w)

    @pl.when(k == nk - 1)
    def _():
        epilogue([acc[...] for acc in acc_refs], e_refs, o_refs, j)


def _mm(a, w, epilogue, out_shape, out_specs, *, tm, tn, tk, nj, w_col_maps=None, w_layer=None,
        extras=(), extra_specs=(), a_buffers=None, first_tile_init=None, name="matmul"):
    m, kdim = a.shape
    ni, nk = m // tm, kdim // tk
    assert ni * tm == m and nk * tk == kdim
    if w_col_maps is None:
        w_col_maps = (lambda j: j,)
    n_w = len(w_col_maps)
    single = not isinstance(out_shape, (tuple, list))
    out_shapes = (out_shape,) if single else tuple(out_shape)
    out_specs_t = (out_specs,) if single else tuple(out_specs)
    a_kwargs = {} if a_buffers is None else {"pipeline_mode": pl.Buffered(a_buffers)}
    in_specs = [pl.BlockSpec((tm, tk), lambda i, j, k: (i, k), **a_kwargs)]
    for cm in w_col_maps:
        if w_layer is None:
            in_specs.append(pl.BlockSpec((tk, tn), functools.partial(
                lambda i, j, k, cm: (k, cm(j)), cm=cm)))
        else:
            in_specs.append(pl.BlockSpec((None, tk, tn), functools.partial(
                lambda i, j, k, cm: (w_layer, k, cm(j)), cm=cm)))
    in_specs += list(extra_specs)
    scratch = [pltpu.VMEM((tm, tn), F32) for _ in range(n_w)] if nk > 1 else []
    kern = functools.partial(_mm_kernel, n_w, len(extras), len(out_shapes), nk, epilogue,
                             first_tile_init)
    res = pl.pallas_call(
        kern,
        grid=(ni, nj, nk),
        in_specs=in_specs,
        out_specs=out_specs_t,
        out_shape=out_shapes,
        scratch_shapes=scratch,
        compiler_params=_params(("parallel", "arbitrary", "arbitrary")),
        name=name,
    )(a, *([w] * n_w), *extras)
    return res[0] if single else res


def _rope_pairs_64(x, cos2, sin2):
    lane = lax.broadcasted_iota(jnp.int32, x.shape, 1)
    first_half = (lane % MLA_ROPE) < (MLA_ROPE // 2)
    partner = jnp.where(first_half,
                        pltpu.roll(x, LANES - MLA_ROPE // 2, 1),
                        pltpu.roll(x, MLA_ROPE // 2, 1))
    return x * cos2 + partner * sin2


def _store_vt_tiles(vt_ref, lead, v):
    for c in range(v.shape[0] // VT_TILE):
        vt_ref[lead + (c,)] = v[c * VT_TILE:(c + 1) * VT_TILE, :].T.astype(vt_ref.dtype)


def _row_scale(acc, ssq_ref, d_model):
    rstd = lax.rsqrt(ssq_ref[...] * (1.0 / d_model) + NORM_EPS)
    return acc * jnp.concatenate([rstd] * (acc.shape[1] // LANES), axis=1)


def _epi_residual(accs, e_refs, o_refs, j):
    o_refs[0][...] = e_refs[0][...] + accs[0]


def _zero_ssq(o_refs):
    o_refs[1][...] = jnp.zeros(o_refs[1].shape, F32)


def _epi_residual_prenorm(n_gains, accs, e_refs, o_refs, j):
    h = e_refs[0][...] + accs[0]
    o_refs[0][...] = h
    ssq_ref = o_refs[1]
    ssq_ref[...] += jnp.broadcast_to(jnp.sum(h * h, axis=-1, keepdims=True), ssq_ref.shape)
    for g_ref, hb_ref in zip(e_refs[1:1 + n_gains], o_refs[2:2 + n_gains]):
        hb_ref[...] = (h * g_ref[...]).astype(hb_ref.dtype)
    for w_ref, wb_ref in zip(e_refs[1 + n_gains:], o_refs[2 + n_gains:]):
        wb_ref[...] = w_ref[...].astype(wb_ref.dtype)


def _epi_swiglu(d_model, accs, e_refs, o_refs, j):
    a = _row_scale(accs[0], e_refs[0], d_model)
    b = _row_scale(accs[1], e_refs[0], d_model)
    o_refs[0][...] = (a * jax.nn.sigmoid(a) * b).astype(o_refs[0].dtype)
    if len(o_refs) > 1:
        o_refs[1][...] = e_refs[1][...].astype(o_refs[1].dtype)


def _epi_sigmoid(d_model, accs, e_refs, o_refs, j):
    o_refs[0][...] = jax.nn.sigmoid(_row_scale(accs[0], e_refs[0], d_model))


def _epi_mla_in(d_model, accs, e_refs, o_refs, j):
    ssq_ref, gq_ref, gkv_ref, cos_ref, sin_ref = e_refs
    u = _row_scale(accs[0], ssq_ref, d_model)
    cq = u[:, :MLA_Q_LORA]
    ms = jnp.mean(cq * cq, axis=-1, keepdims=True)
    o_refs[0][...] = (cq * lax.rsqrt(ms + NORM_EPS) * gq_ref[...]).astype(BF16)
    ckv = u[:, MLA_Q_LORA:MLA_Q_LORA + MLA_KV_LORA]
    ms = jnp.mean(ckv * ckv, axis=-1, keepdims=True)
    o_refs[1][...] = (ckv * lax.rsqrt(ms + NORM_EPS) * gkv_ref[...]).astype(BF16)
    kr = u[:, MLA_Q_LORA + MLA_KV_LORA:]
    o_refs[2][...] = _rope_pairs_64(kr, cos_ref[...], sin_ref[...]).astype(BF16)


def _epi_mla_q(n_nope_tiles, scale, accs, e_refs, o_refs, j):
    cos_ref, sin_ref = e_refs
    acc = accs[0]

    @pl.when(j < n_nope_tiles)
    def _():
        o_refs[0][...] = (acc * scale).astype(BF16)

    @pl.when(j >= n_nope_tiles)
    def _():
        for c in range(acc.shape[1] // LANES):
            x = acc[:, c * LANES:(c + 1) * LANES]
            r = _rope_pairs_64(x, cos_ref[...], sin_ref[...])
            o_refs[0][:, c * LANES:(c + 1) * LANES] = (r * scale).astype(BF16)


def _epi_mla_kv(accs, e_refs, o_refs, j):
    acc = accs[0]
    kn_ref, vt_ref = o_refs
    per = MLA_NOPE + MLA_V
    for r in range(acc.shape[1] // per):
        kn_ref[:, r * MLA_NOPE:(r + 1) * MLA_NOPE] = acc[:, r * per:r * per + MLA_NOPE].astype(BF16)
        _store_vt_tiles(vt_ref, (r,), acc[:, r * per + MLA_NOPE:(r + 1) * per])


def _rope_192(x, cos, sin):
    width = x.shape[1]
    lane = lax.broadcasted_iota(jnp.int32, x.shape, 1)
    partner = jnp.where((lane % NSA_DK) < NSA_HALF,
                        pltpu.roll(x, width - NSA_HALF, 1),
                        pltpu.roll(x, NSA_HALF, 1))
    return x * cos + partner * sin


def _epi_nsa_rope(d_model, scale, accs, e_refs, o_refs, j):
    ssq_ref, cos_ref, sin_ref = e_refs
    acc = _row_scale(accs[0], ssq_ref, d_model)
    roped = _rope_192(acc, cos_ref[...], sin_ref[...])
    if scale is not None:
        roped = roped * scale
    for r in range(acc.shape[1] // NSA_DK):
        o_refs[0][r] = roped[:, r * NSA_DK:(r + 1) * NSA_DK].astype(BF16)


def _epi_nsa_v(d_model, accs, e_refs, o_refs, j):
    v_ref, vt_ref = o_refs
    acc = _row_scale(accs[0], e_refs[0], d_model)
    for r in range(acc.shape[1] // NSA_DV):
        v = acc[:, r * NSA_DV:(r + 1) * NSA_DV]
        v_ref[r] = v.astype(BF16)
        _store_vt_tiles(vt_ref, (r,), v)


SUM_ROWS = 16


def _attend_t(q, k, vt, bias):
    dv = vt.shape[0]
    s = lax.dot_general(k, q, NT_DIMS, preferred_element_type=F32)
    bias = jnp.concatenate([bias] * (s.shape[1] // bias.shape[1]), axis=1)
    free = s.shape[0] - bias.shape[0]
    s = s + bias if free == 0 else jnp.concatenate([s[:free], s[free:] + bias], axis=0)
    p = jnp.exp2(s - jnp.max(s, axis=0, keepdims=True)).astype(BF16)
    vt1 = jnp.concatenate([vt, jnp.ones((SUM_ROWS, vt.shape[1]), BF16)], axis=0)
    acc = jnp.dot(vt1, p, preferred_element_type=F32)
    return acc[:dv] / acc[dv:dv + 1]


def _vt_tiles(vt_ref, lead, c0, n):
    return jnp.concatenate([vt_ref[lead + (c0 + c,)] for c in range(n)], axis=1)


def _mla_attn_kernel(t_blk, seq, qn_ref, qr_ref, kn_ref, kr_ref, vt_ref, o_ref):
    h = pl.program_id(1)
    lane_half = lax.broadcasted_iota(jnp.int32, (t_blk, LANES), 1) // MLA_ROPE
    kpos = lax.broadcasted_iota(jnp.int32, (t_blk, t_blk), 0)
    qpos = lax.broadcasted_iota(jnp.int32, (t_blk, t_blk), 1)
    diag_bias = jnp.where(kpos <= qpos, 0.0, NEG_BIG).astype(F32)

    for qi in range(seq // t_blk):
        q0 = qi * t_blk
        qr = qr_ref[q0:q0 + t_blk, :]
        qr = jnp.where(lane_half == (h % 2), qr, jnp.zeros_like(qr))
        q = jnp.concatenate([qn_ref[q0:q0 + t_blk, :], qr], axis=1)
        n_keys = q0 + t_blk
        k = jnp.concatenate([kn_ref[0:n_keys, :], kr_ref[0:n_keys, :]], axis=1)
        vt = _vt_tiles(vt_ref, (0,), 0, n_keys // VT_TILE)
        o_ref[q0:q0 + t_blk, :] = _attend_t(q, k, vt, diag_bias).T.astype(o_ref.dtype)


def _mla_attention(q, kn, kr, vt, batch, seq, heads, t_blk=512):
    t = batch * seq
    n_nope_blk = heads * MLA_NOPE // LANES
    n_vt = seq // VT_TILE
    kern = functools.partial(_mla_attn_kernel, t_blk, seq)
    return pl.pallas_call(
        kern,
        grid=(batch, heads),
        in_specs=[
            pl.BlockSpec((seq, MLA_NOPE), lambda b, h: (b, h)),
            pl.BlockSpec((seq, LANES), lambda b, h: (b, n_nope_blk + h // 2)),
            pl.BlockSpec((seq, MLA_NOPE), lambda b, h: (b, h)),
            pl.BlockSpec((seq, LANES), lambda b, h: (b, 0)),
            pl.BlockSpec((1, n_vt, MLA_V, VT_TILE), lambda b, h: (h, b, 0, 0)),
        ],
        out_specs=pl.BlockSpec((seq, MLA_V), lambda b, h: (b, h)),
        out_shape=jax.ShapeDtypeStruct((t, heads * MLA_V), BF16),
        compiler_params=_params(("parallel", "parallel")),
        name="mla_attention",
    )(q, q, kn, kr, vt)


def _compress_mlp(a, w1_ref, pe_ref, w2_ref, transposed_out):
    y_lo = jnp.dot(a, w1_ref[0], preferred_element_type=F32)
    y_hi = jnp.dot(a, w1_ref[1], preferred_element_type=F32)
    pe_c = (jnp.dot(pe_ref[0], w1_ref[0], preferred_element_type=F32)
            + jnp.dot(pe_ref[1], w1_ref[1], preferred_element_type=F32))
    n = y_hi.shape[0]
    hid = y_lo + pltpu.roll(y_hi, n - 1, 0) + pe_c[0:1, :]
    hid = jax.nn.gelu(hid, approximate=True)
    if transposed_out:
        return lax.dot_general(w2_ref[...], hid.astype(BF16), NT_DIMS, preferred_element_type=F32)
    return jnp.dot(hid.astype(BF16), w2_ref[...], preferred_element_type=F32)


def _compress_kernel(ak_ref, av_ref, w1k_ref, pek_ref, w2k_ref, w1v_ref, pev_ref, w2vt_ref,
                     kc_ref, vct_ref):
    kc_ref[0] = _compress_mlp(ak_ref[0], w1k_ref, pek_ref, w2k_ref, False).astype(BF16)
    vct_ref[0] = _compress_mlp(av_ref[0], w1v_ref, pev_ref, w2vt_ref, True).astype(BF16)


def _compress(ak, av, wk, wv, batch, seq):
    n_chunk = seq // CMP_STRIDE
    const3 = lambda b, g: (0, 0, 0)
    const2 = lambda b, g: (0, 0)

    def wspecs(w):
        w1, pe, w2 = w
        return [pl.BlockSpec(w1.shape, const3), pl.BlockSpec(pe.shape, const3),
                pl.BlockSpec(w2.shape, const2)]

    return pl.pallas_call(
        _compress_kernel,
        grid=(batch, NSA_G),
        in_specs=[pl.BlockSpec((1, n_chunk, ak.shape[2]), lambda b, g: (g, b, 0)),
                  pl.BlockSpec((1, n_chunk, av.shape[2]), lambda b, g: (g, b, 0))]
        + wspecs(wk) + wspecs(wv),
        out_specs=(pl.BlockSpec((1, n_chunk, NSA_DK), lambda b, g: (b * NSA_G + g, 0, 0)),
                   pl.BlockSpec((1, NSA_DV, n_chunk), lambda b, g: (b * NSA_G + g, 0, 0))),
        out_shape=(jax.ShapeDtypeStruct((batch * NSA_G, n_chunk, NSA_DK), BF16),
                   jax.ShapeDtypeStruct((batch * NSA_G, NSA_DV, n_chunk), BF16)),
        compiler_params=_params(("parallel", "parallel")),
        name="nsa_compress",
    )(ak, av, *wk, *wv)


def _nsa_attn_kernel(hpg, tq, tk, seq, q_ref, kc_ref, vct_ref, ks_ref, vst_ref, kw_ref, vwt_ref,
                     gate_ref, ovl_ref, o_ref, bias_ref, oc_ref, os_ref):
    qi = pl.program_id(2)
    t0 = qi * tq
    n_rows = hpg * tq
    n_cmp = kc_ref.shape[1]
    n_sel = seq // SEL_BLOCK
    n_vt = tk // VT_TILE
    q = q_ref[...].reshape(n_rows, NSA_DK)

    def tile_heads(x):
        return jnp.concatenate([x] * hpg, axis=1)

    s = lax.dot_general(kc_ref[0], q, NT_DIMS, preferred_element_type=F32)
    cidx = lax.broadcasted_iota(jnp.int32, (n_cmp, tq), 0)
    tpos_c = t0 + lax.broadcasted_iota(jnp.int32, (n_cmp, tq), 1)
    mask_c = tile_heads((cidx * CMP_STRIDE + CMP_BLOCK - 1) <= tpos_c)
    s = jnp.where(mask_c, s, NEG_BIG)
    mx = jnp.max(s, axis=0, keepdims=True)
    e = jnp.where(mask_c, jnp.exp2(s - mx), 0.0)
    p = e / jnp.maximum(jnp.sum(e, axis=0, keepdims=True), 1e-30)
    oc_ref[...] = jnp.dot(vct_ref[0], p.astype(BF16), preferred_element_type=F32)

    psum = p[:, 0:tq]
    for h in range(1, hpg):
        psum = psum + p[:, h * tq:(h + 1) * tq]
    p1 = psum.astype(BF16)
    p2 = (psum - p1.astype(F32)).astype(BF16)
    p3 = (psum - p1.astype(F32) - p2.astype(F32)).astype(BF16)
    ovl = ovl_ref[...]
    imp = (jnp.dot(ovl, p1, preferred_element_type=F32)
           + jnp.dot(ovl, p2, preferred_element_type=F32)
           + jnp.dot(ovl, p3, preferred_element_type=F32))
    jrow = lax.broadcasted_iota(jnp.int32, (n_sel, tq), 0)
    tpos = t0 + lax.broadcasted_iota(jnp.int32, (n_sel, tq), 1)
    valid = jrow * SEL_BLOCK <= tpos
    cur = jnp.right_shift(tpos, SEL_BLOCK.bit_length() - 1)
    forced = valid & ((jrow == 0) | (jrow == cur) | (jrow == cur - 1))
    score = jnp.where(valid, imp + jnp.where(forced, FORCE_BONUS, 0.0), -jnp.inf)
    rank = jnp.zeros((n_sel, tq), jnp.int32)
    for i in range(n_sel):
        row = score[i:i + 1, :]
        beats = (row > score) | ((row == score) & (jrow > i))
        rank = rank + beats.astype(jnp.int32)
    sel_bias = jnp.where(valid & (rank < SEL_TOPK), 0.0, NEG_BIG).astype(F32)

    tk_s = bias_ref.shape[1]
    n_sel_tiles = (t0 + tq - 1) // tk_s + 1
    blocks_per_tile = tk_s // SEL_BLOCK
    for jj in range(seq // tk_s):
        @pl.when(jj < n_sel_tiles)
        def _():
            rows = [jnp.broadcast_to(sel_bias[b:b + 1, :], (SEL_BLOCK, tq))
                    for b in range(jj * blocks_per_tile, (jj + 1) * blocks_per_tile)]
            kpos = jj * tk_s + lax.broadcasted_iota(jnp.int32, (tk_s, tq), 0)
            tq_pos = t0 + lax.broadcasted_iota(jnp.int32, (tk_s, tq), 1)
            bias_ref[jj] = jnp.where(kpos <= tq_pos, jnp.concatenate(rows, axis=0), NEG_BIG)

    for n in range(1, seq // tk_s + 1):
        @pl.when(n_sel_tiles == n)
        def _():
            keys = n * tk_s
            vt = _vt_tiles(vst_ref, (0,), 0, keys // VT_TILE)
            os_ref[...] = _attend_t(q, ks_ref[0, 0:keys, :], vt, bias_ref[0:n].reshape(keys, tq))

    k0 = pl.multiple_of(jnp.maximum(t0 + tq - tk, 0), VT_TILE)
    kpos = k0 + lax.broadcasted_iota(jnp.int32, (tk, tq), 0)
    tq_pos = t0 + lax.broadcasted_iota(jnp.int32, (tk, tq), 1)
    ok = (kpos <= tq_pos) & (tq_pos - kpos < WINDOW)
    o_w = _attend_t(q, kw_ref[0, pl.ds(k0, tk), :], _vt_tiles(vwt_ref, (0,), k0 // VT_TILE, n_vt),
                    jnp.where(ok, 0.0, NEG_BIG).astype(F32))

    gates_t = gate_ref[...].T
    for h in range(hpg):
        cols = slice(h * tq, (h + 1) * tq)
        o = (gates_t[3 * h:3 * h + 1, :] * oc_ref[:, cols]
             + gates_t[3 * h + 1:3 * h + 2, :] * os_ref[:, cols]
             + gates_t[3 * h + 2:3 * h + 3, :] * o_w[:, cols])
        o_ref[:, h * NSA_DV:(h + 1) * NSA_DV] = o.T.astype(o_ref.dtype)


def _nsa_attention(q, kc, vct, k3, vt3, gates, ovl, batch, seq, hpg, tq=128, tk_sel=512):
    tk = WINDOW + tq
    t = batch * seq
    nq = seq // tq
    n_chunk = seq // CMP_STRIDE
    n_rows = hpg * tq
    n_vt = seq // VT_TILE
    assert tk % VT_TILE == 0 and seq >= tk and seq % tk_sel == 0
    kern = functools.partial(_nsa_attn_kernel, hpg, tq, tk, seq)

    def k_spec(branch):
        return pl.BlockSpec((1, seq, NSA_DK), lambda b, g, qi: (branch * NSA_G + g, b, 0))

    def vt_spec(branch):
        return pl.BlockSpec((1, n_vt, NSA_DV, VT_TILE), lambda b, g, qi: (branch * NSA_G + g, b, 0, 0))

    return pl.pallas_call(
        kern,
        grid=(batch, NSA_G, nq),
        in_specs=[
            pl.BlockSpec((hpg, tq, NSA_DK), lambda b, g, qi: (g, b * nq + qi, 0)),
            pl.BlockSpec((1, n_chunk, NSA_DK), lambda b, g, qi: (b * NSA_G + g, 0, 0)),
            pl.BlockSpec((1, NSA_DV, n_chunk), lambda b, g, qi: (b * NSA_G + g, 0, 0)),
            k_spec(1), vt_spec(1), k_spec(2), vt_spec(2),
            pl.BlockSpec((tq, LANES), lambda b, g, qi: (b * nq + qi, g)),
            pl.BlockSpec(ovl.shape, lambda b, g, qi: (0, 0)),
        ],
        out_specs=pl.BlockSpec((tq, hpg * NSA_DV), lambda b, g, qi: (b * nq + qi, g)),
        out_shape=jax.ShapeDtypeStruct((t, NSA_G * hpg * NSA_DV), BF16),
        scratch_shapes=[
            pltpu.VMEM((seq // tk_sel, tk_sel, tq), F32),
            pltpu.VMEM((NSA_DV, n_rows), F32),
            pltpu.VMEM((NSA_DV, n_rows), F32),
        ],
        compiler_params=_params(("parallel", "parallel", "arbitrary")),
        name="nsa_attention",
    )(q, kc, vct, k3, vt3, k3, vt3, gates, ovl)


def _rope_tables(pos_flat):
    pos = pos_flat.astype(F32)[:, None]
    inv64 = ROPE_THETA ** (-jnp.arange(0, MLA_ROPE, 2, dtype=F32) / MLA_ROPE)
    ang = pos * inv64
    c, s = jnp.cos(ang), jnp.sin(ang)
    cos_mla = jnp.concatenate([c, c, c, c], axis=1)
    sin_mla = jnp.concatenate([-s, s, -s, s], axis=1)
    inv192 = ROPE_THETA ** (-jnp.arange(0, NSA_DK, 2, dtype=F32) / NSA_DK)
    ang = pos * inv192
    c, s = jnp.cos(ang), jnp.sin(ang)
    reps = NSA_ROPE_TILE // NSA_DK
    cos_nsa = jnp.concatenate([c, c] * reps, axis=1)
    sin_nsa = jnp.concatenate([-s, s] * reps, axis=1)
    return cos_mla, sin_mla, cos_nsa, sin_nsa


def _compress_weights(w1, pe, w2, *, transpose_w2):
    width, d_hid = w1.shape[0] // CMP_BLOCK, w1.shape[1]
    hid_p = -(-d_hid // LANES) * LANES
    w1 = jnp.pad(w1, ((0, 0), (0, hid_p - d_hid)))
    w1 = w1.reshape(2, CMP_STRIDE * width, hid_p).astype(BF16)
    pe = jnp.pad(pe.reshape(2, 1, CMP_STRIDE * width), ((0, 0), (0, 7), (0, 0))).astype(BF16)
    w2 = jnp.pad(w2, ((0, hid_p - d_hid), (0, 0))).astype(BF16)
    return w1, pe, (w2.T if transpose_w2 else w2)


def kernel(x, positions, a_norm, a_w_in, a_q_norm, a_w_uq, a_kv_norm, a_w_ukv, a_w_o, s_norm, s_w_kv, s_cmp_pe_k, s_cmp_w1_k, s_cmp_w2_k, s_cmp_pe_v, s_cmp_w1_v, s_cmp_w2_v, b_norm, b_w_in, b_w_o, f_norm, f_w_in, f_w_out, final_norm):
    batch, seq, d = x.shape
    t = batch * seq
    heads = d // 128
    hpg = heads // NSA_G
    d_ff = f_w_out.shape[1]
    n_a, n_b = a_w_in.shape[0], b_w_in.shape[0]
    assert n_a >= 1 and n_b >= 1
    tm = 1024
    sds = jax.ShapeDtypeStruct

    cos_mla, sin_mla, cos_nsa, sin_nsa = _rope_tables(positions.reshape(t))
    row = lambda i, j, k: (i, 0)
    tile = lambda i, j, k: (i, j)
    tab_spec = pl.BlockSpec((tm, LANES), row)
    ssq_spec = pl.BlockSpec((tm, LANES), row)

    def residual_mm(a, w, w_layer, h, gains, *, tm_, tn, name, a_buffers=None, to_round=()):
        blk = pl.BlockSpec((tm_, tn), tile)
        nj = d // tn
        common = dict(tm=tm_, tn=tn, tk=a.shape[1], nj=nj, w_layer=w_layer, name=name,
                      a_buffers=a_buffers)
        if not gains:
            assert not to_round
            h_new = _mm(a, w, _epi_residual, sds((t, d), F32), blk,
                        extras=(h,), extra_specs=(blk,), **common)
            return h_new, None, (), ()
        steps = (t // tm_) * nj
        slab_specs = []
        for wr in to_round:
            slab = wr.shape[0] // steps
            assert slab * steps == wr.shape[0] and slab % 16 == 0
            slab_specs.append(pl.BlockSpec((slab, wr.shape[1]), lambda i, j, k: (i * nj + j, 0)))
        outs = _mm(a, w, functools.partial(_epi_residual_prenorm, len(gains)),
                   (sds((t, d), F32), sds((t, LANES), F32)) + tuple(sds((t, d), BF16) for _ in gains)
                   + tuple(sds(wr.shape, BF16) for wr in to_round),
                   (blk, pl.BlockSpec((tm_, LANES), row)) + tuple(blk for _ in gains)
                   + tuple(slab_specs),
                   extras=(h,) + tuple(g.reshape(1, d) for g in gains) + tuple(to_round),
                   extra_specs=(blk,) + tuple(pl.BlockSpec((1, tn), lambda i, j, k: (0, j))
                                              for _ in gains) + tuple(slab_specs),
                   first_tile_init=_zero_ssq, **common)
        n_g = len(gains)
        return outs[0], outs[1], tuple(outs[2:2 + n_g]), tuple(outs[2 + n_g:])

    def ffn(hb, ssq, h, layer, gains, to_round=()):
        tn = 256
        nj = d_ff // tn
        tm_in = min(2 * tm, t)
        steps = (t // tm_in) * nj
        slab = d_ff // steps
        assert slab * steps == d_ff and slab % 16 == 0
        act, w_out = _mm(
            hb, f_w_in, functools.partial(_epi_swiglu, d),
            (sds((t, d_ff), BF16), sds((d_ff, d), BF16)),
            (pl.BlockSpec((tm_in, tn), tile),
             pl.BlockSpec((slab, d), lambda i, j, k: (i * nj + j, 0))),
            tm=tm_in, tn=tn, tk=d, nj=nj, w_layer=layer, a_buffers=1,
            w_col_maps=(lambda j: j, lambda j: j + nj),
            extras=(ssq, f_w_out),
            extra_specs=(pl.BlockSpec((tm_in, LANES), row),
                         pl.BlockSpec((None, slab, d), lambda i, j, k: (layer, i * nj + j, 0))),
            name="ffn_in")
        return residual_mm(act, w_out, None, h, gains, tm_=tm, tn=tn, name="ffn_out",
                           a_buffers=1, to_round=to_round)

    h = x.reshape(t, d)
    hb, ssq = _prenorm(h, a_norm[0])

    for i in range(n_a):
        rope_lo = MLA_Q_LORA + MLA_KV_LORA
        w_in = jnp.concatenate([a_w_in[i], a_w_in[i][:, rope_lo:]], axis=1).astype(BF16)
        n_in = w_in.shape[1]
        cq, ckv, kr = _mm(
            hb, w_in, functools.partial(_epi_mla_in, d),
            (sds((t, MLA_Q_LORA), BF16), sds((t, MLA_KV_LORA), BF16), sds((t, LANES), BF16)),
            (pl.BlockSpec((tm, MLA_Q_LORA), row), pl.BlockSpec((tm, MLA_KV_LORA), row),
             pl.BlockSpec((tm, LANES), row)),
            tm=tm, tn=n_in, tk=1024, nj=1,
            extras=(ssq, a_q_norm[i].reshape(1, -1), a_kv_norm[i].reshape(1, -1), cos_mla, sin_mla),
            extra_specs=(ssq_spec,
                         pl.BlockSpec((1, MLA_Q_LORA), lambda i_, j, k: (0, 0)),
                         pl.BlockSpec((1, MLA_KV_LORA), lambda i_, j, k: (0, 0)),
                         tab_spec, tab_spec),
            name="mla_in")

        qd = MLA_NOPE + MLA_ROPE
        w_uq = a_w_uq[i].reshape(MLA_Q_LORA, heads, qd)
        w_uq = jnp.concatenate([w_uq[:, :, :MLA_NOPE].reshape(MLA_Q_LORA, heads * MLA_NOPE),
                                w_uq[:, :, MLA_NOPE:].reshape(MLA_Q_LORA, heads * MLA_ROPE)],
                               axis=1).astype(BF16)
        tn = 512
        q = _mm(cq, w_uq,
                functools.partial(_epi_mla_q, heads * MLA_NOPE // tn, qd ** -0.5 * LOG2_E),
                jax.ShapeDtypeStruct((t, heads * qd), BF16),
                pl.BlockSpec((tm, tn), lambda i_, j, k: (i_, j)),
                tm=tm, tn=tn, tk=MLA_Q_LORA, nj=heads * qd // tn,
                extras=(cos_mla, sin_mla), extra_specs=(tab_spec, tab_spec), name="mla_q")
        heads_per_tile = 4
        kn, vt = _mm(ckv, a_w_ukv[i].astype(BF16), _epi_mla_kv,
                     (jax.ShapeDtypeStruct((t, heads * MLA_NOPE), BF16),
                      jax.ShapeDtypeStruct((heads, t // VT_TILE, MLA_V, VT_TILE), BF16)),
                     (pl.BlockSpec((tm, heads_per_tile * MLA_NOPE), lambda i_, j, k: (i_, j)),
                      pl.BlockSpec((heads_per_tile, tm // VT_TILE, MLA_V, VT_TILE),
                                   lambda i_, j, k: (j, i_, 0, 0))),
                     tm=tm, tn=heads_per_tile * (MLA_NOPE + MLA_V), tk=MLA_KV_LORA,
                     nj=heads // heads_per_tile, name="mla_kv")
        o = _mla_attention(q, kn, kr, vt, batch, seq, heads)
        h, ssq, (hb,), _ = residual_mm(o, a_w_o[i].astype(BF16), None, h, [f_norm[i]],
                                       tm_=tm, tn=512, name="mla_out")
        last_a = i + 1 == n_a
        gains = [s_norm, b_norm[0]] if last_a else [a_norm[i + 1]]
        to_round = (b_w_in[0], s_w_kv, b_w_o[0]) if last_a else ()
        h, ssq, hbs, rounded = ffn(hb, ssq, h, i, gains, to_round)
        hb = hbs[0]

    hb_s, hb_b = hbs
    b_w_in0, s_w_kv_b, b_w_o0 = rounded
    n_sets = 3 * NSA_G
    per_tile = NSA_ROPE_TILE // NSA_DK
    tm_rope = 512
    rope_tab = pl.BlockSpec((tm_rope, NSA_ROPE_TILE), row)
    w_kv = s_w_kv_b.reshape(d, n_sets, NSA_DK + NSA_DV)
    w_k = w_kv[:, :, :NSA_DK].reshape(d, n_sets * NSA_DK)
    w_v = w_kv[:, :, NSA_DK:].reshape(d, n_sets * NSA_DV)
    k3 = _mm(hb_s, w_k, functools.partial(_epi_nsa_rope, d, None),
             sds((n_sets, t, NSA_DK), BF16),
             pl.BlockSpec((per_tile, tm_rope, NSA_DK), lambda i_, j, k: (j, i_, 0)),
             tm=tm_rope, tn=NSA_ROPE_TILE, tk=d, nj=n_sets // per_tile,
             extras=(ssq, cos_nsa, sin_nsa),
             extra_specs=(pl.BlockSpec((tm_rope, LANES), row), rope_tab, rope_tab), name="nsa_k")
    v3, vt3 = _mm(hb_s, w_v, functools.partial(_epi_nsa_v, d),
                  (sds((n_sets, t, NSA_DV), BF16),
                   sds((n_sets, t // VT_TILE, NSA_DV, VT_TILE), BF16)),
                  (pl.BlockSpec((per_tile, tm, NSA_DV), lambda i_, j, k: (j, i_, 0)),
                   pl.BlockSpec((per_tile, tm // VT_TILE, NSA_DV, VT_TILE),
                                lambda i_, j, k: (j, i_, 0, 0))),
                  tm=tm, tn=per_tile * NSA_DV, tk=d, nj=n_sets // per_tile,
                  extras=(ssq,), extra_specs=(ssq_spec,), name="nsa_v")

    ak = k3[:NSA_G].reshape(NSA_G, t // CMP_STRIDE, CMP_STRIDE * NSA_DK)
    av = v3[:NSA_G].reshape(NSA_G, t // CMP_STRIDE, CMP_STRIDE * NSA_DV)
    kc, vct = _compress(ak, av,
                        _compress_weights(s_cmp_w1_k, s_cmp_pe_k, s_cmp_w2_k, transpose_w2=False),
                        _compress_weights(s_cmp_w1_v, s_cmp_pe_v, s_cmp_w2_v, transpose_w2=True),
                        batch, seq)

    n_chunk = seq // CMP_STRIDE
    n_sel = seq // SEL_BLOCK
    c_start = np.arange(n_chunk) * CMP_STRIDE
    j_start = np.arange(n_sel) * SEL_BLOCK
    ovl = ((c_start[None, :] < j_start[:, None] + SEL_BLOCK)
           & (c_start[None, :] + CMP_BLOCK > j_start[:, None])
           & (np.arange(n_chunk)[None, :] < (seq - CMP_BLOCK) // CMP_STRIDE + 1))
    ovl = jnp.asarray(ovl, BF16)

    for i in range(n_b):
        n_q = heads * NSA_DK
        w_q = b_w_in0 if i == 0 else b_w_in[i].astype(BF16)
        w_o = b_w_o0 if i == 0 else b_w_o[i].astype(BF16)
        w_g = b_w_in[i][:, n_q:].reshape(d, NSA_G, 3 * hpg)
        w_g = jnp.pad(w_g, ((0, 0), (0, 0), (0, LANES - 3 * hpg))).reshape(d, NSA_G * LANES)
        w_g = w_g.astype(BF16)
        q = _mm(hb_b, w_q, functools.partial(_epi_nsa_rope, d, NSA_DK ** -0.5 * LOG2_E),
                sds((heads, t, NSA_DK), BF16),
                pl.BlockSpec((per_tile, tm_rope, NSA_DK), lambda i_, j, k: (j, i_, 0)),
                tm=tm_rope, tn=NSA_ROPE_TILE, tk=d, nj=heads // per_tile,
                extras=(ssq, cos_nsa, sin_nsa),
                extra_specs=(pl.BlockSpec((tm_rope, LANES), row), rope_tab, rope_tab),
                name="nsa_q")
        gates = _mm(hb_b, w_g, functools.partial(_epi_sigmoid, d), sds((t, NSA_G * LANES), F32),
                    pl.BlockSpec((tm, NSA_G * LANES), tile),
                    tm=tm, tn=NSA_G * LANES, tk=d, nj=1,
                    extras=(ssq,), extra_specs=(ssq_spec,), name="nsa_gates")
        o = _nsa_attention(q, kc, vct, k3, vt3, gates, ovl, batch, seq, hpg)
        h, ssq, (hb,), _ = residual_mm(o, w_o, None, h, [f_norm[n_a + i]],
                                       tm_=tm, tn=512, name="nsa_out")
        gains = [b_norm[i + 1]] if i + 1 < n_b else []
        h, ssq, hbs, _ = ffn(hb, ssq, h, n_a + i, gains)
        if hbs:
            hb_b = hbs[0]

    out = _rmsnorm(h, final_norm, F32)
    return out.reshape(batch, seq, d)
```

```python
import functools

import numpy as np
import jax
import jax.numpy as jnp
from jax import lax
from jax.experimental import pallas as pl
from jax.experimental.pallas import tpu as pltpu

F32 = jnp.float32
BF16 = jnp.bfloat16

LANES = 128
VMEM_LIMIT_BYTES = 56 * 1024 * 1024

NORM_EPS = 1e-6
ROPE_THETA = 10000.0

MLA_Q_LORA = 1536
MLA_KV_LORA = 512
MLA_NOPE = 128
MLA_ROPE = 64
MLA_V = 128

NSA_G = 4
NSA_DK = 192
NSA_DV = 128
NSA_HALF = NSA_DK // 2
NSA_ROPE_TILE = 4 * NSA_DK
CMP_BLOCK = 32
CMP_STRIDE = 16
SEL_BLOCK = 64
SEL_TOPK = 16
WINDOW = 512
FORCE_BONUS = 1e4
NEG_BIG = -1e30
LOG2_E = 1.4426950408889634

VT_TILE = LANES
NT_DIMS = (((1,), (1,)), ((), ()))


def _params(sem):
    return pltpu.CompilerParams(dimension_semantics=sem, vmem_limit_bytes=VMEM_LIMIT_BYTES)


def _rmsnorm_kernel(x_ref, g_ref, o_ref):
    x = x_ref[...]
    ms = jnp.mean(x * x, axis=-1, keepdims=True)
    o_ref[...] = (x * lax.rsqrt(ms + NORM_EPS) * g_ref[...]).astype(o_ref.dtype)


def _rmsnorm(x, g, out_dtype, tm=256):
    t, d = x.shape
    return pl.pallas_call(
        _rmsnorm_kernel,
        grid=(t // tm,),
        in_specs=[pl.BlockSpec((tm, d), lambda i: (i, 0)),
                  pl.BlockSpec((1, d), lambda i: (0, 0))],
        out_specs=pl.BlockSpec((tm, d), lambda i: (i, 0)),
        out_shape=jax.ShapeDtypeStruct((t, d), out_dtype),
        compiler_params=_params(("parallel",)),
        name="rmsnorm",
    )(x, g.reshape(1, d))


def _prenorm_kernel(x_ref, g_ref, hb_ref, ssq_ref):
    x = x_ref[...]
    hb_ref[...] = (x * g_ref[...]).astype(hb_ref.dtype)
    ssq_ref[...] = jnp.broadcast_to(jnp.sum(x * x, axis=-1, keepdims=True), ssq_ref.shape)


def _prenorm(x, g, tm=256):
    t, d = x.shape
    return pl.pallas_call(
        _prenorm_kernel,
        grid=(t // tm,),
        in_specs=[pl.BlockSpec((tm, d), lambda i: (i, 0)),
                  pl.BlockSpec((1, d), lambda i: (0, 0))],
        out_specs=(pl.BlockSpec((tm, d), lambda i: (i, 0)),
                   pl.BlockSpec((tm, LANES), lambda i: (i, 0))),
        out_shape=(jax.ShapeDtypeStruct((t, d), BF16), jax.ShapeDtypeStruct((t, LANES), F32)),
        compiler_params=_params(("parallel",)),
        name="prenorm",
    )(x, g.reshape(1, d))


def _mm_kernel(n_w, n_e, n_o, nk, epilogue, first_tile_init, *refs):
    a_ref = refs[0]
    w_refs = refs[1:1 + n_w]
    e_refs = refs[1 + n_w:1 + n_w + n_e]
    o_refs = refs[1 + n_w + n_e:1 + n_w + n_e + n_o]
    acc_refs = refs[1 + n_w + n_e + n_o:]
    j = pl.program_id(1)
    if first_tile_init is not None:
        pl.when((j == 0) & (pl.program_id(2) == 0))(lambda: first_tile_init(o_refs))

    def product(w):
        return jnp.dot(a_ref[...], w[...].astype(a_ref.dtype), preferred_element_type=F32)

    if nk == 1:
        epilogue([product(w) for w in w_refs], e_refs, o_refs, j)
        return
    k = pl.program_id(2)

    @pl.when(k == 0)
    def _():
        for w, acc in zip(w_refs, acc_refs):
            acc[...] = product(w)

    @pl.when(k > 0)
    def _():
        for w, acc in zip(w_refs, acc_refs):
            acc[...] += product(w)

    @pl.when(k == nk - 1)
    def _():
        epilogue([acc[...] for acc in acc_refs], e_refs, o_refs, j)


def _mm(a, w, epilogue, out_shape, out_specs, *, tm, tn, tk, nj, w_col_maps=None, w_layer=None,
        extras=(), extra_specs=(), a_buffers=None, first_tile_init=None, name="matmul"):
    m, kdim = a.shape
    ni, nk = m // tm, kdim // tk
    assert ni * tm == m and nk * tk == kdim
    if w_col_maps is None:
        w_col_maps = (lambda j: j,)
    n_w = len(w_col_maps)
    single = not isinstance(out_shape, (tuple, list))
    out_shapes = (out_shape,) if single else tuple(out_shape)
    out_specs_t = (out_specs,) if single else tuple(out_specs)
    a_kwargs = {} if a_buffers is None else {"pipeline_mode": pl.Buffered(a_buffers)}
    in_specs = [pl.BlockSpec((tm, tk), lambda i, j, k: (i, k), **a_kwargs)]
    for cm in w_col_maps:
        if w_layer is None:
            in_specs.append(pl.BlockSpec((tk, tn), functools.partial(
                lambda i, j, k, cm: (k, cm(j)), cm=cm)))
        else:
            in_specs.append(pl.BlockSpec((None, tk, tn), functools.partial(
                lambda i, j, k, cm: (w_layer, k, cm(j)), cm=cm)))
    in_specs += list(extra_specs)
    scratch = [pltpu.VMEM((tm, tn), F32) for _ in range(n_w)] if nk > 1 else []
    kern = functools.partial(_mm_kernel, n_w, len(extras), len(out_shapes), nk, epilogue,
                             first_tile_init)
    res = pl.pallas_call(
        kern,
        grid=(ni, nj, nk),
        in_specs=in_specs,
        out_specs=out_specs_t,
        out_shape=out_shapes,
        scratch_shapes=scratch,
        compiler_params=_params(("parallel", "arbitrary", "arbitrary")),
        name=name,
    )(a, *([w] * n_w), *extras)
    return res[0] if single else res


def _rope_pairs_64(x, cos2, sin2):
    lane = lax.broadcasted_iota(jnp.int32, x.shape, 1)
    first_half = (lane % MLA_ROPE) < (MLA_ROPE // 2)
    partner = jnp.where(first_half,
                        pltpu.roll(x, LANES - MLA_ROPE // 2, 1),
                        pltpu.roll(x, MLA_ROPE // 2, 1))
    return x * cos2 + partner * sin2


def _store_vt_tiles(vt_ref, lead, v):
    for c in range(v.shape[0] // VT_TILE):
        vt_ref[lead + (c,)] = v[c * VT_TILE:(c + 1) * VT_TILE, :].T.astype(vt_ref.dtype)


def _row_scale(acc, ssq_ref, d_model):
    rstd = lax.rsqrt(ssq_ref[...] * (1.0 / d_model) + NORM_EPS)
    return acc * jnp.concatenate([rstd] * (acc.shape[1] // LANES), axis=1)


def _epi_residual(accs, e_refs, o_refs, j):
    o_refs[0][...] = e_refs[0][...] + accs[0]


def _zero_ssq(o_refs):
    o_refs[1][...] = jnp.zeros(o_refs[1].shape, F32)


def _epi_residual_prenorm(n_gains, accs, e_refs, o_refs, j):
    h = e_refs[0][...] + accs[0]
    o_refs[0][...] = h
    ssq_ref = o_refs[1]
    ssq_ref[...] += jnp.broadcast_to(jnp.sum(h * h, axis=-1, keepdims=True), ssq_ref.shape)
    for g_ref, hb_ref in zip(e_refs[1:1 + n_gains], o_refs[2:2 + n_gains]):
        hb_ref[...] = (h * g_ref[...]).astype(hb_ref.dtype)
    for w_ref, wb_ref in zip(e_refs[1 + n_gains:], o_refs[2 + n_gains:]):
        wb_ref[...] = w_ref[...].astype(wb_ref.dtype)


def _epi_swiglu(d_model, accs, e_refs, o_refs, j):
    a = _row_scale(accs[0], e_refs[0], d_model)
    b = _row_scale(accs[1], e_refs[0], d_model)
    o_refs[0][...] = (a * jax.nn.sigmoid(a) * b).astype(o_refs[0].dtype)
    if len(o_refs) > 1:
        o_refs[1][...] = e_refs[1][...].astype(o_refs[1].dtype)


def _epi_sigmoid(d_model, accs, e_refs, o_refs, j):
    o_refs[0][...] = jax.nn.sigmoid(_row_scale(accs[0], e_refs[0], d_model))


def _epi_mla_in(d_model, accs, e_refs, o_refs, j):
    ssq_ref, gq_ref, gkv_ref, cos_ref, sin_ref = e_refs
    u = _row_scale(accs[0], ssq_ref, d_model)
    cq = u[:, :MLA_Q_LORA]
    ms = jnp.mean(cq * cq, axis=-1, keepdims=True)
    o_refs[0][...] = (cq * lax.rsqrt(ms + NORM_EPS) * gq_ref[...]).astype(BF16)
    ckv = u[:, MLA_Q_LORA:MLA_Q_LORA + MLA_KV_LORA]
    ms = jnp.mean(ckv * ckv, axis=-1, keepdims=True)
    o_refs[1][...] = (ckv * lax.rsqrt(ms + NORM_EPS) * gkv_ref[...]).astype(BF16)
    kr = u[:, MLA_Q_LORA + MLA_KV_LORA:]
    o_refs[2][...] = _rope_pairs_64(kr, cos_ref[...], sin_ref[...]).astype(BF16)


def _epi_mla_q(n_nope_tiles, scale, accs, e_refs, o_refs, j):
    cos_ref, sin_ref = e_refs
    acc = accs[0]

    @pl.when(j < n_nope_tiles)
    def _():
        o_refs[0][...] = (acc * scale).astype(BF16)

    @pl.when(j >= n_nope_tiles)
    def _():
        for c in range(acc.shape[1] // LANES):
            x = acc[:, c * LANES:(c + 1) * LANES]
            r = _rope_pairs_64(x, cos_ref[...], sin_ref[...])
            o_refs[0][:, c * LANES:(c + 1) * LANES] = (r * scale).astype(BF16)


def _epi_mla_kv(accs, e_refs, o_refs, j):
    acc = accs[0]
    kn_ref, vt_ref = o_refs
    per = MLA_NOPE + MLA_V
    for r in range(acc.shape[1] // per):
        kn_ref[:, r * MLA_NOPE:(r + 1) * MLA_NOPE] = acc[:, r * per:r * per + MLA_NOPE].astype(BF16)
        _store_vt_tiles(vt_ref, (r,), acc[:, r * per + MLA_NOPE:(r + 1) * per])


def _rope_192(x, cos, sin):
    width = x.shape[1]
    lane = lax.broadcasted_iota(jnp.int32, x.shape, 1)
    partner = jnp.where((lane % NSA_DK) < NSA_HALF,
                        pltpu.roll(x, width - NSA_HALF, 1),
                        pltpu.roll(x, NSA_HALF, 1))
    return x * cos + partner * sin


def _epi_nsa_rope(d_model, scale, accs, e_refs, o_refs, j):
    ssq_ref, cos_ref, sin_ref = e_refs
    acc = _row_scale(accs[0], ssq_ref, d_model)
    reps = acc.shape[1] // NSA_DK
    roped = _rope_192(acc, jnp.concatenate([cos_ref[...]] * reps, axis=1),
                      jnp.concatenate([sin_ref[...]] * reps, axis=1))
    if scale is not None:
        roped = roped * scale
    for r in range(acc.shape[1] // NSA_DK):
        o_refs[0][r] = roped[:, r * NSA_DK:(r + 1) * NSA_DK].astype(BF16)


def _epi_nsa_v(d_model, accs, e_refs, o_refs, j):
    v_ref, vt_ref = o_refs
    acc = _row_scale(accs[0], e_refs[0], d_model)
    for r in range(acc.shape[1] // NSA_DV):
        v = acc[:, r * NSA_DV:(r + 1) * NSA_DV]
        v_ref[r] = v.astype(BF16)
        _store_vt_tiles(vt_ref, (r,), v)


SUM_ROWS = 16


def _attend_t(q, k, vt, bias):
    dv = vt.shape[0]
    s = lax.dot_general(k, q, NT_DIMS, preferred_element_type=F32)
    bias = jnp.concatenate([bias] * (s.shape[1] // bias.shape[1]), axis=1)
    free = s.shape[0] - bias.shape[0]
    s = s + bias if free == 0 else jnp.concatenate([s[:free], s[free:] + bias], axis=0)
    p = jnp.exp2(s - jnp.max(s, axis=0, keepdims=True)).astype(BF16)
    vt1 = jnp.concatenate([vt, jnp.ones((SUM_ROWS, vt.shape[1]), BF16)], axis=0)
    acc = jnp.dot(vt1, p, preferred_element_type=F32)
    return acc[:dv] / acc[dv:dv + 1]


def _vt_tiles(vt_ref, lead, c0, n):
    return jnp.concatenate([vt_ref[lead + (c0 + c,)] for c in range(n)], axis=1)


def _mla_attn_kernel(t_blk, seq, qn_ref, qr_ref, kn_ref, kr_ref, vt_ref, o_ref):
    lane_half = lax.broadcasted_iota(jnp.int32, (t_blk, LANES), 1) // MLA_ROPE
    kpos = lax.broadcasted_iota(jnp.int32, (t_blk, t_blk), 0)
    qpos = lax.broadcasted_iota(jnp.int32, (t_blk, t_blk), 1)
    diag_bias = jnp.where(kpos <= qpos, 0.0, NEG_BIG).astype(F32)

    for qi in range(seq // t_blk):
        q0 = qi * t_blk
        n_keys = q0 + t_blk
        qr_pair = qr_ref[q0:q0 + t_blk, :]
        for hh in range(2):
            qr = jnp.where(lane_half == hh, qr_pair, jnp.zeros_like(qr_pair))
            q = jnp.concatenate([qn_ref[q0:q0 + t_blk, hh * MLA_NOPE:(hh + 1) * MLA_NOPE], qr], axis=1)
            k = jnp.concatenate([kn_ref[0:n_keys, hh * MLA_NOPE:(hh + 1) * MLA_NOPE],
                                 kr_ref[0:n_keys, :]], axis=1)
            vt = _vt_tiles(vt_ref, (hh,), 0, n_keys // VT_TILE)
            o_ref[q0:q0 + t_blk, hh * MLA_V:(hh + 1) * MLA_V] = (
                _attend_t(q, k, vt, diag_bias).T.astype(o_ref.dtype))


def _mla_attention(q, kn, kr, vt, batch, seq, heads, t_blk=512):
    t = batch * seq
    n_nope_blk = heads * MLA_NOPE // LANES
    n_vt = seq // VT_TILE
    kern = functools.partial(_mla_attn_kernel, t_blk, seq)
    return pl.pallas_call(
        kern,
        grid=(batch, heads // 2),
        in_specs=[
            pl.BlockSpec((seq, 2 * MLA_NOPE), lambda b, hp: (b, hp)),
            pl.BlockSpec((seq, LANES), lambda b, hp: (b, n_nope_blk + hp)),
            pl.BlockSpec((seq, 2 * MLA_NOPE), lambda b, hp: (b, hp)),
            pl.BlockSpec((seq, LANES), lambda b, hp: (b, 0)),
            pl.BlockSpec((2, n_vt, MLA_V, VT_TILE), lambda b, hp: (hp, b, 0, 0)),
        ],
        out_specs=pl.BlockSpec((seq, 2 * MLA_V), lambda b, hp: (b, hp)),
        out_shape=jax.ShapeDtypeStruct((t, heads * MLA_V), BF16),
        compiler_params=_params(("parallel", "parallel")),
        name="mla_attention",
    )(q, q, kn, kr, vt)


def _compress_mlp(a, w1_ref, pe_ref, w2_ref, transposed_out):
    y_lo = jnp.dot(a, w1_ref[0], preferred_element_type=F32)
    y_hi = jnp.dot(a, w1_ref[1], preferred_element_type=F32)
    pe_c = (jnp.dot(pe_ref[0], w1_ref[0], preferred_element_type=F32)
            + jnp.dot(pe_ref[1], w1_ref[1], preferred_element_type=F32))
    n = y_hi.shape[0]
    hid = y_lo + pltpu.roll(y_hi, n - 1, 0) + pe_c[0:1, :]
    hid = jax.nn.gelu(hid, approximate=True)
    if transposed_out:
        return lax.dot_general(w2_ref[...], hid.astype(BF16), NT_DIMS, preferred_element_type=F32)
    return jnp.dot(hid.astype(BF16), w2_ref[...], preferred_element_type=F32)


def _compress_kernel(ak_ref, av_ref, w1k_ref, pek_ref, w2k_ref, w1v_ref, pev_ref, w2vt_ref,
                     kc_ref, vct_ref):
    kc_ref[0] = _compress_mlp(ak_ref[0], w1k_ref, pek_ref, w2k_ref, False).astype(BF16)
    vct_ref[0] = _compress_mlp(av_ref[0], w1v_ref, pev_ref, w2vt_ref, True).astype(BF16)


def _compress(ak, av, wk, wv, batch, seq):
    n_chunk = seq // CMP_STRIDE
    const3 = lambda b, g: (0, 0, 0)
    const2 = lambda b, g: (0, 0)

    def wspecs(w):
        w1, pe, w2 = w
        return [pl.BlockSpec(w1.shape, const3), pl.BlockSpec(pe.shape, const3),
                pl.BlockSpec(w2.shape, const2)]

    return pl.pallas_call(
        _compress_kernel,
        grid=(batch, NSA_G),
        in_specs=[pl.BlockSpec((1, n_chunk, ak.shape[2]), lambda b, g: (g, b, 0)),
                  pl.BlockSpec((1, n_chunk, av.shape[2]), lambda b, g: (g, b, 0))]
        + wspecs(wk) + wspecs(wv),
        out_specs=(pl.BlockSpec((1, n_chunk, NSA_DK), lambda b, g: (b * NSA_G + g, 0, 0)),
                   pl.BlockSpec((1, NSA_DV, n_chunk), lambda b, g: (b * NSA_G + g, 0, 0))),
        out_shape=(jax.ShapeDtypeStruct((batch * NSA_G, n_chunk, NSA_DK), BF16),
                   jax.ShapeDtypeStruct((batch * NSA_G, NSA_DV, n_chunk), BF16)),
        compiler_params=_params(("parallel", "parallel")),
        name="nsa_compress",
    )(ak, av, *wk, *wv)


def _nsa_attn_kernel(hpg, tq, tk, seq, q_ref, kc_ref, vct_ref, ks_ref, vst_ref, kw_ref, vwt_ref,
                     gate_ref, ovl_ref, o_ref, bias_ref, oc_ref, os_ref):
    qi = pl.program_id(2)
    t0 = qi * tq
    n_rows = hpg * tq
    n_cmp = kc_ref.shape[1]
    n_sel = seq // SEL_BLOCK
    n_vt = tk // VT_TILE
    q = q_ref[...].reshape(n_rows, NSA_DK)

    def tile_heads(x):
        return jnp.concatenate([x] * hpg, axis=1)

    s = lax.dot_general(kc_ref[0], q, NT_DIMS, preferred_element_type=F32)
    cidx = lax.broadcasted_iota(jnp.int32, (n_cmp, tq), 0)
    tpos_c = t0 + lax.broadcasted_iota(jnp.int32, (n_cmp, tq), 1)
    mask_c = tile_heads((cidx * CMP_STRIDE + CMP_BLOCK - 1) <= tpos_c)
    s = jnp.where(mask_c, s, NEG_BIG)
    mx = jnp.max(s, axis=0, keepdims=True)
    e = jnp.where(mask_c, jnp.exp2(s - mx), 0.0)
    p = e / jnp.maximum(jnp.sum(e, axis=0, keepdims=True), 1e-30)
    oc_ref[...] = jnp.dot(vct_ref[0], p.astype(BF16), preferred_element_type=F32)

    psum = p[:, 0:tq]
    for h in range(1, hpg):
        psum = psum + p[:, h * tq:(h + 1) * tq]
    p1 = psum.astype(BF16)
    p2 = (psum - p1.astype(F32)).astype(BF16)
    p3 = (psum - p1.astype(F32) - p2.astype(F32)).astype(BF16)
    ovl = ovl_ref[...]
    imp = (jnp.dot(ovl, p1, preferred_element_type=F32)
           + jnp.dot(ovl, p2, preferred_element_type=F32)
           + jnp.dot(ovl, p3, preferred_element_type=F32))
    jrow = lax.broadcasted_iota(jnp.int32, (n_sel, tq), 0)
    tpos = t0 + lax.broadcasted_iota(jnp.int32, (n_sel, tq), 1)
    valid = jrow * SEL_BLOCK <= tpos
    cur = jnp.right_shift(tpos, SEL_BLOCK.bit_length() - 1)
    forced = valid & ((jrow == 0) | (jrow == cur) | (jrow == cur - 1))
    score = jnp.where(valid, imp + jnp.where(forced, FORCE_BONUS, 0.0), -jnp.inf)
    rank = jnp.zeros((n_sel, tq), jnp.int32)
    for i in range(n_sel):
        row = score[i:i + 1, :]
        beats = (row > score) | ((row == score) & (jrow > i))
        rank = rank + beats.astype(jnp.int32)
    sel_bias = jnp.where(valid & (rank < SEL_TOPK), 0.0, NEG_BIG).astype(F32)

    tk_s = bias_ref.shape[1]
    n_sel_tiles = (t0 + tq - 1) // tk_s + 1
    blocks_per_tile = tk_s // SEL_BLOCK
    for jj in range(seq // tk_s):
        @pl.when(jj < n_sel_tiles)
        def _():
            rows = [jnp.broadcast_to(sel_bias[b:b + 1, :], (SEL_BLOCK, tq))
                    for b in range(jj * blocks_per_tile, (jj + 1) * blocks_per_tile)]
            kpos = jj * tk_s + lax.broadcasted_iota(jnp.int32, (tk_s, tq), 0)
            tq_pos = t0 + lax.broadcasted_iota(jnp.int32, (tk_s, tq), 1)
            bias_ref[jj] = jnp.where(kpos <= tq_pos, jnp.concatenate(rows, axis=0), NEG_BIG)

    for n in range(1, seq // tk_s + 1):
        @pl.when(n_sel_tiles == n)
        def _():
            keys = n * tk_s
            vt = _vt_tiles(vst_ref, (0,), 0, keys // VT_TILE)
            os_ref[...] = _attend_t(q, ks_ref[0, 0:keys, :], vt, bias_ref[0:n].reshape(keys, tq))

    k0 = pl.multiple_of(jnp.maximum(t0 + tq - tk, 0), VT_TILE)
    kpos = k0 + lax.broadcasted_iota(jnp.int32, (tk, tq), 0)
    tq_pos = t0 + lax.broadcasted_iota(jnp.int32, (tk, tq), 1)
    ok = (kpos <= tq_pos) & (tq_pos - kpos < WINDOW)
    o_w = _attend_t(q, kw_ref[0, pl.ds(k0, tk), :], _vt_tiles(vwt_ref, (0,), k0 // VT_TILE, n_vt),
                    jnp.where(ok, 0.0, NEG_BIG).astype(F32))

    gates_t = gate_ref[...].T
    for h in range(hpg):
        cols = slice(h * tq, (h + 1) * tq)
        o = (gates_t[3 * h:3 * h + 1, :] * oc_ref[:, cols]
             + gates_t[3 * h + 1:3 * h + 2, :] * os_ref[:, cols]
             + gates_t[3 * h + 2:3 * h + 3, :] * o_w[:, cols])
        o_ref[:, h * NSA_DV:(h + 1) * NSA_DV] = o.T.astype(o_ref.dtype)


def _nsa_attention(q, kc, vct, k3, vt3, gates, ovl, batch, seq, hpg, tq=128, tk_sel=512):
    tk = WINDOW + tq
    t = batch * seq
    nq = seq // tq
    n_chunk = seq // CMP_STRIDE
    n_rows = hpg * tq
    n_vt = seq // VT_TILE
    assert tk % VT_TILE == 0 and seq >= tk and seq % tk_sel == 0
    kern = functools.partial(_nsa_attn_kernel, hpg, tq, tk, seq)

    def k_spec(branch):
        return pl.BlockSpec((1, seq, NSA_DK), lambda b, g, qi: (branch * NSA_G + g, b, 0))

    def vt_spec(branch):
        return pl.BlockSpec((1, n_vt, NSA_DV, VT_TILE), lambda b, g, qi: (branch * NSA_G + g, b, 0, 0))

    return pl.pallas_call(
        kern,
        grid=(batch, NSA_G, nq),
        in_specs=[
            pl.BlockSpec((hpg, tq, NSA_DK), lambda b, g, qi: (g, b * nq + qi, 0)),
            pl.BlockSpec((1, n_chunk, NSA_DK), lambda b, g, qi: (b * NSA_G + g, 0, 0)),
            pl.BlockSpec((1, NSA_DV, n_chunk), lambda b, g, qi: (b * NSA_G + g, 0, 0)),
            k_spec(1), vt_spec(1), k_spec(2), vt_spec(2),
            pl.BlockSpec((tq, LANES), lambda b, g, qi: (b * nq + qi, g)),
            pl.BlockSpec(ovl.shape, lambda b, g, qi: (0, 0)),
        ],
        out_specs=pl.BlockSpec((tq, hpg * NSA_DV), lambda b, g, qi: (b * nq + qi, g)),
        out_shape=jax.ShapeDtypeStruct((t, NSA_G * hpg * NSA_DV), BF16),
        scratch_shapes=[
            pltpu.VMEM((seq // tk_sel, tk_sel, tq), F32),
            pltpu.VMEM((NSA_DV, n_rows), F32),
            pltpu.VMEM((NSA_DV, n_rows), F32),
        ],
        compiler_params=_params(("parallel", "parallel", "arbitrary")),
        name="nsa_attention",
    )(q, kc, vct, k3, vt3, k3, vt3, gates, ovl)


def _rope_tables(pos_flat):
    pos = pos_flat.astype(F32)[:, None]
    inv64 = ROPE_THETA ** (-jnp.arange(0, MLA_ROPE, 2, dtype=F32) / MLA_ROPE)
    ang = pos * inv64
    c, s = jnp.cos(ang), jnp.sin(ang)
    cos_mla = jnp.concatenate([c, c, c, c], axis=1)
    sin_mla = jnp.concatenate([-s, s, -s, s], axis=1)
    inv192 = ROPE_THETA ** (-jnp.arange(0, NSA_DK, 2, dtype=F32) / NSA_DK)
    ang = pos * inv192
    c, s = jnp.cos(ang), jnp.sin(ang)
    cos_nsa = jnp.concatenate([c, c], axis=1)
    sin_nsa = jnp.concatenate([-s, s], axis=1)
    return cos_mla, sin_mla, cos_nsa, sin_nsa


def _compress_weights(w1, pe, w2, *, transpose_w2):
    width, d_hid = w1.shape[0] // CMP_BLOCK, w1.shape[1]
    hid_p = -(-d_hid // LANES) * LANES
    w1 = jnp.pad(w1, ((0, 0), (0, hid_p - d_hid)))
    w1 = w1.reshape(2, CMP_STRIDE * width, hid_p).astype(BF16)
    pe = jnp.pad(pe.reshape(2, 1, CMP_STRIDE * width), ((0, 0), (0, 7), (0, 0))).astype(BF16)
    w2 = jnp.pad(w2, ((0, hid_p - d_hid), (0, 0))).astype(BF16)
    return w1, pe, (w2.T if transpose_w2 else w2)


def kernel(x, positions, a_norm, a_w_in, a_q_norm, a_w_uq, a_kv_norm, a_w_ukv, a_w_o, s_norm, s_w_kv, s_cmp_pe_k, s_cmp_w1_k, s_cmp_w2_k, s_cmp_pe_v, s_cmp_w1_v, s_cmp_w2_v, b_norm, b_w_in, b_w_o, f_norm, f_w_in, f_w_out, final_norm):
    batch, seq, d = x.shape
    t = batch * seq
    heads = d // 128
    hpg = heads // NSA_G
    d_ff = f_w_out.shape[1]
    n_a, n_b = a_w_in.shape[0], b_w_in.shape[0]
    assert n_a >= 1 and n_b >= 1
    tm = 1024
    sds = jax.ShapeDtypeStruct

    cos_mla, sin_mla, cos_nsa, sin_nsa = _rope_tables(positions.reshape(t))
    row = lambda i, j, k: (i, 0)
    tile = lambda i, j, k: (i, j)
    tab_spec = pl.BlockSpec((tm, LANES), row)
    ssq_spec = pl.BlockSpec((tm, LANES), row)

    def residual_mm(a, w, w_layer, h, gains, *, tm_, tn, name, a_buffers=None, to_round=()):
        blk = pl.BlockSpec((tm_, tn), tile)
        nj = d // tn
        common = dict(tm=tm_, tn=tn, tk=a.shape[1], nj=nj, w_layer=w_layer, name=name,
                      a_buffers=a_buffers)
        if not gains:
            assert not to_round
            h_new = _mm(a, w, _epi_residual, sds((t, d), F32), blk,
                        extras=(h,), extra_specs=(blk,), **common)
            return h_new, None, (), ()
        steps = (t // tm_) * nj
        slab_specs = []
        for wr in to_round:
            slab = wr.shape[0] // steps
            assert slab * steps == wr.shape[0] and slab % 16 == 0
            slab_specs.append(pl.BlockSpec((slab, wr.shape[1]), lambda i, j, k: (i * nj + j, 0)))
        outs = _mm(a, w, functools.partial(_epi_residual_prenorm, len(gains)),
                   (sds((t, d), F32), sds((t, LANES), F32)) + tuple(sds((t, d), BF16) for _ in gains)
                   + tuple(sds(wr.shape, BF16) for wr in to_round),
                   (blk, pl.BlockSpec((tm_, LANES), row)) + tuple(blk for _ in gains)
                   + tuple(slab_specs),
                   extras=(h,) + tuple(g.reshape(1, d) for g in gains) + tuple(to_round),
                   extra_specs=(blk,) + tuple(pl.BlockSpec((1, tn), lambda i, j, k: (0, j))
                                              for _ in gains) + tuple(slab_specs),
                   first_tile_init=_zero_ssq, **common)
        n_g = len(gains)
        return outs[0], outs[1], tuple(outs[2:2 + n_g]), tuple(outs[2 + n_g:])

    def ffn(hb, ssq, h, layer, gains, to_round=()):
        tn = 256
        nj = d_ff // tn
        tm_in = min(2 * tm, t)
        steps = (t // tm_in) * nj
        slab = d_ff // steps
        assert slab * steps == d_ff and slab % 16 == 0
        act, w_out = _mm(
            hb, f_w_in, functools.partial(_epi_swiglu, d),
            (sds((t, d_ff), BF16), sds((d_ff, d), BF16)),
            (pl.BlockSpec((tm_in, tn), tile),
             pl.BlockSpec((slab, d), lambda i, j, k: (i * nj + j, 0))),
            tm=tm_in, tn=tn, tk=d, nj=nj, w_layer=layer, a_buffers=1,
            w_col_maps=(lambda j: j, lambda j: j + nj),
            extras=(ssq, f_w_out),
            extra_specs=(pl.BlockSpec((tm_in, LANES), row),
                         pl.BlockSpec((None, slab, d), lambda i, j, k: (layer, i * nj + j, 0))),
            name="ffn_in")
        return residual_mm(act, w_out, None, h, gains, tm_=tm, tn=tn, name="ffn_out",
                           a_buffers=1, to_round=to_round)

    h = x.reshape(t, d)
    hb, ssq = _prenorm(h, a_norm[0])

    for i in range(n_a):
        rope_lo = MLA_Q_LORA + MLA_KV_LORA
        w_in = jnp.concatenate([a_w_in[i], a_w_in[i][:, rope_lo:]], axis=1).astype(BF16)
        n_in = w_in.shape[1]
        cq, ckv, kr = _mm(
            hb, w_in, functools.partial(_epi_mla_in, d),
            (sds((t, MLA_Q_LORA), BF16), sds((t, MLA_KV_LORA), BF16), sds((t, LANES), BF16)),
            (pl.BlockSpec((tm, MLA_Q_LORA), row), pl.BlockSpec((tm, MLA_KV_LORA), row),
             pl.BlockSpec((tm, LANES), row)),
            tm=tm, tn=n_in, tk=1024, nj=1,
            extras=(ssq, a_q_norm[i].reshape(1, -1), a_kv_norm[i].reshape(1, -1), cos_mla, sin_mla),
            extra_specs=(ssq_spec,
                         pl.BlockSpec((1, MLA_Q_LORA), lambda i_, j, k: (0, 0)),
                         pl.BlockSpec((1, MLA_KV_LORA), lambda i_, j, k: (0, 0)),
                         tab_spec, tab_spec),
            name="mla_in")

        qd = MLA_NOPE + MLA_ROPE
        w_uq = a_w_uq[i].reshape(MLA_Q_LORA, heads, qd)
        w_uq = jnp.concatenate([w_uq[:, :, :MLA_NOPE].reshape(MLA_Q_LORA, heads * MLA_NOPE),
                                w_uq[:, :, MLA_NOPE:].reshape(MLA_Q_LORA, heads * MLA_ROPE)],
                               axis=1).astype(BF16)
        tn = 512
        q = _mm(cq, w_uq,
                functools.partial(_epi_mla_q, heads * MLA_NOPE // tn, qd ** -0.5 * LOG2_E),
                jax.ShapeDtypeStruct((t, heads * qd), BF16),
                pl.BlockSpec((tm, tn), lambda i_, j, k: (i_, j)),
                tm=tm, tn=tn, tk=MLA_Q_LORA, nj=heads * qd // tn,
                extras=(cos_mla, sin_mla), extra_specs=(tab_spec, tab_spec), name="mla_q")
        heads_per_tile = 4
        kn, vt = _mm(ckv, a_w_ukv[i].astype(BF16), _epi_mla_kv,
                     (jax.ShapeDtypeStruct((t, heads * MLA_NOPE), BF16),
                      jax.ShapeDtypeStruct((heads, t // VT_TILE, MLA_V, VT_TILE), BF16)),
                     (pl.BlockSpec((tm, heads_per_tile * MLA_NOPE), lambda i_, j, k: (i_, j)),
                      pl.BlockSpec((heads_per_tile, tm // VT_TILE, MLA_V, VT_TILE),
                                   lambda i_, j, k: (j, i_, 0, 0))),
                     tm=tm, tn=heads_per_tile * (MLA_NOPE + MLA_V), tk=MLA_KV_LORA,
                     nj=heads // heads_per_tile, name="mla_kv")
        o = _mla_attention(q, kn, kr, vt, batch, seq, heads)
        h, ssq, (hb,), _ = residual_mm(o, a_w_o[i].astype(BF16), None, h, [f_norm[i]],
                                       tm_=tm, tn=512, name="mla_out")
        last_a = i + 1 == n_a
        gains = [s_norm, b_norm[0]] if last_a else [a_norm[i + 1]]
        to_round = (b_w_in[0], s_w_kv, b_w_o[0]) if last_a else ()
        h, ssq, hbs, rounded = ffn(hb, ssq, h, i, gains, to_round)
        hb = hbs[0]

    hb_s, hb_b = hbs
    b_w_in0, s_w_kv_b, b_w_o0 = rounded
    n_sets = 3 * NSA_G
    per_tile = NSA_ROPE_TILE // NSA_DK
    tm_rope = tm
    rope_tab = pl.BlockSpec((tm_rope, NSA_DK), row)
    w_kv = s_w_kv_b.reshape(d, n_sets, NSA_DK + NSA_DV)
    w_k = w_kv[:, :, :NSA_DK].reshape(d, n_sets * NSA_DK)
    w_v = w_kv[:, :, NSA_DK:].reshape(d, n_sets * NSA_DV)
    k3 = _mm(hb_s, w_k, functools.partial(_epi_nsa_rope, d, None),
             sds((n_sets, t, NSA_DK), BF16),
             pl.BlockSpec((per_tile, tm_rope, NSA_DK), lambda i_, j, k: (j, i_, 0)),
             tm=tm_rope, tn=NSA_ROPE_TILE, tk=d, nj=n_sets // per_tile,
             extras=(ssq, cos_nsa, sin_nsa),
             extra_specs=(pl.BlockSpec((tm_rope, LANES), row), rope_tab, rope_tab),
             a_buffers=1, name="nsa_k")
    v3, vt3 = _mm(hb_s, w_v, functools.partial(_epi_nsa_v, d),
                  (sds((n_sets, t, NSA_DV), BF16),
                   sds((n_sets, t // VT_TILE, NSA_DV, VT_TILE), BF16)),
                  (pl.BlockSpec((per_tile, tm, NSA_DV), lambda i_, j, k: (j, i_, 0)),
                   pl.BlockSpec((per_tile, tm // VT_TILE, NSA_DV, VT_TILE),
                                lambda i_, j, k: (j, i_, 0, 0))),
                  tm=tm, tn=per_tile * NSA_DV, tk=d, nj=n_sets // per_tile,
                  extras=(ssq,), extra_specs=(ssq_spec,), name="nsa_v")

    ak = k3[:NSA_G].reshape(NSA_G, t // CMP_STRIDE, CMP_STRIDE * NSA_DK)
    av = v3[:NSA_G].reshape(NSA_G, t // CMP_STRIDE, CMP_STRIDE * NSA_DV)
    kc, vct = _compress(ak, av,
                        _compress_weights(s_cmp_w1_k, s_cmp_pe_k, s_cmp_w2_k, transpose_w2=False),
                        _compress_weights(s_cmp_w1_v, s_cmp_pe_v, s_cmp_w2_v, transpose_w2=True),
                        batch, seq)

    n_chunk = seq // CMP_STRIDE
    n_sel = seq // SEL_BLOCK
    c_start = np.arange(n_chunk) * CMP_STRIDE
    j_start = np.arange(n_sel) * SEL_BLOCK
    ovl = ((c_start[None, :] < j_start[:, None] + SEL_BLOCK)
           & (c_start[None, :] + CMP_BLOCK > j_start[:, None])
           & (np.arange(n_chunk)[None, :] < (seq - CMP_BLOCK) // CMP_STRIDE + 1))
    ovl = jnp.asarray(ovl, BF16)

    for i in range(n_b):
        n_q = heads * NSA_DK
        w_q = b_w_in0 if i == 0 else b_w_in[i].astype(BF16)
        w_o = b_w_o0 if i == 0 else b_w_o[i].astype(BF16)
        w_g = b_w_in[i][:, n_q:].reshape(d, NSA_G, 3 * hpg)
        w_g = jnp.pad(w_g, ((0, 0), (0, 0), (0, LANES - 3 * hpg))).reshape(d, NSA_G * LANES)
        w_g = w_g.astype(BF16)
        q = _mm(hb_b, w_q, functools.partial(_epi_nsa_rope, d, NSA_DK ** -0.5 * LOG2_E),
                sds((heads, t, NSA_DK), BF16),
                pl.BlockSpec((per_tile, tm_rope, NSA_DK), lambda i_, j, k: (j, i_, 0)),
                tm=tm_rope, tn=NSA_ROPE_TILE, tk=d, nj=heads // per_tile,
                extras=(ssq, cos_nsa, sin_nsa),
                extra_specs=(pl.BlockSpec((tm_rope, LANES), row), rope_tab, rope_tab),
                a_buffers=1,
                name="nsa_q")
        gates = _mm(hb_b, w_g, functools.partial(_epi_sigmoid, d), sds((t, NSA_G * LANES), F32),
                    pl.BlockSpec((tm, NSA_G * LANES), tile),
                    tm=tm, tn=NSA_G * LANES, tk=d, nj=1,
                    extras=(ssq,), extra_specs=(ssq_spec,), name="nsa_gates")
        o = _nsa_attention(q, kc, vct, k3, vt3, gates, ovl, batch, seq, hpg)
        h, ssq, (hb,), _ = residual_mm(o, w_o, None, h, [f_norm[n_a + i]],
                                       tm_=tm, tn=512, name="nsa_out")
        gains = [b_norm[i + 1]] if i + 1 < n_b else []
        h, ssq, hbs, _ = ffn(hb, ssq, h, n_a + i, gains)
        if hbs:
            hb_b = hbs[0]

    out = _rmsnorm(h, final_norm, F32)
    return out.reshape(batch, seq, d)
```

```python
import functools

import numpy as np
import jax
import jax.numpy as jnp
from jax import lax
from jax.experimental import pallas as pl
from jax.experimental.pallas import tpu as pltpu

F32 = jnp.float32
BF16 = jnp.bfloat16

LANES = 128
VMEM_LIMIT_BYTES = 56 * 1024 * 1024

NORM_EPS = 1e-6
ROPE_THETA = 10000.0

MLA_Q_LORA = 1536
MLA_KV_LORA = 512
MLA_NOPE = 128
MLA_ROPE = 64
MLA_V = 128

NSA_G = 4
NSA_DK = 192
NSA_DV = 128
NSA_HALF = NSA_DK // 2
NSA_ROPE_TILE = 4 * NSA_DK
CMP_BLOCK = 32
CMP_STRIDE = 16
SEL_BLOCK = 64
SEL_TOPK = 16
WINDOW = 512
FORCE_BONUS = 1e4
NEG_BIG = -1e30
LOG2_E = 1.4426950408889634

VT_TILE = LANES
NT_DIMS = (((1,), (1,)), ((), ()))


def _params(sem):
    return pltpu.CompilerParams(dimension_semantics=sem, vmem_limit_bytes=VMEM_LIMIT_BYTES)


def _rmsnorm_kernel(x_ref, g_ref, o_ref):
    x = x_ref[...]
    ms = jnp.mean(x * x, axis=-1, keepdims=True)
    o_ref[...] = (x * lax.rsqrt(ms + NORM_EPS) * g_ref[...]).astype(o_ref.dtype)


def _rmsnorm(x, g, out_dtype, tm=256):
    t, d = x.shape
    return pl.pallas_call(
        _rmsnorm_kernel,
        grid=(t // tm,),
        in_specs=[pl.BlockSpec((tm, d), lambda i: (i, 0)),
                  pl.BlockSpec((1, d), lambda i: (0, 0))],
        out_specs=pl.BlockSpec((tm, d), lambda i: (i, 0)),
        out_shape=jax.ShapeDtypeStruct((t, d), out_dtype),
        compiler_params=_params(("parallel",)),
        name="rmsnorm",
    )(x, g.reshape(1, d))


def _prenorm_kernel(x_ref, g_ref, hb_ref, ssq_ref):
    x = x_ref[...]
    hb_ref[...] = (x * g_ref[...]).astype(hb_ref.dtype)
    ssq_ref[...] = jnp.broadcast_to(jnp.sum(x * x, axis=-1, keepdims=True), ssq_ref.shape)


def _prenorm(x, g, tm=256):
    t, d = x.shape
    return pl.pallas_call(
        _prenorm_kernel,
        grid=(t // tm,),
        in_specs=[pl.BlockSpec((tm, d), lambda i: (i, 0)),
                  pl.BlockSpec((1, d), lambda i: (0, 0))],
        out_specs=(pl.BlockSpec((tm, d), lambda i: (i, 0)),
                   pl.BlockSpec((tm, LANES), lambda i: (i, 0))),
        out_shape=(jax.ShapeDtypeStruct((t, d), BF16), jax.ShapeDtypeStruct((t, LANES), F32)),
        compiler_params=_params(("parallel",)),
        name="prenorm",
    )(x, g.reshape(1, d))


def _mm_kernel(n_w, n_e, n_o, nk, epilogue, first_tile_init, *refs):
    a_ref = refs[0]
    w_refs = refs[1:1 + n_w]
    e_refs = refs[1 + n_w:1 + n_w + n_e]
    o_refs = refs[1 + n_w + n_e:1 + n_w + n_e + n_o]
    acc_refs = refs[1 + n_w + n_e + n_o:]
    j = pl.program_id(1)
    if first_tile_init is not None:
        pl.when((j == 0) & (pl.program_id(2) == 0))(lambda: first_tile_init(o_refs))

    def product(w):
        return jnp.dot(a_ref[...], w[...].astype(a_ref.dtype), preferred_element_type=F32)

    if nk == 1:
        epilogue([product(w) for w in w_refs], e_refs, o_refs, j)
        return
    k = pl.program_id(2)

    @pl.when(k == 0)
    def _():
        for w, acc in zip(w_refs, acc_refs):
            acc[...] = product(w)

    @pl.when(k > 0)
    def _():
        for w, acc in zip(w_refs, acc_refs):
            acc[...] += product(w)

    @pl.when(k == nk - 1)
    def _():
        epilogue([acc[...] for acc in acc_refs], e_refs, o_refs, j)


def _mm(a, w, epilogue, out_shape, out_specs, *, tm, tn, tk, nj, w_col_maps=None, w_layer=None,
        extras=(), extra_specs=(), a_buffers=None, first_tile_init=None, name="matmul"):
    m, kdim = a.shape
    ni, nk = m // tm, kdim // tk
    assert ni * tm == m and nk * tk == kdim
    if w_col_maps is None:
        w_col_maps = (lambda j: j,)
    n_w = len(w_col_maps)
    single = not isinstance(out_shape, (tuple, list))
    out_shapes = (out_shape,) if single else tuple(out_shape)
    out_specs_t = (out_specs,) if single else tuple(out_specs)
    a_kwargs = {} if a_buffers is None else {"pipeline_mode": pl.Buffered(a_buffers)}
    in_specs = [pl.BlockSpec((tm, tk), lambda i, j, k: (i, k), **a_kwargs)]
    for cm in w_col_maps:
        if w_layer is None:
            in_specs.append(pl.BlockSpec((tk, tn), functools.partial(
                lambda i, j, k, cm: (k, cm(j)), cm=cm)))
        else:
            in_specs.append(pl.BlockSpec((None, tk, tn), functools.partial(
                lambda i, j, k, cm: (w_layer, k, cm(j)), cm=cm)))
    in_specs += list(extra_specs)
    scratch = [pltpu.VMEM((tm, tn), F32) for _ in range(n_w)] if nk > 1 else []
    kern = functools.partial(_mm_kernel, n_w, len(extras), len(out_shapes), nk, epilogue,
                             first_tile_init)
    res = pl.pallas_call(
        kern,
        grid=(ni, nj, nk),
        in_specs=in_specs,
        out_specs=out_specs_t,
        out_shape=out_shapes,
        scratch_shapes=scratch,
        compiler_params=_params(("parallel", "arbitrary", "arbitrary")),
        name=name,
    )(a, *([w] * n_w), *extras)
    return res[0] if single else res


def _rope_pairs_64(x, cos2, sin2):
    lane = lax.broadcasted_iota(jnp.int32, x.shape, 1)
    first_half = (lane % MLA_ROPE) < (MLA_ROPE // 2)
    partner = jnp.where(first_half,
                        pltpu.roll(x, LANES - MLA_ROPE // 2, 1),
                        pltpu.roll(x, MLA_ROPE // 2, 1))
    return x * cos2 + partner * sin2


def _store_vt_tiles(vt_ref, lead, v):
    for c in range(v.shape[0] // VT_TILE):
        vt_ref[lead + (c,)] = v[c * VT_TILE:(c + 1) * VT_TILE, :].T.astype(vt_ref.dtype)


def _row_scale(acc, ssq_ref, d_model):
    rstd = lax.rsqrt(ssq_ref[...] * (1.0 / d_model) + NORM_EPS)
    return acc * jnp.concatenate([rstd] * (acc.shape[1] // LANES), axis=1)


def _epi_residual(accs, e_refs, o_refs, j):
    o_refs[0][...] = e_refs[0][...] + accs[0]


def _zero_ssq(o_refs):
    o_refs[1][...] = jnp.zeros(o_refs[1].shape, F32)


def _epi_residual_prenorm(n_gains, accs, e_refs, o_refs, j):
    h = e_refs[0][...] + accs[0]
    o_refs[0][...] = h
    ssq_ref = o_refs[1]
    ssq_ref[...] += jnp.broadcast_to(jnp.sum(h * h, axis=-1, keepdims=True), ssq_ref.shape)
    for g_ref, hb_ref in zip(e_refs[1:1 + n_gains], o_refs[2:2 + n_gains]):
        hb_ref[...] = (h * g_ref[...]).astype(hb_ref.dtype)
    for w_ref, wb_ref in zip(e_refs[1 + n_gains:], o_refs[2 + n_gains:]):
        wb_ref[...] = w_ref[...].astype(wb_ref.dtype)


def _epi_swiglu(d_model, accs, e_refs, o_refs, j):
    a = _row_scale(accs[0], e_refs[0], d_model)
    b = _row_scale(accs[1], e_refs[0], d_model)
    o_refs[0][...] = (a * jax.nn.sigmoid(a) * b).astype(o_refs[0].dtype)
    if len(o_refs) > 1:
        o_refs[1][...] = e_refs[1][...].astype(o_refs[1].dtype)


def _epi_sigmoid(d_model, accs, e_refs, o_refs, j):
    o_refs[0][...] = jax.nn.sigmoid(_row_scale(accs[0], e_refs[0], d_model))


def _epi_mla_in(d_model, accs, e_refs, o_refs, j):
    ssq_ref, gq_ref, gkv_ref, cos_ref, sin_ref = e_refs
    u = _row_scale(accs[0], ssq_ref, d_model)
    cq = u[:, :MLA_Q_LORA]
    ms = jnp.mean(cq * cq, axis=-1, keepdims=True)
    o_refs[0][...] = (cq * lax.rsqrt(ms + NORM_EPS) * gq_ref[...]).astype(BF16)
    ckv = u[:, MLA_Q_LORA:MLA_Q_LORA + MLA_KV_LORA]
    ms = jnp.mean(ckv * ckv, axis=-1, keepdims=True)
    o_refs[1][...] = (ckv * lax.rsqrt(ms + NORM_EPS) * gkv_ref[...]).astype(BF16)
    kr = u[:, MLA_Q_LORA + MLA_KV_LORA:]
    o_refs[2][...] = _rope_pairs_64(kr, cos_ref[...], sin_ref[...]).astype(BF16)


def _epi_mla_q(n_nope_tiles, scale, accs, e_refs, o_refs, j):
    cos_ref, sin_ref = e_refs
    acc = accs[0]

    @pl.when(j < n_nope_tiles)
    def _():
        o_refs[0][...] = (acc * scale).astype(BF16)

    @pl.when(j >= n_nope_tiles)
    def _():
        for c in range(acc.shape[1] // LANES):
            x = acc[:, c * LANES:(c + 1) * LANES]
            r = _rope_pairs_64(x, cos_ref[...], sin_ref[...])
            o_refs[0][:, c * LANES:(c + 1) * LANES] = (r * scale).astype(BF16)


def _epi_mla_kv(accs, e_refs, o_refs, j):
    acc = accs[0]
    kn_ref, vt_ref = o_refs
    per = MLA_NOPE + MLA_V
    for r in range(acc.shape[1] // per):
        kn_ref[:, r * MLA_NOPE:(r + 1) * MLA_NOPE] = acc[:, r * per:r * per + MLA_NOPE].astype(BF16)
        _store_vt_tiles(vt_ref, (r,), acc[:, r * per + MLA_NOPE:(r + 1) * per])


def _rope_192(x, cos, sin):
    width = x.shape[1]
    lane = lax.broadcasted_iota(jnp.int32, x.shape, 1)
    partner = jnp.where((lane % NSA_DK) < NSA_HALF,
                        pltpu.roll(x, width - NSA_HALF, 1),
                        pltpu.roll(x, NSA_HALF, 1))
    return x * cos + partner * sin


def _epi_nsa_rope(d_model, scale, accs, e_refs, o_refs, j):
    ssq_ref, cos_ref, sin_ref = e_refs
    acc = _row_scale(accs[0], ssq_ref, d_model)
    reps = acc.shape[1] // NSA_DK
    roped = _rope_192(acc, jnp.concatenate([cos_ref[...]] * reps, axis=1),
                      jnp.concatenate([sin_ref[...]] * reps, axis=1))
    if scale is not None:
        roped = roped * scale
    for r in range(acc.shape[1] // NSA_DK):
        o_refs[0][r] = roped[:, r * NSA_DK:(r + 1) * NSA_DK].astype(BF16)


def _epi_nsa_v(d_model, accs, e_refs, o_refs, j):
    v_ref, vt_ref = o_refs
    acc = _row_scale(accs[0], e_refs[0], d_model)
    for r in range(acc.shape[1] // NSA_DV):
        v = acc[:, r * NSA_DV:(r + 1) * NSA_DV]
        v_ref[r] = v.astype(BF16)
        _store_vt_tiles(vt_ref, (r,), v)


SUM_ROWS = 16


def _attend_t(q, k, vt, bias):
    dv = vt.shape[0]
    s = lax.dot_general(k, q, NT_DIMS, preferred_element_type=F32)
    bias = jnp.concatenate([bias] * (s.shape[1] // bias.shape[1]), axis=1)
    free = s.shape[0] - bias.shape[0]
    s = s + bias if free == 0 else jnp.concatenate([s[:free], s[free:] + bias], axis=0)
    p = jnp.exp2(s - jnp.max(s, axis=0, keepdims=True)).astype(BF16)
    vt1 = jnp.concatenate([vt, jnp.ones((SUM_ROWS, vt.shape[1]), BF16)], axis=0)
    acc = jnp.dot(vt1, p, preferred_element_type=F32)
    return acc[:dv] / acc[dv:dv + 1]


def _vt_tiles(vt_ref, lead, c0, n):
    return jnp.concatenate([vt_ref[lead + (c0 + c,)] for c in range(n)], axis=1)


def _mla_attn_kernel(t_blk, seq, qn_ref, qr_ref, kn_ref, kr_ref, vt_ref, o_ref):
    lane_half = lax.broadcasted_iota(jnp.int32, (t_blk, LANES), 1) // MLA_ROPE
    kpos = lax.broadcasted_iota(jnp.int32, (t_blk, t_blk), 0)
    qpos = lax.broadcasted_iota(jnp.int32, (t_blk, t_blk), 1)
    diag_bias = jnp.where(kpos <= qpos, 0.0, NEG_BIG).astype(F32)

    for qi in range(seq // t_blk):
        q0 = qi * t_blk
        n_keys = q0 + t_blk
        qr_pair = qr_ref[q0:q0 + t_blk, :]
        for hh in range(2):
            qr = jnp.where(lane_half == hh, qr_pair, jnp.zeros_like(qr_pair))
            q = jnp.concatenate([qn_ref[q0:q0 + t_blk, hh * MLA_NOPE:(hh + 1) * MLA_NOPE], qr], axis=1)
            k = jnp.concatenate([kn_ref[0:n_keys, hh * MLA_NOPE:(hh + 1) * MLA_NOPE],
                                 kr_ref[0:n_keys, :]], axis=1)
            vt = _vt_tiles(vt_ref, (hh,), 0, n_keys // VT_TILE)
            o_ref[q0:q0 + t_blk, hh * MLA_V:(hh + 1) * MLA_V] = (
                _attend_t(q, k, vt, diag_bias).T.astype(o_ref.dtype))


def _mla_attention(q, kn, kr, vt, batch, seq, heads, t_blk=512):
    t = batch * seq
    n_nope_blk = heads * MLA_NOPE // LANES
    n_vt = seq // VT_TILE
    kern = functools.partial(_mla_attn_kernel, t_blk, seq)
    return pl.pallas_call(
        kern,
        grid=(batch, heads // 2),
        in_specs=[
            pl.BlockSpec((seq, 2 * MLA_NOPE), lambda b, hp: (b, hp)),
            pl.BlockSpec((seq, LANES), lambda b, hp: (b, n_nope_blk + hp)),
            pl.BlockSpec((seq, 2 * MLA_NOPE), lambda b, hp: (b, hp)),
            pl.BlockSpec((seq, LANES), lambda b, hp: (b, 0)),
            pl.BlockSpec((2, n_vt, MLA_V, VT_TILE), lambda b, hp: (hp, b, 0, 0)),
        ],
        out_specs=pl.BlockSpec((seq, 2 * MLA_V), lambda b, hp: (b, hp)),
        out_shape=jax.ShapeDtypeStruct((t, heads * MLA_V), BF16),
        compiler_params=_params(("parallel", "parallel")),
        name="mla_attention",
    )(q, q, kn, kr, vt)


def _compress_mlp(a, w1_ref, pe_ref, w2_ref, transposed_out):
    y_lo = jnp.dot(a, w1_ref[0], preferred_element_type=F32)
    y_hi = jnp.dot(a, w1_ref[1], preferred_element_type=F32)
    pe_c = (jnp.dot(pe_ref[0], w1_ref[0], preferred_element_type=F32)
            + jnp.dot(pe_ref[1], w1_ref[1], preferred_element_type=F32))
    n = y_hi.shape[0]
    hid = y_lo + pltpu.roll(y_hi, n - 1, 0) + pe_c[0:1, :]
    hid = jax.nn.gelu(hid, approximate=True)
    if transposed_out:
        return lax.dot_general(w2_ref[...], hid.astype(BF16), NT_DIMS, preferred_element_type=F32)
    return jnp.dot(hid.astype(BF16), w2_ref[...], preferred_element_type=F32)


def _compress_kernel(ak_ref, av_ref, w1k_ref, pek_ref, w2k_ref, w1v_ref, pev_ref, w2vt_ref,
                     kc_ref, vct_ref):
    kc_ref[0] = _compress_mlp(ak_ref[0], w1k_ref, pek_ref, w2k_ref, False).astype(BF16)
    vct_ref[0] = _compress_mlp(av_ref[0], w1v_ref, pev_ref, w2vt_ref, True).astype(BF16)


def _compress(ak, av, wk, wv, batch, seq):
    n_chunk = seq // CMP_STRIDE
    const3 = lambda b, g: (0, 0, 0)
    const2 = lambda b, g: (0, 0)

    def wspecs(w):
        w1, pe, w2 = w
        return [pl.BlockSpec(w1.shape, const3), pl.BlockSpec(pe.shape, const3),
                pl.BlockSpec(w2.shape, const2)]

    return pl.pallas_call(
        _compress_kernel,
        grid=(batch, NSA_G),
        in_specs=[pl.BlockSpec((1, n_chunk, ak.shape[2]), lambda b, g: (g, b, 0)),
                  pl.BlockSpec((1, n_chunk, av.shape[2]), lambda b, g: (g, b, 0))]
        + wspecs(wk) + wspecs(wv),
        out_specs=(pl.BlockSpec((1, n_chunk, NSA_DK), lambda b, g: (b * NSA_G + g, 0, 0)),
                   pl.BlockSpec((1, NSA_DV, n_chunk), lambda b, g: (b * NSA_G + g, 0, 0))),
        out_shape=(jax.ShapeDtypeStruct((batch * NSA_G, n_chunk, NSA_DK), BF16),
                   jax.ShapeDtypeStruct((batch * NSA_G, NSA_DV, n_chunk), BF16)),
        compiler_params=_params(("parallel", "parallel")),
        name="nsa_compress",
    )(ak, av, *wk, *wv)


def _nsa_attn_kernel(hpg, tq, tk, seq, q_ref, kc_ref, vct_ref, ks_ref, vst_ref, kw_ref, vwt_ref,
                     gate_ref, ovl_ref, o_ref, bias_ref, oc_ref, os_ref, gt_ref):
    qi = pl.program_id(2)
    t0 = qi * tq
    n_rows = hpg * tq
    n_cmp = kc_ref.shape[1]
    n_sel = seq // SEL_BLOCK
    n_vt = tk // VT_TILE
    q = q_ref[...].reshape(n_rows, NSA_DK)

    def tile_heads(x):
        return jnp.concatenate([x] * hpg, axis=1)

    s = lax.dot_general(kc_ref[0], q, NT_DIMS, preferred_element_type=F32)
    cidx = lax.broadcasted_iota(jnp.int32, (n_cmp, tq), 0)
    tpos_c = t0 + lax.broadcasted_iota(jnp.int32, (n_cmp, tq), 1)
    mask_c = tile_heads((cidx * CMP_STRIDE + CMP_BLOCK - 1) <= tpos_c)
    s = jnp.where(mask_c, s, NEG_BIG)
    mx = jnp.max(s, axis=0, keepdims=True)
    e = jnp.where(mask_c, jnp.exp2(s - mx), 0.0)
    p = e / jnp.maximum(jnp.sum(e, axis=0, keepdims=True), 1e-30)
    oc_ref[...] = jnp.dot(vct_ref[0], p.astype(BF16), preferred_element_type=F32)

    psum = p[:, 0:tq]
    for h in range(1, hpg):
        psum = psum + p[:, h * tq:(h + 1) * tq]
    p1 = psum.astype(BF16)
    p2 = (psum - p1.astype(F32)).astype(BF16)
    p3 = (psum - p1.astype(F32) - p2.astype(F32)).astype(BF16)
    ovl = ovl_ref[...]
    imp = (jnp.dot(ovl, p1, preferred_element_type=F32)
           + jnp.dot(ovl, p2, preferred_element_type=F32)
           + jnp.dot(ovl, p3, preferred_element_type=F32))
    jrow = lax.broadcasted_iota(jnp.int32, (n_sel, tq), 0)
    tpos = t0 + lax.broadcasted_iota(jnp.int32, (n_sel, tq), 1)
    valid = jrow * SEL_BLOCK <= tpos
    cur = jnp.right_shift(tpos, SEL_BLOCK.bit_length() - 1)
    forced = valid & ((jrow == 0) | (jrow == cur) | (jrow == cur - 1))
    score = jnp.where(valid, imp + jnp.where(forced, FORCE_BONUS, 0.0), -jnp.inf)
    rank = jnp.zeros((n_sel, tq), jnp.int32)
    for i in range(n_sel):
        row = score[i:i + 1, :]
        beats = (row > score) | ((row == score) & (jrow > i))
        rank = rank + beats.astype(jnp.int32)
    sel_bias = jnp.where(valid & (rank < SEL_TOPK), 0.0, NEG_BIG).astype(F32)

    tk_s = bias_ref.shape[1]
    n_sel_tiles = (t0 + tq - 1) // tk_s + 1
    blocks_per_tile = tk_s // SEL_BLOCK
    for jj in range(seq // tk_s):
        @pl.when(jj < n_sel_tiles)
        def _():
            rows = [jnp.broadcast_to(sel_bias[b:b + 1, :], (SEL_BLOCK, tq))
                    for b in range(jj * blocks_per_tile, (jj + 1) * blocks_per_tile)]
            kpos = jj * tk_s + lax.broadcasted_iota(jnp.int32, (tk_s, tq), 0)
            tq_pos = t0 + lax.broadcasted_iota(jnp.int32, (tk_s, tq), 1)
            bias_ref[jj] = jnp.where(kpos <= tq_pos, jnp.concatenate(rows, axis=0), NEG_BIG)

    for n in range(1, seq // tk_s + 1):
        @pl.when(n_sel_tiles == n)
        def _():
            keys = n * tk_s
            vt = _vt_tiles(vst_ref, (0,), 0, keys // VT_TILE)
            os_ref[...] = _attend_t(q, ks_ref[0, 0:keys, :], vt, bias_ref[0:n].reshape(keys, tq))

    k0 = pl.multiple_of(jnp.maximum(t0 + tq - tk, 0), VT_TILE)
    kpos = k0 + lax.broadcasted_iota(jnp.int32, (tk, tq), 0)
    tq_pos = t0 + lax.broadcasted_iota(jnp.int32, (tk, tq), 1)
    ok = (kpos <= tq_pos) & (tq_pos - kpos < WINDOW)
    o_w = _attend_t(q, kw_ref[0, pl.ds(k0, tk), :], _vt_tiles(vwt_ref, (0,), k0 // VT_TILE, n_vt),
                    jnp.where(ok, 0.0, NEG_BIG).astype(F32))

    gt_ref[...] = gate_ref[...].T
    base = pl.program_id(1) * (3 * hpg)
    for h in range(hpg):
        cols = slice(h * tq, (h + 1) * tq)
        o = (gt_ref[pl.ds(base + 3 * h, 1), :] * oc_ref[:, cols]
             + gt_ref[pl.ds(base + 3 * h + 1, 1), :] * os_ref[:, cols]
             + gt_ref[pl.ds(base + 3 * h + 2, 1), :] * o_w[:, cols])
        o_ref[:, h * NSA_DV:(h + 1) * NSA_DV] = o.T.astype(o_ref.dtype)


def _nsa_attention(q, kc, vct, k3, vt3, gates, ovl, batch, seq, hpg, tq=128, tk_sel=512):
    tk = WINDOW + tq
    t = batch * seq
    nq = seq // tq
    n_chunk = seq // CMP_STRIDE
    n_rows = hpg * tq
    n_vt = seq // VT_TILE
    assert tk % VT_TILE == 0 and seq >= tk and seq % tk_sel == 0
    kern = functools.partial(_nsa_attn_kernel, hpg, tq, tk, seq)

    def k_spec(branch):
        return pl.BlockSpec((1, seq, NSA_DK), lambda b, g, qi: (branch * NSA_G + g, b, 0))

    def vt_spec(branch):
        return pl.BlockSpec((1, n_vt, NSA_DV, VT_TILE), lambda b, g, qi: (branch * NSA_G + g, b, 0, 0))

    return pl.pallas_call(
        kern,
        grid=(batch, NSA_G, nq),
        in_specs=[
            pl.BlockSpec((hpg, tq, NSA_DK), lambda b, g, qi: (g, b * nq + qi, 0)),
            pl.BlockSpec((1, n_chunk, NSA_DK), lambda b, g, qi: (b * NSA_G + g, 0, 0)),
            pl.BlockSpec((1, NSA_DV, n_chunk), lambda b, g, qi: (b * NSA_G + g, 0, 0)),
            k_spec(1), vt_spec(1), k_spec(2), vt_spec(2),
            pl.BlockSpec((tq, LANES), lambda b, g, qi: (b * nq + qi, 0)),
            pl.BlockSpec(ovl.shape, lambda b, g, qi: (0, 0)),
        ],
        out_specs=pl.BlockSpec((tq, hpg * NSA_DV), lambda b, g, qi: (b * nq + qi, g)),
        out_shape=jax.ShapeDtypeStruct((t, NSA_G * hpg * NSA_DV), BF16),
        scratch_shapes=[
            pltpu.VMEM((seq // tk_sel, tk_sel, tq), F32),
            pltpu.VMEM((NSA_DV, n_rows), F32),
            pltpu.VMEM((NSA_DV, n_rows), F32),
            pltpu.VMEM((LANES, tq), F32),
        ],
        compiler_params=_params(("parallel", "parallel", "arbitrary")),
        name="nsa_attention",
    )(q, kc, vct, k3, vt3, k3, vt3, gates, ovl)


def _rope_tables(pos_flat):
    pos = pos_flat.astype(F32)[:, None]
    inv64 = ROPE_THETA ** (-jnp.arange(0, MLA_ROPE, 2, dtype=F32) / MLA_ROPE)
    ang = pos * inv64
    c, s = jnp.cos(ang), jnp.sin(ang)
    cos_mla = jnp.concatenate([c, c, c, c], axis=1)
    sin_mla = jnp.concatenate([-s, s, -s, s], axis=1)
    inv192 = ROPE_THETA ** (-jnp.arange(0, NSA_DK, 2, dtype=F32) / NSA_DK)
    ang = pos * inv192
    c, s = jnp.cos(ang), jnp.sin(ang)
    cos_nsa = jnp.concatenate([c, c], axis=1)
    sin_nsa = jnp.concatenate([-s, s], axis=1)
    return cos_mla, sin_mla, cos_nsa, sin_nsa


def _compress_weights(w1, pe, w2, *, transpose_w2):
    width, d_hid = w1.shape[0] // CMP_BLOCK, w1.shape[1]
    hid_p = -(-d_hid // LANES) * LANES
    w1 = jnp.pad(w1, ((0, 0), (0, hid_p - d_hid)))
    w1 = w1.reshape(2, CMP_STRIDE * width, hid_p).astype(BF16)
    pe = jnp.pad(pe.reshape(2, 1, CMP_STRIDE * width), ((0, 0), (0, 7), (0, 0))).astype(BF16)
    w2 = jnp.pad(w2, ((0, hid_p - d_hid), (0, 0))).astype(BF16)
    return w1, pe, (w2.T if transpose_w2 else w2)


def kernel(x, positions, a_norm, a_w_in, a_q_norm, a_w_uq, a_kv_norm, a_w_ukv, a_w_o, s_norm, s_w_kv, s_cmp_pe_k, s_cmp_w1_k, s_cmp_w2_k, s_cmp_pe_v, s_cmp_w1_v, s_cmp_w2_v, b_norm, b_w_in, b_w_o, f_norm, f_w_in, f_w_out, final_norm):
    batch, seq, d = x.shape
    t = batch * seq
    heads = d // 128
    hpg = heads // NSA_G
    d_ff = f_w_out.shape[1]
    n_a, n_b = a_w_in.shape[0], b_w_in.shape[0]
    assert n_a >= 1 and n_b >= 1
    tm = 1024
    sds = jax.ShapeDtypeStruct

    cos_mla, sin_mla, cos_nsa, sin_nsa = _rope_tables(positions.reshape(t))
    row = lambda i, j, k: (i, 0)
    tile = lambda i, j, k: (i, j)
    tab_spec = pl.BlockSpec((tm, LANES), row)
    ssq_spec = pl.BlockSpec((tm, LANES), row)

    def residual_mm(a, w, w_layer, h, gains, *, tm_, tn, name, a_buffers=None, to_round=()):
        blk = pl.BlockSpec((tm_, tn), tile)
        nj = d // tn
        common = dict(tm=tm_, tn=tn, tk=a.shape[1], nj=nj, w_layer=w_layer, name=name,
                      a_buffers=a_buffers)
        if not gains:
            assert not to_round
            h_new = _mm(a, w, _epi_residual, sds((t, d), F32), blk,
                        extras=(h,), extra_specs=(blk,), **common)
            return h_new, None, (), ()
        steps = (t // tm_) * nj
        slab_specs = []
        for wr in to_round:
            slab = wr.shape[0] // steps
            assert slab * steps == wr.shape[0] and slab % 16 == 0
            slab_specs.append(pl.BlockSpec((slab, wr.shape[1]), lambda i, j, k: (i * nj + j, 0)))
        outs = _mm(a, w, functools.partial(_epi_residual_prenorm, len(gains)),
                   (sds((t, d), F32), sds((t, LANES), F32)) + tuple(sds((t, d), BF16) for _ in gains)
                   + tuple(sds(wr.shape, BF16) for wr in to_round),
                   (blk, pl.BlockSpec((tm_, LANES), row)) + tuple(blk for _ in gains)
                   + tuple(slab_specs),
                   extras=(h,) + tuple(g.reshape(1, d) for g in gains) + tuple(to_round),
                   extra_specs=(blk,) + tuple(pl.BlockSpec((1, tn), lambda i, j, k: (0, j))
                                              for _ in gains) + tuple(slab_specs),
                   first_tile_init=_zero_ssq, **common)
        n_g = len(gains)
        return outs[0], outs[1], tuple(outs[2:2 + n_g]), tuple(outs[2 + n_g:])

    def ffn(hb, ssq, h, layer, gains, to_round=()):
        tn = 256
        nj = d_ff // tn
        tm_in = min(2 * tm, t)
        steps = (t // tm_in) * nj
        slab = d_ff // steps
        assert slab * steps == d_ff and slab % 16 == 0
        act, w_out = _mm(
            hb, f_w_in, functools.partial(_epi_swiglu, d),
            (sds((t, d_ff), BF16), sds((d_ff, d), BF16)),
            (pl.BlockSpec((tm_in, tn), tile),
             pl.BlockSpec((slab, d), lambda i, j, k: (i * nj + j, 0))),
            tm=tm_in, tn=tn, tk=d, nj=nj, w_layer=layer, a_buffers=1,
            w_col_maps=(lambda j: j, lambda j: j + nj),
            extras=(ssq, f_w_out),
            extra_specs=(pl.BlockSpec((tm_in, LANES), row),
                         pl.BlockSpec((None, slab, d), lambda i, j, k: (layer, i * nj + j, 0))),
            name="ffn_in")
        return residual_mm(act, w_out, None, h, gains, tm_=tm, tn=tn, name="ffn_out",
                           a_buffers=1, to_round=to_round)

    h = x.reshape(t, d)
    hb, ssq = _prenorm(h, a_norm[0])

    for i in range(n_a):
        rope_lo = MLA_Q_LORA + MLA_KV_LORA
        w_in = jnp.concatenate([a_w_in[i], a_w_in[i][:, rope_lo:]], axis=1).astype(BF16)
        n_in = w_in.shape[1]
        cq, ckv, kr = _mm(
            hb, w_in, functools.partial(_epi_mla_in, d),
            (sds((t, MLA_Q_LORA), BF16), sds((t, MLA_KV_LORA), BF16), sds((t, LANES), BF16)),
            (pl.BlockSpec((tm, MLA_Q_LORA), row), pl.BlockSpec((tm, MLA_KV_LORA), row),
             pl.BlockSpec((tm, LANES), row)),
            tm=tm, tn=n_in, tk=1024, nj=1,
            extras=(ssq, a_q_norm[i].reshape(1, -1), a_kv_norm[i].reshape(1, -1), cos_mla, sin_mla),
            extra_specs=(ssq_spec,
                         pl.BlockSpec((1, MLA_Q_LORA), lambda i_, j, k: (0, 0)),
                         pl.BlockSpec((1, MLA_KV_LORA), lambda i_, j, k: (0, 0)),
                         tab_spec, tab_spec),
            name="mla_in")

        qd = MLA_NOPE + MLA_ROPE
        w_uq = a_w_uq[i].reshape(MLA_Q_LORA, heads, qd)
        w_uq = jnp.concatenate([w_uq[:, :, :MLA_NOPE].reshape(MLA_Q_LORA, heads * MLA_NOPE),
                                w_uq[:, :, MLA_NOPE:].reshape(MLA_Q_LORA, heads * MLA_ROPE)],
                               axis=1).astype(BF16)
        tn = 512
        q = _mm(cq, w_uq,
                functools.partial(_epi_mla_q, heads * MLA_NOPE // tn, qd ** -0.5 * LOG2_E),
                jax.ShapeDtypeStruct((t, heads * qd), BF16),
                pl.BlockSpec((tm, tn), lambda i_, j, k: (i_, j)),
                tm=tm, tn=tn, tk=MLA_Q_LORA, nj=heads * qd // tn,
                extras=(cos_mla, sin_mla), extra_specs=(tab_spec, tab_spec), name="mla_q")
        heads_per_tile = 4
        kn, vt = _mm(ckv, a_w_ukv[i].astype(BF16), _epi_mla_kv,
                     (jax.ShapeDtypeStruct((t, heads * MLA_NOPE), BF16),
                      jax.ShapeDtypeStruct((heads, t // VT_TILE, MLA_V, VT_TILE), BF16)),
                     (pl.BlockSpec((tm, heads_per_tile * MLA_NOPE), lambda i_, j, k: (i_, j)),
                      pl.BlockSpec((heads_per_tile, tm // VT_TILE, MLA_V, VT_TILE),
                                   lambda i_, j, k: (j, i_, 0, 0))),
                     tm=tm, tn=heads_per_tile * (MLA_NOPE + MLA_V), tk=MLA_KV_LORA,
                     nj=heads // heads_per_tile, name="mla_kv")
        o = _mla_attention(q, kn, kr, vt, batch, seq, heads)
        h, ssq, (hb,), _ = residual_mm(o, a_w_o[i].astype(BF16), None, h, [f_norm[i]],
                                       tm_=tm, tn=512, name="mla_out")
        last_a = i + 1 == n_a
        gains = [s_norm, b_norm[0]] if last_a else [a_norm[i + 1]]
        to_round = (b_w_in[0], s_w_kv, b_w_o[0]) if last_a else ()
        h, ssq, hbs, rounded = ffn(hb, ssq, h, i, gains, to_round)
        hb = hbs[0]

    hb_s, hb_b = hbs
    b_w_in0, s_w_kv_b, b_w_o0 = rounded
    n_sets = 3 * NSA_G
    per_tile = NSA_ROPE_TILE // NSA_DK
    tm_rope = tm
    rope_tab = pl.BlockSpec((tm_rope, NSA_DK), row)
    w_kv = s_w_kv_b.reshape(d, n_sets, NSA_DK + NSA_DV)
    w_k = w_kv[:, :, :NSA_DK].reshape(d, n_sets * NSA_DK)
    w_v = w_kv[:, :, NSA_DK:].reshape(d, n_sets * NSA_DV)
    k3 = _mm(hb_s, w_k, functools.partial(_epi_nsa_rope, d, None),
             sds((n_sets, t, NSA_DK), BF16),
             pl.BlockSpec((per_tile, tm_rope, NSA_DK), lambda i_, j, k: (j, i_, 0)),
             tm=tm_rope, tn=NSA_ROPE_TILE, tk=d, nj=n_sets // per_tile,
             extras=(ssq, cos_nsa, sin_nsa),
             extra_specs=(pl.BlockSpec((tm_rope, LANES), row), rope_tab, rope_tab),
             a_buffers=1, name="nsa_k")
    v3, vt3 = _mm(hb_s, w_v, functools.partial(_epi_nsa_v, d),
                  (sds((n_sets, t, NSA_DV), BF16),
                   sds((n_sets, t // VT_TILE, NSA_DV, VT_TILE), BF16)),
                  (pl.BlockSpec((per_tile, tm, NSA_DV), lambda i_, j, k: (j, i_, 0)),
                   pl.BlockSpec((per_tile, tm // VT_TILE, NSA_DV, VT_TILE),
                                lambda i_, j, k: (j, i_, 0, 0))),
                  tm=tm, tn=per_tile * NSA_DV, tk=d, nj=n_sets // per_tile,
                  extras=(ssq,), extra_specs=(ssq_spec,), name="nsa_v")

    ak = k3[:NSA_G].reshape(NSA_G, t // CMP_STRIDE, CMP_STRIDE * NSA_DK)
    av = v3[:NSA_G].reshape(NSA_G, t // CMP_STRIDE, CMP_STRIDE * NSA_DV)
    kc, vct = _compress(ak, av,
                        _compress_weights(s_cmp_w1_k, s_cmp_pe_k, s_cmp_w2_k, transpose_w2=False),
                        _compress_weights(s_cmp_w1_v, s_cmp_pe_v, s_cmp_w2_v, transpose_w2=True),
                        batch, seq)

    n_chunk = seq // CMP_STRIDE
    n_sel = seq // SEL_BLOCK
    c_start = np.arange(n_chunk) * CMP_STRIDE
    j_start = np.arange(n_sel) * SEL_BLOCK
    ovl = ((c_start[None, :] < j_start[:, None] + SEL_BLOCK)
           & (c_start[None, :] + CMP_BLOCK > j_start[:, None])
           & (np.arange(n_chunk)[None, :] < (seq - CMP_BLOCK) // CMP_STRIDE + 1))
    ovl = jnp.asarray(ovl, BF16)

    for i in range(n_b):
        n_q = heads * NSA_DK
        w_q = b_w_in0 if i == 0 else b_w_in[i].astype(BF16)
        w_o = b_w_o0 if i == 0 else b_w_o[i].astype(BF16)
        q = _mm(hb_b, w_q, functools.partial(_epi_nsa_rope, d, NSA_DK ** -0.5 * LOG2_E),
                sds((heads, t, NSA_DK), BF16),
                pl.BlockSpec((per_tile, tm_rope, NSA_DK), lambda i_, j, k: (j, i_, 0)),
                tm=tm_rope, tn=NSA_ROPE_TILE, tk=d, nj=heads // per_tile,
                extras=(ssq, cos_nsa, sin_nsa),
                extra_specs=(pl.BlockSpec((tm_rope, LANES), row), rope_tab, rope_tab),
                a_buffers=1,
                name="nsa_q")
        assert n_q % LANES == 0 and 3 * heads <= LANES
        gates = _mm(hb_b, w_q, functools.partial(_epi_sigmoid, d), sds((t, LANES), F32),
                    pl.BlockSpec((tm, LANES), row),
                    tm=tm, tn=LANES, tk=d, nj=1, w_col_maps=(lambda j: n_q // LANES,),
                    extras=(ssq,), extra_specs=(ssq_spec,), name="nsa_gates")
        o = _nsa_attention(q, kc, vct, k3, vt3, gates, ovl, batch, seq, hpg)
        h, ssq, (hb,), _ = residual_mm(o, w_o, None, h, [f_norm[n_a + i]],
                                       tm_=tm, tn=512, name="nsa_out")
        gains = [b_norm[i + 1]] if i + 1 < n_b else []
        h, ssq, hbs, _ = ffn(hb, ssq, h, n_a + i, gains)
        if hbs:
            hb_b = hbs[0]

    out = _rmsnorm(h, final_norm, F32)
    return out.reshape(batch, seq, d)
```

```python
import functools

import numpy as np
import jax
import jax.numpy as jnp
from jax import lax
from jax.experimental import pallas as pl
from jax.experimental.pallas import tpu as pltpu

F32 = jnp.float32
BF16 = jnp.bfloat16

LANES = 128
VMEM_LIMIT_BYTES = 56 * 1024 * 1024

NORM_EPS = 1e-6
ROPE_THETA = 10000.0

MLA_Q_LORA = 1536
MLA_KV_LORA = 512
MLA_NOPE = 128
MLA_ROPE = 64
MLA_V = 128

NSA_G = 4
NSA_DK = 192
NSA_DV = 128
NSA_HALF = NSA_DK // 2
NSA_ROPE_TILE = 4 * NSA_DK
CMP_BLOCK = 32
CMP_STRIDE = 16
SEL_BLOCK = 64
SEL_TOPK = 16
WINDOW = 512
FORCE_BONUS = 1e4
NEG_BIG = -1e30
LOG2_E = 1.4426950408889634

VT_TILE = LANES
NT_DIMS = (((1,), (1,)), ((), ()))


def _params(sem):
    return pltpu.CompilerParams(dimension_semantics=sem, vmem_limit_bytes=VMEM_LIMIT_BYTES)


def _rmsnorm_kernel(x_ref, g_ref, o_ref):
    x = x_ref[...]
    ms = jnp.mean(x * x, axis=-1, keepdims=True)
    o_ref[...] = (x * lax.rsqrt(ms + NORM_EPS) * g_ref[...]).astype(o_ref.dtype)


def _rmsnorm(x, g, out_dtype, tm=256):
    t, d = x.shape
    return pl.pallas_call(
        _rmsnorm_kernel,
        grid=(t // tm,),
        in_specs=[pl.BlockSpec((tm, d), lambda i: (i, 0)),
                  pl.BlockSpec((1, d), lambda i: (0, 0))],
        out_specs=pl.BlockSpec((tm, d), lambda i: (i, 0)),
        out_shape=jax.ShapeDtypeStruct((t, d), out_dtype),
        compiler_params=_params(("parallel",)),
        name="rmsnorm",
    )(x, g.reshape(1, d))


def _prenorm_kernel(x_ref, g_ref, hb_ref, ssq_ref):
    x = x_ref[...]
    hb_ref[...] = (x * g_ref[...]).astype(hb_ref.dtype)
    ssq_ref[...] = jnp.broadcast_to(jnp.sum(x * x, axis=-1, keepdims=True), ssq_ref.shape)


def _prenorm(x, g, tm=256):
    t, d = x.shape
    return pl.pallas_call(
        _prenorm_kernel,
        grid=(t // tm,),
        in_specs=[pl.BlockSpec((tm, d), lambda i: (i, 0)),
                  pl.BlockSpec((1, d), lambda i: (0, 0))],
        out_specs=(pl.BlockSpec((tm, d), lambda i: (i, 0)),
                   pl.BlockSpec((tm, LANES), lambda i: (i, 0))),
        out_shape=(jax.ShapeDtypeStruct((t, d), BF16), jax.ShapeDtypeStruct((t, LANES), F32)),
        compiler_params=_params(("parallel",)),
        name="prenorm",
    )(x, g.reshape(1, d))


def _mm_kernel(n_w, n_e, n_o, nk, epilogue, first_tile_init, *refs):
    a_ref = refs[0]
    w_refs = refs[1:1 + n_w]
    e_refs = refs[1 + n_w:1 + n_w + n_e]
    o_refs = refs[1 + n_w + n_e:1 + n_w + n_e + n_o]
    acc_refs = refs[1 + n_w + n_e + n_o:]
    j = pl.program_id(1)
    if first_tile_init is not None:
        pl.when((j == 0) & (pl.program_id(2) == 0))(lambda: first_tile_init(o_refs))

    def product(w):
        return jnp.dot(a_ref[...], w[...].astype(a_ref.dtype), preferred_element_type=F32)

    if nk == 1:
        epilogue([product(w) for w in w_refs], e_refs, o_refs, j)
        return
    k = pl.program_id(2)

    @pl.when(k == 0)
    def _():
        for w, acc in zip(w_refs, acc_refs):
            acc[...] = product(w)

    @pl.when(k > 0)
    def _():
        for w, acc in zip(w_refs, acc_refs):
            acc[...] += product(w)

    @pl.when(k == nk - 1)
    def _():
        epilogue([acc[...] for acc in acc_refs], e_refs, o_refs, j)


def _mm(a, w, epilogue, out_shape, out_specs, *, tm, tn, tk, nj, w_col_maps=None, w_layer=None,
        extras=(), extra_specs=(), a_buffers=None, first_tile_init=None, name="matmul"):
    m, kdim = a.shape
    ni, nk = m // tm, kdim // tk
    assert ni * tm == m and nk * tk == kdim
    if w_col_maps is None:
        w_col_maps = (lambda j: j,)
    n_w = len(w_col_maps)
    single = not isinstance(out_shape, (tuple, list))
    out_shapes = (out_shape,) if single else tuple(out_shape)
    out_specs_t = (out_specs,) if single else tuple(out_specs)
    a_kwargs = {} if a_buffers is None else {"pipeline_mode": pl.Buffered(a_buffers)}
    in_specs = [pl.BlockSpec((tm, tk), lambda i, j, k: (i, k), **a_kwargs)]
    for cm in w_col_maps:
        if w_layer is None:
            in_specs.append(pl.BlockSpec((tk, tn), functools.partial(
                lambda i, j, k, cm: (k, cm(j)), cm=cm)))
        else:
            in_specs.append(pl.BlockSpec((None, tk, tn), functools.partial(
                lambda i, j, k, cm: (w_layer, k, cm(j)), cm=cm)))
    in_specs += list(extra_specs)
    scratch = [pltpu.VMEM((tm, tn), F32) for _ in range(n_w)] if nk > 1 else []
    kern = functools.partial(_mm_kernel, n_w, len(extras), len(out_shapes), nk, epilogue,
                             first_tile_init)
    res = pl.pallas_call(
        kern,
        grid=(ni, nj, nk),
        in_specs=in_specs,
        out_specs=out_specs_t,
        out_shape=out_shapes,
        scratch_shapes=scratch,
        compiler_params=_params(("parallel", "arbitrary", "arbitrary")),
        name=name,
    )(a, *([w] * n_w), *extras)
    return res[0] if single else res


def _rope_pairs_64(x, cos2, sin2):
    lane = lax.broadcasted_iota(jnp.int32, x.shape, 1)
    first_half = (lane % MLA_ROPE) < (MLA_ROPE // 2)
    partner = jnp.where(first_half,
                        pltpu.roll(x, LANES - MLA_ROPE // 2, 1),
                        pltpu.roll(x, MLA_ROPE // 2, 1))
    return x * cos2 + partner * sin2


def _store_vt_tiles(vt_ref, lead, v):
    for c in range(v.shape[0] // VT_TILE):
        vt_ref[lead + (c,)] = v[c * VT_TILE:(c + 1) * VT_TILE, :].T.astype(vt_ref.dtype)


def _row_scale(acc, ssq_ref, d_model):
    rstd = lax.rsqrt(ssq_ref[...] * (1.0 / d_model) + NORM_EPS)
    return acc * jnp.concatenate([rstd] * (acc.shape[1] // LANES), axis=1)


def _epi_residual(accs, e_refs, o_refs, j):
    o_refs[0][...] = e_refs[0][...] + accs[0]


def _zero_ssq(o_refs):
    o_refs[1][...] = jnp.zeros(o_refs[1].shape, F32)


def _epi_residual_prenorm(n_gains, accs, e_refs, o_refs, j):
    h = e_refs[0][...] + accs[0]
    o_refs[0][...] = h
    ssq_ref = o_refs[1]
    ssq_ref[...] += jnp.broadcast_to(jnp.sum(h * h, axis=-1, keepdims=True), ssq_ref.shape)
    for g_ref, hb_ref in zip(e_refs[1:1 + n_gains], o_refs[2:2 + n_gains]):
        hb_ref[...] = (h * g_ref[...]).astype(hb_ref.dtype)
    for w_ref, wb_ref in zip(e_refs[1 + n_gains:], o_refs[2 + n_gains:]):
        wb_ref[...] = w_ref[...].astype(wb_ref.dtype)


def _epi_swiglu(d_model, accs, e_refs, o_refs, j):
    a = _row_scale(accs[0], e_refs[0], d_model)
    b = _row_scale(accs[1], e_refs[0], d_model)
    o_refs[0][...] = (a * jax.nn.sigmoid(a) * b).astype(o_refs[0].dtype)
    if len(o_refs) > 1:
        o_refs[1][...] = e_refs[1][...].astype(o_refs[1].dtype)


def _epi_sigmoid(d_model, accs, e_refs, o_refs, j):
    o_refs[0][...] = jax.nn.sigmoid(_row_scale(accs[0], e_refs[0], d_model))


def _epi_mla_in(d_model, accs, e_refs, o_refs, j):
    ssq_ref, gq_ref, gkv_ref, cos_ref, sin_ref = e_refs
    u = _row_scale(accs[0], ssq_ref, d_model)
    cq = u[:, :MLA_Q_LORA]
    ms = jnp.mean(cq * cq, axis=-1, keepdims=True)
    o_refs[0][...] = (cq * lax.rsqrt(ms + NORM_EPS) * gq_ref[...]).astype(BF16)
    ckv = u[:, MLA_Q_LORA:MLA_Q_LORA + MLA_KV_LORA]
    ms = jnp.mean(ckv * ckv, axis=-1, keepdims=True)
    o_refs[1][...] = (ckv * lax.rsqrt(ms + NORM_EPS) * gkv_ref[...]).astype(BF16)
    kr = u[:, MLA_Q_LORA + MLA_KV_LORA:]
    o_refs[2][...] = _rope_pairs_64(kr, cos_ref[...], sin_ref[...]).astype(BF16)


def _epi_mla_q(n_nope_tiles, scale, accs, e_refs, o_refs, j):
    cos_ref, sin_ref = e_refs
    acc = accs[0]

    @pl.when(j < n_nope_tiles)
    def _():
        o_refs[0][...] = (acc * scale).astype(BF16)

    @pl.when(j >= n_nope_tiles)
    def _():
        for c in range(acc.shape[1] // LANES):
            x = acc[:, c * LANES:(c + 1) * LANES]
            r = _rope_pairs_64(x, cos_ref[...], sin_ref[...])
            o_refs[0][:, c * LANES:(c + 1) * LANES] = (r * scale).astype(BF16)


def _epi_mla_kv(accs, e_refs, o_refs, j):
    acc = accs[0]
    kn_ref, vt_ref = o_refs
    per = MLA_NOPE + MLA_V
    for r in range(acc.shape[1] // per):
        kn_ref[:, r * MLA_NOPE:(r + 1) * MLA_NOPE] = acc[:, r * per:r * per + MLA_NOPE].astype(BF16)
        _store_vt_tiles(vt_ref, (r,), acc[:, r * per + MLA_NOPE:(r + 1) * per])


def _rope_192(x, cos, sin):
    width = x.shape[1]
    lane = lax.broadcasted_iota(jnp.int32, x.shape, 1)
    partner = jnp.where((lane % NSA_DK) < NSA_HALF,
                        pltpu.roll(x, width - NSA_HALF, 1),
                        pltpu.roll(x, NSA_HALF, 1))
    return x * cos + partner * sin


def _epi_nsa_rope(d_model, scale, accs, e_refs, o_refs, j):
    ssq_ref, cos_ref, sin_ref = e_refs
    acc = _row_scale(accs[0], ssq_ref, d_model)
    reps = acc.shape[1] // NSA_DK
    roped = _rope_192(acc, jnp.concatenate([cos_ref[...]] * reps, axis=1),
                      jnp.concatenate([sin_ref[...]] * reps, axis=1))
    if scale is not None:
        roped = roped * scale
    for r in range(acc.shape[1] // NSA_DK):
        o_refs[0][r] = roped[:, r * NSA_DK:(r + 1) * NSA_DK].astype(BF16)


def _epi_nsa_v(d_model, accs, e_refs, o_refs, j):
    v_ref, vt_ref = o_refs
    acc = _row_scale(accs[0], e_refs[0], d_model)
    for r in range(acc.shape[1] // NSA_DV):
        v = acc[:, r * NSA_DV:(r + 1) * NSA_DV]
        v_ref[r] = v.astype(BF16)
        _store_vt_tiles(vt_ref, (r,), v)


SUM_ROWS = 16


def _attend_t(q, k, vt, bias):
    dv = vt.shape[0]
    s = lax.dot_general(k, q, NT_DIMS, preferred_element_type=F32)
    bias = jnp.concatenate([bias] * (s.shape[1] // bias.shape[1]), axis=1)
    free = s.shape[0] - bias.shape[0]
    s = s + bias if free == 0 else jnp.concatenate([s[:free], s[free:] + bias], axis=0)
    p = jnp.exp2(s - jnp.max(s, axis=0, keepdims=True)).astype(BF16)
    vt1 = jnp.concatenate([vt, jnp.ones((SUM_ROWS, vt.shape[1]), BF16)], axis=0)
    acc = jnp.dot(vt1, p, preferred_element_type=F32)
    return acc[:dv] / acc[dv:dv + 1]


def _vt_tiles(vt_ref, lead, c0, n):
    return jnp.concatenate([vt_ref[lead + (c0 + c,)] for c in range(n)], axis=1)


def _mla_attn_kernel(t_blk, seq, qn_ref, qr_ref, kn_ref, kr_ref, vt_ref, o_ref):
    lane_half = lax.broadcasted_iota(jnp.int32, (t_blk, LANES), 1) // MLA_ROPE
    kpos = lax.broadcasted_iota(jnp.int32, (t_blk, t_blk), 0)
    qpos = lax.broadcasted_iota(jnp.int32, (t_blk, t_blk), 1)
    diag_bias = jnp.where(kpos <= qpos, 0.0, NEG_BIG).astype(F32)

    for qi in range(seq // t_blk):
        q0 = qi * t_blk
        n_keys = q0 + t_blk
        qr_pair = qr_ref[q0:q0 + t_blk, :]
        for hh in range(2):
            qr = jnp.where(lane_half == hh, qr_pair, jnp.zeros_like(qr_pair))
            q = jnp.concatenate([qn_ref[q0:q0 + t_blk, hh * MLA_NOPE:(hh + 1) * MLA_NOPE], qr], axis=1)
            k = jnp.concatenate([kn_ref[0:n_keys, hh * MLA_NOPE:(hh + 1) * MLA_NOPE],
                                 kr_ref[0:n_keys, :]], axis=1)
            vt = _vt_tiles(vt_ref, (hh,), 0, n_keys // VT_TILE)
            o_ref[q0:q0 + t_blk, hh * MLA_V:(hh + 1) * MLA_V] = (
                _attend_t(q, k, vt, diag_bias).T.astype(o_ref.dtype))


def _mla_attention(q, kn, kr, vt, batch, seq, heads, t_blk=512):
    t = batch * seq
    n_nope_blk = heads * MLA_NOPE // LANES
    n_vt = seq // VT_TILE
    kern = functools.partial(_mla_attn_kernel, t_blk, seq)
    return pl.pallas_call(
        kern,
        grid=(batch, heads // 2),
        in_specs=[
            pl.BlockSpec((seq, 2 * MLA_NOPE), lambda b, hp: (b, hp)),
            pl.BlockSpec((seq, LANES), lambda b, hp: (b, n_nope_blk + hp)),
            pl.BlockSpec((seq, 2 * MLA_NOPE), lambda b, hp: (b, hp)),
            pl.BlockSpec((seq, LANES), lambda b, hp: (b, 0)),
            pl.BlockSpec((2, n_vt, MLA_V, VT_TILE), lambda b, hp: (hp, b, 0, 0)),
        ],
        out_specs=pl.BlockSpec((seq, 2 * MLA_V), lambda b, hp: (b, hp)),
        out_shape=jax.ShapeDtypeStruct((t, heads * MLA_V), BF16),
        compiler_params=_params(("parallel", "parallel")),
        name="mla_attention",
    )(q, q, kn, kr, vt)


def _compress_mlp(a, w1_ref, pe_ref, w2_ref, transposed_out):
    y_lo = jnp.dot(a, w1_ref[0], preferred_element_type=F32)
    y_hi = jnp.dot(a, w1_ref[1], preferred_element_type=F32)
    pe_c = (jnp.dot(pe_ref[0], w1_ref[0], preferred_element_type=F32)
            + jnp.dot(pe_ref[1], w1_ref[1], preferred_element_type=F32))
    n = y_hi.shape[0]
    hid = y_lo + pltpu.roll(y_hi, n - 1, 0) + pe_c[0:1, :]
    hid = jax.nn.gelu(hid, approximate=True)
    if transposed_out:
        return lax.dot_general(w2_ref[...], hid.astype(BF16), NT_DIMS, preferred_element_type=F32)
    return jnp.dot(hid.astype(BF16), w2_ref[...], preferred_element_type=F32)


def _compress_kernel(ak_ref, av_ref, w1k_ref, pek_ref, w2k_ref, w1v_ref, pev_ref, w2vt_ref,
                     kc_ref, vct_ref):
    kc_ref[0] = _compress_mlp(ak_ref[0], w1k_ref, pek_ref, w2k_ref, False).astype(BF16)
    vct_ref[0] = _compress_mlp(av_ref[0], w1v_ref, pev_ref, w2vt_ref, True).astype(BF16)


def _compress(ak, av, wk, wv, batch, seq):
    n_chunk = seq // CMP_STRIDE
    const3 = lambda b, g: (0, 0, 0)
    const2 = lambda b, g: (0, 0)

    def wspecs(w):
        w1, pe, w2 = w
        return [pl.BlockSpec(w1.shape, const3), pl.BlockSpec(pe.shape, const3),
                pl.BlockSpec(w2.shape, const2)]

    return pl.pallas_call(
        _compress_kernel,
        grid=(batch, NSA_G),
        in_specs=[pl.BlockSpec((1, n_chunk, ak.shape[2]), lambda b, g: (g, b, 0)),
                  pl.BlockSpec((1, n_chunk, av.shape[2]), lambda b, g: (g, b, 0))]
        + wspecs(wk) + wspecs(wv),
        out_specs=(pl.BlockSpec((1, n_chunk, NSA_DK), lambda b, g: (b * NSA_G + g, 0, 0)),
                   pl.BlockSpec((1, NSA_DV, n_chunk), lambda b, g: (b * NSA_G + g, 0, 0))),
        out_shape=(jax.ShapeDtypeStruct((batch * NSA_G, n_chunk, NSA_DK), BF16),
                   jax.ShapeDtypeStruct((batch * NSA_G, NSA_DV, n_chunk), BF16)),
        compiler_params=_params(("parallel", "parallel")),
        name="nsa_compress",
    )(ak, av, *wk, *wv)


def _nsa_attn_kernel(hpg, tq, tk, seq, q_ref, kc_ref, vct_ref, ks_ref, vst_ref, kw_ref, vwt_ref,
                     gate_ref, ovl_ref, o_ref, bias_ref, oc_ref, os_ref, gt_ref):
    qi = pl.program_id(2)
    t0 = qi * tq
    n_rows = hpg * tq
    n_cmp = kc_ref.shape[1]
    n_sel = seq // SEL_BLOCK
    n_vt = tk // VT_TILE
    q = q_ref[...].reshape(n_rows, NSA_DK)

    def tile_heads(x):
        return jnp.concatenate([x] * hpg, axis=1)

    s = lax.dot_general(kc_ref[0], q, NT_DIMS, preferred_element_type=F32)
    cidx = lax.broadcasted_iota(jnp.int32, (n_cmp, tq), 0)
    tpos_c = t0 + lax.broadcasted_iota(jnp.int32, (n_cmp, tq), 1)
    mask_c = tile_heads((cidx * CMP_STRIDE + CMP_BLOCK - 1) <= tpos_c)
    s = jnp.where(mask_c, s, NEG_BIG)
    mx = jnp.max(s, axis=0, keepdims=True)
    e = jnp.where(mask_c, jnp.exp2(s - mx), 0.0)
    p = e / jnp.maximum(jnp.sum(e, axis=0, keepdims=True), 1e-30)
    oc_ref[...] = jnp.dot(vct_ref[0], p.astype(BF16), preferred_element_type=F32)

    psum = p[:, 0:tq]
    for h in range(1, hpg):
        psum = psum + p[:, h * tq:(h + 1) * tq]
    p1 = psum.astype(BF16)
    p2 = (psum - p1.astype(F32)).astype(BF16)
    p3 = (psum - p1.astype(F32) - p2.astype(F32)).astype(BF16)
    ovl = ovl_ref[...]
    imp = (jnp.dot(ovl, p1, preferred_element_type=F32)
           + jnp.dot(ovl, p2, preferred_element_type=F32)
           + jnp.dot(ovl, p3, preferred_element_type=F32))
    jrow = lax.broadcasted_iota(jnp.int32, (n_sel, tq), 0)
    tpos = t0 + lax.broadcasted_iota(jnp.int32, (n_sel, tq), 1)
    valid = jrow * SEL_BLOCK <= tpos
    cur = jnp.right_shift(tpos, SEL_BLOCK.bit_length() - 1)
    forced = valid & ((jrow == 0) | (jrow == cur) | (jrow == cur - 1))
    score = jnp.where(valid, imp + jnp.where(forced, FORCE_BONUS, 0.0), -jnp.inf)
    rank = jnp.zeros((n_sel, tq), jnp.int32)
    for i in range(n_sel):
        row = score[i:i + 1, :]
        beats = (row > score) | ((row == score) & (jrow > i))
        rank = rank + beats.astype(jnp.int32)
    sel_bias = jnp.where(valid & (rank < SEL_TOPK), 0.0, NEG_BIG).astype(F32)

    tk_s = bias_ref.shape[1]
    n_sel_tiles = (t0 + tq - 1) // tk_s + 1
    blocks_per_tile = tk_s // SEL_BLOCK
    for jj in range(seq // tk_s):
        @pl.when(jj < n_sel_tiles)
        def _():
            rows = [jnp.broadcast_to(sel_bias[b:b + 1, :], (SEL_BLOCK, tq))
                    for b in range(jj * blocks_per_tile, (jj + 1) * blocks_per_tile)]
            kpos = jj * tk_s + lax.broadcasted_iota(jnp.int32, (tk_s, tq), 0)
            tq_pos = t0 + lax.broadcasted_iota(jnp.int32, (tk_s, tq), 1)
            bias_ref[jj] = jnp.where(kpos <= tq_pos, jnp.concatenate(rows, axis=0), NEG_BIG)

    for n in range(1, seq // tk_s + 1):
        @pl.when(n_sel_tiles == n)
        def _():
            keys = n * tk_s
            vt = _vt_tiles(vst_ref, (0,), 0, keys // VT_TILE)
            os_ref[...] = _attend_t(q, ks_ref[0, 0:keys, :], vt, bias_ref[0:n].reshape(keys, tq))

    k0 = pl.multiple_of(jnp.maximum(t0 + tq - tk, 0), VT_TILE)
    kpos = k0 + lax.broadcasted_iota(jnp.int32, (tk, tq), 0)
    tq_pos = t0 + lax.broadcasted_iota(jnp.int32, (tk, tq), 1)
    ok = (kpos <= tq_pos) & (tq_pos - kpos < WINDOW)
    o_w = _attend_t(q, kw_ref[0, pl.ds(k0, tk), :], _vt_tiles(vwt_ref, (0,), k0 // VT_TILE, n_vt),
                    jnp.where(ok, 0.0, NEG_BIG).astype(F32))

    gt_ref[...] = gate_ref[...].T
    base = pl.program_id(1) * (3 * hpg)
    for h in range(hpg):
        cols = slice(h * tq, (h + 1) * tq)
        o = (gt_ref[pl.ds(base + 3 * h, 1), :] * oc_ref[:, cols]
             + gt_ref[pl.ds(base + 3 * h + 1, 1), :] * os_ref[:, cols]
             + gt_ref[pl.ds(base + 3 * h + 2, 1), :] * o_w[:, cols])
        o_ref[:, h * NSA_DV:(h + 1) * NSA_DV] = o.T.astype(o_ref.dtype)


def _nsa_attention(q, kc, vct, k3, vt3, gates, ovl, batch, seq, hpg, tq=128, tk_sel=512):
    tk = WINDOW + tq
    t = batch * seq
    nq = seq // tq
    n_chunk = seq // CMP_STRIDE
    n_rows = hpg * tq
    n_vt = seq // VT_TILE
    assert tk % VT_TILE == 0 and seq >= tk and seq % tk_sel == 0
    kern = functools.partial(_nsa_attn_kernel, hpg, tq, tk, seq)

    def k_spec(branch):
        return pl.BlockSpec((1, seq, NSA_DK), lambda b, g, qi: (branch * NSA_G + g, b, 0))

    def vt_spec(branch):
        return pl.BlockSpec((1, n_vt, NSA_DV, VT_TILE), lambda b, g, qi: (branch * NSA_G + g, b, 0, 0))

    return pl.pallas_call(
        kern,
        grid=(batch, NSA_G, nq),
        in_specs=[
            pl.BlockSpec((hpg, tq, NSA_DK), lambda b, g, qi: (g, b * nq + qi, 0)),
            pl.BlockSpec((1, n_chunk, NSA_DK), lambda b, g, qi: (b * NSA_G + g, 0, 0)),
            pl.BlockSpec((1, NSA_DV, n_chunk), lambda b, g, qi: (b * NSA_G + g, 0, 0)),
            k_spec(1), vt_spec(1), k_spec(2), vt_spec(2),
            pl.BlockSpec((tq, LANES), lambda b, g, qi: (b * nq + qi, 0)),
            pl.BlockSpec(ovl.shape, lambda b, g, qi: (0, 0)),
        ],
        out_specs=pl.BlockSpec((tq, hpg * NSA_DV), lambda b, g, qi: (b * nq + qi, g)),
        out_shape=jax.ShapeDtypeStruct((t, NSA_G * hpg * NSA_DV), BF16),
        scratch_shapes=[
            pltpu.VMEM((seq // tk_sel, tk_sel, tq), F32),
            pltpu.VMEM((NSA_DV, n_rows), F32),
            pltpu.VMEM((NSA_DV, n_rows), F32),
            pltpu.VMEM((LANES, tq), F32),
        ],
        compiler_params=_params(("parallel", "parallel", "arbitrary")),
        name="nsa_attention",
    )(q, kc, vct, k3, vt3, k3, vt3, gates, ovl)


def _rope_tables(pos_flat):
    pos = pos_flat.astype(F32)[:, None]
    inv64 = ROPE_THETA ** (-jnp.arange(0, MLA_ROPE, 2, dtype=F32) / MLA_ROPE)
    ang = pos * inv64
    c, s = jnp.cos(ang), jnp.sin(ang)
    cos_mla = jnp.concatenate([c, c, c, c], axis=1)
    sin_mla = jnp.concatenate([-s, s, -s, s], axis=1)
    inv192 = ROPE_THETA ** (-jnp.arange(0, NSA_DK, 2, dtype=F32) / NSA_DK)
    ang = pos * inv192
    c, s = jnp.cos(ang), jnp.sin(ang)
    cos_nsa = jnp.concatenate([c, c], axis=1)
    sin_nsa = jnp.concatenate([-s, s], axis=1)
    return cos_mla, sin_mla, cos_nsa, sin_nsa


def _compress_weights(w1, pe, w2, *, transpose_w2):
    width, d_hid = w1.shape[0] // CMP_BLOCK, w1.shape[1]
    hid_p = -(-d_hid // LANES) * LANES
    w1 = jnp.pad(w1, ((0, 0), (0, hid_p - d_hid)))
    w1 = w1.reshape(2, CMP_STRIDE * width, hid_p).astype(BF16)
    pe = jnp.pad(pe.reshape(2, 1, CMP_STRIDE * width), ((0, 0), (0, 7), (0, 0))).astype(BF16)
    w2 = jnp.pad(w2, ((0, hid_p - d_hid), (0, 0))).astype(BF16)
    return w1, pe, (w2.T if transpose_w2 else w2)


def kernel(x, positions, a_norm, a_w_in, a_q_norm, a_w_uq, a_kv_norm, a_w_ukv, a_w_o, s_norm, s_w_kv, s_cmp_pe_k, s_cmp_w1_k, s_cmp_w2_k, s_cmp_pe_v, s_cmp_w1_v, s_cmp_w2_v, b_norm, b_w_in, b_w_o, f_norm, f_w_in, f_w_out, final_norm):
    batch, seq, d = x.shape
    t = batch * seq
    heads = d // 128
    hpg = heads // NSA_G
    d_ff = f_w_out.shape[1]
    n_a, n_b = a_w_in.shape[0], b_w_in.shape[0]
    assert n_a >= 1 and n_b >= 1
    tm = 1024
    sds = jax.ShapeDtypeStruct

    cos_mla, sin_mla, cos_nsa, sin_nsa = _rope_tables(positions.reshape(t))
    row = lambda i, j, k: (i, 0)
    tile = lambda i, j, k: (i, j)
    tab_spec = pl.BlockSpec((tm, LANES), row)
    ssq_spec = pl.BlockSpec((tm, LANES), row)

    def residual_mm(a, w, w_layer, h, gains, *, tm_, tn, name, a_buffers=None, to_round=()):
        blk = pl.BlockSpec((tm_, tn), tile)
        nj = d // tn
        common = dict(tm=tm_, tn=tn, tk=a.shape[1], nj=nj, w_layer=w_layer, name=name,
                      a_buffers=a_buffers)
        if not gains:
            assert not to_round
            h_new = _mm(a, w, _epi_residual, sds((t, d), F32), blk,
                        extras=(h,), extra_specs=(blk,), **common)
            return h_new, None, (), ()
        steps = (t // tm_) * nj
        slabs_in, slabs_out, flats = [], [], []
        for wr, layer in to_round:
            rows, cols = wr.shape[1:]
            slab = rows // steps
            assert slab * steps == rows and slab % 16 == 0
            flats.append(wr.reshape(-1, cols))
            slabs_in.append(pl.BlockSpec(
                (slab, cols), functools.partial(lambda i, j, k, off: (off + i * nj + j, 0),
                                                off=layer * steps)))
            slabs_out.append(pl.BlockSpec((slab, cols), lambda i, j, k: (i * nj + j, 0)))
        outs = _mm(a, w, functools.partial(_epi_residual_prenorm, len(gains)),
                   (sds((t, d), F32), sds((t, LANES), F32)) + tuple(sds((t, d), BF16) for _ in gains)
                   + tuple(sds(wr.shape[1:], BF16) for wr, _ in to_round),
                   (blk, pl.BlockSpec((tm_, LANES), row)) + tuple(blk for _ in gains)
                   + tuple(slabs_out),
                   extras=(h,) + tuple(g.reshape(1, d) for g in gains) + tuple(flats),
                   extra_specs=(blk,) + tuple(pl.BlockSpec((1, tn), lambda i, j, k: (0, j))
                                              for _ in gains) + tuple(slabs_in),
                   first_tile_init=_zero_ssq, **common)
        n_g = len(gains)
        return outs[0], outs[1], tuple(outs[2:2 + n_g]), tuple(outs[2 + n_g:])

    def ffn(hb, ssq, h, layer, gains, to_round=()):
        tn = 256
        nj = d_ff // tn
        tm_in = min(2 * tm, t)
        steps = (t // tm_in) * nj
        slab = d_ff // steps
        assert slab * steps == d_ff and slab % 16 == 0
        act, w_out = _mm(
            hb, f_w_in, functools.partial(_epi_swiglu, d),
            (sds((t, d_ff), BF16), sds((d_ff, d), BF16)),
            (pl.BlockSpec((tm_in, tn), tile),
             pl.BlockSpec((slab, d), lambda i, j, k: (i * nj + j, 0))),
            tm=tm_in, tn=tn, tk=d, nj=nj, w_layer=layer, a_buffers=1,
            w_col_maps=(lambda j: j, lambda j: j + nj),
            extras=(ssq, f_w_out),
            extra_specs=(pl.BlockSpec((tm_in, LANES), row),
                         pl.BlockSpec((None, slab, d), lambda i, j, k: (layer, i * nj + j, 0))),
            name="ffn_in")
        return residual_mm(act, w_out, None, h, gains, tm_=tm, tn=tn, name="ffn_out",
                           a_buffers=1, to_round=to_round)

    h = x.reshape(t, d)
    hb, ssq = _prenorm(h, a_norm[0])

    for i in range(n_a):
        rope_lo = MLA_Q_LORA + MLA_KV_LORA
        w_in = jnp.concatenate([a_w_in[i], a_w_in[i][:, rope_lo:]], axis=1).astype(BF16)
        n_in = w_in.shape[1]
        cq, ckv, kr = _mm(
            hb, w_in, functools.partial(_epi_mla_in, d),
            (sds((t, MLA_Q_LORA), BF16), sds((t, MLA_KV_LORA), BF16), sds((t, LANES), BF16)),
            (pl.BlockSpec((tm, MLA_Q_LORA), row), pl.BlockSpec((tm, MLA_KV_LORA), row),
             pl.BlockSpec((tm, LANES), row)),
            tm=tm, tn=n_in, tk=1024, nj=1,
            extras=(ssq, a_q_norm[i].reshape(1, -1), a_kv_norm[i].reshape(1, -1), cos_mla, sin_mla),
            extra_specs=(ssq_spec,
                         pl.BlockSpec((1, MLA_Q_LORA), lambda i_, j, k: (0, 0)),
                         pl.BlockSpec((1, MLA_KV_LORA), lambda i_, j, k: (0, 0)),
                         tab_spec, tab_spec),
            name="mla_in")

        qd = MLA_NOPE + MLA_ROPE
        w_uq = a_w_uq[i].reshape(MLA_Q_LORA, heads, qd)
        w_uq = jnp.concatenate([w_uq[:, :, :MLA_NOPE].reshape(MLA_Q_LORA, heads * MLA_NOPE),
                                w_uq[:, :, MLA_NOPE:].reshape(MLA_Q_LORA, heads * MLA_ROPE)],
                               axis=1).astype(BF16)
        tn = 512
        q = _mm(cq, w_uq,
                functools.partial(_epi_mla_q, heads * MLA_NOPE // tn, qd ** -0.5 * LOG2_E),
                jax.ShapeDtypeStruct((t, heads * qd), BF16),
                pl.BlockSpec((tm, tn), lambda i_, j, k: (i_, j)),
                tm=tm, tn=tn, tk=MLA_Q_LORA, nj=heads * qd // tn,
                extras=(cos_mla, sin_mla), extra_specs=(tab_spec, tab_spec), name="mla_q")
        heads_per_tile = 4
        kn, vt = _mm(ckv, a_w_ukv[i].astype(BF16), _epi_mla_kv,
                     (jax.ShapeDtypeStruct((t, heads * MLA_NOPE), BF16),
                      jax.ShapeDtypeStruct((heads, t // VT_TILE, MLA_V, VT_TILE), BF16)),
                     (pl.BlockSpec((tm, heads_per_tile * MLA_NOPE), lambda i_, j, k: (i_, j)),
                      pl.BlockSpec((heads_per_tile, tm // VT_TILE, MLA_V, VT_TILE),
                                   lambda i_, j, k: (j, i_, 0, 0))),
                     tm=tm, tn=heads_per_tile * (MLA_NOPE + MLA_V), tk=MLA_KV_LORA,
                     nj=heads // heads_per_tile, name="mla_kv")
        o = _mla_attention(q, kn, kr, vt, batch, seq, heads)
        h, ssq, (hb,), _ = residual_mm(o, a_w_o[i].astype(BF16), None, h, [f_norm[i]],
                                       tm_=tm, tn=512, name="mla_out")
        last_a = i + 1 == n_a
        gains = [s_norm, b_norm[0]] if last_a else [a_norm[i + 1]]
        to_round = ((b_w_in, 0), (s_w_kv[None], 0), (b_w_o, 0)) if last_a else ()
        h, ssq, hbs, rounded = ffn(hb, ssq, h, i, gains, to_round)
        hb = hbs[0]

    hb_s, hb_b = hbs
    b_w_in0, s_w_kv_b, b_w_o0 = rounded
    n_sets = 3 * NSA_G
    per_tile = NSA_ROPE_TILE // NSA_DK
    tm_rope = tm
    rope_tab = pl.BlockSpec((tm_rope, NSA_DK), row)
    w_kv = s_w_kv_b.reshape(d, n_sets, NSA_DK + NSA_DV)
    w_k = w_kv[:, :, :NSA_DK].reshape(d, n_sets * NSA_DK)
    w_v = w_kv[:, :, NSA_DK:].reshape(d, n_sets * NSA_DV)
    k3 = _mm(hb_s, w_k, functools.partial(_epi_nsa_rope, d, None),
             sds((n_sets, t, NSA_DK), BF16),
             pl.BlockSpec((per_tile, tm_rope, NSA_DK), lambda i_, j, k: (j, i_, 0)),
             tm=tm_rope, tn=NSA_ROPE_TILE, tk=d, nj=n_sets // per_tile,
             extras=(ssq, cos_nsa, sin_nsa),
             extra_specs=(pl.BlockSpec((tm_rope, LANES), row), rope_tab, rope_tab),
             a_buffers=1, name="nsa_k")
    v3, vt3 = _mm(hb_s, w_v, functools.partial(_epi_nsa_v, d),
                  (sds((n_sets, t, NSA_DV), BF16),
                   sds((n_sets, t // VT_TILE, NSA_DV, VT_TILE), BF16)),
                  (pl.BlockSpec((per_tile, tm, NSA_DV), lambda i_, j, k: (j, i_, 0)),
                   pl.BlockSpec((per_tile, tm // VT_TILE, NSA_DV, VT_TILE),
                                lambda i_, j, k: (j, i_, 0, 0))),
                  tm=tm, tn=per_tile * NSA_DV, tk=d, nj=n_sets // per_tile,
                  extras=(ssq,), extra_specs=(ssq_spec,), name="nsa_v")

    ak = k3[:NSA_G].reshape(NSA_G, t // CMP_STRIDE, CMP_STRIDE * NSA_DK)
    av = v3[:NSA_G].reshape(NSA_G, t // CMP_STRIDE, CMP_STRIDE * NSA_DV)
    kc, vct = _compress(ak, av,
                        _compress_weights(s_cmp_w1_k, s_cmp_pe_k, s_cmp_w2_k, transpose_w2=False),
                        _compress_weights(s_cmp_w1_v, s_cmp_pe_v, s_cmp_w2_v, transpose_w2=True),
                        batch, seq)

    n_chunk = seq // CMP_STRIDE
    n_sel = seq // SEL_BLOCK
    c_start = np.arange(n_chunk) * CMP_STRIDE
    j_start = np.arange(n_sel) * SEL_BLOCK
    ovl = ((c_start[None, :] < j_start[:, None] + SEL_BLOCK)
           & (c_start[None, :] + CMP_BLOCK > j_start[:, None])
           & (np.arange(n_chunk)[None, :] < (seq - CMP_BLOCK) // CMP_STRIDE + 1))
    ovl = jnp.asarray(ovl, BF16)

    for i in range(n_b):
        n_q = heads * NSA_DK
        w_q = b_w_in0 if i == 0 else b_w_in[i].astype(BF16)
        w_o = b_w_o0 if i == 0 else b_w_o[i].astype(BF16)
        q = _mm(hb_b, w_q, functools.partial(_epi_nsa_rope, d, NSA_DK ** -0.5 * LOG2_E),
                sds((heads, t, NSA_DK), BF16),
                pl.BlockSpec((per_tile, tm_rope, NSA_DK), lambda i_, j, k: (j, i_, 0)),
                tm=tm_rope, tn=NSA_ROPE_TILE, tk=d, nj=heads // per_tile,
                extras=(ssq, cos_nsa, sin_nsa),
                extra_specs=(pl.BlockSpec((tm_rope, LANES), row), rope_tab, rope_tab),
                a_buffers=1,
                name="nsa_q")
        assert n_q % LANES == 0 and 3 * heads <= LANES
        gates = _mm(hb_b, w_q, functools.partial(_epi_sigmoid, d), sds((t, LANES), F32),
                    pl.BlockSpec((tm, LANES), row),
                    tm=tm, tn=LANES, tk=d, nj=1, w_col_maps=(lambda j: n_q // LANES,),
                    extras=(ssq,), extra_specs=(ssq_spec,), name="nsa_gates")
        o = _nsa_attention(q, kc, vct, k3, vt3, gates, ovl, batch, seq, hpg)
        h, ssq, (hb,), _ = residual_mm(o, w_o, None, h, [f_norm[n_a + i]],
                                       tm_=tm, tn=512, name="nsa_out")
        gains = [b_norm[i + 1]] if i + 1 < n_b else []
        h, ssq, hbs, _ = ffn(hb, ssq, h, n_a + i, gains)
        if hbs:
            hb_b = hbs[0]

    out = _rmsnorm(h, final_norm, F32)
    return out.reshape(batch, seq, d)
```

```python
import functools

import numpy as np
import jax
import jax.numpy as jnp
from jax import lax
from jax.experimental import pallas as pl
from jax.experimental.pallas import tpu as pltpu

F32 = jnp.float32
BF16 = jnp.bfloat16

LANES = 128
VMEM_LIMIT_BYTES = 56 * 1024 * 1024

NORM_EPS = 1e-6
ROPE_THETA = 10000.0

MLA_Q_LORA = 1536
MLA_KV_LORA = 512
MLA_NOPE = 128
MLA_ROPE = 64
MLA_V = 128

NSA_G = 4
NSA_DK = 192
NSA_DV = 128
NSA_HALF = NSA_DK // 2
NSA_ROPE_TILE = 4 * NSA_DK
CMP_BLOCK = 32
CMP_STRIDE = 16
SEL_BLOCK = 64
SEL_TOPK = 16
WINDOW = 512
FORCE_BONUS = 1e4
NEG_BIG = -1e30
LOG2_E = 1.4426950408889634

VT_TILE = LANES
NT_DIMS = (((1,), (1,)), ((), ()))


def _params(sem):
    return pltpu.CompilerParams(dimension_semantics=sem, vmem_limit_bytes=VMEM_LIMIT_BYTES)


def _rmsnorm_kernel(x_ref, g_ref, o_ref):
    x = x_ref[...]
    ms = jnp.mean(x * x, axis=-1, keepdims=True)
    o_ref[...] = (x * lax.rsqrt(ms + NORM_EPS) * g_ref[...]).astype(o_ref.dtype)


def _rmsnorm(x, g, out_dtype, tm=256):
    t, d = x.shape
    return pl.pallas_call(
        _rmsnorm_kernel,
        grid=(t // tm,),
        in_specs=[pl.BlockSpec((tm, d), lambda i: (i, 0)),
                  pl.BlockSpec((1, d), lambda i: (0, 0))],
        out_specs=pl.BlockSpec((tm, d), lambda i: (i, 0)),
        out_shape=jax.ShapeDtypeStruct((t, d), out_dtype),
        compiler_params=_params(("parallel",)),
        name="rmsnorm",
    )(x, g.reshape(1, d))


def _prenorm_kernel(x_ref, g_ref, hb_ref, ssq_ref):
    x = x_ref[...]
    hb_ref[...] = (x * g_ref[...]).astype(hb_ref.dtype)
    ssq_ref[...] = jnp.broadcast_to(jnp.sum(x * x, axis=-1, keepdims=True), ssq_ref.shape)


def _prenorm(x, g, tm=256):
    t, d = x.shape
    return pl.pallas_call(
        _prenorm_kernel,
        grid=(t // tm,),
        in_specs=[pl.BlockSpec((tm, d), lambda i: (i, 0)),
                  pl.BlockSpec((1, d), lambda i: (0, 0))],
        out_specs=(pl.BlockSpec((tm, d), lambda i: (i, 0)),
                   pl.BlockSpec((tm, LANES), lambda i: (i, 0))),
        out_shape=(jax.ShapeDtypeStruct((t, d), BF16), jax.ShapeDtypeStruct((t, LANES), F32)),
        compiler_params=_params(("parallel",)),
        name="prenorm",
    )(x, g.reshape(1, d))


def _mm_kernel(n_w, n_e, n_o, nk, epilogue, first_tile_init, w_transposed, *refs):
    a_ref = refs[0]
    w_refs = refs[1:1 + n_w]
    e_refs = refs[1 + n_w:1 + n_w + n_e]
    o_refs = refs[1 + n_w + n_e:1 + n_w + n_e + n_o]
    acc_refs = refs[1 + n_w + n_e + n_o:]
    j = pl.program_id(1)
    if first_tile_init is not None:
        pl.when((j == 0) & (pl.program_id(2) == 0))(lambda: first_tile_init(o_refs))

    def product(w):
        wv = w[...].astype(a_ref.dtype)
        if w_transposed:
            return lax.dot_general(a_ref[...], wv, NT_DIMS, preferred_element_type=F32)
        return jnp.dot(a_ref[...], wv, preferred_element_type=F32)

    if nk == 1:
        epilogue([product(w) for w in w_refs], e_refs, o_refs, j)
        return
    k = pl.program_id(2)

    @pl.when(k == 0)
    def _():
        for w, acc in zip(w_refs, acc_refs):
            acc[...] = product(w)

    @pl.when(k > 0)
    def _():
        for w, acc in zip(w_refs, acc_refs):
            acc[...] += product(w)

    @pl.when(k == nk - 1)
    def _():
        epilogue([acc[...] for acc in acc_refs], e_refs, o_refs, j)


def _mm(a, w, epilogue, out_shape, out_specs, *, tm, tn, tk, nj, w_col_maps=None, w_layer=None,
        extras=(), extra_specs=(), a_buffers=None, first_tile_init=None, w_transposed=False,
        name="matmul"):
    m, kdim = a.shape
    ni, nk = m // tm, kdim // tk
    assert ni * tm == m and nk * tk == kdim
    if w_col_maps is None:
        w_col_maps = (lambda j: j,)
    n_w = len(w_col_maps)
    single = not isinstance(out_shape, (tuple, list))
    out_shapes = (out_shape,) if single else tuple(out_shape)
    out_specs_t = (out_specs,) if single else tuple(out_specs)
    a_kwargs = {} if a_buffers is None else {"pipeline_mode": pl.Buffered(a_buffers)}
    in_specs = [pl.BlockSpec((tm, tk), lambda i, j, k: (i, k), **a_kwargs)]
    for cm in w_col_maps:
        if w_transposed:
            assert w_layer is None
            in_specs.append(pl.BlockSpec((tn, tk), functools.partial(
                lambda i, j, k, cm: (cm(j), k), cm=cm)))
        elif w_layer is None:
            in_specs.append(pl.BlockSpec((tk, tn), functools.partial(
                lambda i, j, k, cm: (k, cm(j)), cm=cm)))
        else:
            in_specs.append(pl.BlockSpec((None, tk, tn), functools.partial(
                lambda i, j, k, cm: (w_layer, k, cm(j)), cm=cm)))
    in_specs += list(extra_specs)
    scratch = [pltpu.VMEM((tm, tn), F32) for _ in range(n_w)] if nk > 1 else []
    kern = functools.partial(_mm_kernel, n_w, len(extras), len(out_shapes), nk, epilogue,
                             first_tile_init, w_transposed)
    res = pl.pallas_call(
        kern,
        grid=(ni, nj, nk),
        in_specs=in_specs,
        out_specs=out_specs_t,
        out_shape=out_shapes,
        scratch_shapes=scratch,
        compiler_params=_params(("parallel", "arbitrary", "arbitrary")),
        name=name,
    )(a, *([w] * n_w), *extras)
    return res[0] if single else res


def _rope_pairs_64(x, cos2, sin2):
    lane = lax.broadcasted_iota(jnp.int32, x.shape, 1)
    first_half = (lane % MLA_ROPE) < (MLA_ROPE // 2)
    partner = jnp.where(first_half,
                        pltpu.roll(x, LANES - MLA_ROPE // 2, 1),
                        pltpu.roll(x, MLA_ROPE // 2, 1))
    return x * cos2 + partner * sin2


def _store_vt_tiles(vt_ref, lead, v):
    for c in range(v.shape[0] // VT_TILE):
        vt_ref[lead + (c,)] = v[c * VT_TILE:(c + 1) * VT_TILE, :].T.astype(vt_ref.dtype)


def _row_scale(acc, ssq_ref, d_model):
    rstd = lax.rsqrt(ssq_ref[...] * (1.0 / d_model) + NORM_EPS)
    return acc * jnp.concatenate([rstd] * (acc.shape[1] // LANES), axis=1)


def _epi_residual(accs, e_refs, o_refs, j):
    o_refs[0][...] = e_refs[0][...] + accs[0]


def _slab_rows(rows, steps):
    return next((s for s in range(16, rows + 1, 16) if rows % s == 0 and rows // s <= steps), None)


def _zero_ssq(o_refs):
    o_refs[1][...] = jnp.zeros(o_refs[1].shape, F32)


def _epi_residual_prenorm(n_gains, accs, e_refs, o_refs, j):
    h = e_refs[0][...] + accs[0]
    o_refs[0][...] = h
    ssq_ref = o_refs[1]
    ssq_ref[...] += jnp.broadcast_to(jnp.sum(h * h, axis=-1, keepdims=True), ssq_ref.shape)
    for g_ref, hb_ref in zip(e_refs[1:1 + n_gains], o_refs[2:2 + n_gains]):
        hb_ref[...] = (h * g_ref[...]).astype(hb_ref.dtype)
    for w_ref, wb_ref in zip(e_refs[1 + n_gains:], o_refs[2 + n_gains:]):
        wb_ref[...] = w_ref[...].astype(wb_ref.dtype)


def _epi_swiglu(d_model, accs, e_refs, o_refs, j):
    a = _row_scale(accs[0], e_refs[0], d_model)
    b = _row_scale(accs[1], e_refs[0], d_model)
    o_refs[0][...] = (a * jax.nn.sigmoid(a) * b).astype(o_refs[0].dtype)
    if len(o_refs) > 1:
        o_refs[1][...] = e_refs[1][...].astype(o_refs[1].dtype)


def _epi_sigmoid(d_model, accs, e_refs, o_refs, j):
    o_refs[0][...] = jax.nn.sigmoid(_row_scale(accs[0], e_refs[0], d_model))


def _epi_mla_in(d_model, accs, e_refs, o_refs, j):
    ssq_ref, gq_ref, gkv_ref, cos_ref, sin_ref = e_refs
    u = _row_scale(accs[0], ssq_ref, d_model)
    cq = u[:, :MLA_Q_LORA]
    ms = jnp.mean(cq * cq, axis=-1, keepdims=True)
    o_refs[0][...] = (cq * lax.rsqrt(ms + NORM_EPS) * gq_ref[...]).astype(BF16)
    ckv = u[:, MLA_Q_LORA:MLA_Q_LORA + MLA_KV_LORA]
    ms = jnp.mean(ckv * ckv, axis=-1, keepdims=True)
    o_refs[1][...] = (ckv * lax.rsqrt(ms + NORM_EPS) * gkv_ref[...]).astype(BF16)
    kr = u[:, MLA_Q_LORA + MLA_KV_LORA:]
    o_refs[2][...] = _rope_pairs_64(kr, cos_ref[...], sin_ref[...]).astype(BF16)


def _epi_mla_q(n_nope_tiles, scale, accs, e_refs, o_refs, j):
    cos_ref, sin_ref = e_refs
    acc = accs[0]

    @pl.when(j < n_nope_tiles)
    def _():
        o_refs[0][...] = (acc * scale).astype(BF16)

    @pl.when(j >= n_nope_tiles)
    def _():
        for c in range(acc.shape[1] // LANES):
            x = acc[:, c * LANES:(c + 1) * LANES]
            r = _rope_pairs_64(x, cos_ref[...], sin_ref[...])
            o_refs[0][:, c * LANES:(c + 1) * LANES] = (r * scale).astype(BF16)


def _epi_mla_kv(accs, e_refs, o_refs, j):
    acc = accs[0]
    kn_ref, vt_ref = o_refs
    per = MLA_NOPE + MLA_V
    for r in range(acc.shape[1] // per):
        kn_ref[:, r * MLA_NOPE:(r + 1) * MLA_NOPE] = acc[:, r * per:r * per + MLA_NOPE].astype(BF16)
        _store_vt_tiles(vt_ref, (r,), acc[:, r * per + MLA_NOPE:(r + 1) * per])


def _rope_192(x, cos, sin):
    width = x.shape[1]
    lane = lax.broadcasted_iota(jnp.int32, x.shape, 1)
    partner = jnp.where((lane % NSA_DK) < NSA_HALF,
                        pltpu.roll(x, width - NSA_HALF, 1),
                        pltpu.roll(x, NSA_HALF, 1))
    return x * cos + partner * sin


def _epi_nsa_rope(d_model, scale, accs, e_refs, o_refs, j):
    ssq_ref, cos_ref, sin_ref = e_refs
    acc = _row_scale(accs[0], ssq_ref, d_model)
    reps = acc.shape[1] // NSA_DK
    roped = _rope_192(acc, jnp.concatenate([cos_ref[...]] * reps, axis=1),
                      jnp.concatenate([sin_ref[...]] * reps, axis=1))
    if scale is not None:
        roped = roped * scale
    for r in range(acc.shape[1] // NSA_DK):
        o_refs[0][r] = roped[:, r * NSA_DK:(r + 1) * NSA_DK].astype(BF16)


def _epi_nsa_v(d_model, accs, e_refs, o_refs, j):
    v_ref, vt_ref = o_refs
    acc = _row_scale(accs[0], e_refs[0], d_model)
    for r in range(acc.shape[1] // NSA_DV):
        v = acc[:, r * NSA_DV:(r + 1) * NSA_DV]
        v_ref[r] = v.astype(BF16)
        _store_vt_tiles(vt_ref, (r,), v)


SUM_ROWS = 16


def _attend_t(q, k, vt, bias):
    dv = vt.shape[0]
    s = lax.dot_general(k, q, NT_DIMS, preferred_element_type=F32)
    bias = jnp.concatenate([bias] * (s.shape[1] // bias.shape[1]), axis=1)
    free = s.shape[0] - bias.shape[0]
    s = s + bias if free == 0 else jnp.concatenate([s[:free], s[free:] + bias], axis=0)
    p = jnp.exp2(s - jnp.max(s, axis=0, keepdims=True)).astype(BF16)
    vt1 = jnp.concatenate([vt, jnp.ones((SUM_ROWS, vt.shape[1]), BF16)], axis=0)
    acc = jnp.dot(vt1, p, preferred_element_type=F32)
    return acc[:dv] / acc[dv:dv + 1]


def _vt_tiles(vt_ref, lead, c0, n):
    return jnp.concatenate([vt_ref[lead + (c0 + c,)] for c in range(n)], axis=1)


def _mla_attn_kernel(t_blk, seq, qn_ref, qr_ref, kn_ref, kr_ref, vt_ref, o_ref):
    lane_half = lax.broadcasted_iota(jnp.int32, (t_blk, LANES), 1) // MLA_ROPE
    kpos = lax.broadcasted_iota(jnp.int32, (t_blk, t_blk), 0)
    qpos = lax.broadcasted_iota(jnp.int32, (t_blk, t_blk), 1)
    diag_bias = jnp.where(kpos <= qpos, 0.0, NEG_BIG).astype(F32)

    for qi in range(seq // t_blk):
        q0 = qi * t_blk
        n_keys = q0 + t_blk
        qr_pair = qr_ref[q0:q0 + t_blk, :]
        for hh in range(2):
            qr = jnp.where(lane_half == hh, qr_pair, jnp.zeros_like(qr_pair))
            q = jnp.concatenate([qn_ref[q0:q0 + t_blk, hh * MLA_NOPE:(hh + 1) * MLA_NOPE], qr], axis=1)
            k = jnp.concatenate([kn_ref[0:n_keys, hh * MLA_NOPE:(hh + 1) * MLA_NOPE],
                                 kr_ref[0:n_keys, :]], axis=1)
            vt = _vt_tiles(vt_ref, (hh,), 0, n_keys // VT_TILE)
            o_ref[q0:q0 + t_blk, hh * MLA_V:(hh + 1) * MLA_V] = (
                _attend_t(q, k, vt, diag_bias).T.astype(o_ref.dtype))


def _mla_attention(q, kn, kr, vt, batch, seq, heads, t_blk=512):
    t = batch * seq
    n_nope_blk = heads * MLA_NOPE // LANES
    n_vt = seq // VT_TILE
    kern = functools.partial(_mla_attn_kernel, t_blk, seq)
    return pl.pallas_call(
        kern,
        grid=(batch, heads // 2),
        in_specs=[
            pl.BlockSpec((seq, 2 * MLA_NOPE), lambda b, hp: (b, hp)),
            pl.BlockSpec((seq, LANES), lambda b, hp: (b, n_nope_blk + hp)),
            pl.BlockSpec((seq, 2 * MLA_NOPE), lambda b, hp: (b, hp)),
            pl.BlockSpec((seq, LANES), lambda b, hp: (b, 0)),
            pl.BlockSpec((2, n_vt, MLA_V, VT_TILE), lambda b, hp: (hp, b, 0, 0)),
        ],
        out_specs=pl.BlockSpec((seq, 2 * MLA_V), lambda b, hp: (b, hp)),
        out_shape=jax.ShapeDtypeStruct((t, heads * MLA_V), BF16),
        compiler_params=_params(("parallel", "parallel")),
        name="mla_attention",
    )(q, q, kn, kr, vt)


def _compress_mlp(a, w1_ref, pe_ref, w2_ref, transposed_out):
    y_lo = jnp.dot(a, w1_ref[0], preferred_element_type=F32)
    y_hi = jnp.dot(a, w1_ref[1], preferred_element_type=F32)
    pe_c = (jnp.dot(pe_ref[0], w1_ref[0], preferred_element_type=F32)
            + jnp.dot(pe_ref[1], w1_ref[1], preferred_element_type=F32))
    n = y_hi.shape[0]
    hid = y_lo + pltpu.roll(y_hi, n - 1, 0) + pe_c[0:1, :]
    hid = jax.nn.gelu(hid, approximate=True)
    if transposed_out:
        return lax.dot_general(w2_ref[...], hid.astype(BF16), NT_DIMS, preferred_element_type=F32)
    return jnp.dot(hid.astype(BF16), w2_ref[...], preferred_element_type=F32)


def _compress_kernel(ak_ref, av_ref, w1k_ref, pek_ref, w2k_ref, w1v_ref, pev_ref, w2vt_ref,
                     kc_ref, vct_ref):
    kc_ref[0] = _compress_mlp(ak_ref[0], w1k_ref, pek_ref, w2k_ref, False).astype(BF16)
    vct_ref[0] = _compress_mlp(av_ref[0], w1v_ref, pev_ref, w2vt_ref, True).astype(BF16)


def _compress(ak, av, wk, wv, batch, seq):
    n_chunk = seq // CMP_STRIDE
    const3 = lambda b, g: (0, 0, 0)
    const2 = lambda b, g: (0, 0)

    def wspecs(w):
        w1, pe, w2 = w
        return [pl.BlockSpec(w1.shape, const3), pl.BlockSpec(pe.shape, const3),
                pl.BlockSpec(w2.shape, const2)]

    return pl.pallas_call(
        _compress_kernel,
        grid=(batch, NSA_G),
        in_specs=[pl.BlockSpec((1, n_chunk, ak.shape[2]), lambda b, g: (g, b, 0)),
                  pl.BlockSpec((1, n_chunk, av.shape[2]), lambda b, g: (g, b, 0))]
        + wspecs(wk) + wspecs(wv),
        out_specs=(pl.BlockSpec((1, n_chunk, NSA_DK), lambda b, g: (b * NSA_G + g, 0, 0)),
                   pl.BlockSpec((1, NSA_DV, n_chunk), lambda b, g: (b * NSA_G + g, 0, 0))),
        out_shape=(jax.ShapeDtypeStruct((batch * NSA_G, n_chunk, NSA_DK), BF16),
                   jax.ShapeDtypeStruct((batch * NSA_G, NSA_DV, n_chunk), BF16)),
        compiler_params=_params(("parallel", "parallel")),
        name="nsa_compress",
    )(ak, av, *wk, *wv)


def _nsa_attn_kernel(hpg, tq, tk, seq, q_ref, kc_ref, vct_ref, ks_ref, vst_ref, kw_ref, vwt_ref,
                     gate_ref, ovl_ref, o_ref, bias_ref, oc_ref, os_ref, gt_ref):
    qi = pl.program_id(2)
    t0 = qi * tq
    n_rows = hpg * tq
    n_cmp = kc_ref.shape[1]
    n_sel = seq // SEL_BLOCK
    n_vt = tk // VT_TILE
    q = q_ref[...].reshape(n_rows, NSA_DK)

    def tile_heads(x):
        return jnp.concatenate([x] * hpg, axis=1)

    s = lax.dot_general(kc_ref[0], q, NT_DIMS, preferred_element_type=F32)
    cidx = lax.broadcasted_iota(jnp.int32, (n_cmp, tq), 0)
    tpos_c = t0 + lax.broadcasted_iota(jnp.int32, (n_cmp, tq), 1)
    mask_c = tile_heads((cidx * CMP_STRIDE + CMP_BLOCK - 1) <= tpos_c)
    s = jnp.where(mask_c, s, NEG_BIG)
    mx = jnp.max(s, axis=0, keepdims=True)
    e = jnp.where(mask_c, jnp.exp2(s - mx), 0.0)
    p = e / jnp.maximum(jnp.sum(e, axis=0, keepdims=True), 1e-30)
    oc_ref[...] = jnp.dot(vct_ref[0], p.astype(BF16), preferred_element_type=F32)

    psum = p[:, 0:tq]
    for h in range(1, hpg):
        psum = psum + p[:, h * tq:(h + 1) * tq]
    p1 = psum.astype(BF16)
    p2 = (psum - p1.astype(F32)).astype(BF16)
    p3 = (psum - p1.astype(F32) - p2.astype(F32)).astype(BF16)
    ovl = ovl_ref[...]
    imp = (jnp.dot(ovl, p1, preferred_element_type=F32)
           + jnp.dot(ovl, p2, preferred_element_type=F32)
           + jnp.dot(ovl, p3, preferred_element_type=F32))
    jrow = lax.broadcasted_iota(jnp.int32, (n_sel, tq), 0)
    tpos = t0 + lax.broadcasted_iota(jnp.int32, (n_sel, tq), 1)
    valid = jrow * SEL_BLOCK <= tpos
    cur = jnp.right_shift(tpos, SEL_BLOCK.bit_length() - 1)
    forced = valid & ((jrow == 0) | (jrow == cur) | (jrow == cur - 1))
    score = jnp.where(valid, imp + jnp.where(forced, FORCE_BONUS, 0.0), -jnp.inf)
    rank = jnp.zeros((n_sel, tq), jnp.int32)
    for i in range(n_sel):
        row = score[i:i + 1, :]
        beats = (row > score) | ((row == score) & (jrow > i))
        rank = rank + beats.astype(jnp.int32)
    sel_bias = jnp.where(valid & (rank < SEL_TOPK), 0.0, NEG_BIG).astype(F32)

    tk_s = bias_ref.shape[1]
    n_sel_tiles = (t0 + tq - 1) // tk_s + 1
    blocks_per_tile = tk_s // SEL_BLOCK
    for jj in range(seq // tk_s):
        @pl.when(jj < n_sel_tiles)
        def _():
            rows = [jnp.broadcast_to(sel_bias[b:b + 1, :], (SEL_BLOCK, tq))
                    for b in range(jj * blocks_per_tile, (jj + 1) * blocks_per_tile)]
            kpos = jj * tk_s + lax.broadcasted_iota(jnp.int32, (tk_s, tq), 0)
            tq_pos = t0 + lax.broadcasted_iota(jnp.int32, (tk_s, tq), 1)
            bias_ref[jj] = jnp.where(kpos <= tq_pos, jnp.concatenate(rows, axis=0), NEG_BIG)

    for n in range(1, seq // tk_s + 1):
        @pl.when(n_sel_tiles == n)
        def _():
            keys = n * tk_s
            vt = _vt_tiles(vst_ref, (0,), 0, keys // VT_TILE)
            os_ref[...] = _attend_t(q, ks_ref[0, 0:keys, :], vt, bias_ref[0:n].reshape(keys, tq))

    k0 = pl.multiple_of(jnp.maximum(t0 + tq - tk, 0), VT_TILE)
    kpos = k0 + lax.broadcasted_iota(jnp.int32, (tk, tq), 0)
    tq_pos = t0 + lax.broadcasted_iota(jnp.int32, (tk, tq), 1)
    ok = (kpos <= tq_pos) & (tq_pos - kpos < WINDOW)
    o_w = _attend_t(q, kw_ref[0, pl.ds(k0, tk), :], _vt_tiles(vwt_ref, (0,), k0 // VT_TILE, n_vt),
                    jnp.where(ok, 0.0, NEG_BIG).astype(F32))

    gt_ref[...] = gate_ref[...].T
    base = pl.program_id(1) * (3 * hpg)
    for h in range(hpg):
        cols = slice(h * tq, (h + 1) * tq)
        o = (gt_ref[pl.ds(base + 3 * h, 1), :] * oc_ref[:, cols]
             + gt_ref[pl.ds(base + 3 * h + 1, 1), :] * os_ref[:, cols]
             + gt_ref[pl.ds(base + 3 * h + 2, 1), :] * o_w[:, cols])
        o_ref[:, h * NSA_DV:(h + 1) * NSA_DV] = o.T.astype(o_ref.dtype)


def _nsa_attention(q, kc, vct, k3, vt3, gates, ovl, batch, seq, hpg, tq=128, tk_sel=512):
    tk = WINDOW + tq
    t = batch * seq
    nq = seq // tq
    n_chunk = seq // CMP_STRIDE
    n_rows = hpg * tq
    n_vt = seq // VT_TILE
    assert tk % VT_TILE == 0 and seq >= tk and seq % tk_sel == 0
    kern = functools.partial(_nsa_attn_kernel, hpg, tq, tk, seq)

    def k_spec(branch):
        return pl.BlockSpec((1, seq, NSA_DK), lambda b, g, qi: (branch * NSA_G + g, b, 0))

    def vt_spec(branch):
        return pl.BlockSpec((1, n_vt, NSA_DV, VT_TILE), lambda b, g, qi: (branch * NSA_G + g, b, 0, 0))

    return pl.pallas_call(
        kern,
        grid=(batch, NSA_G, nq),
        in_specs=[
            pl.BlockSpec((hpg, tq, NSA_DK), lambda b, g, qi: (g, b * nq + qi, 0)),
            pl.BlockSpec((1, n_chunk, NSA_DK), lambda b, g, qi: (b * NSA_G + g, 0, 0)),
            pl.BlockSpec((1, NSA_DV, n_chunk), lambda b, g, qi: (b * NSA_G + g, 0, 0)),
            k_spec(1), vt_spec(1), k_spec(2), vt_spec(2),
            pl.BlockSpec((tq, LANES), lambda b, g, qi: (b * nq + qi, 0)),
            pl.BlockSpec(ovl.shape, lambda b, g, qi: (0, 0)),
        ],
        out_specs=pl.BlockSpec((tq, hpg * NSA_DV), lambda b, g, qi: (b * nq + qi, g)),
        out_shape=jax.ShapeDtypeStruct((t, NSA_G * hpg * NSA_DV), BF16),
        scratch_shapes=[
            pltpu.VMEM((seq // tk_sel, tk_sel, tq), F32),
            pltpu.VMEM((NSA_DV, n_rows), F32),
            pltpu.VMEM((NSA_DV, n_rows), F32),
            pltpu.VMEM((LANES, tq), F32),
        ],
        compiler_params=_params(("parallel", "parallel", "arbitrary")),
        name="nsa_attention",
    )(q, kc, vct, k3, vt3, k3, vt3, gates, ovl)


def _rope_tables(pos_flat):
    pos = pos_flat.astype(F32)[:, None]
    inv64 = ROPE_THETA ** (-jnp.arange(0, MLA_ROPE, 2, dtype=F32) / MLA_ROPE)
    ang = pos * inv64
    c, s = jnp.cos(ang), jnp.sin(ang)
    cos_mla = jnp.concatenate([c, c, c, c], axis=1)
    sin_mla = jnp.concatenate([-s, s, -s, s], axis=1)
    inv192 = ROPE_THETA ** (-jnp.arange(0, NSA_DK, 2, dtype=F32) / NSA_DK)
    ang = pos * inv192
    c, s = jnp.cos(ang), jnp.sin(ang)
    cos_nsa = jnp.concatenate([c, c], axis=1)
    sin_nsa = jnp.concatenate([-s, s], axis=1)
    return cos_mla, sin_mla, cos_nsa, sin_nsa


def _compress_weights(w1, pe, w2, *, transpose_w2):
    width, d_hid = w1.shape[0] // CMP_BLOCK, w1.shape[1]
    hid_p = -(-d_hid // LANES) * LANES
    w1 = jnp.pad(w1, ((0, 0), (0, hid_p - d_hid)))
    w1 = w1.reshape(2, CMP_STRIDE * width, hid_p).astype(BF16)
    pe = jnp.pad(pe.reshape(2, 1, CMP_STRIDE * width), ((0, 0), (0, 7), (0, 0))).astype(BF16)
    w2 = jnp.pad(w2, ((0, hid_p - d_hid), (0, 0))).astype(BF16)
    return w1, pe, (w2.T if transpose_w2 else w2)


def kernel(x, positions, a_norm, a_w_in, a_q_norm, a_w_uq, a_kv_norm, a_w_ukv, a_w_o, s_norm, s_w_kv, s_cmp_pe_k, s_cmp_w1_k, s_cmp_w2_k, s_cmp_pe_v, s_cmp_w1_v, s_cmp_w2_v, b_norm, b_w_in, b_w_o, f_norm, f_w_in, f_w_out, final_norm):
    batch, seq, d = x.shape
    t = batch * seq
    heads = d // 128
    hpg = heads // NSA_G
    d_ff = f_w_out.shape[1]
    n_a, n_b = a_w_in.shape[0], b_w_in.shape[0]
    assert n_a >= 1 and n_b >= 1
    tm = 1024
    sds = jax.ShapeDtypeStruct

    cos_mla, sin_mla, cos_nsa, sin_nsa = _rope_tables(positions.reshape(t))
    row = lambda i, j, k: (i, 0)
    tile = lambda i, j, k: (i, j)
    tab_spec = pl.BlockSpec((tm, LANES), row)
    ssq_spec = pl.BlockSpec((tm, LANES), row)

    def residual_mm(a, w, w_layer, h, gains, *, tm_, tn, name, a_buffers=None, to_round=()):
        blk = pl.BlockSpec((tm_, tn), tile)
        nj = d // tn
        common = dict(tm=tm_, tn=tn, tk=a.shape[1], nj=nj, w_layer=w_layer, name=name,
                      a_buffers=a_buffers)
        if not gains:
            assert not to_round
            h_new = _mm(a, w, _epi_residual, sds((t, d), F32), blk,
                        extras=(h,), extra_specs=(blk,), **common)
            return h_new, None, (), ()
        steps = (t // tm_) * nj
        slabs_in, slabs_out, flats = [], [], []
        for wr, layer in to_round:
            rows, cols = wr.shape[1:]
            slab = _slab_rows(rows, steps)
            n_slabs = rows // slab
            flats.append(wr.reshape(-1, cols))
            step_of = functools.partial(lambda i, j, n: jnp.minimum(i * nj + j, n - 1), n=n_slabs)
            slabs_in.append(pl.BlockSpec(
                (slab, cols), functools.partial(lambda i, j, k, off, s: (off + s(i, j), 0),
                                                off=layer * n_slabs, s=step_of)))
            slabs_out.append(pl.BlockSpec(
                (slab, cols), functools.partial(lambda i, j, k, s: (s(i, j), 0), s=step_of)))
        outs = _mm(a, w, functools.partial(_epi_residual_prenorm, len(gains)),
                   (sds((t, d), F32), sds((t, LANES), F32)) + tuple(sds((t, d), BF16) for _ in gains)
                   + tuple(sds(wr.shape[1:], BF16) for wr, _ in to_round),
                   (blk, pl.BlockSpec((tm_, LANES), row)) + tuple(blk for _ in gains)
                   + tuple(slabs_out),
                   extras=(h,) + tuple(g.reshape(1, d) for g in gains) + tuple(flats),
                   extra_specs=(blk,) + tuple(pl.BlockSpec((1, tn), lambda i, j, k: (0, j))
                                              for _ in gains) + tuple(slabs_in),
                   first_tile_init=_zero_ssq, **common)
        n_g = len(gains)
        return outs[0], outs[1], tuple(outs[2:2 + n_g]), tuple(outs[2 + n_g:])

    def ffn(hb, ssq, h, layer, gains, to_round=()):
        tn = 256
        nj = d_ff // tn
        tm_in = min(2 * tm, t)
        steps = (t // tm_in) * nj
        slab = d_ff // steps
        assert slab * steps == d_ff and slab % 16 == 0
        act, w_out = _mm(
            hb, f_w_in, functools.partial(_epi_swiglu, d),
            (sds((t, d_ff), BF16), sds((d_ff, d), BF16)),
            (pl.BlockSpec((tm_in, tn), tile),
             pl.BlockSpec((slab, d), lambda i, j, k: (i * nj + j, 0))),
            tm=tm_in, tn=tn, tk=d, nj=nj, w_layer=layer, a_buffers=1,
            w_col_maps=(lambda j: j, lambda j: j + nj),
            extras=(ssq, f_w_out),
            extra_specs=(pl.BlockSpec((tm_in, LANES), row),
                         pl.BlockSpec((None, slab, d), lambda i, j, k: (layer, i * nj + j, 0))),
            name="ffn_in")
        return residual_mm(act, w_out, None, h, gains, tm_=tm, tn=tn, name="ffn_out",
                           a_buffers=1, to_round=to_round)

    b_w_in_t = jnp.swapaxes(b_w_in, 1, 2)
    round_w_in_t = _slab_rows(b_w_in_t.shape[1], (t // tm) * (d // 256)) is not None

    h = x.reshape(t, d)
    hb, ssq = _prenorm(h, a_norm[0])

    for i in range(n_a):
        rope_lo = MLA_Q_LORA + MLA_KV_LORA
        w_in = jnp.concatenate([a_w_in[i], a_w_in[i][:, rope_lo:]], axis=1).astype(BF16)
        n_in = w_in.shape[1]
        cq, ckv, kr = _mm(
            hb, w_in, functools.partial(_epi_mla_in, d),
            (sds((t, MLA_Q_LORA), BF16), sds((t, MLA_KV_LORA), BF16), sds((t, LANES), BF16)),
            (pl.BlockSpec((tm, MLA_Q_LORA), row), pl.BlockSpec((tm, MLA_KV_LORA), row),
             pl.BlockSpec((tm, LANES), row)),
            tm=tm, tn=n_in, tk=1024, nj=1,
            extras=(ssq, a_q_norm[i].reshape(1, -1), a_kv_norm[i].reshape(1, -1), cos_mla, sin_mla),
            extra_specs=(ssq_spec,
                         pl.BlockSpec((1, MLA_Q_LORA), lambda i_, j, k: (0, 0)),
                         pl.BlockSpec((1, MLA_KV_LORA), lambda i_, j, k: (0, 0)),
                         tab_spec, tab_spec),
            name="mla_in")

        qd = MLA_NOPE + MLA_ROPE
        w_uq = a_w_uq[i].reshape(MLA_Q_LORA, heads, qd)
        w_uq = jnp.concatenate([w_uq[:, :, :MLA_NOPE].reshape(MLA_Q_LORA, heads * MLA_NOPE),
                                w_uq[:, :, MLA_NOPE:].reshape(MLA_Q_LORA, heads * MLA_ROPE)],
                               axis=1).astype(BF16)
        tn = 512
        q = _mm(cq, w_uq,
                functools.partial(_epi_mla_q, heads * MLA_NOPE // tn, qd ** -0.5 * LOG2_E),
                jax.ShapeDtypeStruct((t, heads * qd), BF16),
                pl.BlockSpec((tm, tn), lambda i_, j, k: (i_, j)),
                tm=tm, tn=tn, tk=MLA_Q_LORA, nj=heads * qd // tn,
                extras=(cos_mla, sin_mla), extra_specs=(tab_spec, tab_spec), name="mla_q")
        heads_per_tile = 4
        kn, vt = _mm(ckv, a_w_ukv[i].astype(BF16), _epi_mla_kv,
                     (jax.ShapeDtypeStruct((t, heads * MLA_NOPE), BF16),
                      jax.ShapeDtypeStruct((heads, t // VT_TILE, MLA_V, VT_TILE), BF16)),
                     (pl.BlockSpec((tm, heads_per_tile * MLA_NOPE), lambda i_, j, k: (i_, j)),
                      pl.BlockSpec((heads_per_tile, tm // VT_TILE, MLA_V, VT_TILE),
                                   lambda i_, j, k: (j, i_, 0, 0))),
                     tm=tm, tn=heads_per_tile * (MLA_NOPE + MLA_V), tk=MLA_KV_LORA,
                     nj=heads // heads_per_tile, name="mla_kv")
        o = _mla_attention(q, kn, kr, vt, batch, seq, heads)
        h, ssq, (hb,), _ = residual_mm(o, a_w_o[i].astype(BF16), None, h, [f_norm[i]],
                                       tm_=tm, tn=512, name="mla_out")
        last_a = i + 1 == n_a
        gains = [s_norm, b_norm[0]] if last_a else [a_norm[i + 1]]
        to_round = ()
        if last_a:
            to_round = ((s_w_kv[None], 0), (b_w_o, 0)) + (((b_w_in_t, 0),) if round_w_in_t else ())
        h, ssq, hbs, rounded = ffn(hb, ssq, h, i, gains, to_round)
        hb = hbs[0]

    hb_s, hb_b = hbs
    s_w_kv_b, b_w_o0 = rounded[:2]
    b_w_in_t0 = rounded[2] if round_w_in_t else b_w_in_t[0].astype(BF16)
    n_sets = 3 * NSA_G
    per_tile = NSA_ROPE_TILE // NSA_DK
    tm_rope = tm
    rope_tab = pl.BlockSpec((tm_rope, NSA_DK), row)
    w_kv = s_w_kv_b.reshape(d, n_sets, NSA_DK + NSA_DV)
    w_k = w_kv[:, :, :NSA_DK].reshape(d, n_sets * NSA_DK)
    w_v = w_kv[:, :, NSA_DK:].reshape(d, n_sets * NSA_DV)
    k3 = _mm(hb_s, w_k, functools.partial(_epi_nsa_rope, d, None),
             sds((n_sets, t, NSA_DK), BF16),
             pl.BlockSpec((per_tile, tm_rope, NSA_DK), lambda i_, j, k: (j, i_, 0)),
             tm=tm_rope, tn=NSA_ROPE_TILE, tk=d, nj=n_sets // per_tile,
             extras=(ssq, cos_nsa, sin_nsa),
             extra_specs=(pl.BlockSpec((tm_rope, LANES), row), rope_tab, rope_tab),
             a_buffers=1, name="nsa_k")
    v3, vt3 = _mm(hb_s, w_v, functools.partial(_epi_nsa_v, d),
                  (sds((n_sets, t, NSA_DV), BF16),
                   sds((n_sets, t // VT_TILE, NSA_DV, VT_TILE), BF16)),
                  (pl.BlockSpec((per_tile, tm, NSA_DV), lambda i_, j, k: (j, i_, 0)),
                   pl.BlockSpec((per_tile, tm // VT_TILE, NSA_DV, VT_TILE),
                                lambda i_, j, k: (j, i_, 0, 0))),
                  tm=tm, tn=per_tile * NSA_DV, tk=d, nj=n_sets // per_tile,
                  extras=(ssq,), extra_specs=(ssq_spec,), name="nsa_v")

    ak = k3[:NSA_G].reshape(NSA_G, t // CMP_STRIDE, CMP_STRIDE * NSA_DK)
    av = v3[:NSA_G].reshape(NSA_G, t // CMP_STRIDE, CMP_STRIDE * NSA_DV)
    kc, vct = _compress(ak, av,
                        _compress_weights(s_cmp_w1_k, s_cmp_pe_k, s_cmp_w2_k, transpose_w2=False),
                        _compress_weights(s_cmp_w1_v, s_cmp_pe_v, s_cmp_w2_v, transpose_w2=True),
                        batch, seq)

    n_chunk = seq // CMP_STRIDE
    n_sel = seq // SEL_BLOCK
    c_start = np.arange(n_chunk) * CMP_STRIDE
    j_start = np.arange(n_sel) * SEL_BLOCK
    ovl = ((c_start[None, :] < j_start[:, None] + SEL_BLOCK)
           & (c_start[None, :] + CMP_BLOCK > j_start[:, None])
           & (np.arange(n_chunk)[None, :] < (seq - CMP_BLOCK) // CMP_STRIDE + 1))
    ovl = jnp.asarray(ovl, BF16)

    for i in range(n_b):
        n_q = heads * NSA_DK
        w_qt = b_w_in_t0 if i == 0 else b_w_in_t[i].astype(BF16)
        w_o = b_w_o0 if i == 0 else b_w_o[i].astype(BF16)
        q = _mm(hb_b, w_qt, functools.partial(_epi_nsa_rope, d, NSA_DK ** -0.5 * LOG2_E),
                sds((heads, t, NSA_DK), BF16),
                pl.BlockSpec((per_tile, tm_rope, NSA_DK), lambda i_, j, k: (j, i_, 0)),
                tm=tm_rope, tn=NSA_ROPE_TILE, tk=d, nj=heads // per_tile,
                extras=(ssq, cos_nsa, sin_nsa),
                extra_specs=(pl.BlockSpec((tm_rope, LANES), row), rope_tab, rope_tab),
                a_buffers=1, w_transposed=True, name="nsa_q")
        assert n_q % LANES == 0 and 3 * heads <= LANES
        gates = _mm(hb_b, w_qt, functools.partial(_epi_sigmoid, d), sds((t, LANES), F32),
                    pl.BlockSpec((tm, LANES), row),
                    tm=tm, tn=LANES, tk=d, nj=1, w_col_maps=(lambda j: n_q // LANES,),
                    extras=(ssq,), extra_specs=(ssq_spec,), w_transposed=True, name="nsa_gates")
        o = _nsa_attention(q, kc, vct, k3, vt3, gates, ovl, batch, seq, hpg)
        h, ssq, (hb,), _ = residual_mm(o, w_o, None, h, [f_norm[n_a + i]],
                                       tm_=tm, tn=512, name="nsa_out")
        gains = [b_norm[i + 1]] if i + 1 < n_b else []
        h, ssq, hbs, _ = ffn(hb, ssq, h, n_a + i, gains)
        if hbs:
            hb_b = hbs[0]

    out = _rmsnorm(h, final_norm, F32)
    return out.reshape(batch, seq, d)
```

```python
import functools

import numpy as np
import jax
import jax.numpy as jnp
from jax import lax
from jax.experimental import pallas as pl
from jax.experimental.pallas import tpu as pltpu

F32 = jnp.float32
BF16 = jnp.bfloat16

LANES = 128
VMEM_LIMIT_BYTES = 56 * 1024 * 1024

NORM_EPS = 1e-6
ROPE_THETA = 10000.0

MLA_Q_LORA = 1536
MLA_KV_LORA = 512
MLA_NOPE = 128
MLA_ROPE = 64
MLA_V = 128

NSA_G = 4
NSA_DK = 192
NSA_DV = 128
NSA_HALF = NSA_DK // 2
NSA_ROPE_TILE = 4 * NSA_DK
CMP_BLOCK = 32
CMP_STRIDE = 16
SEL_BLOCK = 64
SEL_TOPK = 16
WINDOW = 512
FORCE_BONUS = 1e4
NEG_BIG = -1e30
LOG2_E = 1.4426950408889634

VT_TILE = LANES
NT_DIMS = (((1,), (1,)), ((), ()))


def _params(sem):
    return pltpu.CompilerParams(dimension_semantics=sem, vmem_limit_bytes=VMEM_LIMIT_BYTES)


def _rmsnorm_kernel(x_ref, g_ref, o_ref):
    x = x_ref[...]
    ms = jnp.mean(x * x, axis=-1, keepdims=True)
    o_ref[...] = (x * lax.rsqrt(ms + NORM_EPS) * g_ref[...]).astype(o_ref.dtype)


def _rmsnorm(x, g, out_dtype, tm=256):
    t, d = x.shape
    return pl.pallas_call(
        _rmsnorm_kernel,
        grid=(t // tm,),
        in_specs=[pl.BlockSpec((tm, d), lambda i: (i, 0)),
                  pl.BlockSpec((1, d), lambda i: (0, 0))],
        out_specs=pl.BlockSpec((tm, d), lambda i: (i, 0)),
        out_shape=jax.ShapeDtypeStruct((t, d), out_dtype),
        compiler_params=_params(("parallel",)),
        name="rmsnorm",
    )(x, g.reshape(1, d))


def _mm_kernel(n_w, n_e, n_o, nk, epilogue, first_tile_init, w_transposed, prenorm, *refs):
    a_ref = refs[0]
    w_refs = refs[1:1 + n_w]
    e_refs = refs[1 + n_w:1 + n_w + n_e]
    o_refs = refs[1 + n_w + n_e:1 + n_w + n_e + n_o]
    acc_refs = refs[1 + n_w + n_e + n_o:]
    j = pl.program_id(1)
    if first_tile_init is not None:
        pl.when((j == 0) & (pl.program_id(2) == 0))(lambda: first_tile_init(o_refs))

    if prenorm:
        g_ref, ssq_ref, acc_refs = e_refs[0], acc_refs[-1], acc_refs[:-1]
        e_refs = (ssq_ref,) + tuple(e_refs[1:])
        xa = a_ref[...]
        part = jnp.broadcast_to(jnp.sum(xa * xa, axis=-1, keepdims=True), ssq_ref.shape)
        first = pl.program_id(2) == 0

        @pl.when(first)
        def _():
            ssq_ref[...] = part

        @pl.when(jnp.logical_not(first))
        def _():
            ssq_ref[...] += part

        a_val = (xa * g_ref[...]).astype(BF16)
    else:
        a_val = a_ref[...]

    def product(w):
        wv = w[...].astype(a_val.dtype)
        if w_transposed:
            return lax.dot_general(a_val, wv, NT_DIMS, preferred_element_type=F32)
        return jnp.dot(a_val, wv, preferred_element_type=F32)

    if nk == 1:
        epilogue([product(w) for w in w_refs], e_refs, o_refs, j)
        return
    k = pl.program_id(2)

    @pl.when(k == 0)
    def _():
        for w, acc in zip(w_refs, acc_refs):
            acc[...] = product(w)

    @pl.when(k > 0)
    def _():
        for w, acc in zip(w_refs, acc_refs):
            acc[...] += product(w)

    @pl.when(k == nk - 1)
    def _():
        epilogue([acc[...] for acc in acc_refs], e_refs, o_refs, j)


def _mm(a, w, epilogue, out_shape, out_specs, *, tm, tn, tk, nj, w_col_maps=None, w_layer=None,
        extras=(), extra_specs=(), a_buffers=None, first_tile_init=None, w_transposed=False,
        prenorm=False, name="matmul"):
    m, kdim = a.shape
    ni, nk = m // tm, kdim // tk
    assert ni * tm == m and nk * tk == kdim
    if w_col_maps is None:
        w_col_maps = (lambda j: j,)
    n_w = len(w_col_maps)
    single = not isinstance(out_shape, (tuple, list))
    out_shapes = (out_shape,) if single else tuple(out_shape)
    out_specs_t = (out_specs,) if single else tuple(out_specs)
    a_kwargs = {} if a_buffers is None else {"pipeline_mode": pl.Buffered(a_buffers)}
    in_specs = [pl.BlockSpec((tm, tk), lambda i, j, k: (i, k), **a_kwargs)]
    for cm in w_col_maps:
        if w_transposed:
            assert w_layer is None
            in_specs.append(pl.BlockSpec((tn, tk), functools.partial(
                lambda i, j, k, cm: (cm(j), k), cm=cm)))
        elif w_layer is None:
            in_specs.append(pl.BlockSpec((tk, tn), functools.partial(
                lambda i, j, k, cm: (k, cm(j)), cm=cm)))
        else:
            in_specs.append(pl.BlockSpec((None, tk, tn), functools.partial(
                lambda i, j, k, cm: (w_layer, k, cm(j)), cm=cm)))
    in_specs += list(extra_specs)
    scratch = [pltpu.VMEM((tm, tn), F32) for _ in range(n_w)] if nk > 1 else []
    if prenorm:
        scratch.append(pltpu.VMEM((tm, LANES), F32))
    kern = functools.partial(_mm_kernel, n_w, len(extras), len(out_shapes), nk, epilogue,
                             first_tile_init, w_transposed, prenorm)
    res = pl.pallas_call(
        kern,
        grid=(ni, nj, nk),
        in_specs=in_specs,
        out_specs=out_specs_t,
        out_shape=out_shapes,
        scratch_shapes=scratch,
        compiler_params=_params(("parallel", "arbitrary", "arbitrary")),
        name=name,
    )(a, *([w] * n_w), *extras)
    return res[0] if single else res


def _rope_pairs_64(x, cos2, sin2):
    lane = lax.broadcasted_iota(jnp.int32, x.shape, 1)
    first_half = (lane % MLA_ROPE) < (MLA_ROPE // 2)
    partner = jnp.where(first_half,
                        pltpu.roll(x, LANES - MLA_ROPE // 2, 1),
                        pltpu.roll(x, MLA_ROPE // 2, 1))
    return x * cos2 + partner * sin2


def _store_vt_tiles(vt_ref, lead, v):
    for c in range(v.shape[0] // VT_TILE):
        vt_ref[lead + (c,)] = v[c * VT_TILE:(c + 1) * VT_TILE, :].T.astype(vt_ref.dtype)


def _row_scale(acc, ssq_ref, d_model):
    rstd = lax.rsqrt(ssq_ref[...] * (1.0 / d_model) + NORM_EPS)
    return acc * jnp.concatenate([rstd] * (acc.shape[1] // LANES), axis=1)


def _epi_residual(accs, e_refs, o_refs, j):
    o_refs[0][...] = e_refs[0][...] + accs[0]


def _slab_rows(rows, steps):
    return next((s for s in range(16, rows + 1, 16) if rows % s == 0 and rows // s <= steps), None)


def _zero_ssq(o_refs):
    o_refs[1][...] = jnp.zeros(o_refs[1].shape, F32)


def _epi_residual_prenorm(n_gains, accs, e_refs, o_refs, j):
    h = e_refs[0][...] + accs[0]
    o_refs[0][...] = h
    ssq_ref = o_refs[1]
    ssq_ref[...] += jnp.broadcast_to(jnp.sum(h * h, axis=-1, keepdims=True), ssq_ref.shape)
    for g_ref, hb_ref in zip(e_refs[1:1 + n_gains], o_refs[2:2 + n_gains]):
        hb_ref[...] = (h * g_ref[...]).astype(hb_ref.dtype)
    for w_ref, wb_ref in zip(e_refs[1 + n_gains:], o_refs[2 + n_gains:]):
        wb_ref[...] = w_ref[...].astype(wb_ref.dtype)


def _epi_swiglu(d_model, accs, e_refs, o_refs, j):
    a = _row_scale(accs[0], e_refs[0], d_model)
    b = _row_scale(accs[1], e_refs[0], d_model)
    o_refs[0][...] = (a * jax.nn.sigmoid(a) * b).astype(o_refs[0].dtype)
    if len(o_refs) > 1:
        o_refs[1][...] = e_refs[1][...].astype(o_refs[1].dtype)


def _epi_sigmoid(d_model, accs, e_refs, o_refs, j):
    o_refs[0][...] = jax.nn.sigmoid(_row_scale(accs[0], e_refs[0], d_model))


def _epi_mla_in(d_model, accs, e_refs, o_refs, j):
    ssq_ref, gq_ref, gkv_ref, cos_ref, sin_ref = e_refs
    u = _row_scale(accs[0], ssq_ref, d_model)
    cq = u[:, :MLA_Q_LORA]
    ms = jnp.mean(cq * cq, axis=-1, keepdims=True)
    o_refs[0][...] = (cq * lax.rsqrt(ms + NORM_EPS) * gq_ref[...]).astype(BF16)
    ckv = u[:, MLA_Q_LORA:MLA_Q_LORA + MLA_KV_LORA]
    ms = jnp.mean(ckv * ckv, axis=-1, keepdims=True)
    o_refs[1][...] = (ckv * lax.rsqrt(ms + NORM_EPS) * gkv_ref[...]).astype(BF16)
    kr = u[:, MLA_Q_LORA + MLA_KV_LORA:]
    o_refs[2][...] = _rope_pairs_64(kr, cos_ref[...], sin_ref[...]).astype(BF16)


def _epi_mla_q(n_nope_tiles, scale, accs, e_refs, o_refs, j):
    cos_ref, sin_ref = e_refs
    acc = accs[0]

    @pl.when(j < n_nope_tiles)
    def _():
        o_refs[0][...] = (acc * scale).astype(BF16)

    @pl.when(j >= n_nope_tiles)
    def _():
        for c in range(acc.shape[1] // LANES):
            x = acc[:, c * LANES:(c + 1) * LANES]
            r = _rope_pairs_64(x, cos_ref[...], sin_ref[...])
            o_refs[0][:, c * LANES:(c + 1) * LANES] = (r * scale).astype(BF16)


def _epi_mla_kv(accs, e_refs, o_refs, j):
    acc = accs[0]
    kn_ref, vt_ref = o_refs
    per = MLA_NOPE + MLA_V
    for r in range(acc.shape[1] // per):
        kn_ref[:, r * MLA_NOPE:(r + 1) * MLA_NOPE] = acc[:, r * per:r * per + MLA_NOPE].astype(BF16)
        _store_vt_tiles(vt_ref, (r,), acc[:, r * per + MLA_NOPE:(r + 1) * per])


def _rope_192(x, cos, sin):
    width = x.shape[1]
    lane = lax.broadcasted_iota(jnp.int32, x.shape, 1)
    partner = jnp.where((lane % NSA_DK) < NSA_HALF,
                        pltpu.roll(x, width - NSA_HALF, 1),
                        pltpu.roll(x, NSA_HALF, 1))
    return x * cos + partner * sin


def _epi_nsa_rope(d_model, scale, accs, e_refs, o_refs, j):
    ssq_ref, cos_ref, sin_ref = e_refs
    acc = _row_scale(accs[0], ssq_ref, d_model)
    reps = acc.shape[1] // NSA_DK
    roped = _rope_192(acc, jnp.concatenate([cos_ref[...]] * reps, axis=1),
                      jnp.concatenate([sin_ref[...]] * reps, axis=1))
    if scale is not None:
        roped = roped * scale
    for r in range(acc.shape[1] // NSA_DK):
        o_refs[0][r] = roped[:, r * NSA_DK:(r + 1) * NSA_DK].astype(BF16)


def _epi_nsa_v(d_model, accs, e_refs, o_refs, j):
    v_ref, vt_ref = o_refs
    acc = _row_scale(accs[0], e_refs[0], d_model)
    for r in range(acc.shape[1] // NSA_DV):
        v = acc[:, r * NSA_DV:(r + 1) * NSA_DV]
        v_ref[r] = v.astype(BF16)
        _store_vt_tiles(vt_ref, (r,), v)


SUM_ROWS = 16


def _attend_t(q, k, vt, bias):
    dv = vt.shape[0]
    s = lax.dot_general(k, q, NT_DIMS, preferred_element_type=F32)
    bias = jnp.concatenate([bias] * (s.shape[1] // bias.shape[1]), axis=1)
    free = s.shape[0] - bias.shape[0]
    s = s + bias if free == 0 else jnp.concatenate([s[:free], s[free:] + bias], axis=0)
    p = jnp.exp2(s - jnp.max(s, axis=0, keepdims=True)).astype(BF16)
    vt1 = jnp.concatenate([vt, jnp.ones((SUM_ROWS, vt.shape[1]), BF16)], axis=0)
    acc = jnp.dot(vt1, p, preferred_element_type=F32)
    return acc[:dv] / acc[dv:dv + 1]


def _vt_tiles(vt_ref, lead, c0, n):
    return jnp.concatenate([vt_ref[lead + (c0 + c,)] for c in range(n)], axis=1)


def _mla_attn_kernel(t_blk, seq, qn_ref, qr_ref, kn_ref, kr_ref, vt_ref, o_ref):
    lane_half = lax.broadcasted_iota(jnp.int32, (t_blk, LANES), 1) // MLA_ROPE
    kpos = lax.broadcasted_iota(jnp.int32, (t_blk, t_blk), 0)
    qpos = lax.broadcasted_iota(jnp.int32, (t_blk, t_blk), 1)
    diag_bias = jnp.where(kpos <= qpos, 0.0, NEG_BIG).astype(F32)

    for qi in range(seq // t_blk):
        q0 = qi * t_blk
        n_keys = q0 + t_blk
        qr_pair = qr_ref[q0:q0 + t_blk, :]
        for hh in range(2):
            qr = jnp.where(lane_half == hh, qr_pair, jnp.zeros_like(qr_pair))
            q = jnp.concatenate([qn_ref[q0:q0 + t_blk, hh * MLA_NOPE:(hh + 1) * MLA_NOPE], qr], axis=1)
            k = jnp.concatenate([kn_ref[0:n_keys, hh * MLA_NOPE:(hh + 1) * MLA_NOPE],
                                 kr_ref[0:n_keys, :]], axis=1)
            vt = _vt_tiles(vt_ref, (hh,), 0, n_keys // VT_TILE)
            o_ref[q0:q0 + t_blk, hh * MLA_V:(hh + 1) * MLA_V] = (
                _attend_t(q, k, vt, diag_bias).T.astype(o_ref.dtype))


def _mla_attention(q, kn, kr, vt, batch, seq, heads, t_blk=512):
    t = batch * seq
    n_nope_blk = heads * MLA_NOPE // LANES
    n_vt = seq // VT_TILE
    kern = functools.partial(_mla_attn_kernel, t_blk, seq)
    return pl.pallas_call(
        kern,
        grid=(batch, heads // 2),
        in_specs=[
            pl.BlockSpec((seq, 2 * MLA_NOPE), lambda b, hp: (b, hp)),
            pl.BlockSpec((seq, LANES), lambda b, hp: (b, n_nope_blk + hp)),
            pl.BlockSpec((seq, 2 * MLA_NOPE), lambda b, hp: (b, hp)),
            pl.BlockSpec((seq, LANES), lambda b, hp: (b, 0)),
            pl.BlockSpec((2, n_vt, MLA_V, VT_TILE), lambda b, hp: (hp, b, 0, 0)),
        ],
        out_specs=pl.BlockSpec((seq, 2 * MLA_V), lambda b, hp: (b, hp)),
        out_shape=jax.ShapeDtypeStruct((t, heads * MLA_V), BF16),
        compiler_params=_params(("parallel", "parallel")),
        name="mla_attention",
    )(q, q, kn, kr, vt)


def _compress_mlp(a, w1_ref, pe_ref, w2_ref, transposed_out):
    y_lo = jnp.dot(a, w1_ref[0], preferred_element_type=F32)
    y_hi = jnp.dot(a, w1_ref[1], preferred_element_type=F32)
    pe_c = (jnp.dot(pe_ref[0], w1_ref[0], preferred_element_type=F32)
            + jnp.dot(pe_ref[1], w1_ref[1], preferred_element_type=F32))
    n = y_hi.shape[0]
    hid = y_lo + pltpu.roll(y_hi, n - 1, 0) + pe_c[0:1, :]
    hid = jax.nn.gelu(hid, approximate=True)
    if transposed_out:
        return lax.dot_general(w2_ref[...], hid.astype(BF16), NT_DIMS, preferred_element_type=F32)
    return jnp.dot(hid.astype(BF16), w2_ref[...], preferred_element_type=F32)


def _compress_kernel(ak_ref, av_ref, w1k_ref, pek_ref, w2k_ref, w1v_ref, pev_ref, w2vt_ref,
                     kc_ref, vct_ref):
    kc_ref[0] = _compress_mlp(ak_ref[0], w1k_ref, pek_ref, w2k_ref, False).astype(BF16)
    vct_ref[0] = _compress_mlp(av_ref[0], w1v_ref, pev_ref, w2vt_ref, True).astype(BF16)


def _compress(ak, av, wk, wv, batch, seq):
    n_chunk = seq // CMP_STRIDE
    const3 = lambda b, g: (0, 0, 0)
    const2 = lambda b, g: (0, 0)

    def wspecs(w):
        w1, pe, w2 = w
        return [pl.BlockSpec(w1.shape, const3), pl.BlockSpec(pe.shape, const3),
                pl.BlockSpec(w2.shape, const2)]

    return pl.pallas_call(
        _compress_kernel,
        grid=(batch, NSA_G),
        in_specs=[pl.BlockSpec((1, n_chunk, ak.shape[2]), lambda b, g: (g, b, 0)),
                  pl.BlockSpec((1, n_chunk, av.shape[2]), lambda b, g: (g, b, 0))]
        + wspecs(wk) + wspecs(wv),
        out_specs=(pl.BlockSpec((1, n_chunk, NSA_DK), lambda b, g: (b * NSA_G + g, 0, 0)),
                   pl.BlockSpec((1, NSA_DV, n_chunk), lambda b, g: (b * NSA_G + g, 0, 0))),
        out_shape=(jax.ShapeDtypeStruct((batch * NSA_G, n_chunk, NSA_DK), BF16),
                   jax.ShapeDtypeStruct((batch * NSA_G, NSA_DV, n_chunk), BF16)),
        compiler_params=_params(("parallel", "parallel")),
        name="nsa_compress",
    )(ak, av, *wk, *wv)


def _nsa_attn_kernel(hpg, tq, tk, seq, q_ref, kc_ref, vct_ref, ks_ref, vst_ref, kw_ref, vwt_ref,
                     gate_ref, ovl_ref, o_ref, bias_ref, oc_ref, os_ref, gt_ref):
    qi = pl.program_id(2)
    t0 = qi * tq
    n_rows = hpg * tq
    n_cmp = kc_ref.shape[1]
    n_sel = seq // SEL_BLOCK
    n_vt = tk // VT_TILE
    q = q_ref[...].reshape(n_rows, NSA_DK)

    def tile_heads(x):
        return jnp.concatenate([x] * hpg, axis=1)

    s = lax.dot_general(kc_ref[0], q, NT_DIMS, preferred_element_type=F32)
    cidx = lax.broadcasted_iota(jnp.int32, (n_cmp, tq), 0)
    tpos_c = t0 + lax.broadcasted_iota(jnp.int32, (n_cmp, tq), 1)
    mask_c = tile_heads((cidx * CMP_STRIDE + CMP_BLOCK - 1) <= tpos_c)
    s = jnp.where(mask_c, s, NEG_BIG)
    mx = jnp.max(s, axis=0, keepdims=True)
    e = jnp.where(mask_c, jnp.exp2(s - mx), 0.0)
    p = e / jnp.maximum(jnp.sum(e, axis=0, keepdims=True), 1e-30)
    oc_ref[...] = jnp.dot(vct_ref[0], p.astype(BF16), preferred_element_type=F32)

    psum = p[:, 0:tq]
    for h in range(1, hpg):
        psum = psum + p[:, h * tq:(h + 1) * tq]
    p1 = psum.astype(BF16)
    p2 = (psum - p1.astype(F32)).astype(BF16)
    p3 = (psum - p1.astype(F32) - p2.astype(F32)).astype(BF16)
    ovl = ovl_ref[...]
    imp = (jnp.dot(ovl, p1, preferred_element_type=F32)
           + jnp.dot(ovl, p2, preferred_element_type=F32)
           + jnp.dot(ovl, p3, preferred_element_type=F32))
    jrow = lax.broadcasted_iota(jnp.int32, (n_sel, tq), 0)
    tpos = t0 + lax.broadcasted_iota(jnp.int32, (n_sel, tq), 1)
    valid = jrow * SEL_BLOCK <= tpos
    cur = jnp.right_shift(tpos, SEL_BLOCK.bit_length() - 1)
    forced = valid & ((jrow == 0) | (jrow == cur) | (jrow == cur - 1))
    score = jnp.where(valid, imp + jnp.where(forced, FORCE_BONUS, 0.0), -jnp.inf)
    rank = jnp.zeros((n_sel, tq), jnp.int32)
    for i in range(n_sel):
        row = score[i:i + 1, :]
        beats = (row > score) | ((row == score) & (jrow > i))
        rank = rank + beats.astype(jnp.int32)
    sel_bias = jnp.where(valid & (rank < SEL_TOPK), 0.0, NEG_BIG).astype(F32)

    tk_s = bias_ref.shape[1]
    n_sel_tiles = (t0 + tq - 1) // tk_s + 1
    blocks_per_tile = tk_s // SEL_BLOCK
    for jj in range(seq // tk_s):
        @pl.when(jj < n_sel_tiles)
        def _():
            rows = [jnp.broadcast_to(sel_bias[b:b + 1, :], (SEL_BLOCK, tq))
                    for b in range(jj * blocks_per_tile, (jj + 1) * blocks_per_tile)]
            kpos = jj * tk_s + lax.broadcasted_iota(jnp.int32, (tk_s, tq), 0)
            tq_pos = t0 + lax.broadcasted_iota(jnp.int32, (tk_s, tq), 1)
            bias_ref[jj] = jnp.where(kpos <= tq_pos, jnp.concatenate(rows, axis=0), NEG_BIG)

    for n in range(1, seq // tk_s + 1):
        @pl.when(n_sel_tiles == n)
        def _():
            keys = n * tk_s
            vt = _vt_tiles(vst_ref, (0,), 0, keys // VT_TILE)
            os_ref[...] = _attend_t(q, ks_ref[0, 0:keys, :], vt, bias_ref[0:n].reshape(keys, tq))

    k0 = pl.multiple_of(jnp.maximum(t0 + tq - tk, 0), VT_TILE)
    kpos = k0 + lax.broadcasted_iota(jnp.int32, (tk, tq), 0)
    tq_pos = t0 + lax.broadcasted_iota(jnp.int32, (tk, tq), 1)
    ok = (kpos <= tq_pos) & (tq_pos - kpos < WINDOW)
    o_w = _attend_t(q, kw_ref[0, pl.ds(k0, tk), :], _vt_tiles(vwt_ref, (0,), k0 // VT_TILE, n_vt),
                    jnp.where(ok, 0.0, NEG_BIG).astype(F32))

    gt_ref[...] = gate_ref[...].T
    base = pl.program_id(1) * (3 * hpg)
    for h in range(hpg):
        cols = slice(h * tq, (h + 1) * tq)
        o = (gt_ref[pl.ds(base + 3 * h, 1), :] * oc_ref[:, cols]
             + gt_ref[pl.ds(base + 3 * h + 1, 1), :] * os_ref[:, cols]
             + gt_ref[pl.ds(base + 3 * h + 2, 1), :] * o_w[:, cols])
        o_ref[:, h * NSA_DV:(h + 1) * NSA_DV] = o.T.astype(o_ref.dtype)


def _nsa_attention(q, kc, vct, k3, vt3, gates, ovl, batch, seq, hpg, tq=128, tk_sel=512):
    tk = WINDOW + tq
    t = batch * seq
    nq = seq // tq
    n_chunk = seq // CMP_STRIDE
    n_rows = hpg * tq
    n_vt = seq // VT_TILE
    assert tk % VT_TILE == 0 and seq >= tk and seq % tk_sel == 0
    kern = functools.partial(_nsa_attn_kernel, hpg, tq, tk, seq)

    def k_spec(branch):
        return pl.BlockSpec((1, seq, NSA_DK), lambda b, g, qi: (branch * NSA_G + g, b, 0))

    def vt_spec(branch):
        return pl.BlockSpec((1, n_vt, NSA_DV, VT_TILE), lambda b, g, qi: (branch * NSA_G + g, b, 0, 0))

    return pl.pallas_call(
        kern,
        grid=(batch, NSA_G, nq),
        in_specs=[
            pl.BlockSpec((hpg, tq, NSA_DK), lambda b, g, qi: (g, b * nq + qi, 0)),
            pl.BlockSpec((1, n_chunk, NSA_DK), lambda b, g, qi: (b * NSA_G + g, 0, 0)),
            pl.BlockSpec((1, NSA_DV, n_chunk), lambda b, g, qi: (b * NSA_G + g, 0, 0)),
            k_spec(1), vt_spec(1), k_spec(2), vt_spec(2),
            pl.BlockSpec((tq, LANES), lambda b, g, qi: (b * nq + qi, 0)),
            pl.BlockSpec(ovl.shape, lambda b, g, qi: (0, 0)),
        ],
        out_specs=pl.BlockSpec((tq, hpg * NSA_DV), lambda b, g, qi: (b * nq + qi, g)),
        out_shape=jax.ShapeDtypeStruct((t, NSA_G * hpg * NSA_DV), BF16),
        scratch_shapes=[
            pltpu.VMEM((seq // tk_sel, tk_sel, tq), F32),
            pltpu.VMEM((NSA_DV, n_rows), F32),
            pltpu.VMEM((NSA_DV, n_rows), F32),
            pltpu.VMEM((LANES, tq), F32),
        ],
        compiler_params=_params(("parallel", "parallel", "arbitrary")),
        name="nsa_attention",
    )(q, kc, vct, k3, vt3, k3, vt3, gates, ovl)


def _rope_tables(pos_flat):
    pos = pos_flat.astype(F32)[:, None]
    inv64 = ROPE_THETA ** (-jnp.arange(0, MLA_ROPE, 2, dtype=F32) / MLA_ROPE)
    ang = pos * inv64
    c, s = jnp.cos(ang), jnp.sin(ang)
    cos_mla = jnp.concatenate([c, c, c, c], axis=1)
    sin_mla = jnp.concatenate([-s, s, -s, s], axis=1)
    inv192 = ROPE_THETA ** (-jnp.arange(0, NSA_DK, 2, dtype=F32) / NSA_DK)
    ang = pos * inv192
    c, s = jnp.cos(ang), jnp.sin(ang)
    cos_nsa = jnp.concatenate([c, c], axis=1)
    sin_nsa = jnp.concatenate([-s, s], axis=1)
    return cos_mla, sin_mla, cos_nsa, sin_nsa


def _compress_weights(w1, pe, w2, *, transpose_w2):
    width, d_hid = w1.shape[0] // CMP_BLOCK, w1.shape[1]
    hid_p = -(-d_hid // LANES) * LANES
    w1 = jnp.pad(w1, ((0, 0), (0, hid_p - d_hid)))
    w1 = w1.reshape(2, CMP_STRIDE * width, hid_p).astype(BF16)
    pe = jnp.pad(pe.reshape(2, 1, CMP_STRIDE * width), ((0, 0), (0, 7), (0, 0))).astype(BF16)
    w2 = jnp.pad(w2, ((0, hid_p - d_hid), (0, 0))).astype(BF16)
    return w1, pe, (w2.T if transpose_w2 else w2)


def kernel(x, positions, a_norm, a_w_in, a_q_norm, a_w_uq, a_kv_norm, a_w_ukv, a_w_o, s_norm, s_w_kv, s_cmp_pe_k, s_cmp_w1_k, s_cmp_w2_k, s_cmp_pe_v, s_cmp_w1_v, s_cmp_w2_v, b_norm, b_w_in, b_w_o, f_norm, f_w_in, f_w_out, final_norm):
    batch, seq, d = x.shape
    t = batch * seq
    heads = d // 128
    hpg = heads // NSA_G
    d_ff = f_w_out.shape[1]
    n_a, n_b = a_w_in.shape[0], b_w_in.shape[0]
    assert n_a >= 1 and n_b >= 1
    tm = 1024
    sds = jax.ShapeDtypeStruct

    cos_mla, sin_mla, cos_nsa, sin_nsa = _rope_tables(positions.reshape(t))
    row = lambda i, j, k: (i, 0)
    tile = lambda i, j, k: (i, j)
    tab_spec = pl.BlockSpec((tm, LANES), row)
    ssq_spec = pl.BlockSpec((tm, LANES), row)

    def residual_mm(a, w, w_layer, h, gains, *, tm_, tn, name, a_buffers=None, to_round=()):
        blk = pl.BlockSpec((tm_, tn), tile)
        nj = d // tn
        common = dict(tm=tm_, tn=tn, tk=a.shape[1], nj=nj, w_layer=w_layer, name=name,
                      a_buffers=a_buffers)
        if not gains:
            assert not to_round
            h_new = _mm(a, w, _epi_residual, sds((t, d), F32), blk,
                        extras=(h,), extra_specs=(blk,), **common)
            return h_new, None, (), ()
        steps = (t // tm_) * nj
        slabs_in, slabs_out, flats = [], [], []
        for wr, layer in to_round:
            rows, cols = wr.shape[1:]
            slab = _slab_rows(rows, steps)
            n_slabs = rows // slab
            flats.append(wr.reshape(-1, cols))
            step_of = functools.partial(lambda i, j, n: jnp.minimum(i * nj + j, n - 1), n=n_slabs)
            slabs_in.append(pl.BlockSpec(
                (slab, cols), functools.partial(lambda i, j, k, off, s: (off + s(i, j), 0),
                                                off=layer * n_slabs, s=step_of)))
            slabs_out.append(pl.BlockSpec(
                (slab, cols), functools.partial(lambda i, j, k, s: (s(i, j), 0), s=step_of)))
        outs = _mm(a, w, functools.partial(_epi_residual_prenorm, len(gains)),
                   (sds((t, d), F32), sds((t, LANES), F32)) + tuple(sds((t, d), BF16) for _ in gains)
                   + tuple(sds(wr.shape[1:], BF16) for wr, _ in to_round),
                   (blk, pl.BlockSpec((tm_, LANES), row)) + tuple(blk for _ in gains)
                   + tuple(slabs_out),
                   extras=(h,) + tuple(g.reshape(1, d) for g in gains) + tuple(flats),
                   extra_specs=(blk,) + tuple(pl.BlockSpec((1, tn), lambda i, j, k: (0, j))
                                              for _ in gains) + tuple(slabs_in),
                   first_tile_init=_zero_ssq, **common)
        n_g = len(gains)
        return outs[0], outs[1], tuple(outs[2:2 + n_g]), tuple(outs[2 + n_g:])

    def ffn(hb, ssq, h, layer, gains, to_round=()):
        tn = 256
        nj = d_ff // tn
        tm_in = min(2 * tm, t)
        steps = (t // tm_in) * nj
        slab = d_ff // steps
        assert slab * steps == d_ff and slab % 16 == 0
        act, w_out = _mm(
            hb, f_w_in, functools.partial(_epi_swiglu, d),
            (sds((t, d_ff), BF16), sds((d_ff, d), BF16)),
            (pl.BlockSpec((tm_in, tn), tile),
             pl.BlockSpec((slab, d), lambda i, j, k: (i * nj + j, 0))),
            tm=tm_in, tn=tn, tk=d, nj=nj, w_layer=layer, a_buffers=1,
            w_col_maps=(lambda j: j, lambda j: j + nj),
            extras=(ssq, f_w_out),
            extra_specs=(pl.BlockSpec((tm_in, LANES), row),
                         pl.BlockSpec((None, slab, d), lambda i, j, k: (layer, i * nj + j, 0))),
            name="ffn_in")
        return residual_mm(act, w_out, None, h, gains, tm_=tm, tn=tn, name="ffn_out",
                           a_buffers=1, to_round=to_round)

    b_w_in_t = jnp.swapaxes(b_w_in, 1, 2)
    round_w_in_t = _slab_rows(b_w_in_t.shape[1], (t // tm) * (d // 256)) is not None

    h = x.reshape(t, d)
    hb = ssq = None
    a_w_in_t = jnp.swapaxes(a_w_in, 1, 2)

    for i in range(n_a):
        rope_lo = MLA_Q_LORA + MLA_KV_LORA
        w_in_t = jnp.concatenate([a_w_in_t[i], a_w_in_t[i][rope_lo:]], axis=0).astype(BF16)
        tk_in = 1024
        if i == 0:
            a_in, norm_in = h, a_norm[0].reshape(1, d)
            norm_spec = pl.BlockSpec((1, tk_in), lambda i_, j, k: (0, k))
        else:
            a_in, norm_in, norm_spec = hb, ssq, ssq_spec
        cq, ckv, kr = _mm(
            a_in, w_in_t, functools.partial(_epi_mla_in, d),
            (sds((t, MLA_Q_LORA), BF16), sds((t, MLA_KV_LORA), BF16), sds((t, LANES), BF16)),
            (pl.BlockSpec((tm, MLA_Q_LORA), row), pl.BlockSpec((tm, MLA_KV_LORA), row),
             pl.BlockSpec((tm, LANES), row)),
            tm=tm, tn=w_in_t.shape[0], tk=tk_in, nj=1,
            extras=(norm_in, a_q_norm[i].reshape(1, -1), a_kv_norm[i].reshape(1, -1),
                    cos_mla, sin_mla),
            extra_specs=(norm_spec,
                         pl.BlockSpec((1, MLA_Q_LORA), lambda i_, j, k: (0, 0)),
                         pl.BlockSpec((1, MLA_KV_LORA), lambda i_, j, k: (0, 0)),
                         tab_spec, tab_spec),
            w_transposed=True, prenorm=(i == 0), name="mla_in")

        qd = MLA_NOPE + MLA_ROPE
        w_uq = a_w_uq[i].reshape(MLA_Q_LORA, heads, qd)
        w_uq = jnp.concatenate([w_uq[:, :, :MLA_NOPE].reshape(MLA_Q_LORA, heads * MLA_NOPE),
                                w_uq[:, :, MLA_NOPE:].reshape(MLA_Q_LORA, heads * MLA_ROPE)],
                               axis=1).astype(BF16)
        tn = 512
        q = _mm(cq, w_uq,
                functools.partial(_epi_mla_q, heads * MLA_NOPE // tn, qd ** -0.5 * LOG2_E),
                jax.ShapeDtypeStruct((t, heads * qd), BF16),
                pl.BlockSpec((tm, tn), lambda i_, j, k: (i_, j)),
                tm=tm, tn=tn, tk=MLA_Q_LORA, nj=heads * qd // tn,
                extras=(cos_mla, sin_mla), extra_specs=(tab_spec, tab_spec), name="mla_q")
        heads_per_tile = 4
        kn, vt = _mm(ckv, a_w_ukv[i].astype(BF16), _epi_mla_kv,
                     (jax.ShapeDtypeStruct((t, heads * MLA_NOPE), BF16),
                      jax.ShapeDtypeStruct((heads, t // VT_TILE, MLA_V, VT_TILE), BF16)),
                     (pl.BlockSpec((tm, heads_per_tile * MLA_NOPE), lambda i_, j, k: (i_, j)),
                      pl.BlockSpec((heads_per_tile, tm // VT_TILE, MLA_V, VT_TILE),
                                   lambda i_, j, k: (j, i_, 0, 0))),
                     tm=tm, tn=heads_per_tile * (MLA_NOPE + MLA_V), tk=MLA_KV_LORA,
                     nj=heads // heads_per_tile, name="mla_kv")
        o = _mla_attention(q, kn, kr, vt, batch, seq, heads)
        h, ssq, (hb,), _ = residual_mm(o, a_w_o[i].astype(BF16), None, h, [f_norm[i]],
                                       tm_=tm, tn=512, name="mla_out")
        last_a = i + 1 == n_a
        gains = [s_norm, b_norm[0]] if last_a else [a_norm[i + 1]]
        to_round = ()
        if last_a:
            to_round = ((s_w_kv[None], 0), (b_w_o, 0)) + (((b_w_in_t, 0),) if round_w_in_t else ())
        h, ssq, hbs, rounded = ffn(hb, ssq, h, i, gains, to_round)
        hb = hbs[0]

    hb_s, hb_b = hbs
    s_w_kv_b, b_w_o0 = rounded[:2]
    b_w_in_t0 = rounded[2] if round_w_in_t else b_w_in_t[0].astype(BF16)
    n_sets = 3 * NSA_G
    per_tile = NSA_ROPE_TILE // NSA_DK
    tm_rope = tm
    rope_tab = pl.BlockSpec((tm_rope, NSA_DK), row)
    w_kv = s_w_kv_b.reshape(d, n_sets, NSA_DK + NSA_DV)
    w_k = w_kv[:, :, :NSA_DK].reshape(d, n_sets * NSA_DK)
    w_v = w_kv[:, :, NSA_DK:].reshape(d, n_sets * NSA_DV)
    k3 = _mm(hb_s, w_k, functools.partial(_epi_nsa_rope, d, None),
             sds((n_sets, t, NSA_DK), BF16),
             pl.BlockSpec((per_tile, tm_rope, NSA_DK), lambda i_, j, k: (j, i_, 0)),
             tm=tm_rope, tn=NSA_ROPE_TILE, tk=d, nj=n_sets // per_tile,
             extras=(ssq, cos_nsa, sin_nsa),
             extra_specs=(pl.BlockSpec((tm_rope, LANES), row), rope_tab, rope_tab),
             a_buffers=1, name="nsa_k")
    v3, vt3 = _mm(hb_s, w_v, functools.partial(_epi_nsa_v, d),
                  (sds((n_sets, t, NSA_DV), BF16),
                   sds((n_sets, t // VT_TILE, NSA_DV, VT_TILE), BF16)),
                  (pl.BlockSpec((per_tile, tm, NSA_DV), lambda i_, j, k: (j, i_, 0)),
                   pl.BlockSpec((per_tile, tm // VT_TILE, NSA_DV, VT_TILE),
                                lambda i_, j, k: (j, i_, 0, 0))),
                  tm=tm, tn=per_tile * NSA_DV, tk=d, nj=n_sets // per_tile,
                  extras=(ssq,), extra_specs=(ssq_spec,), name="nsa_v")

    ak = k3[:NSA_G].reshape(NSA_G, t // CMP_STRIDE, CMP_STRIDE * NSA_DK)
    av = v3[:NSA_G].reshape(NSA_G, t // CMP_STRIDE, CMP_STRIDE * NSA_DV)
    kc, vct = _compress(ak, av,
                        _compress_weights(s_cmp_w1_k, s_cmp_pe_k, s_cmp_w2_k, transpose_w2=False),
                        _compress_weights(s_cmp_w1_v, s_cmp_pe_v, s_cmp_w2_v, transpose_w2=True),
                        batch, seq)

    n_chunk = seq // CMP_STRIDE
    n_sel = seq // SEL_BLOCK
    c_start = np.arange(n_chunk) * CMP_STRIDE
    j_start = np.arange(n_sel) * SEL_BLOCK
    ovl = ((c_start[None, :] < j_start[:, None] + SEL_BLOCK)
           & (c_start[None, :] + CMP_BLOCK > j_start[:, None])
           & (np.arange(n_chunk)[None, :] < (seq - CMP_BLOCK) // CMP_STRIDE + 1))
    ovl = jnp.asarray(ovl, BF16)

    for i in range(n_b):
        n_q = heads * NSA_DK
        w_qt = b_w_in_t0 if i == 0 else b_w_in_t[i].astype(BF16)
        w_o = b_w_o0 if i == 0 else b_w_o[i].astype(BF16)
        q = _mm(hb_b, w_qt, functools.partial(_epi_nsa_rope, d, NSA_DK ** -0.5 * LOG2_E),
                sds((heads, t, NSA_DK), BF16),
                pl.BlockSpec((per_tile, tm_rope, NSA_DK), lambda i_, j, k: (j, i_, 0)),
                tm=tm_rope, tn=NSA_ROPE_TILE, tk=d, nj=heads // per_tile,
                extras=(ssq, cos_nsa, sin_nsa),
                extra_specs=(pl.BlockSpec((tm_rope, LANES), row), rope_tab, rope_tab),
                a_buffers=1, w_transposed=True, name="nsa_q")
        assert n_q % LANES == 0 and 3 * heads <= LANES
        gates = _mm(hb_b, w_qt, functools.partial(_epi_sigmoid, d), sds((t, LANES), F32),
                    pl.BlockSpec((tm, LANES), row),
                    tm=tm, tn=LANES, tk=d, nj=1, w_col_maps=(lambda j: n_q // LANES,),
                    extras=(ssq,), extra_specs=(ssq_spec,), w_transposed=True, name="nsa_gates")
        o = _nsa_attention(q, kc, vct, k3, vt3, gates, ovl, batch, seq, hpg)
        h, ssq, (hb,), _ = residual_mm(o, w_o, None, h, [f_norm[n_a + i]],
                                       tm_=tm, tn=512, name="nsa_out")
        gains = [b_norm[i + 1]] if i + 1 < n_b else []
        h, ssq, hbs, _ = ffn(hb, ssq, h, n_a + i, gains)
        if hbs:
            hb_b = hbs[0]

    out = _rmsnorm(h, final_norm, F32)
    return out.reshape(batch, seq, d)
```

```python
import functools

import numpy as np
import jax
import jax.numpy as jnp
from jax import lax
from jax.experimental import pallas as pl
from jax.experimental.pallas import tpu as pltpu

F32 = jnp.float32
BF16 = jnp.bfloat16

LANES = 128
VMEM_LIMIT_BYTES = 56 * 1024 * 1024

NORM_EPS = 1e-6
ROPE_THETA = 10000.0

MLA_Q_LORA = 1536
MLA_KV_LORA = 512
MLA_NOPE = 128
MLA_ROPE = 64
MLA_V = 128

NSA_G = 4
NSA_DK = 192
NSA_DV = 128
NSA_HALF = NSA_DK // 2
NSA_ROPE_TILE = 4 * NSA_DK
CMP_BLOCK = 32
CMP_STRIDE = 16
SEL_BLOCK = 64
SEL_TOPK = 16
WINDOW = 512
FORCE_BONUS = 1e4
NEG_BIG = -1e30
LOG2_E = 1.4426950408889634

VT_TILE = LANES
NT_DIMS = (((1,), (1,)), ((), ()))


def _params(sem):
    return pltpu.CompilerParams(dimension_semantics=sem, vmem_limit_bytes=VMEM_LIMIT_BYTES)


def _rmsnorm_kernel(x_ref, g_ref, o_ref):
    x = x_ref[...]
    ms = jnp.mean(x * x, axis=-1, keepdims=True)
    o_ref[...] = (x * lax.rsqrt(ms + NORM_EPS) * g_ref[...]).astype(o_ref.dtype)


def _rmsnorm(x, g, out_dtype, tm=256):
    t, d = x.shape
    return pl.pallas_call(
        _rmsnorm_kernel,
        grid=(t // tm,),
        in_specs=[pl.BlockSpec((tm, d), lambda i: (i, 0)),
                  pl.BlockSpec((1, d), lambda i: (0, 0))],
        out_specs=pl.BlockSpec((tm, d), lambda i: (i, 0)),
        out_shape=jax.ShapeDtypeStruct((t, d), out_dtype),
        compiler_params=_params(("parallel",)),
        name="rmsnorm",
    )(x, g.reshape(1, d))


def _mm_kernel(n_w, n_e, n_o, nk, epilogue, first_tile_init, w_transposed, prenorm, *refs):
    a_ref = refs[0]
    w_refs = refs[1:1 + n_w]
    e_refs = refs[1 + n_w:1 + n_w + n_e]
    o_refs = refs[1 + n_w + n_e:1 + n_w + n_e + n_o]
    acc_refs = refs[1 + n_w + n_e + n_o:]
    j = pl.program_id(1)
    if first_tile_init is not None:
        pl.when((j == 0) & (pl.program_id(2) == 0))(lambda: first_tile_init(o_refs))

    if prenorm:
        g_ref, ssq_ref, acc_refs = e_refs[0], acc_refs[-1], acc_refs[:-1]
        e_refs = (ssq_ref,) + tuple(e_refs[1:])
        xa = a_ref[...]
        part = jnp.broadcast_to(jnp.sum(xa * xa, axis=-1, keepdims=True), ssq_ref.shape)
        first = pl.program_id(2) == 0

        @pl.when(first)
        def _():
            ssq_ref[...] = part

        @pl.when(jnp.logical_not(first))
        def _():
            ssq_ref[...] += part

        a_val = (xa * g_ref[...]).astype(BF16)
    else:
        a_val = a_ref[...]

    def product(w):
        wv = w[...].astype(a_val.dtype)
        if w_transposed:
            return lax.dot_general(a_val, wv, NT_DIMS, preferred_element_type=F32)
        return jnp.dot(a_val, wv, preferred_element_type=F32)

    if nk == 1:
        epilogue([product(w) for w in w_refs], e_refs, o_refs, j)
        return
    k = pl.program_id(2)

    @pl.when(k == 0)
    def _():
        for w, acc in zip(w_refs, acc_refs):
            acc[...] = product(w)

    @pl.when(k > 0)
    def _():
        for w, acc in zip(w_refs, acc_refs):
            acc[...] += product(w)

    @pl.when(k == nk - 1)
    def _():
        epilogue([acc[...] for acc in acc_refs], e_refs, o_refs, j)


def _mm(a, w, epilogue, out_shape, out_specs, *, tm, tn, tk, nj, w_col_maps=None, w_layer=None,
        extras=(), extra_specs=(), a_buffers=None, first_tile_init=None, w_transposed=False,
        prenorm=False, name="matmul"):
    m, kdim = a.shape
    ni, nk = m // tm, kdim // tk
    assert ni * tm == m and nk * tk == kdim
    if w_col_maps is None:
        w_col_maps = (lambda j: j,)
    n_w = len(w_col_maps)
    single = not isinstance(out_shape, (tuple, list))
    out_shapes = (out_shape,) if single else tuple(out_shape)
    out_specs_t = (out_specs,) if single else tuple(out_specs)
    a_kwargs = {} if a_buffers is None else {"pipeline_mode": pl.Buffered(a_buffers)}
    in_specs = [pl.BlockSpec((tm, tk), lambda i, j, k: (i, k), **a_kwargs)]
    for cm in w_col_maps:
        if w_transposed:
            assert w_layer is None
            in_specs.append(pl.BlockSpec((tn, tk), functools.partial(
                lambda i, j, k, cm: (cm(j), k), cm=cm)))
        elif w_layer is None:
            in_specs.append(pl.BlockSpec((tk, tn), functools.partial(
                lambda i, j, k, cm: (k, cm(j)), cm=cm)))
        else:
            in_specs.append(pl.BlockSpec((None, tk, tn), functools.partial(
                lambda i, j, k, cm: (w_layer, k, cm(j)), cm=cm)))
    in_specs += list(extra_specs)
    scratch = [pltpu.VMEM((tm, tn), F32) for _ in range(n_w)] if nk > 1 else []
    if prenorm:
        scratch.append(pltpu.VMEM((tm, LANES), F32))
    kern = functools.partial(_mm_kernel, n_w, len(extras), len(out_shapes), nk, epilogue,
                             first_tile_init, w_transposed, prenorm)
    res = pl.pallas_call(
        kern,
        grid=(ni, nj, nk),
        in_specs=in_specs,
        out_specs=out_specs_t,
        out_shape=out_shapes,
        scratch_shapes=scratch,
        compiler_params=_params(("parallel", "arbitrary", "arbitrary")),
        name=name,
    )(a, *([w] * n_w), *extras)
    return res[0] if single else res


def _rope_pairs_64(x, cos2, sin2):
    lane = lax.broadcasted_iota(jnp.int32, x.shape, 1)
    first_half = (lane % MLA_ROPE) < (MLA_ROPE // 2)
    partner = jnp.where(first_half,
                        pltpu.roll(x, LANES - MLA_ROPE // 2, 1),
                        pltpu.roll(x, MLA_ROPE // 2, 1))
    return x * cos2 + partner * sin2


def _store_vt_tiles(vt_ref, lead, v):
    for c in range(v.shape[0] // VT_TILE):
        vt_ref[lead + (c,)] = v[c * VT_TILE:(c + 1) * VT_TILE, :].T.astype(vt_ref.dtype)


def _row_scale(acc, ssq_ref, d_model):
    rstd = lax.rsqrt(ssq_ref[...] * (1.0 / d_model) + NORM_EPS)
    return acc * jnp.concatenate([rstd] * (acc.shape[1] // LANES), axis=1)


def _epi_residual(accs, e_refs, o_refs, j):
    o_refs[0][...] = e_refs[0][...] + accs[0]


def _slab_rows(rows, steps):
    return next((s for s in range(16, rows + 1, 16) if rows % s == 0 and rows // s <= steps), None)


def _zero_ssq(o_refs):
    o_refs[1][...] = jnp.zeros(o_refs[1].shape, F32)


def _epi_residual_prenorm(n_gains, accs, e_refs, o_refs, j):
    h = e_refs[0][...] + accs[0]
    o_refs[0][...] = h
    ssq_ref = o_refs[1]
    ssq_ref[...] += jnp.broadcast_to(jnp.sum(h * h, axis=-1, keepdims=True), ssq_ref.shape)
    for g_ref, hb_ref in zip(e_refs[1:1 + n_gains], o_refs[2:2 + n_gains]):
        hb_ref[...] = (h * g_ref[...]).astype(hb_ref.dtype)
    for w_ref, wb_ref in zip(e_refs[1 + n_gains:], o_refs[2 + n_gains:]):
        wb_ref[...] = w_ref[...].astype(wb_ref.dtype)


def _epi_swiglu(d_model, accs, e_refs, o_refs, j):
    a = _row_scale(accs[0], e_refs[0], d_model)
    b = _row_scale(accs[1], e_refs[0], d_model)
    o_refs[0][...] = (a * jax.nn.sigmoid(a) * b).astype(o_refs[0].dtype)
    if len(o_refs) > 1:
        o_refs[1][...] = e_refs[1][...].astype(o_refs[1].dtype)


def _epi_sigmoid(d_model, accs, e_refs, o_refs, j):
    o_refs[0][...] = jax.nn.sigmoid(_row_scale(accs[0], e_refs[0], d_model))


def _epi_mla_in(d_model, accs, e_refs, o_refs, j):
    ssq_ref, gq_ref, gkv_ref, cos_ref, sin_ref = e_refs
    u = _row_scale(accs[0], ssq_ref, d_model)
    cq = u[:, :MLA_Q_LORA]
    ms = jnp.mean(cq * cq, axis=-1, keepdims=True)
    o_refs[0][...] = (cq * lax.rsqrt(ms + NORM_EPS) * gq_ref[...]).astype(BF16)
    ckv = u[:, MLA_Q_LORA:MLA_Q_LORA + MLA_KV_LORA]
    ms = jnp.mean(ckv * ckv, axis=-1, keepdims=True)
    o_refs[1][...] = (ckv * lax.rsqrt(ms + NORM_EPS) * gkv_ref[...]).astype(BF16)
    kr = u[:, MLA_Q_LORA + MLA_KV_LORA:]
    o_refs[2][...] = _rope_pairs_64(kr, cos_ref[...], sin_ref[...]).astype(BF16)


def _epi_mla_q(n_nope_tiles, scale, accs, e_refs, o_refs, j):
    cos_ref, sin_ref = e_refs
    acc = accs[0]

    @pl.when(j < n_nope_tiles)
    def _():
        o_refs[0][...] = (acc * scale).astype(BF16)

    @pl.when(j >= n_nope_tiles)
    def _():
        for c in range(acc.shape[1] // LANES):
            x = acc[:, c * LANES:(c + 1) * LANES]
            r = _rope_pairs_64(x, cos_ref[...], sin_ref[...])
            o_refs[0][:, c * LANES:(c + 1) * LANES] = (r * scale).astype(BF16)


def _epi_mla_kv(accs, e_refs, o_refs, j):
    acc = accs[0]
    kn_ref, vt_ref = o_refs
    per = MLA_NOPE + MLA_V
    for r in range(acc.shape[1] // per):
        kn_ref[:, r * MLA_NOPE:(r + 1) * MLA_NOPE] = acc[:, r * per:r * per + MLA_NOPE].astype(BF16)
        _store_vt_tiles(vt_ref, (r,), acc[:, r * per + MLA_NOPE:(r + 1) * per])


def _rope_192(x, cos, sin):
    width = x.shape[1]
    lane = lax.broadcasted_iota(jnp.int32, x.shape, 1)
    partner = jnp.where((lane % NSA_DK) < NSA_HALF,
                        pltpu.roll(x, width - NSA_HALF, 1),
                        pltpu.roll(x, NSA_HALF, 1))
    return x * cos + partner * sin


def _epi_nsa_rope(d_model, scale, accs, e_refs, o_refs, j):
    ssq_ref, cos_ref, sin_ref = e_refs
    acc = _row_scale(accs[0], ssq_ref, d_model)
    reps = acc.shape[1] // NSA_DK
    roped = _rope_192(acc, jnp.concatenate([cos_ref[...]] * reps, axis=1),
                      jnp.concatenate([sin_ref[...]] * reps, axis=1))
    if scale is not None:
        roped = roped * scale
    for r in range(acc.shape[1] // NSA_DK):
        o_refs[0][r] = roped[:, r * NSA_DK:(r + 1) * NSA_DK].astype(BF16)


def _epi_nsa_v(d_model, accs, e_refs, o_refs, j):
    v_ref, vt_ref = o_refs
    acc = _row_scale(accs[0], e_refs[0], d_model)
    for r in range(acc.shape[1] // NSA_DV):
        v = acc[:, r * NSA_DV:(r + 1) * NSA_DV]
        v_ref[r] = v.astype(BF16)
        _store_vt_tiles(vt_ref, (r,), v)


SUM_ROWS = 16


def _attend_t(q, k, vt, bias):
    dv = vt.shape[0]
    s = lax.dot_general(k, q, NT_DIMS, preferred_element_type=F32)
    bias = jnp.concatenate([bias] * (s.shape[1] // bias.shape[1]), axis=1)
    free = s.shape[0] - bias.shape[0]
    s = s + bias if free == 0 else jnp.concatenate([s[:free], s[free:] + bias], axis=0)
    p = jnp.exp2(s - jnp.max(s, axis=0, keepdims=True)).astype(BF16)
    vt1 = jnp.concatenate([vt, jnp.ones((SUM_ROWS, vt.shape[1]), BF16)], axis=0)
    acc = jnp.dot(vt1, p, preferred_element_type=F32)
    return acc[:dv] / acc[dv:dv + 1]


def _vt_tiles(vt_ref, lead, c0, n):
    return jnp.concatenate([vt_ref[lead + (c0 + c,)] for c in range(n)], axis=1)


def _mla_attn_kernel(t_blk, seq, qn_ref, qr_ref, kn_ref, kr_ref, vt_ref, o_ref):
    lane_half = lax.broadcasted_iota(jnp.int32, (t_blk, LANES), 1) // MLA_ROPE
    kpos = lax.broadcasted_iota(jnp.int32, (t_blk, t_blk), 0)
    qpos = lax.broadcasted_iota(jnp.int32, (t_blk, t_blk), 1)
    diag_bias = jnp.where(kpos <= qpos, 0.0, NEG_BIG).astype(F32)

    for qi in range(seq // t_blk):
        q0 = qi * t_blk
        n_keys = q0 + t_blk
        qr_pair = qr_ref[q0:q0 + t_blk, :]
        for hh in range(2):
            qr = jnp.where(lane_half == hh, qr_pair, jnp.zeros_like(qr_pair))
            q = jnp.concatenate([qn_ref[q0:q0 + t_blk, hh * MLA_NOPE:(hh + 1) * MLA_NOPE], qr], axis=1)
            k = jnp.concatenate([kn_ref[0:n_keys, hh * MLA_NOPE:(hh + 1) * MLA_NOPE],
                                 kr_ref[0:n_keys, :]], axis=1)
            vt = _vt_tiles(vt_ref, (hh,), 0, n_keys // VT_TILE)
            o_ref[q0:q0 + t_blk, hh * MLA_V:(hh + 1) * MLA_V] = (
                _attend_t(q, k, vt, diag_bias).T.astype(o_ref.dtype))


def _mla_attention(q, kn, kr, vt, batch, seq, heads, t_blk=512):
    t = batch * seq
    n_nope_blk = heads * MLA_NOPE // LANES
    n_vt = seq // VT_TILE
    kern = functools.partial(_mla_attn_kernel, t_blk, seq)
    return pl.pallas_call(
        kern,
        grid=(batch, heads // 2),
        in_specs=[
            pl.BlockSpec((seq, 2 * MLA_NOPE), lambda b, hp: (b, hp)),
            pl.BlockSpec((seq, LANES), lambda b, hp: (b, n_nope_blk + hp)),
            pl.BlockSpec((seq, 2 * MLA_NOPE), lambda b, hp: (b, hp)),
            pl.BlockSpec((seq, LANES), lambda b, hp: (b, 0)),
            pl.BlockSpec((2, n_vt, MLA_V, VT_TILE), lambda b, hp: (hp, b, 0, 0)),
        ],
        out_specs=pl.BlockSpec((seq, 2 * MLA_V), lambda b, hp: (b, hp)),
        out_shape=jax.ShapeDtypeStruct((t, heads * MLA_V), BF16),
        compiler_params=_params(("parallel", "parallel")),
        name="mla_attention",
    )(q, q, kn, kr, vt)


def _compress_mlp(a, w1_ref, pe_ref, w2_ref, transposed_out):
    y_lo = jnp.dot(a, w1_ref[0], preferred_element_type=F32)
    y_hi = jnp.dot(a, w1_ref[1], preferred_element_type=F32)
    pe_c = (jnp.dot(pe_ref[0], w1_ref[0], preferred_element_type=F32)
            + jnp.dot(pe_ref[1], w1_ref[1], preferred_element_type=F32))
    n = y_hi.shape[0]
    hid = y_lo + pltpu.roll(y_hi, n - 1, 0) + pe_c[0:1, :]
    hid = jax.nn.gelu(hid, approximate=True)
    if transposed_out:
        return lax.dot_general(w2_ref[...], hid.astype(BF16), NT_DIMS, preferred_element_type=F32)
    return jnp.dot(hid.astype(BF16), w2_ref[...], preferred_element_type=F32)


def _compress_kernel(ak_ref, av_ref, w1k_ref, pek_ref, w2k_ref, w1v_ref, pev_ref, w2vt_ref,
                     kc_ref, vct_ref):
    kc_ref[0] = _compress_mlp(ak_ref[0], w1k_ref, pek_ref, w2k_ref, False).astype(BF16)
    vct_ref[0] = _compress_mlp(av_ref[0], w1v_ref, pev_ref, w2vt_ref, True).astype(BF16)


def _compress(ak, av, wk, wv, batch, seq):
    n_chunk = seq // CMP_STRIDE
    const3 = lambda b, g: (0, 0, 0)
    const2 = lambda b, g: (0, 0)

    def wspecs(w):
        w1, pe, w2 = w
        return [pl.BlockSpec(w1.shape, const3), pl.BlockSpec(pe.shape, const3),
                pl.BlockSpec(w2.shape, const2)]

    return pl.pallas_call(
        _compress_kernel,
        grid=(batch, NSA_G),
        in_specs=[pl.BlockSpec((1, n_chunk, ak.shape[2]), lambda b, g: (g, b, 0)),
                  pl.BlockSpec((1, n_chunk, av.shape[2]), lambda b, g: (g, b, 0))]
        + wspecs(wk) + wspecs(wv),
        out_specs=(pl.BlockSpec((1, n_chunk, NSA_DK), lambda b, g: (b * NSA_G + g, 0, 0)),
                   pl.BlockSpec((1, NSA_DV, n_chunk), lambda b, g: (b * NSA_G + g, 0, 0))),
        out_shape=(jax.ShapeDtypeStruct((batch * NSA_G, n_chunk, NSA_DK), BF16),
                   jax.ShapeDtypeStruct((batch * NSA_G, NSA_DV, n_chunk), BF16)),
        compiler_params=_params(("parallel", "parallel")),
        name="nsa_compress",
    )(ak, av, *wk, *wv)


def _nsa_attn_kernel(hpg, tq, tk, seq, q_ref, kc_ref, vct_ref, ks_ref, vst_ref, kw_ref, vwt_ref,
                     gate_ref, ovl_ref, o_ref, bias_ref, oc_ref, os_ref, gt_ref):
    qi = pl.program_id(2)
    t0 = qi * tq
    n_rows = hpg * tq
    n_cmp = kc_ref.shape[1]
    n_sel = seq // SEL_BLOCK
    n_vt = tk // VT_TILE
    q = q_ref[...].reshape(n_rows, NSA_DK)

    def tile_heads(x):
        return jnp.concatenate([x] * hpg, axis=1)

    s = lax.dot_general(kc_ref[0], q, NT_DIMS, preferred_element_type=F32)
    cidx = lax.broadcasted_iota(jnp.int32, (n_cmp, tq), 0)
    tpos_c = t0 + lax.broadcasted_iota(jnp.int32, (n_cmp, tq), 1)
    mask_c = tile_heads((cidx * CMP_STRIDE + CMP_BLOCK - 1) <= tpos_c)
    s = jnp.where(mask_c, s, NEG_BIG)
    mx = jnp.max(s, axis=0, keepdims=True)
    e = jnp.where(mask_c, jnp.exp2(s - mx), 0.0)
    p = e / jnp.maximum(jnp.sum(e, axis=0, keepdims=True), 1e-30)
    oc_ref[...] = jnp.dot(vct_ref[0], p.astype(BF16), preferred_element_type=F32)

    psum = p[:, 0:tq]
    for h in range(1, hpg):
        psum = psum + p[:, h * tq:(h + 1) * tq]
    p1 = psum.astype(BF16)
    p2 = (psum - p1.astype(F32)).astype(BF16)
    p3 = (psum - p1.astype(F32) - p2.astype(F32)).astype(BF16)
    ovl = ovl_ref[...]
    imp = (jnp.dot(ovl, p1, preferred_element_type=F32)
           + jnp.dot(ovl, p2, preferred_element_type=F32)
           + jnp.dot(ovl, p3, preferred_element_type=F32))
    jrow = lax.broadcasted_iota(jnp.int32, (n_sel, tq), 0)
    tpos = t0 + lax.broadcasted_iota(jnp.int32, (n_sel, tq), 1)
    valid = jrow * SEL_BLOCK <= tpos
    cur = jnp.right_shift(tpos, SEL_BLOCK.bit_length() - 1)
    forced = valid & ((jrow == 0) | (jrow == cur) | (jrow == cur - 1))
    score = jnp.where(valid, imp + jnp.where(forced, FORCE_BONUS, 0.0), -jnp.inf)
    rank = jnp.zeros((n_sel, tq), jnp.int32)
    for i in range(n_sel):
        row = score[i:i + 1, :]
        beats = (row > score) | ((row == score) & (jrow > i))
        rank = rank + beats.astype(jnp.int32)
    sel_bias = jnp.where(valid & (rank < SEL_TOPK), 0.0, NEG_BIG).astype(F32)

    tk_s = bias_ref.shape[1]
    n_sel_tiles = (t0 + tq - 1) // tk_s + 1
    blocks_per_tile = tk_s // SEL_BLOCK
    for jj in range(seq // tk_s):
        @pl.when(jj < n_sel_tiles)
        def _():
            rows = [jnp.broadcast_to(sel_bias[b:b + 1, :], (SEL_BLOCK, tq))
                    for b in range(jj * blocks_per_tile, (jj + 1) * blocks_per_tile)]
            kpos = jj * tk_s + lax.broadcasted_iota(jnp.int32, (tk_s, tq), 0)
            tq_pos = t0 + lax.broadcasted_iota(jnp.int32, (tk_s, tq), 1)
            bias_ref[jj] = jnp.where(kpos <= tq_pos, jnp.concatenate(rows, axis=0), NEG_BIG)

    for n in range(1, seq // tk_s + 1):
        @pl.when(n_sel_tiles == n)
        def _():
            keys = n * tk_s
            vt = _vt_tiles(vst_ref, (0,), 0, keys // VT_TILE)
            os_ref[...] = _attend_t(q, ks_ref[0, 0:keys, :], vt, bias_ref[0:n].reshape(keys, tq))

    k0 = pl.multiple_of(jnp.maximum(t0 + tq - tk, 0), VT_TILE)
    kpos = k0 + lax.broadcasted_iota(jnp.int32, (tk, tq), 0)
    tq_pos = t0 + lax.broadcasted_iota(jnp.int32, (tk, tq), 1)
    ok = (kpos <= tq_pos) & (tq_pos - kpos < WINDOW)
    o_w = _attend_t(q, kw_ref[0, pl.ds(k0, tk), :], _vt_tiles(vwt_ref, (0,), k0 // VT_TILE, n_vt),
                    jnp.where(ok, 0.0, NEG_BIG).astype(F32))

    gt_ref[...] = gate_ref[...].T
    base = pl.program_id(1) * (3 * hpg)
    for h in range(hpg):
        cols = slice(h * tq, (h + 1) * tq)
        o = (gt_ref[pl.ds(base + 3 * h, 1), :] * oc_ref[:, cols]
             + gt_ref[pl.ds(base + 3 * h + 1, 1), :] * os_ref[:, cols]
             + gt_ref[pl.ds(base + 3 * h + 2, 1), :] * o_w[:, cols])
        o_ref[:, h * NSA_DV:(h + 1) * NSA_DV] = o.T.astype(o_ref.dtype)


def _nsa_attention(q, kc, vct, k3, vt3, gates, ovl, batch, seq, hpg, tq=128, tk_sel=256):
    tk = WINDOW + tq
    t = batch * seq
    nq = seq // tq
    n_chunk = seq // CMP_STRIDE
    n_rows = hpg * tq
    n_vt = seq // VT_TILE
    assert tk % VT_TILE == 0 and seq >= tk and seq % tk_sel == 0
    kern = functools.partial(_nsa_attn_kernel, hpg, tq, tk, seq)

    def k_spec(branch):
        return pl.BlockSpec((1, seq, NSA_DK), lambda b, g, qi: (branch * NSA_G + g, b, 0))

    def vt_spec(branch):
        return pl.BlockSpec((1, n_vt, NSA_DV, VT_TILE), lambda b, g, qi: (branch * NSA_G + g, b, 0, 0))

    return pl.pallas_call(
        kern,
        grid=(batch, NSA_G, nq),
        in_specs=[
            pl.BlockSpec((hpg, tq, NSA_DK), lambda b, g, qi: (g, b * nq + qi, 0)),
            pl.BlockSpec((1, n_chunk, NSA_DK), lambda b, g, qi: (b * NSA_G + g, 0, 0)),
            pl.BlockSpec((1, NSA_DV, n_chunk), lambda b, g, qi: (b * NSA_G + g, 0, 0)),
            k_spec(1), vt_spec(1), k_spec(2), vt_spec(2),
            pl.BlockSpec((tq, LANES), lambda b, g, qi: (b * nq + qi, 0)),
            pl.BlockSpec(ovl.shape, lambda b, g, qi: (0, 0)),
        ],
        out_specs=pl.BlockSpec((tq, hpg * NSA_DV), lambda b, g, qi: (b * nq + qi, g)),
        out_shape=jax.ShapeDtypeStruct((t, NSA_G * hpg * NSA_DV), BF16),
        scratch_shapes=[
            pltpu.VMEM((seq // tk_sel, tk_sel, tq), F32),
            pltpu.VMEM((NSA_DV, n_rows), F32),
            pltpu.VMEM((NSA_DV, n_rows), F32),
            pltpu.VMEM((LANES, tq), F32),
        ],
        compiler_params=_params(("parallel", "parallel", "arbitrary")),
        name="nsa_attention",
    )(q, kc, vct, k3, vt3, k3, vt3, gates, ovl)


def _rope_tables(pos_flat):
    pos = pos_flat.astype(F32)[:, None]
    inv64 = ROPE_THETA ** (-jnp.arange(0, MLA_ROPE, 2, dtype=F32) / MLA_ROPE)
    ang = pos * inv64
    c, s = jnp.cos(ang), jnp.sin(ang)
    cos_mla = jnp.concatenate([c, c, c, c], axis=1)
    sin_mla = jnp.concatenate([-s, s, -s, s], axis=1)
    inv192 = ROPE_THETA ** (-jnp.arange(0, NSA_DK, 2, dtype=F32) / NSA_DK)
    ang = pos * inv192
    c, s = jnp.cos(ang), jnp.sin(ang)
    cos_nsa = jnp.concatenate([c, c], axis=1)
    sin_nsa = jnp.concatenate([-s, s], axis=1)
    return cos_mla, sin_mla, cos_nsa, sin_nsa


def _compress_weights(w1, pe, w2, *, transpose_w2):
    width, d_hid = w1.shape[0] // CMP_BLOCK, w1.shape[1]
    hid_p = -(-d_hid // LANES) * LANES
    w1 = jnp.pad(w1, ((0, 0), (0, hid_p - d_hid)))
    w1 = w1.reshape(2, CMP_STRIDE * width, hid_p).astype(BF16)
    pe = jnp.pad(pe.reshape(2, 1, CMP_STRIDE * width), ((0, 0), (0, 7), (0, 0))).astype(BF16)
    w2 = jnp.pad(w2, ((0, hid_p - d_hid), (0, 0))).astype(BF16)
    return w1, pe, (w2.T if transpose_w2 else w2)


def kernel(x, positions, a_norm, a_w_in, a_q_norm, a_w_uq, a_kv_norm, a_w_ukv, a_w_o, s_norm, s_w_kv, s_cmp_pe_k, s_cmp_w1_k, s_cmp_w2_k, s_cmp_pe_v, s_cmp_w1_v, s_cmp_w2_v, b_norm, b_w_in, b_w_o, f_norm, f_w_in, f_w_out, final_norm):
    batch, seq, d = x.shape
    t = batch * seq
    heads = d // 128
    hpg = heads // NSA_G
    d_ff = f_w_out.shape[1]
    n_a, n_b = a_w_in.shape[0], b_w_in.shape[0]
    assert n_a >= 1 and n_b >= 1
    tm = 1024
    sds = jax.ShapeDtypeStruct

    cos_mla, sin_mla, cos_nsa, sin_nsa = _rope_tables(positions.reshape(t))
    row = lambda i, j, k: (i, 0)
    tile = lambda i, j, k: (i, j)
    tab_spec = pl.BlockSpec((tm, LANES), row)
    ssq_spec = pl.BlockSpec((tm, LANES), row)

    def residual_mm(a, w, w_layer, h, gains, *, tm_, tn, name, a_buffers=None, to_round=()):
        blk = pl.BlockSpec((tm_, tn), tile)
        nj = d // tn
        common = dict(tm=tm_, tn=tn, tk=a.shape[1], nj=nj, w_layer=w_layer, name=name,
                      a_buffers=a_buffers)
        if not gains:
            assert not to_round
            h_new = _mm(a, w, _epi_residual, sds((t, d), F32), blk,
                        extras=(h,), extra_specs=(blk,), **common)
            return h_new, None, (), ()
        steps = (t // tm_) * nj
        slabs_in, slabs_out, flats = [], [], []
        for wr, layer in to_round:
            rows, cols = wr.shape[1:]
            slab = _slab_rows(rows, steps)
            n_slabs = rows // slab
            flats.append(wr.reshape(-1, cols))
            step_of = functools.partial(lambda i, j, n: jnp.minimum(i * nj + j, n - 1), n=n_slabs)
            slabs_in.append(pl.BlockSpec(
                (slab, cols), functools.partial(lambda i, j, k, off, s: (off + s(i, j), 0),
                                                off=layer * n_slabs, s=step_of)))
            slabs_out.append(pl.BlockSpec(
                (slab, cols), functools.partial(lambda i, j, k, s: (s(i, j), 0), s=step_of)))
        outs = _mm(a, w, functools.partial(_epi_residual_prenorm, len(gains)),
                   (sds((t, d), F32), sds((t, LANES), F32)) + tuple(sds((t, d), BF16) for _ in gains)
                   + tuple(sds(wr.shape[1:], BF16) for wr, _ in to_round),
                   (blk, pl.BlockSpec((tm_, LANES), row)) + tuple(blk for _ in gains)
                   + tuple(slabs_out),
                   extras=(h,) + tuple(g.reshape(1, d) for g in gains) + tuple(flats),
                   extra_specs=(blk,) + tuple(pl.BlockSpec((1, tn), lambda i, j, k: (0, j))
                                              for _ in gains) + tuple(slabs_in),
                   first_tile_init=_zero_ssq, **common)
        n_g = len(gains)
        return outs[0], outs[1], tuple(outs[2:2 + n_g]), tuple(outs[2 + n_g:])

    def ffn(hb, ssq, h, layer, gains, to_round=()):
        tn = 256
        nj = d_ff // tn
        tm_in = min(2 * tm, t)
        steps = (t // tm_in) * nj
        slab = d_ff // steps
        assert slab * steps == d_ff and slab % 16 == 0
        act, w_out = _mm(
            hb, f_w_in, functools.partial(_epi_swiglu, d),
            (sds((t, d_ff), BF16), sds((d_ff, d), BF16)),
            (pl.BlockSpec((tm_in, tn), tile),
             pl.BlockSpec((slab, d), lambda i, j, k: (i * nj + j, 0))),
            tm=tm_in, tn=tn, tk=d, nj=nj, w_layer=layer, a_buffers=1,
            w_col_maps=(lambda j: j, lambda j: j + nj),
            extras=(ssq, f_w_out),
            extra_specs=(pl.BlockSpec((tm_in, LANES), row),
                         pl.BlockSpec((None, slab, d), lambda i, j, k: (layer, i * nj + j, 0))),
            name="ffn_in")
        return residual_mm(act, w_out, None, h, gains, tm_=tm, tn=tn, name="ffn_out",
                           a_buffers=1, to_round=to_round)

    b_w_in_t = jnp.swapaxes(b_w_in, 1, 2)
    round_w_in_t = _slab_rows(b_w_in_t.shape[1], (t // tm) * (d // 256)) is not None

    h = x.reshape(t, d)
    hb = ssq = None
    a_w_in_t = jnp.swapaxes(a_w_in, 1, 2)

    for i in range(n_a):
        rope_lo = MLA_Q_LORA + MLA_KV_LORA
        w_in_t = jnp.concatenate([a_w_in_t[i], a_w_in_t[i][rope_lo:]], axis=0).astype(BF16)
        tk_in = 1024
        if i == 0:
            a_in, norm_in = h, a_norm[0].reshape(1, d)
            norm_spec = pl.BlockSpec((1, tk_in), lambda i_, j, k: (0, k))
        else:
            a_in, norm_in, norm_spec = hb, ssq, ssq_spec
        cq, ckv, kr = _mm(
            a_in, w_in_t, functools.partial(_epi_mla_in, d),
            (sds((t, MLA_Q_LORA), BF16), sds((t, MLA_KV_LORA), BF16), sds((t, LANES), BF16)),
            (pl.BlockSpec((tm, MLA_Q_LORA), row), pl.BlockSpec((tm, MLA_KV_LORA), row),
             pl.BlockSpec((tm, LANES), row)),
            tm=tm, tn=w_in_t.shape[0], tk=tk_in, nj=1,
            extras=(norm_in, a_q_norm[i].reshape(1, -1), a_kv_norm[i].reshape(1, -1),
                    cos_mla, sin_mla),
            extra_specs=(norm_spec,
                         pl.BlockSpec((1, MLA_Q_LORA), lambda i_, j, k: (0, 0)),
                         pl.BlockSpec((1, MLA_KV_LORA), lambda i_, j, k: (0, 0)),
                         tab_spec, tab_spec),
            w_transposed=True, prenorm=(i == 0), name="mla_in")

        qd = MLA_NOPE + MLA_ROPE
        w_uq = a_w_uq[i].reshape(MLA_Q_LORA, heads, qd)
        w_uq = jnp.concatenate([w_uq[:, :, :MLA_NOPE].reshape(MLA_Q_LORA, heads * MLA_NOPE),
                                w_uq[:, :, MLA_NOPE:].reshape(MLA_Q_LORA, heads * MLA_ROPE)],
                               axis=1).astype(BF16)
        tn = 512
        q = _mm(cq, w_uq,
                functools.partial(_epi_mla_q, heads * MLA_NOPE // tn, qd ** -0.5 * LOG2_E),
                jax.ShapeDtypeStruct((t, heads * qd), BF16),
                pl.BlockSpec((tm, tn), lambda i_, j, k: (i_, j)),
                tm=tm, tn=tn, tk=MLA_Q_LORA, nj=heads * qd // tn,
                extras=(cos_mla, sin_mla), extra_specs=(tab_spec, tab_spec), name="mla_q")
        heads_per_tile = 4
        kn, vt = _mm(ckv, a_w_ukv[i].astype(BF16), _epi_mla_kv,
                     (jax.ShapeDtypeStruct((t, heads * MLA_NOPE), BF16),
                      jax.ShapeDtypeStruct((heads, t // VT_TILE, MLA_V, VT_TILE), BF16)),
                     (pl.BlockSpec((tm, heads_per_tile * MLA_NOPE), lambda i_, j, k: (i_, j)),
                      pl.BlockSpec((heads_per_tile, tm // VT_TILE, MLA_V, VT_TILE),
                                   lambda i_, j, k: (j, i_, 0, 0))),
                     tm=tm, tn=heads_per_tile * (MLA_NOPE + MLA_V), tk=MLA_KV_LORA,
                     nj=heads // heads_per_tile, name="mla_kv")
        o = _mla_attention(q, kn, kr, vt, batch, seq, heads)
        h, ssq, (hb,), _ = residual_mm(o, a_w_o[i].astype(BF16), None, h, [f_norm[i]],
                                       tm_=tm, tn=512, name="mla_out")
        last_a = i + 1 == n_a
        gains = [s_norm, b_norm[0]] if last_a else [a_norm[i + 1]]
        to_round = ()
        if last_a:
            to_round = ((s_w_kv[None], 0), (b_w_o, 0)) + (((b_w_in_t, 0),) if round_w_in_t else ())
        h, ssq, hbs, rounded = ffn(hb, ssq, h, i, gains, to_round)
        hb = hbs[0]

    hb_s, hb_b = hbs
    s_w_kv_b, b_w_o0 = rounded[:2]
    b_w_in_t0 = rounded[2] if round_w_in_t else b_w_in_t[0].astype(BF16)
    n_sets = 3 * NSA_G
    per_tile = NSA_ROPE_TILE // NSA_DK
    tm_rope = tm
    rope_tab = pl.BlockSpec((tm_rope, NSA_DK), row)
    w_kv = s_w_kv_b.reshape(d, n_sets, NSA_DK + NSA_DV)
    w_k = w_kv[:, :, :NSA_DK].reshape(d, n_sets * NSA_DK)
    w_v = w_kv[:, :, NSA_DK:].reshape(d, n_sets * NSA_DV)
    k3 = _mm(hb_s, w_k, functools.partial(_epi_nsa_rope, d, None),
             sds((n_sets, t, NSA_DK), BF16),
             pl.BlockSpec((per_tile, tm_rope, NSA_DK), lambda i_, j, k: (j, i_, 0)),
             tm=tm_rope, tn=NSA_ROPE_TILE, tk=d, nj=n_sets // per_tile,
             extras=(ssq, cos_nsa, sin_nsa),
             extra_specs=(pl.BlockSpec((tm_rope, LANES), row), rope_tab, rope_tab),
             a_buffers=1, name="nsa_k")
    v3, vt3 = _mm(hb_s, w_v, functools.partial(_epi_nsa_v, d),
                  (sds((n_sets, t, NSA_DV), BF16),
                   sds((n_sets, t // VT_TILE, NSA_DV, VT_TILE), BF16)),
                  (pl.BlockSpec((per_tile, tm, NSA_DV), lambda i_, j, k: (j, i_, 0)),
                   pl.BlockSpec((per_tile, tm // VT_TILE, NSA_DV, VT_TILE),
                                lambda i_, j, k: (j, i_, 0, 0))),
                  tm=tm, tn=per_tile * NSA_DV, tk=d, nj=n_sets // per_tile,
                  extras=(ssq,), extra_specs=(ssq_spec,), name="nsa_v")

    ak = k3[:NSA_G].reshape(NSA_G, t // CMP_STRIDE, CMP_STRIDE * NSA_DK)
    av = v3[:NSA_G].reshape(NSA_G, t // CMP_STRIDE, CMP_STRIDE * NSA_DV)
    kc, vct = _compress(ak, av,
                        _compress_weights(s_cmp_w1_k, s_cmp_pe_k, s_cmp_w2_k, transpose_w2=False),
                        _compress_weights(s_cmp_w1_v, s_cmp_pe_v, s_cmp_w2_v, transpose_w2=True),
                        batch, seq)

    n_chunk = seq // CMP_STRIDE
    n_sel = seq // SEL_BLOCK
    c_start = np.arange(n_chunk) * CMP_STRIDE
    j_start = np.arange(n_sel) * SEL_BLOCK
    ovl = ((c_start[None, :] < j_start[:, None] + SEL_BLOCK)
           & (c_start[None, :] + CMP_BLOCK > j_start[:, None])
           & (np.arange(n_chunk)[None, :] < (seq - CMP_BLOCK) // CMP_STRIDE + 1))
    ovl = jnp.asarray(ovl, BF16)

    for i in range(n_b):
        n_q = heads * NSA_DK
        w_qt = b_w_in_t0 if i == 0 else b_w_in_t[i].astype(BF16)
        w_o = b_w_o0 if i == 0 else b_w_o[i].astype(BF16)
        q = _mm(hb_b, w_qt, functools.partial(_epi_nsa_rope, d, NSA_DK ** -0.5 * LOG2_E),
                sds((heads, t, NSA_DK), BF16),
                pl.BlockSpec((per_tile, tm_rope, NSA_DK), lambda i_, j, k: (j, i_, 0)),
                tm=tm_rope, tn=NSA_ROPE_TILE, tk=d, nj=heads // per_tile,
                extras=(ssq, cos_nsa, sin_nsa),
                extra_specs=(pl.BlockSpec((tm_rope, LANES), row), rope_tab, rope_tab),
                a_buffers=1, w_transposed=True, name="nsa_q")
        assert n_q % LANES == 0 and 3 * heads <= LANES
        gates = _mm(hb_b, w_qt, functools.partial(_epi_sigmoid, d), sds((t, LANES), F32),
                    pl.BlockSpec((tm, LANES), row),
                    tm=tm, tn=LANES, tk=d, nj=1, w_col_maps=(lambda j: n_q // LANES,),
                    extras=(ssq,), extra_specs=(ssq_spec,), w_transposed=True, name="nsa_gates")
        o = _nsa_attention(q, kc, vct, k3, vt3, gates, ovl, batch, seq, hpg)
        h, ssq, (hb,), _ = residual_mm(o, w_o, None, h, [f_norm[n_a + i]],
                                       tm_=tm, tn=512, name="nsa_out")
        gains = [b_norm[i + 1]] if i + 1 < n_b else []
        h, ssq, hbs, _ = ffn(hb, ssq, h, n_a + i, gains)
        if hbs:
            hb_b = hbs[0]

    out = _rmsnorm(h, final_norm, F32)
    return out.reshape(batch, seq, d)
```

```python
import functools

import numpy as np
import jax
import jax.numpy as jnp
from jax import lax
from jax.experimental import pallas as pl
from jax.experimental.pallas import tpu as pltpu

F32 = jnp.float32
BF16 = jnp.bfloat16

LANES = 128
VMEM_LIMIT_BYTES = 56 * 1024 * 1024

NORM_EPS = 1e-6
ROPE_THETA = 10000.0

MLA_Q_LORA = 1536
MLA_KV_LORA = 512
MLA_NOPE = 128
MLA_ROPE = 64
MLA_V = 128

NSA_G = 4
NSA_DK = 192
NSA_DV = 128
NSA_HALF = NSA_DK // 2
NSA_ROPE_TILE = 4 * NSA_DK
CMP_BLOCK = 32
CMP_STRIDE = 16
SEL_BLOCK = 64
SEL_TOPK = 16
WINDOW = 512
FORCE_BONUS = 1e4
NEG_BIG = -1e30
LOG2_E = 1.4426950408889634

VT_TILE = LANES
NT_DIMS = (((1,), (1,)), ((), ()))


def _params(sem):
    return pltpu.CompilerParams(dimension_semantics=sem, vmem_limit_bytes=VMEM_LIMIT_BYTES)


def _rmsnorm_kernel(x_ref, g_ref, o_ref):
    x = x_ref[...]
    ms = jnp.mean(x * x, axis=-1, keepdims=True)
    o_ref[...] = (x * lax.rsqrt(ms + NORM_EPS) * g_ref[...]).astype(o_ref.dtype)


def _rmsnorm(x, g, out_dtype, tm=256):
    t, d = x.shape
    return pl.pallas_call(
        _rmsnorm_kernel,
        grid=(t // tm,),
        in_specs=[pl.BlockSpec((tm, d), lambda i: (i, 0)),
                  pl.BlockSpec((1, d), lambda i: (0, 0))],
        out_specs=pl.BlockSpec((tm, d), lambda i: (i, 0)),
        out_shape=jax.ShapeDtypeStruct((t, d), out_dtype),
        compiler_params=_params(("parallel",)),
        name="rmsnorm",
    )(x, g.reshape(1, d))


def _mm_kernel(n_w, n_e, n_o, nk, epilogue, first_tile_init, w_transposed, prenorm, *refs):
    a_ref = refs[0]
    w_refs = refs[1:1 + n_w]
    e_refs = refs[1 + n_w:1 + n_w + n_e]
    o_refs = refs[1 + n_w + n_e:1 + n_w + n_e + n_o]
    acc_refs = refs[1 + n_w + n_e + n_o:]
    j = pl.program_id(1)
    if first_tile_init is not None:
        pl.when((j == 0) & (pl.program_id(2) == 0))(lambda: first_tile_init(o_refs))

    if prenorm:
        g_ref, ssq_ref, acc_refs = e_refs[0], acc_refs[-1], acc_refs[:-1]
        e_refs = (ssq_ref,) + tuple(e_refs[1:])
        xa = a_ref[...]
        part = jnp.broadcast_to(jnp.sum(xa * xa, axis=-1, keepdims=True), ssq_ref.shape)
        first = pl.program_id(2) == 0

        @pl.when(first)
        def _():
            ssq_ref[...] = part

        @pl.when(jnp.logical_not(first))
        def _():
            ssq_ref[...] += part

        a_val = (xa * g_ref[...]).astype(BF16)
    else:
        a_val = a_ref[...]

    def product(w):
        wv = w[...].astype(a_val.dtype)
        if w_transposed:
            return lax.dot_general(a_val, wv, NT_DIMS, preferred_element_type=F32)
        return jnp.dot(a_val, wv, preferred_element_type=F32)

    if nk == 1:
        epilogue([product(w) for w in w_refs], e_refs, o_refs, j)
        return
    k = pl.program_id(2)

    @pl.when(k == 0)
    def _():
        for w, acc in zip(w_refs, acc_refs):
            acc[...] = product(w)

    @pl.when(k > 0)
    def _():
        for w, acc in zip(w_refs, acc_refs):
            acc[...] += product(w)

    @pl.when(k == nk - 1)
    def _():
        epilogue([acc[...] for acc in acc_refs], e_refs, o_refs, j)


def _mm(a, w, epilogue, out_shape, out_specs, *, tm, tn, tk, nj, w_col_maps=None, w_layer=None,
        extras=(), extra_specs=(), a_buffers=None, first_tile_init=None, w_transposed=False,
        prenorm=False, name="matmul"):
    m, kdim = a.shape
    ni, nk = m // tm, kdim // tk
    assert ni * tm == m and nk * tk == kdim
    if w_col_maps is None:
        w_col_maps = (lambda j: j,)
    n_w = len(w_col_maps)
    single = not isinstance(out_shape, (tuple, list))
    out_shapes = (out_shape,) if single else tuple(out_shape)
    out_specs_t = (out_specs,) if single else tuple(out_specs)
    a_kwargs = {} if a_buffers is None else {"pipeline_mode": pl.Buffered(a_buffers)}
    in_specs = [pl.BlockSpec((tm, tk), lambda i, j, k: (i, k), **a_kwargs)]
    for cm in w_col_maps:
        if w_transposed:
            assert w_layer is None
            in_specs.append(pl.BlockSpec((tn, tk), functools.partial(
                lambda i, j, k, cm: (cm(j), k), cm=cm)))
        elif w_layer is None:
            in_specs.append(pl.BlockSpec((tk, tn), functools.partial(
                lambda i, j, k, cm: (k, cm(j)), cm=cm)))
        else:
            in_specs.append(pl.BlockSpec((None, tk, tn), functools.partial(
                lambda i, j, k, cm: (w_layer, k, cm(j)), cm=cm)))
    in_specs += list(extra_specs)
    scratch = [pltpu.VMEM((tm, tn), F32) for _ in range(n_w)] if nk > 1 else []
    if prenorm:
        scratch.append(pltpu.VMEM((tm, LANES), F32))
    kern = functools.partial(_mm_kernel, n_w, len(extras), len(out_shapes), nk, epilogue,
                             first_tile_init, w_transposed, prenorm)
    res = pl.pallas_call(
        kern,
        grid=(ni, nj, nk),
        in_specs=in_specs,
        out_specs=out_specs_t,
        out_shape=out_shapes,
        scratch_shapes=scratch,
        compiler_params=_params(("parallel", "arbitrary", "arbitrary")),
        name=name,
    )(a, *([w] * n_w), *extras)
    return res[0] if single else res


def _rope_pairs_64(x, cos2, sin2):
    lane = lax.broadcasted_iota(jnp.int32, x.shape, 1)
    first_half = (lane % MLA_ROPE) < (MLA_ROPE // 2)
    partner = jnp.where(first_half,
                        pltpu.roll(x, LANES - MLA_ROPE // 2, 1),
                        pltpu.roll(x, MLA_ROPE // 2, 1))
    return x * cos2 + partner * sin2


def _store_vt_tiles(vt_ref, lead, v):
    for c in range(v.shape[0] // VT_TILE):
        vt_ref[lead + (c,)] = v[c * VT_TILE:(c + 1) * VT_TILE, :].T.astype(vt_ref.dtype)


def _row_scale(acc, ssq_ref, d_model):
    rstd = lax.rsqrt(ssq_ref[...] * (1.0 / d_model) + NORM_EPS)
    return acc * jnp.concatenate([rstd] * (acc.shape[1] // LANES), axis=1)


def _epi_residual(accs, e_refs, o_refs, j):
    o_refs[0][...] = e_refs[0][...] + accs[0]


def _slab_rows(rows, steps):
    return next((s for s in range(16, rows + 1, 16) if rows % s == 0 and rows // s <= steps), None)


def _zero_ssq(o_refs):
    o_refs[1][...] = jnp.zeros(o_refs[1].shape, F32)


def _epi_residual_prenorm(n_gains, accs, e_refs, o_refs, j):
    h = e_refs[0][...] + accs[0]
    o_refs[0][...] = h
    ssq_ref = o_refs[1]
    ssq_ref[...] += jnp.broadcast_to(jnp.sum(h * h, axis=-1, keepdims=True), ssq_ref.shape)
    for g_ref, hb_ref in zip(e_refs[1:1 + n_gains], o_refs[2:2 + n_gains]):
        hb_ref[...] = (h * g_ref[...]).astype(hb_ref.dtype)
    for w_ref, wb_ref in zip(e_refs[1 + n_gains:], o_refs[2 + n_gains:]):
        wb_ref[...] = w_ref[...].astype(wb_ref.dtype)


def _epi_swiglu(d_model, accs, e_refs, o_refs, j):
    a = _row_scale(accs[0], e_refs[0], d_model)
    b = _row_scale(accs[1], e_refs[0], d_model)
    o_refs[0][...] = (a * jax.nn.sigmoid(a) * b).astype(o_refs[0].dtype)
    if len(o_refs) > 1:
        o_refs[1][...] = e_refs[1][...].astype(o_refs[1].dtype)


def _epi_sigmoid(d_model, accs, e_refs, o_refs, j):
    o_refs[0][...] = jax.nn.sigmoid(_row_scale(accs[0], e_refs[0], d_model))


def _epi_mla_in(d_model, accs, e_refs, o_refs, j):
    ssq_ref, gq_ref, gkv_ref, cos_ref, sin_ref = e_refs
    u = _row_scale(accs[0], ssq_ref, d_model)
    cq = u[:, :MLA_Q_LORA]
    ms = jnp.mean(cq * cq, axis=-1, keepdims=True)
    o_refs[0][...] = (cq * lax.rsqrt(ms + NORM_EPS) * gq_ref[...]).astype(BF16)
    ckv = u[:, MLA_Q_LORA:MLA_Q_LORA + MLA_KV_LORA]
    ms = jnp.mean(ckv * ckv, axis=-1, keepdims=True)
    o_refs[1][...] = (ckv * lax.rsqrt(ms + NORM_EPS) * gkv_ref[...]).astype(BF16)
    kr = u[:, MLA_Q_LORA + MLA_KV_LORA:]
    o_refs[2][...] = _rope_pairs_64(kr, cos_ref[...], sin_ref[...]).astype(BF16)


def _epi_mla_q(n_nope_tiles, scale, accs, e_refs, o_refs, j):
    cos_ref, sin_ref = e_refs
    acc = accs[0]

    @pl.when(j < n_nope_tiles)
    def _():
        o_refs[0][...] = (acc * scale).astype(BF16)

    @pl.when(j >= n_nope_tiles)
    def _():
        for c in range(acc.shape[1] // LANES):
            x = acc[:, c * LANES:(c + 1) * LANES]
            r = _rope_pairs_64(x, cos_ref[...], sin_ref[...])
            o_refs[0][:, c * LANES:(c + 1) * LANES] = (r * scale).astype(BF16)


def _epi_mla_kv(accs, e_refs, o_refs, j):
    acc = accs[0]
    kn_ref, vt_ref = o_refs
    per = MLA_NOPE + MLA_V
    for r in range(acc.shape[1] // per):
        kn_ref[:, r * MLA_NOPE:(r + 1) * MLA_NOPE] = acc[:, r * per:r * per + MLA_NOPE].astype(BF16)
        _store_vt_tiles(vt_ref, (r,), acc[:, r * per + MLA_NOPE:(r + 1) * per])


def _rope_192(x, cos, sin):
    width = x.shape[1]
    lane = lax.broadcasted_iota(jnp.int32, x.shape, 1)
    partner = jnp.where((lane % NSA_DK) < NSA_HALF,
                        pltpu.roll(x, width - NSA_HALF, 1),
                        pltpu.roll(x, NSA_HALF, 1))
    return x * cos + partner * sin


def _epi_nsa_rope(d_model, scale, accs, e_refs, o_refs, j):
    ssq_ref, cos_ref, sin_ref = e_refs
    acc = _row_scale(accs[0], ssq_ref, d_model)
    reps = acc.shape[1] // NSA_DK
    roped = _rope_192(acc, jnp.concatenate([cos_ref[...]] * reps, axis=1),
                      jnp.concatenate([sin_ref[...]] * reps, axis=1))
    if scale is not None:
        roped = roped * scale
    for r in range(acc.shape[1] // NSA_DK):
        o_refs[0][r] = roped[:, r * NSA_DK:(r + 1) * NSA_DK].astype(BF16)


def _epi_nsa_v(d_model, accs, e_refs, o_refs, j):
    v_ref, vt_ref = o_refs
    acc = _row_scale(accs[0], e_refs[0], d_model)
    for r in range(acc.shape[1] // NSA_DV):
        v = acc[:, r * NSA_DV:(r + 1) * NSA_DV]
        v_ref[r] = v.astype(BF16)
        _store_vt_tiles(vt_ref, (r,), v)


SUM_ROWS = 16


def _attend_t(q, k, vt, bias):
    dv = vt.shape[0]
    s = lax.dot_general(k, q, NT_DIMS, preferred_element_type=F32)
    bias = jnp.concatenate([bias] * (s.shape[1] // bias.shape[1]), axis=1)
    free = s.shape[0] - bias.shape[0]
    s = s + bias if free == 0 else jnp.concatenate([s[:free], s[free:] + bias], axis=0)
    p = jnp.exp2(s - jnp.max(s, axis=0, keepdims=True)).astype(BF16)
    vt1 = jnp.concatenate([vt, jnp.ones((SUM_ROWS, vt.shape[1]), BF16)], axis=0)
    acc = jnp.dot(vt1, p, preferred_element_type=F32)
    return acc[:dv] / acc[dv:dv + 1]


def _vt_tiles(vt_ref, lead, c0, n):
    return jnp.concatenate([vt_ref[lead + (c0 + c,)] for c in range(n)], axis=1)


def _mla_attn_kernel(t_blk, seq, qn_ref, qr_ref, kn_ref, kr_ref, vt_ref, o_ref):
    lane_half = lax.broadcasted_iota(jnp.int32, (t_blk, LANES), 1) // MLA_ROPE
    kpos = lax.broadcasted_iota(jnp.int32, (t_blk, t_blk), 0)
    qpos = lax.broadcasted_iota(jnp.int32, (t_blk, t_blk), 1)
    diag_bias = jnp.where(kpos <= qpos, 0.0, NEG_BIG).astype(F32)

    for qi in range(seq // t_blk):
        q0 = qi * t_blk
        n_keys = q0 + t_blk
        qr_pair = qr_ref[q0:q0 + t_blk, :]
        for hh in range(2):
            qr = jnp.where(lane_half == hh, qr_pair, jnp.zeros_like(qr_pair))
            q = jnp.concatenate([qn_ref[q0:q0 + t_blk, hh * MLA_NOPE:(hh + 1) * MLA_NOPE], qr], axis=1)
            k = jnp.concatenate([kn_ref[0:n_keys, hh * MLA_NOPE:(hh + 1) * MLA_NOPE],
                                 kr_ref[0:n_keys, :]], axis=1)
            vt = _vt_tiles(vt_ref, (hh,), 0, n_keys // VT_TILE)
            o_ref[q0:q0 + t_blk, hh * MLA_V:(hh + 1) * MLA_V] = (
                _attend_t(q, k, vt, diag_bias).T.astype(o_ref.dtype))


def _mla_attention(q, kn, kr, vt, batch, seq, heads, t_blk=512):
    t = batch * seq
    n_nope_blk = heads * MLA_NOPE // LANES
    n_vt = seq // VT_TILE
    kern = functools.partial(_mla_attn_kernel, t_blk, seq)
    return pl.pallas_call(
        kern,
        grid=(batch, heads // 2),
        in_specs=[
            pl.BlockSpec((seq, 2 * MLA_NOPE), lambda b, hp: (b, hp)),
            pl.BlockSpec((seq, LANES), lambda b, hp: (b, n_nope_blk + hp)),
            pl.BlockSpec((seq, 2 * MLA_NOPE), lambda b, hp: (b, hp)),
            pl.BlockSpec((seq, LANES), lambda b, hp: (b, 0)),
            pl.BlockSpec((2, n_vt, MLA_V, VT_TILE), lambda b, hp: (hp, b, 0, 0)),
        ],
        out_specs=pl.BlockSpec((seq, 2 * MLA_V), lambda b, hp: (b, hp)),
        out_shape=jax.ShapeDtypeStruct((t, heads * MLA_V), BF16),
        compiler_params=_params(("parallel", "parallel")),
        name="mla_attention",
    )(q, q, kn, kr, vt)


def _compress_mlp(a, w1_ref, pe_ref, w2_ref, transposed_out):
    y_lo = jnp.dot(a, w1_ref[0], preferred_element_type=F32)
    y_hi = jnp.dot(a, w1_ref[1], preferred_element_type=F32)
    pe_c = (jnp.dot(pe_ref[0], w1_ref[0], preferred_element_type=F32)
            + jnp.dot(pe_ref[1], w1_ref[1], preferred_element_type=F32))
    n = y_hi.shape[0]
    hid = y_lo + pltpu.roll(y_hi, n - 1, 0) + pe_c[0:1, :]
    hid = jax.nn.gelu(hid, approximate=True)
    if transposed_out:
        return lax.dot_general(w2_ref[...], hid.astype(BF16), NT_DIMS, preferred_element_type=F32)
    return jnp.dot(hid.astype(BF16), w2_ref[...], preferred_element_type=F32)


def _compress_kernel(ak_ref, av_ref, w1k_ref, pek_ref, w2k_ref, w1v_ref, pev_ref, w2vt_ref,
                     kc_ref, vct_ref):
    kc_ref[0] = _compress_mlp(ak_ref[0], w1k_ref, pek_ref, w2k_ref, False).astype(BF16)
    vct_ref[0] = _compress_mlp(av_ref[0], w1v_ref, pev_ref, w2vt_ref, True).astype(BF16)


def _compress(ak, av, wk, wv, batch, seq):
    n_chunk = seq // CMP_STRIDE
    const3 = lambda b, g: (0, 0, 0)
    const2 = lambda b, g: (0, 0)

    def wspecs(w):
        w1, pe, w2 = w
        return [pl.BlockSpec(w1.shape, const3), pl.BlockSpec(pe.shape, const3),
                pl.BlockSpec(w2.shape, const2)]

    return pl.pallas_call(
        _compress_kernel,
        grid=(batch, NSA_G),
        in_specs=[pl.BlockSpec((1, n_chunk, ak.shape[2]), lambda b, g: (g, b, 0)),
                  pl.BlockSpec((1, n_chunk, av.shape[2]), lambda b, g: (g, b, 0))]
        + wspecs(wk) + wspecs(wv),
        out_specs=(pl.BlockSpec((1, n_chunk, NSA_DK), lambda b, g: (b * NSA_G + g, 0, 0)),
                   pl.BlockSpec((1, NSA_DV, n_chunk), lambda b, g: (b * NSA_G + g, 0, 0))),
        out_shape=(jax.ShapeDtypeStruct((batch * NSA_G, n_chunk, NSA_DK), BF16),
                   jax.ShapeDtypeStruct((batch * NSA_G, NSA_DV, n_chunk), BF16)),
        compiler_params=_params(("parallel", "parallel")),
        name="nsa_compress",
    )(ak, av, *wk, *wv)


def _nsa_attn_kernel(hpg, tq, tk, seq, q_ref, kc_ref, vct_ref, ks_ref, vst_ref, kw_ref, vwt_ref,
                     gate_ref, ovl_ref, o_ref, bias_ref, oc_ref, os_ref, gt_ref):
    qi = pl.program_id(2)
    t0 = qi * tq
    n_rows = hpg * tq
    n_cmp = kc_ref.shape[1]
    n_sel = seq // SEL_BLOCK
    n_vt = tk // VT_TILE
    q = q_ref[...].reshape(n_rows, NSA_DK)

    def tile_heads(x):
        return jnp.concatenate([x] * hpg, axis=1)

    s = lax.dot_general(kc_ref[0], q, NT_DIMS, preferred_element_type=F32)
    cidx = lax.broadcasted_iota(jnp.int32, (n_cmp, tq), 0)
    tpos_c = t0 + lax.broadcasted_iota(jnp.int32, (n_cmp, tq), 1)
    mask_c = tile_heads((cidx * CMP_STRIDE + CMP_BLOCK - 1) <= tpos_c)
    s = jnp.where(mask_c, s, NEG_BIG)
    mx = jnp.max(s, axis=0, keepdims=True)
    e = jnp.where(mask_c, jnp.exp2(s - mx), 0.0)
    p = e / jnp.maximum(jnp.sum(e, axis=0, keepdims=True), 1e-30)
    oc_ref[...] = jnp.dot(vct_ref[0], p.astype(BF16), preferred_element_type=F32)

    psum = p[:, 0:tq]
    for h in range(1, hpg):
        psum = psum + p[:, h * tq:(h + 1) * tq]
    p1 = psum.astype(BF16)
    p2 = (psum - p1.astype(F32)).astype(BF16)
    p3 = (psum - p1.astype(F32) - p2.astype(F32)).astype(BF16)
    ovl = ovl_ref[...]
    imp = (jnp.dot(ovl, p1, preferred_element_type=F32)
           + jnp.dot(ovl, p2, preferred_element_type=F32)
           + jnp.dot(ovl, p3, preferred_element_type=F32))
    jrow = lax.broadcasted_iota(jnp.int32, (n_sel, tq), 0)
    tpos = t0 + lax.broadcasted_iota(jnp.int32, (n_sel, tq), 1)
    valid = jrow * SEL_BLOCK <= tpos
    cur = jnp.right_shift(tpos, SEL_BLOCK.bit_length() - 1)
    forced = valid & ((jrow == 0) | (jrow == cur) | (jrow == cur - 1))
    score = jnp.where(valid, imp + jnp.where(forced, FORCE_BONUS, 0.0), -jnp.inf)
    rank = jnp.zeros((n_sel, tq), jnp.int32)
    for i in range(n_sel):
        row = score[i:i + 1, :]
        beats = (row > score) | ((row == score) & (jrow > i))
        rank = rank + beats.astype(jnp.int32)
    sel_bias = jnp.where(valid & (rank < SEL_TOPK), 0.0, NEG_BIG).astype(F32)

    tk_s = bias_ref.shape[1]
    n_sel_tiles = (t0 + tq - 1) // tk_s + 1
    blocks_per_tile = tk_s // SEL_BLOCK
    for jj in range(seq // tk_s):
        @pl.when(jj < n_sel_tiles)
        def _():
            rows = [jnp.broadcast_to(sel_bias[b:b + 1, :], (SEL_BLOCK, tq))
                    for b in range(jj * blocks_per_tile, (jj + 1) * blocks_per_tile)]
            kpos = jj * tk_s + lax.broadcasted_iota(jnp.int32, (tk_s, tq), 0)
            tq_pos = t0 + lax.broadcasted_iota(jnp.int32, (tk_s, tq), 1)
            bias_ref[jj] = jnp.where(kpos <= tq_pos, jnp.concatenate(rows, axis=0), NEG_BIG)

    for n in range(1, seq // tk_s + 1):
        @pl.when(n_sel_tiles == n)
        def _():
            keys = n * tk_s
            vt = _vt_tiles(vst_ref, (0,), 0, keys // VT_TILE)
            os_ref[...] = _attend_t(q, ks_ref[0, 0:keys, :], vt, bias_ref[0:n].reshape(keys, tq))

    k0 = pl.multiple_of(jnp.maximum(t0 + tq - tk, 0), VT_TILE)
    kpos = k0 + lax.broadcasted_iota(jnp.int32, (tk, tq), 0)
    tq_pos = t0 + lax.broadcasted_iota(jnp.int32, (tk, tq), 1)
    ok = (kpos <= tq_pos) & (tq_pos - kpos < WINDOW)
    o_w = _attend_t(q, kw_ref[0, pl.ds(k0, tk), :], _vt_tiles(vwt_ref, (0,), k0 // VT_TILE, n_vt),
                    jnp.where(ok, 0.0, NEG_BIG).astype(F32))

    gt_ref[...] = gate_ref[...].T
    base = pl.program_id(1) * (3 * hpg)
    for h in range(hpg):
        cols = slice(h * tq, (h + 1) * tq)
        o = (gt_ref[pl.ds(base + 3 * h, 1), :] * oc_ref[:, cols]
             + gt_ref[pl.ds(base + 3 * h + 1, 1), :] * os_ref[:, cols]
             + gt_ref[pl.ds(base + 3 * h + 2, 1), :] * o_w[:, cols])
        o_ref[:, h * NSA_DV:(h + 1) * NSA_DV] = o.T.astype(o_ref.dtype)


def _nsa_attention(q, kc, vct, k3, vt3, gates, ovl, batch, seq, hpg, tq=128, tk_sel=512):
    tk = WINDOW + tq
    t = batch * seq
    nq = seq // tq
    n_chunk = seq // CMP_STRIDE
    n_rows = hpg * tq
    n_vt = seq // VT_TILE
    assert tk % VT_TILE == 0 and seq >= tk and seq % tk_sel == 0
    kern = functools.partial(_nsa_attn_kernel, hpg, tq, tk, seq)

    def k_spec(branch):
        return pl.BlockSpec((1, seq, NSA_DK), lambda b, g, qi: (branch * NSA_G + g, b, 0))

    def vt_spec(branch):
        return pl.BlockSpec((1, n_vt, NSA_DV, VT_TILE), lambda b, g, qi: (branch * NSA_G + g, b, 0, 0))

    return pl.pallas_call(
        kern,
        grid=(batch, NSA_G, nq),
        in_specs=[
            pl.BlockSpec((hpg, tq, NSA_DK), lambda b, g, qi: (g, b * nq + qi, 0)),
            pl.BlockSpec((1, n_chunk, NSA_DK), lambda b, g, qi: (b * NSA_G + g, 0, 0)),
            pl.BlockSpec((1, NSA_DV, n_chunk), lambda b, g, qi: (b * NSA_G + g, 0, 0)),
            k_spec(1), vt_spec(1), k_spec(2), vt_spec(2),
            pl.BlockSpec((tq, LANES), lambda b, g, qi: (b * nq + qi, 0)),
            pl.BlockSpec(ovl.shape, lambda b, g, qi: (0, 0)),
        ],
        out_specs=pl.BlockSpec((tq, hpg * NSA_DV), lambda b, g, qi: (b * nq + qi, g)),
        out_shape=jax.ShapeDtypeStruct((t, NSA_G * hpg * NSA_DV), BF16),
        scratch_shapes=[
            pltpu.VMEM((seq // tk_sel, tk_sel, tq), F32),
            pltpu.VMEM((NSA_DV, n_rows), F32),
            pltpu.VMEM((NSA_DV, n_rows), F32),
            pltpu.VMEM((LANES, tq), F32),
        ],
        compiler_params=_params(("parallel", "parallel", "arbitrary")),
        name="nsa_attention",
    )(q, kc, vct, k3, vt3, k3, vt3, gates, ovl)


def _rope_tables(pos_flat):
    pos = pos_flat.astype(F32)[:, None]
    inv64 = ROPE_THETA ** (-jnp.arange(0, MLA_ROPE, 2, dtype=F32) / MLA_ROPE)
    ang = pos * inv64
    c, s = jnp.cos(ang), jnp.sin(ang)
    cos_mla = jnp.concatenate([c, c, c, c], axis=1)
    sin_mla = jnp.concatenate([-s, s, -s, s], axis=1)
    inv192 = ROPE_THETA ** (-jnp.arange(0, NSA_DK, 2, dtype=F32) / NSA_DK)
    ang = pos * inv192
    c, s = jnp.cos(ang), jnp.sin(ang)
    cos_nsa = jnp.concatenate([c, c], axis=1)
    sin_nsa = jnp.concatenate([-s, s], axis=1)
    return cos_mla, sin_mla, cos_nsa, sin_nsa


def _compress_weights(w1, pe, w2, *, transpose_w2):
    width, d_hid = w1.shape[0] // CMP_BLOCK, w1.shape[1]
    hid_p = -(-d_hid // LANES) * LANES
    w1 = jnp.pad(w1, ((0, 0), (0, hid_p - d_hid)))
    w1 = w1.reshape(2, CMP_STRIDE * width, hid_p).astype(BF16)
    pe = jnp.pad(pe.reshape(2, 1, CMP_STRIDE * width), ((0, 0), (0, 7), (0, 0))).astype(BF16)
    w2 = jnp.pad(w2, ((0, hid_p - d_hid), (0, 0))).astype(BF16)
    return w1, pe, (w2.T if transpose_w2 else w2)


def kernel(x, positions, a_norm, a_w_in, a_q_norm, a_w_uq, a_kv_norm, a_w_ukv, a_w_o, s_norm, s_w_kv, s_cmp_pe_k, s_cmp_w1_k, s_cmp_w2_k, s_cmp_pe_v, s_cmp_w1_v, s_cmp_w2_v, b_norm, b_w_in, b_w_o, f_norm, f_w_in, f_w_out, final_norm):
    batch, seq, d = x.shape
    t = batch * seq
    heads = d // 128
    hpg = heads // NSA_G
    d_ff = f_w_out.shape[1]
    n_a, n_b = a_w_in.shape[0], b_w_in.shape[0]
    assert n_a >= 1 and n_b >= 1
    tm = 1024
    sds = jax.ShapeDtypeStruct

    cos_mla, sin_mla, cos_nsa, sin_nsa = _rope_tables(positions.reshape(t))
    row = lambda i, j, k: (i, 0)
    tile = lambda i, j, k: (i, j)
    tab_spec = pl.BlockSpec((tm, LANES), row)
    ssq_spec = pl.BlockSpec((tm, LANES), row)

    def residual_mm(a, w, w_layer, h, gains, *, tm_, tn, name, a_buffers=None, to_round=()):
        blk = pl.BlockSpec((tm_, tn), tile)
        nj = d // tn
        common = dict(tm=tm_, tn=tn, tk=a.shape[1], nj=nj, w_layer=w_layer, name=name,
                      a_buffers=a_buffers)
        if not gains:
            assert not to_round
            h_new = _mm(a, w, _epi_residual, sds((t, d), F32), blk,
                        extras=(h,), extra_specs=(blk,), **common)
            return h_new, None, (), ()
        steps = (t // tm_) * nj
        slabs_in, slabs_out, flats = [], [], []
        for wr, layer in to_round:
            rows, cols = wr.shape[1:]
            slab = _slab_rows(rows, steps)
            n_slabs = rows // slab
            flats.append(wr.reshape(-1, cols))
            step_of = functools.partial(lambda i, j, n: jnp.minimum(i * nj + j, n - 1), n=n_slabs)
            slabs_in.append(pl.BlockSpec(
                (slab, cols), functools.partial(lambda i, j, k, off, s: (off + s(i, j), 0),
                                                off=layer * n_slabs, s=step_of)))
            slabs_out.append(pl.BlockSpec(
                (slab, cols), functools.partial(lambda i, j, k, s: (s(i, j), 0), s=step_of)))
        outs = _mm(a, w, functools.partial(_epi_residual_prenorm, len(gains)),
                   (sds((t, d), F32), sds((t, LANES), F32)) + tuple(sds((t, d), BF16) for _ in gains)
                   + tuple(sds(wr.shape[1:], BF16) for wr, _ in to_round),
                   (blk, pl.BlockSpec((tm_, LANES), row)) + tuple(blk for _ in gains)
                   + tuple(slabs_out),
                   extras=(h,) + tuple(g.reshape(1, d) for g in gains) + tuple(flats),
                   extra_specs=(blk,) + tuple(pl.BlockSpec((1, tn), lambda i, j, k: (0, j))
                                              for _ in gains) + tuple(slabs_in),
                   first_tile_init=_zero_ssq, **common)
        n_g = len(gains)
        return outs[0], outs[1], tuple(outs[2:2 + n_g]), tuple(outs[2 + n_g:])

    def ffn(hb, ssq, h, layer, gains, to_round=()):
        tn = 256
        nj = d_ff // tn
        tm_in = min(2 * tm, t)
        steps = (t // tm_in) * nj
        slab = d_ff // steps
        assert slab * steps == d_ff and slab % 16 == 0
        act, w_out = _mm(
            hb, f_w_in, functools.partial(_epi_swiglu, d),
            (sds((t, d_ff), BF16), sds((d_ff, d), BF16)),
            (pl.BlockSpec((tm_in, tn), tile),
             pl.BlockSpec((slab, d), lambda i, j, k: (i * nj + j, 0))),
            tm=tm_in, tn=tn, tk=d, nj=nj, w_layer=layer, a_buffers=1,
            w_col_maps=(lambda j: j, lambda j: j + nj),
            extras=(ssq, f_w_out),
            extra_specs=(pl.BlockSpec((tm_in, LANES), row),
                         pl.BlockSpec((None, slab, d), lambda i, j, k: (layer, i * nj + j, 0))),
            name="ffn_in")
        return residual_mm(act, w_out, None, h, gains, tm_=tm, tn=tn, name="ffn_out",
                           a_buffers=1, to_round=to_round)

    b_w_in_t = jnp.swapaxes(b_w_in, 1, 2)
    round_w_in_t = _slab_rows(b_w_in_t.shape[1], (t // tm) * (d // 256)) is not None

    h = x.reshape(t, d)
    hb = ssq = None
    a_w_in_t = jnp.swapaxes(a_w_in, 1, 2)

    for i in range(n_a):
        rope_lo = MLA_Q_LORA + MLA_KV_LORA
        w_in_t = jnp.concatenate([a_w_in_t[i], a_w_in_t[i][rope_lo:]], axis=0).astype(BF16)
        tk_in = 1024
        if i == 0:
            a_in, norm_in = h, a_norm[0].reshape(1, d)
            norm_spec = pl.BlockSpec((1, tk_in), lambda i_, j, k: (0, k))
        else:
            a_in, norm_in, norm_spec = hb, ssq, ssq_spec
        cq, ckv, kr = _mm(
            a_in, w_in_t, functools.partial(_epi_mla_in, d),
            (sds((t, MLA_Q_LORA), BF16), sds((t, MLA_KV_LORA), BF16), sds((t, LANES), BF16)),
            (pl.BlockSpec((tm, MLA_Q_LORA), row), pl.BlockSpec((tm, MLA_KV_LORA), row),
             pl.BlockSpec((tm, LANES), row)),
            tm=tm, tn=w_in_t.shape[0], tk=tk_in, nj=1,
            extras=(norm_in, a_q_norm[i].reshape(1, -1), a_kv_norm[i].reshape(1, -1),
                    cos_mla, sin_mla),
            extra_specs=(norm_spec,
                         pl.BlockSpec((1, MLA_Q_LORA), lambda i_, j, k: (0, 0)),
                         pl.BlockSpec((1, MLA_KV_LORA), lambda i_, j, k: (0, 0)),
                         tab_spec, tab_spec),
            w_transposed=True, prenorm=(i == 0), name="mla_in")

        qd = MLA_NOPE + MLA_ROPE
        w_uq = a_w_uq[i].reshape(MLA_Q_LORA, heads, qd)
        w_uq = jnp.concatenate([w_uq[:, :, :MLA_NOPE].reshape(MLA_Q_LORA, heads * MLA_NOPE),
                                w_uq[:, :, MLA_NOPE:].reshape(MLA_Q_LORA, heads * MLA_ROPE)],
                               axis=1).astype(BF16)
        tn = 512
        q = _mm(cq, w_uq,
                functools.partial(_epi_mla_q, heads * MLA_NOPE // tn, qd ** -0.5 * LOG2_E),
                jax.ShapeDtypeStruct((t, heads * qd), BF16),
                pl.BlockSpec((tm, tn), lambda i_, j, k: (i_, j)),
                tm=tm, tn=tn, tk=MLA_Q_LORA, nj=heads * qd // tn,
                extras=(cos_mla, sin_mla), extra_specs=(tab_spec, tab_spec), name="mla_q")
        heads_per_tile = 4
        kn, vt = _mm(ckv, a_w_ukv[i].astype(BF16), _epi_mla_kv,
                     (jax.ShapeDtypeStruct((t, heads * MLA_NOPE), BF16),
                      jax.ShapeDtypeStruct((heads, t // VT_TILE, MLA_V, VT_TILE), BF16)),
                     (pl.BlockSpec((tm, heads_per_tile * MLA_NOPE), lambda i_, j, k: (i_, j)),
                      pl.BlockSpec((heads_per_tile, tm // VT_TILE, MLA_V, VT_TILE),
                                   lambda i_, j, k: (j, i_, 0, 0))),
                     tm=tm, tn=heads_per_tile * (MLA_NOPE + MLA_V), tk=MLA_KV_LORA,
                     nj=heads // heads_per_tile, name="mla_kv")
        o = _mla_attention(q, kn, kr, vt, batch, seq, heads)
        h, ssq, (hb,), _ = residual_mm(o, a_w_o[i].astype(BF16), None, h, [f_norm[i]],
                                       tm_=tm, tn=512, name="mla_out")
        last_a = i + 1 == n_a
        gains = [s_norm, b_norm[0]] if last_a else [a_norm[i + 1]]
        to_round = ()
        if last_a:
            to_round = ((s_w_kv[None], 0), (b_w_o, 0)) + (((b_w_in_t, 0),) if round_w_in_t else ())
        h, ssq, hbs, rounded = ffn(hb, ssq, h, i, gains, to_round)
        hb = hbs[0]

    hb_s, hb_b = hbs
    s_w_kv_b, b_w_o0 = rounded[:2]
    b_w_in_t0 = rounded[2] if round_w_in_t else b_w_in_t[0].astype(BF16)
    n_sets = 3 * NSA_G
    per_tile = NSA_ROPE_TILE // NSA_DK
    tm_rope = tm
    rope_tab = pl.BlockSpec((tm_rope, NSA_DK), row)
    w_kv = s_w_kv_b.reshape(d, n_sets, NSA_DK + NSA_DV)
    w_k = w_kv[:, :, :NSA_DK].reshape(d, n_sets * NSA_DK)
    w_v = w_kv[:, :, NSA_DK:].reshape(d, n_sets * NSA_DV)
    k3 = _mm(hb_s, w_k, functools.partial(_epi_nsa_rope, d, None),
             sds((n_sets, t, NSA_DK), BF16),
             pl.BlockSpec((per_tile, tm_rope, NSA_DK), lambda i_, j, k: (j, i_, 0)),
             tm=tm_rope, tn=NSA_ROPE_TILE, tk=d, nj=n_sets // per_tile,
             extras=(ssq, cos_nsa, sin_nsa),
             extra_specs=(pl.BlockSpec((tm_rope, LANES), row), rope_tab, rope_tab),
             a_buffers=1, name="nsa_k")
    v3, vt3 = _mm(hb_s, w_v, functools.partial(_epi_nsa_v, d),
                  (sds((n_sets, t, NSA_DV), BF16),
                   sds((n_sets, t // VT_TILE, NSA_DV, VT_TILE), BF16)),
                  (pl.BlockSpec((per_tile, tm, NSA_DV), lambda i_, j, k: (j, i_, 0)),
                   pl.BlockSpec((per_tile, tm // VT_TILE, NSA_DV, VT_TILE),
                                lambda i_, j, k: (j, i_, 0, 0))),
                  tm=tm, tn=per_tile * NSA_DV, tk=d, nj=n_sets // per_tile,
                  extras=(ssq,), extra_specs=(ssq_spec,), name="nsa_v")

    ak = k3[:NSA_G].reshape(NSA_G, t // CMP_STRIDE, CMP_STRIDE * NSA_DK)
    av = v3[:NSA_G].reshape(NSA_G, t // CMP_STRIDE, CMP_STRIDE * NSA_DV)
    kc, vct = _compress(ak, av,
                        _compress_weights(s_cmp_w1_k, s_cmp_pe_k, s_cmp_w2_k, transpose_w2=False),
                        _compress_weights(s_cmp_w1_v, s_cmp_pe_v, s_cmp_w2_v, transpose_w2=True),
                        batch, seq)

    n_chunk = seq // CMP_STRIDE
    n_sel = seq // SEL_BLOCK
    c_start = np.arange(n_chunk) * CMP_STRIDE
    j_start = np.arange(n_sel) * SEL_BLOCK
    ovl = ((c_start[None, :] < j_start[:, None] + SEL_BLOCK)
           & (c_start[None, :] + CMP_BLOCK > j_start[:, None])
           & (np.arange(n_chunk)[None, :] < (seq - CMP_BLOCK) // CMP_STRIDE + 1))
    ovl = jnp.asarray(ovl, BF16)

    for i in range(n_b):
        n_q = heads * NSA_DK
        w_qt = b_w_in_t0 if i == 0 else b_w_in_t[i].astype(BF16)
        w_o = b_w_o0 if i == 0 else b_w_o[i].astype(BF16)
        q = _mm(hb_b, w_qt, functools.partial(_epi_nsa_rope, d, NSA_DK ** -0.5 * LOG2_E),
                sds((heads, t, NSA_DK), BF16),
                pl.BlockSpec((per_tile, tm_rope, NSA_DK), lambda i_, j, k: (j, i_, 0)),
                tm=tm_rope, tn=NSA_ROPE_TILE, tk=d, nj=heads // per_tile,
                extras=(ssq, cos_nsa, sin_nsa),
                extra_specs=(pl.BlockSpec((tm_rope, LANES), row), rope_tab, rope_tab),
                a_buffers=1, w_transposed=True, name="nsa_q")
        n_gates = 3 * heads
        assert n_gates <= LANES
        w_gt = jnp.pad(b_w_in_t[i][n_q:], ((0, LANES - n_gates), (0, 0))).astype(BF16)
        gates = _mm(hb_b, w_gt, functools.partial(_epi_sigmoid, d), sds((t, LANES), F32),
                    pl.BlockSpec((tm, LANES), row),
                    tm=tm, tn=LANES, tk=d, nj=1,
                    extras=(ssq,), extra_specs=(ssq_spec,), w_transposed=True, name="nsa_gates")
        o = _nsa_attention(q, kc, vct, k3, vt3, gates, ovl, batch, seq, hpg)
        h, ssq, (hb,), _ = residual_mm(o, w_o, None, h, [f_norm[n_a + i]],
                                       tm_=tm, tn=512, name="nsa_out")
        gains = [b_norm[i + 1]] if i + 1 < n_b else []
        h, ssq, hbs, _ = ffn(hb, ssq, h, n_a + i, gains)
        if hbs:
            hb_b = hbs[0]

    out = _rmsnorm(h, final_norm, F32)
    return out.reshape(batch, seq, d)
```
